```python
import jax
import jax.numpy as jnp
from jax import lax
import numpy as np

D_MODEL = 1024
BATCH = 8
SEQ = 4096
DEPTH = 1

N_MEM = 256
EPS = 1e-6
NEG_INF = -1e30
SEL_FORCE = 1e4

R_HEADS = 4
R_DK = 128
R_DV = 256
R_CHUNK = 128
ROPE_BASE = 10000.0

NSA_HEADS = 8
NSA_GROUPS = 2
NSA_HPG = NSA_HEADS // NSA_GROUPS
NSA_DH = 128
CMP_LEN = 32
CMP_STRIDE = 16
SEL_LEN = 64
SEL_TOPK = 16
SEL_QCHUNK = 32
WINDOW = 512
WIN_QBLOCK = 128

X_HEADS = 4
X_DH = D_MODEL // X_HEADS

N_EGROUPS = 4
EXP_PER_GROUP = 8
N_EXPERTS = N_EGROUPS * EXP_PER_GROUP
EXP_TOPK = 2
D_EXPERT = 512
MOE_BLOCK = 256

RET_QK = R_HEADS * R_DK
RET_V = R_HEADS * R_DV
NSA_Q = NSA_HEADS * NSA_DH
NSA_KV = NSA_GROUPS * NSA_DH
SPLITS = (RET_QK, RET_QK, RET_V, RET_V, NSA_Q, NSA_KV, NSA_KV, NSA_KV, NSA_KV, NSA_KV, NSA_KV,
          3 * NSA_HEADS, D_MODEL, D_MODEL)
D_IN = sum(SPLITS)

kernel_name = 'hybrid_retention_nsa_hmoe_block'


def rmsnorm(x, g):
    xf = x.astype(jnp.float32)
    y = xf * lax.rsqrt(jnp.mean(xf * xf, axis=-1, keepdims=True) + EPS)
    return (y * g.astype(jnp.float32)).astype(x.dtype)


def masked_softmax(s, mask):
    p = jax.nn.softmax(jnp.where(mask, s, NEG_INF), axis=-1)
    return jnp.where(jnp.any(mask, axis=-1, keepdims=True), p, 0.0)


def rotate_pairs(t, cos, sin):
    t1 = t[..., 0::2]
    t2 = t[..., 1::2]
    out = jnp.stack([t1 * cos - t2 * sin, t1 * sin + t2 * cos], axis=-1)
    return out.reshape(t.shape).astype(t.dtype)


def retention(q, k, v, g):
    bsz, seq = q.shape[0], q.shape[1]
    nc = seq // R_CHUNK
    pos = jnp.arange(seq, dtype=jnp.float32)
    inv_freq = ROPE_BASE ** (-jnp.arange(0, R_DK, 2, dtype=jnp.float32) / R_DK)
    ang = pos[:, None] * inv_freq[None, :]
    cos = jnp.cos(ang)[None, :, None, :]
    sin = jnp.sin(ang)[None, :, None, :]
    q = rotate_pairs(q, cos, sin)
    k = rotate_pairs(k, cos, sin) * (R_DK ** -0.5)
    log_g = jnp.log1p(-jnp.exp2(-5.0 - jnp.arange(R_HEADS, dtype=jnp.float32)))
    n = jnp.arange(R_CHUNK, dtype=jnp.float32)
    diff = n[:, None] - n[None, :]
    decay = jnp.where(diff >= 0, jnp.exp(log_g[:, None, None] * jnp.maximum(diff, 0.0)), 0.0)
    qc = q.reshape(bsz, nc, R_CHUNK, R_HEADS, R_DK)
    kc = k.reshape(bsz, nc, R_CHUNK, R_HEADS, R_DK)
    vc = v.reshape(bsz, nc, R_CHUNK, R_HEADS, R_DV)
    scores = jnp.einsum('bcnhd,bcmhd->bchnm', qc, kc) * decay[None, None]
    inner = jnp.einsum('bchnm,bcmhe->bcnhe', scores, vc)
    zeta = jnp.exp(log_g[:, None] * (R_CHUNK - 1.0 - n)[None, :])
    kv = jnp.einsum('bcmhd,hm,bcmhe->bchde', kc, zeta, vc)
    chunk_decay = jnp.exp(log_g * R_CHUNK).astype(kv.dtype)

    def step(state, kv_c):
        return state * chunk_decay[None, :, None, None] + kv_c, state

    _, prev = lax.scan(step, jnp.zeros((bsz, R_HEADS, R_DK, R_DV), kv.dtype), jnp.moveaxis(kv, 1, 0))
    prev = jnp.moveaxis(prev, 0, 1)
    xi = jnp.exp(log_g[:, None] * (n + 1.0)[None, :])
    cross = jnp.einsum('bcnhd,bchde->bcnhe', qc, prev) * xi.T[None, None, :, :, None]
    o = (inner + cross).reshape(bsz, seq, R_HEADS, R_DV).astype(jnp.float32)
    o = o * lax.rsqrt(jnp.mean(o * o, axis=-1, keepdims=True) + EPS)
    return jax.nn.silu(g) * o.reshape(bsz, seq, RET_V).astype(g.dtype)


def compress_blocks(kv, pe, w1, w2):
    seq = kv.shape[1]
    ncmp = (seq - CMP_LEN) // CMP_STRIDE + 1
    idx = jnp.arange(ncmp)[:, None] * CMP_STRIDE + jnp.arange(CMP_LEN)[None, :]
    blocks = kv[:, idx] + pe[None, None, :, None, :]
    hid = jax.nn.gelu(jnp.einsum('bnlgd,lde->bnge', blocks, w1))
    return jnp.einsum('bnge,ef->bngf', hid, w2)


def gather_blocks(blocks, idx):
    return jax.vmap(jax.vmap(lambda blk, ix: blk[ix]))(blocks, idx)


def nsa(q, k_c, v_c, k_s, v_s, k_w, v_w, gate_logits, pe_k, w1_k, w2_k, pe_v, w1_v, w2_v):
    bsz, seq = q.shape[0], q.shape[1]
    scale = NSA_DH ** -0.5
    qg = q.reshape(bsz, seq, NSA_GROUPS, NSA_HPG, NSA_DH)
    shp = (bsz, seq, NSA_GROUPS, NSA_DH)
    k_c, v_c, k_s, v_s, k_w, v_w = [a.reshape(shp) for a in (k_c, v_c, k_s, v_s, k_w, v_w)]
    t = jnp.arange(seq)

    kcmp = compress_blocks(k_c, pe_k, w1_k, w2_k)
    vcmp = compress_blocks(v_c, pe_v, w1_v, w2_v)
    ncmp = kcmp.shape[1]
    cstart = jnp.arange(ncmp) * CMP_STRIDE
    cmask = (cstart + CMP_LEN - 1)[None, :] <= t[:, None]
    s_c = jnp.einsum('bsghd,bngd->bghsn', qg, kcmp).astype(jnp.float32) * scale
    p_c = masked_softmax(s_c, cmask)
    o_cmp = jnp.einsum('bghsn,bngd->bsghd', p_c.astype(q.dtype), vcmp)

    nb = seq // SEL_LEN
    jstart = jnp.arange(nb) * SEL_LEN
    overlap = ((cstart[:, None] < jstart[None, :] + SEL_LEN) &
               (cstart[:, None] + CMP_LEN > jstart[None, :])).astype(jnp.float32)
    imp = jnp.einsum('bghsn,nj->bgsj', p_c, overlap)
    tb = t // SEL_LEN
    jj = jnp.arange(nb)
    forced = (jj[None, :] == 0) | (jj[None, :] == tb[:, None]) | (jj[None, :] == tb[:, None] - 1)
    future = jj[None, :] > tb[:, None]
    imp = jnp.where(future, -SEL_FORCE, jnp.where(forced, SEL_FORCE, imp))
    k_sel = min(SEL_TOPK, nb)
    _, sel_idx = lax.top_k(imp, k_sel)

    ksb = k_s.reshape(bsz, nb, SEL_LEN, NSA_GROUPS, NSA_DH).transpose(0, 3, 1, 2, 4)
    vsb = v_s.reshape(bsz, nb, SEL_LEN, NSA_GROUPS, NSA_DH).transpose(0, 3, 1, 2, 4)
    nqc = seq // SEL_QCHUNK
    q_chunks = jnp.moveaxis(qg.reshape(bsz, nqc, SEL_QCHUNK, NSA_GROUPS, NSA_HPG, NSA_DH), 1, 0)
    idx_chunks = jnp.moveaxis(sel_idx.reshape(bsz, NSA_GROUPS, nqc, SEL_QCHUNK, k_sel), 2, 0)
    q_starts = jnp.arange(nqc) * SEL_QCHUNK

    def sel_body(args):
        qc, ix, st = args
        kg = gather_blocks(ksb, ix)
        vg = gather_blocks(vsb, ix)
        s = jnp.einsum('bqghd,bgqnld->bghqnl', qc, kg).astype(jnp.float32) * scale
        kpos = ix[..., None] * SEL_LEN + jnp.arange(SEL_LEN)
        tq = st + jnp.arange(SEL_QCHUNK)
        m = (kpos <= tq[None, None, :, None, None])[:, :, None]
        s = jnp.where(m, s, NEG_INF).reshape(bsz, NSA_GROUPS, NSA_HPG, SEL_QCHUNK, k_sel * SEL_LEN)
        p = jax.nn.softmax(s, axis=-1).reshape(bsz, NSA_GROUPS, NSA_HPG, SEL_QCHUNK, k_sel, SEL_LEN)
        return jnp.einsum('bghqnl,bgqnld->bqghd', p.astype(qc.dtype), vg)

    o_sel = lax.map(sel_body, (q_chunks, idx_chunks, q_starts))
    o_sel = jnp.moveaxis(o_sel, 0, 1).reshape(bsz, seq, NSA_GROUPS, NSA_HPG, NSA_DH)

    kpad = jnp.pad(k_w, ((0, 0), (WINDOW, 0), (0, 0), (0, 0)))
    vpad = jnp.pad(v_w, ((0, 0), (WINDOW, 0), (0, 0), (0, 0)))
    nwb = seq // WIN_QBLOCK
    q_blocks = jnp.moveaxis(qg.reshape(bsz, nwb, WIN_QBLOCK, NSA_GROUPS, NSA_HPG, NSA_DH), 1, 0)

    def win_body(args):
        qb, i = args
        st = i * WIN_QBLOCK
        kb = lax.dynamic_slice_in_dim(kpad, st, WINDOW + WIN_QBLOCK, axis=1)
        vb = lax.dynamic_slice_in_dim(vpad, st, WINDOW + WIN_QBLOCK, axis=1)
        s = jnp.einsum('bqghd,bkgd->bghqk', qb, kb).astype(jnp.float32) * scale
        kp = st - WINDOW + jnp.arange(WINDOW + WIN_QBLOCK)
        tq = st + jnp.arange(WIN_QBLOCK)
        m = (kp[None, :] <= tq[:, None]) & (kp[None, :] > tq[:, None] - WINDOW) & (kp[None, :] >= 0)
        p = jax.nn.softmax(jnp.where(m, s, NEG_INF), axis=-1)
        return jnp.einsum('bghqk,bkgd->bqghd', p.astype(qb.dtype), vb)

    o_win = lax.map(win_body, (q_blocks, jnp.arange(nwb)))
    o_win = jnp.moveaxis(o_win, 0, 1).reshape(bsz, seq, NSA_GROUPS, NSA_HPG, NSA_DH)

    gates = jax.nn.sigmoid(gate_logits.astype(jnp.float32)).reshape(bsz, seq, NSA_GROUPS, NSA_HPG, 3)
    gates = gates.astype(q.dtype)
    o = gates[..., 0, None] * o_cmp + gates[..., 1, None] * o_sel + gates[..., 2, None] * o_win
    return o.reshape(bsz, seq, NSA_Q)


def cross_attention(h, m, w_q, w_kv, w_o):
    bsz, seq = h.shape[0], h.shape[1]
    nmem = m.shape[1]
    q = (h @ w_q).reshape(bsz, seq, X_HEADS, X_DH)
    k, v = jnp.split(m @ w_kv, 2, axis=-1)
    k = k.reshape(bsz, nmem, X_HEADS, X_DH)
    v = v.reshape(bsz, nmem, X_HEADS, X_DH)
    s = jnp.einsum('bshd,bmhd->bhsm', q, k).astype(jnp.float32) * (X_DH ** -0.5)
    p = jax.nn.softmax(s, axis=-1).astype(h.dtype)
    o = jnp.einsum('bhsm,bmhd->bshd', p, v).reshape(bsz, seq, D_MODEL)
    return o @ w_o


def hier_moe(h, w_grp, b_grp, w_rt, b_rt, w1, w3, w2):
    n_tok = h.shape[0]
    lg = (h @ w_grp + b_grp).astype(jnp.float32)
    pg = jax.nn.softmax(lg, axis=-1)
    grp = jnp.argmax(lg, axis=-1)
    g_gate = jnp.take_along_axis(pg, grp[:, None], axis=-1)[:, 0]
    le = (h @ w_rt + b_rt).astype(jnp.float32).reshape(n_tok, N_EGROUPS, EXP_PER_GROUP)
    le_g = jnp.take_along_axis(le, grp[:, None, None], axis=1)[:, 0]
    pe = jax.nn.softmax(le_g, axis=-1)
    top_p, top_i = lax.top_k(pe, EXP_TOPK)
    wts = g_gate[:, None] * top_p / jnp.sum(top_p, axis=-1, keepdims=True)
    eid = (grp[:, None] * EXP_PER_GROUP + top_i).reshape(-1).astype(jnp.int32)
    wflat = wts.reshape(-1)
    n_asg = n_tok * EXP_TOPK
    tok = (jnp.arange(n_asg) // EXP_TOPK).astype(jnp.int32)
    order = jnp.argsort(eid)
    se = eid[order]
    counts = jnp.bincount(eid, length=N_EXPERTS)
    padded = (counts + MOE_BLOCK - 1) // MOE_BLOCK * MOE_BLOCK
    starts = jnp.cumsum(counts) - counts
    pends = jnp.cumsum(padded)
    pstarts = pends - padded
    dest = pstarts[se] + (jnp.arange(n_asg) - starts[se])
    cap = ((n_asg + MOE_BLOCK - 1) // MOE_BLOCK + N_EXPERTS) * MOE_BLOCK
    nblk = cap // MOE_BLOCK
    buf_tok = jnp.full((cap,), n_tok, jnp.int32).at[dest].set(tok[order])
    buf_w = jnp.zeros((cap,), h.dtype).at[dest].set(wflat[order].astype(h.dtype))
    blk_e = jnp.minimum(jnp.searchsorted(pends, jnp.arange(nblk) * MOE_BLOCK, side='right'), N_EXPERTS - 1)
    hp = jnp.concatenate([h, jnp.zeros((1, h.shape[1]), h.dtype)], axis=0)

    def body(args):
        e, tk, wb = args
        xb = hp[tk]
        y = (jax.nn.silu(xb @ w1[e]) * (xb @ w3[e])) @ w2[e]
        return y * wb[:, None]

    ys = lax.map(body, (blk_e, buf_tok.reshape(nblk, MOE_BLOCK), buf_w.reshape(nblk, MOE_BLOCK)))
    out = jnp.zeros((n_tok + 1, h.shape[1]), h.dtype).at[buf_tok].add(ys.reshape(cap, h.shape[1]))
    return out[:n_tok]


def setup_inputs(seed: int = 0) -> dict:
    key = jax.random.key(seed)
    ks = jax.random.split(key, 32)
    f32 = jnp.float32

    def nrm(k, shape, fan_in):
        return jax.random.normal(k, shape, f32) * (fan_in ** -0.5)

    def gain(k, shape):
        return 1.0 + 0.02 * jax.random.normal(k, shape, f32)

    L = DEPTH
    return {
        'x': jax.random.normal(ks[0], (BATCH, SEQ, D_MODEL), f32),
        'mem': jax.random.normal(ks[1], (BATCH, N_MEM, D_MODEL), f32),
        'norm_mix_g': gain(ks[2], (L, D_MODEL)),
        'w_in': nrm(ks[3], (L, D_MODEL, D_IN), D_MODEL),
        'w_ret_o': nrm(ks[4], (L, RET_V, D_MODEL), RET_V),
        'w_nsa_o': nrm(ks[5], (L, NSA_Q, D_MODEL), NSA_Q),
        'w_out': nrm(ks[6], (L, D_MODEL, D_MODEL), D_MODEL),
        'cmp_pe_k': 0.1 * jax.random.normal(ks[7], (L, CMP_LEN, NSA_DH), f32),
        'cmp_w1_k': nrm(ks[8], (L, CMP_LEN, NSA_DH, NSA_DH), CMP_LEN * NSA_DH),
        'cmp_w2_k': nrm(ks[9], (L, NSA_DH, NSA_DH), NSA_DH),
        'cmp_pe_v': 0.1 * jax.random.normal(ks[10], (L, CMP_LEN, NSA_DH), f32),
        'cmp_w1_v': nrm(ks[11], (L, CMP_LEN, NSA_DH, NSA_DH), CMP_LEN * NSA_DH),
        'cmp_w2_v': nrm(ks[12], (L, NSA_DH, NSA_DH), NSA_DH),
        'norm_x_g': gain(ks[13], (L, D_MODEL)),
        'norm_mem_g': gain(ks[14], (L, D_MODEL)),
        'w_xq': nrm(ks[15], (L, D_MODEL, D_MODEL), D_MODEL),
        'w_xkv': nrm(ks[16], (L, D_MODEL, 2 * D_MODEL), D_MODEL),
        'w_xo': nrm(ks[17], (L, D_MODEL, D_MODEL), D_MODEL),
        'norm_ffn_g': gain(ks[18], (L, D_MODEL)),
        'w_grp': nrm(ks[19], (L, D_MODEL, N_EGROUPS), D_MODEL),
        'b_grp': 0.01 * jax.random.normal(ks[20], (L, N_EGROUPS), f32),
        'w_rt': nrm(ks[21], (L, D_MODEL, N_EXPERTS), D_MODEL),
        'b_rt': 0.01 * jax.random.normal(ks[22], (L, N_EXPERTS), f32),
        'w_e1': nrm(ks[23], (L, N_EXPERTS, D_MODEL, D_EXPERT), D_MODEL),
        'w_e3': nrm(ks[24], (L, N_EXPERTS, D_MODEL, D_EXPERT), D_MODEL),
        'w_e2': nrm(ks[25], (L, N_EXPERTS, D_EXPERT, D_MODEL), D_EXPERT),
        'norm_f_g': gain(ks[26], (D_MODEL,)),
    }


def reference(x, mem, norm_mix_g, w_in, w_ret_o, w_nsa_o, w_out, cmp_pe_k, cmp_w1_k, cmp_w2_k,
              cmp_pe_v, cmp_w1_v, cmp_w2_v, norm_x_g, norm_mem_g, w_xq, w_xkv, w_xo, norm_ffn_g,
              w_grp, b_grp, w_rt, b_rt, w_e1, w_e3, w_e2, norm_f_g):
    bsz, seq = x.shape[0], x.shape[1]
    split_points = np.cumsum(SPLITS)[:-1].tolist()
    for l in range(DEPTH):
        h = rmsnorm(x, norm_mix_g[l])
        proj = h @ w_in[l]
        (rq, rk, rv, rg, nq, ck, cv, sk, sv, wk, wv, ngl, ga, gb) = jnp.split(proj, split_points, axis=-1)
        y_ret = retention(rq.reshape(bsz, seq, R_HEADS, R_DK), rk.reshape(bsz, seq, R_HEADS, R_DK),
                          rv.reshape(bsz, seq, R_HEADS, R_DV), rg) @ w_ret_o[l]
        y_nsa = nsa(nq, ck, cv, sk, sv, wk, wv, ngl, cmp_pe_k[l], cmp_w1_k[l], cmp_w2_k[l],
                    cmp_pe_v[l], cmp_w1_v[l], cmp_w2_v[l]) @ w_nsa_o[l]
        y = jax.nn.sigmoid(ga) * y_ret + jax.nn.sigmoid(gb) * y_nsa
        x = x + y @ w_out[l]
        x = x + cross_attention(rmsnorm(x, norm_x_g[l]), rmsnorm(mem, norm_mem_g[l]),
                                w_xq[l], w_xkv[l], w_xo[l])
        hf = rmsnorm(x, norm_ffn_g[l]).reshape(bsz * seq, D_MODEL)
        x = x + hier_moe(hf, w_grp[l], b_grp[l], w_rt[l], b_rt[l], w_e1[l], w_e3[l], w_e2[l]).reshape(bsz, seq, D_MODEL)
    return rmsnorm(x, norm_f_g)
```

```python
import functools

import numpy as np
import jax
import jax.numpy as jnp
from jax import lax
from jax.experimental import pallas as pl
from jax.experimental.pallas import tpu as pltpu

MXU_DTYPE = jnp.bfloat16
F32 = jnp.float32

D_MODEL = 1024
N_MEM = 256
EPS = 1e-6
NEG_INF = -1e30
SEL_FORCE = 1e4

R_HEADS = 4
R_DK = 128
R_DV = 256
R_CHUNK = 128
ROPE_BASE = 10000.0

NSA_HEADS = 8
NSA_GROUPS = 2
NSA_HPG = NSA_HEADS // NSA_GROUPS
NSA_DH = 128
CMP_LEN = 32
CMP_STRIDE = 16
SEL_LEN = 64
SEL_TOPK = 16
WINDOW = 512

X_HEADS = 4
X_DH = D_MODEL // X_HEADS

N_EGROUPS = 4
EXP_PER_GROUP = 8
N_EXPERTS = N_EGROUPS * EXP_PER_GROUP
EXP_TOPK = 2
D_EXPERT = 512
MOE_BLOCK = 256

RET_QK = R_HEADS * R_DK
RET_V = R_HEADS * R_DV
NSA_Q = NSA_HEADS * NSA_DH
NSA_KV = NSA_GROUPS * NSA_DH
SPLITS = (RET_QK, RET_QK, RET_V, RET_V, NSA_Q, NSA_KV, NSA_KV, NSA_KV, NSA_KV, NSA_KV, NSA_KV,
          3 * NSA_HEADS, D_MODEL, D_MODEL)

LANES = 128
VMEM_LIMIT = 56 * 1024 * 1024

PROJ_TM = 512
RET_ROWS = 512
ATT_TQ = 256
ATT_TK = 256
ROW_TM = 512


def _params(*sem):
    return pltpu.CompilerParams(dimension_semantics=sem, vmem_limit_bytes=VMEM_LIMIT)


def _dot(a, b):
    return jnp.dot(a.astype(MXU_DTYPE), b.astype(MXU_DTYPE), preferred_element_type=F32)


def _dot_nt(a, b):
    return lax.dot_general(a.astype(MXU_DTYPE), b.astype(MXU_DTYPE), (((1,), (1,)), ((), ())),
                           preferred_element_type=F32)


def _dot_tn(a, b):
    return lax.dot_general(a.astype(MXU_DTYPE), b.astype(MXU_DTYPE), (((0,), (0,)), ((), ())),
                           preferred_element_type=F32)


def _split3(p):
    hi = p.astype(MXU_DTYPE)
    r1 = p - hi.astype(F32)
    mid = r1.astype(MXU_DTYPE)
    lo = (r1 - mid.astype(F32)).astype(MXU_DTYPE)
    return hi, mid, lo


def _rms(x, g):
    return x * lax.rsqrt(jnp.mean(x * x, axis=-1, keepdims=True) + EPS) * g


_C_RQK = 0
_C_RV = _C_RQK + 2 * RET_QK
_C_RG = _C_RV + RET_V
_C_NQ = _C_RG + RET_V
_C_CKV = _C_NQ + NSA_Q
_C_SKV = _C_CKV + 2 * NSA_KV
_C_WKV = _C_SKV + 2 * NSA_KV
_C_GAB = _C_WKV + 2 * NSA_KV
_C_NGL = _C_GAB + 2 * D_MODEL
_C_END = _C_NGL + LANES


def _proj_kernel(x_ref, g_ref, w_ref, cos_ref, sin_ref,
                 qk_ref, rv_ref, rg_ref, nq_ref, ckv_ref, skv_ref, wkv_ref, gab_ref, ngl_ref):
    hb = _rms(x_ref[...], g_ref[...]).astype(MXU_DTYPE)

    def mm(off, width):
        return jnp.dot(hb, w_ref[:, off:off + width], preferred_element_type=F32)

    cos = cos_ref[...]
    sin = sin_ref[...]
    qk = mm(_C_RQK, 2 * RET_QK)
    for i in range(2 * R_HEADS):
        t = qk[:, i * R_DK:(i + 1) * R_DK]
        r = t * cos + pltpu.roll(t, R_DK // 2, axis=1) * sin
        if i >= R_HEADS:
            r = r * (R_DK ** -0.5)
        qk_ref[:, i * R_DK:(i + 1) * R_DK] = r.astype(qk_ref.dtype)
    rv_ref[...] = mm(_C_RV, RET_V).astype(rv_ref.dtype)
    rg_ref[...] = mm(_C_RG, RET_V)
    nq_ref[...] = mm(_C_NQ, NSA_Q).astype(nq_ref.dtype)
    ckv_ref[...] = mm(_C_CKV, 2 * NSA_KV)
    skv_ref[...] = mm(_C_SKV, 2 * NSA_KV).astype(skv_ref.dtype)
    wkv_ref[...] = mm(_C_WKV, 2 * NSA_KV).astype(wkv_ref.dtype)
    gab_ref[:, :D_MODEL] = mm(_C_GAB, D_MODEL)
    gab_ref[:, D_MODEL:] = mm(_C_GAB + D_MODEL, D_MODEL)
    ngl_ref[...] = mm(_C_NGL, LANES)


def _permute_w_in(w_in):
    offs = np.cumsum((0,) + SPLITS)
    seg = [w_in[:, offs[i]:offs[i + 1]] for i in range(len(SPLITS))]
    rq, rk, rv, rg, nq, ck, cv, sk, sv, wk, wv, ngl, ga, gb = seg
    half = np.concatenate([np.arange(0, R_DK, 2), np.arange(1, R_DK, 2)])
    perm = (np.arange(R_HEADS)[:, None] * R_DK + half[None, :]).reshape(-1)
    ngl = jnp.pad(ngl, ((0, 0), (0, LANES - ngl.shape[1])))
    return jnp.concatenate([rq[:, perm], rk[:, perm], rv, rg, nq, ck, cv, sk, sv, wk, wv, ga, gb, ngl],
                           axis=1).astype(MXU_DTYPE)


def _rope_tables(seq):
    pos = jnp.arange(seq, dtype=F32)
    inv_freq = ROPE_BASE ** (-jnp.arange(0, R_DK, 2, dtype=F32) / R_DK)
    ang = pos[:, None] * inv_freq[None, :]
    cos = jnp.cos(ang)
    sin = jnp.sin(ang)
    return jnp.concatenate([cos, cos], axis=1), jnp.concatenate([-sin, sin], axis=1)


def _proj(x2d, g, w, cos, sin, seq):
    n = x2d.shape[0]
    tm = PROJ_TM
    nt = seq // tm
    row = lambda i: (i, 0)
    const = lambda i: (0, 0)
    widths = (2 * RET_QK, RET_V, RET_V, NSA_Q, 2 * NSA_KV, 2 * NSA_KV, 2 * NSA_KV, 2 * D_MODEL, LANES)
    dtypes = (MXU_DTYPE, MXU_DTYPE, F32, MXU_DTYPE, F32, MXU_DTYPE, MXU_DTYPE, F32, F32)
    return pl.pallas_call(
        _proj_kernel,
        grid=(n // tm,),
        in_specs=[pl.BlockSpec((tm, D_MODEL), row),
                  pl.BlockSpec((1, D_MODEL), const),
                  pl.BlockSpec((D_MODEL, _C_END), const, pipeline_mode=pl.Buffered(1)),
                  pl.BlockSpec((tm, R_DK), lambda i: (i % nt, 0)),
                  pl.BlockSpec((tm, R_DK), lambda i: (i % nt, 0))],
        out_specs=[pl.BlockSpec((tm, wd), row) for wd in widths],
        out_shape=[jax.ShapeDtypeStruct((n, wd), dt) for wd, dt in zip(widths, dtypes)],
        compiler_params=_params("parallel"),
        name="proj",
    )(x2d, g, w, cos, sin)


def _ret_kernel(qk_ref, v_ref, g_ref, decay_ref, zeta_ref, xi_ref, cd_ref, y_ref, state_ref):
    @pl.when(pl.program_id(1) == 0)
    def _():
        state_ref[...] = jnp.zeros_like(state_ref)

    for c in range(RET_ROWS // R_CHUNK):
        rows = slice(c * R_CHUNK, (c + 1) * R_CHUNK)
        for h in range(R_HEADS):
            q = qk_ref[rows, h * R_DK:(h + 1) * R_DK]
            k = qk_ref[rows, RET_QK + h * R_DK:RET_QK + (h + 1) * R_DK]
            v = v_ref[rows, h * R_DV:(h + 1) * R_DV]
            st = state_ref[h]
            scores = _dot_nt(q, k) * decay_ref[h]
            o = _dot(scores, v) + _dot(q, st) * xi_ref[h]
            o = o * lax.rsqrt(jnp.mean(o * o, axis=-1, keepdims=True) + EPS)
            g = g_ref[rows, h * R_DV:(h + 1) * R_DV]
            y_ref[rows, h * R_DV:(h + 1) * R_DV] = (g * jax.nn.sigmoid(g) * o).astype(y_ref.dtype)
            kz = k.astype(F32) * zeta_ref[h]
            state_ref[h] = st * cd_ref[h] + _dot_tn(kz, v)


def _retention(qk, rv, rg, bsz, seq):
    n = qk.shape[0]
    nt = seq // RET_ROWS
    log_g = jnp.log1p(-jnp.exp2(-5.0 - jnp.arange(R_HEADS, dtype=F32)))
    idx = jnp.arange(R_CHUNK, dtype=F32)
    diff = idx[:, None] - idx[None, :]
    decay = jnp.where(diff >= 0, jnp.exp(log_g[:, None, None] * jnp.maximum(diff, 0.0)), 0.0)
    zeta = jnp.exp(log_g[:, None] * (R_CHUNK - 1.0 - idx)[None, :])[:, :, None]
    xi = jnp.exp(log_g[:, None] * (idx + 1.0)[None, :])[:, :, None]
    cd = jnp.exp(log_g * R_CHUNK)[:, None, None]
    row = lambda b, i: (b * nt + i, 0)
    const3 = lambda b, i: (0, 0, 0)
    return pl.pallas_call(
        _ret_kernel,
        grid=(bsz, nt),
        in_specs=[pl.BlockSpec((RET_ROWS, 2 * RET_QK), row),
                  pl.BlockSpec((RET_ROWS, RET_V), row),
                  pl.BlockSpec((RET_ROWS, RET_V), row),
                  pl.BlockSpec((R_HEADS, R_CHUNK, R_CHUNK), const3),
                  pl.BlockSpec((R_HEADS, R_CHUNK, 1), const3),
                  pl.BlockSpec((R_HEADS, R_CHUNK, 1), const3),
                  pl.BlockSpec((R_HEADS, 1, 1), const3)],
        out_specs=pl.BlockSpec((RET_ROWS, RET_V), row),
        out_shape=jax.ShapeDtypeStruct((n, RET_V), MXU_DTYPE),
        scratch_shapes=[pltpu.VMEM((R_HEADS, R_DK, R_DV), F32)],
        compiler_params=_params("parallel", "arbitrary"),
        name="retention",
    )(qk, rv, rg, decay, zeta, xi, cd)


def _compress_kernel(x_ref, pe_ref, w1_ref, w2_ref, o_ref, buf_ref, *, seq):
    ncp = seq // CMP_STRIDE
    buf_ref[0:seq, :] = x_ref[...]
    buf_ref[seq:seq + LANES, :] = jnp.zeros((LANES, NSA_DH), F32)
    acc = jnp.zeros((ncp, NSA_DH), F32)
    for l in range(CMP_LEN):
        xl = buf_ref[pl.ds(l, ncp, stride=CMP_STRIDE), :] + pe_ref[l:l + 1, :]
        acc = acc + _dot(xl, w1_ref[l])
    o_ref[...] = _dot(jax.nn.gelu(acc), w2_ref[...]).astype(o_ref.dtype)


def _compress(ckv, pe, w1, w2, bsz, seq):
    ncp = seq // CMP_STRIDE
    nj = 2 * NSA_GROUPS
    return pl.pallas_call(
        functools.partial(_compress_kernel, seq=seq),
        grid=(bsz, nj),
        in_specs=[pl.BlockSpec((seq, NSA_DH), lambda b, j: (b, j)),
                  pl.BlockSpec((None, CMP_LEN, NSA_DH), lambda b, j: (j // NSA_GROUPS, 0, 0)),
                  pl.BlockSpec((None, CMP_LEN, NSA_DH, NSA_DH), lambda b, j: (j // NSA_GROUPS, 0, 0, 0)),
                  pl.BlockSpec((None, NSA_DH, NSA_DH), lambda b, j: (j // NSA_GROUPS, 0, 0))],
        out_specs=pl.BlockSpec((None, None, ncp, NSA_DH), lambda b, j: (b, j, 0, 0)),
        out_shape=jax.ShapeDtypeStruct((bsz, nj, ncp, NSA_DH), MXU_DTYPE),
        scratch_shapes=[pltpu.VMEM((seq + LANES, NSA_DH), F32)],
        compiler_params=_params("parallel", "parallel"),
        name="compress",
    )(ckv, pe, w1, w2)


def _cmp_attn_kernel(q_ref, k_ref, v_ref, ov_ref, o_ref, sel_ref, *, ncp, nb):
    tq = ATT_TQ
    t0 = pl.program_id(2) * tq
    t = t0 + lax.broadcasted_iota(jnp.int32, (tq, ncp), 0)
    n = lax.broadcasted_iota(jnp.int32, (tq, ncp), 1)
    valid = (n * CMP_STRIDE + (CMP_LEN - 1) <= t) & (n < ncp - 1)
    any_valid = (t0 + lax.broadcasted_iota(jnp.int32, (tq, 1), 0)) >= CMP_LEN - 1
    k = k_ref[...]
    v = v_ref[...]
    psum = jnp.zeros((tq, ncp), F32)
    for h in range(NSA_HPG):
        s = _dot_nt(q_ref[:, h * NSA_DH:(h + 1) * NSA_DH], k) * (NSA_DH ** -0.5)
        s = jnp.where(valid, s, NEG_INF)
        e = jnp.exp(s - jnp.max(s, axis=-1, keepdims=True))
        p = e / jnp.sum(e, axis=-1, keepdims=True)
        p = jnp.where(any_valid, p, 0.0)
        o_ref[:, h * NSA_DH:(h + 1) * NSA_DH] = _dot(p, v)
        psum = psum + p

    ov = ov_ref[...]
    imp = sum(_dot_nt(ov, part) for part in _split3(psum))
    j = lax.broadcasted_iota(jnp.int32, (nb, tq), 0)
    tb = (t0 + lax.broadcasted_iota(jnp.int32, (nb, tq), 1)) // SEL_LEN
    forced = (j == 0) | (j == tb) | (j == tb - 1)
    imp = jnp.where(j > tb, -SEL_FORCE, jnp.where(forced, SEL_FORCE, imp))

    sub = 8
    grp = [imp[r * sub:(r + 1) * sub] for r in range(nb // sub)]
    cnt = [jnp.zeros((sub, tq), F32) for _ in grp]
    for i in range(nb):
        row = jnp.broadcast_to(imp[i:i + 1, :], (sub, tq))
        for r in range(nb // sub):
            ge = jnp.where(row >= grp[r], 1.0, 0.0)
            gt = jnp.where(row > grp[r], 1.0, 0.0)
            if r * sub > i:
                beats = ge
            elif r * sub + sub - 1 < i:
                beats = gt
            else:
                jr = r * sub + lax.broadcasted_iota(jnp.int32, (sub, tq), 0)
                beats = jnp.where(jr > i, ge, gt)
            cnt[r] = cnt[r] + beats
    k_sel = min(SEL_TOPK, nb)
    sel_t = jnp.concatenate([jnp.where(c < k_sel, 1.0, 0.0) for c in cnt], axis=0)
    eye = (lax.broadcasted_iota(jnp.int32, (tq, tq), 0) ==
           lax.broadcasted_iota(jnp.int32, (tq, tq), 1)).astype(MXU_DTYPE)
    sel_ref[...] = _dot_nt(eye, sel_t).astype(sel_ref.dtype)


def _cmp_attention(nq, cmp_kv, bsz, seq):
    n = nq.shape[0]
    ncp = seq // CMP_STRIDE
    nb = seq // SEL_LEN
    nt = seq // ATT_TQ
    gw = NSA_HPG * NSA_DH
    cstart = np.arange(ncp) * CMP_STRIDE
    jstart = np.arange(nb) * SEL_LEN
    ov = ((cstart[None, :] < jstart[:, None] + SEL_LEN) & (cstart[None, :] + CMP_LEN > jstart[:, None])
          & (np.arange(ncp)[None, :] < ncp - 1))
    ov = jnp.asarray(ov, MXU_DTYPE)
    return pl.pallas_call(
        functools.partial(_cmp_attn_kernel, ncp=ncp, nb=nb),
        grid=(bsz, NSA_GROUPS, nt),
        in_specs=[pl.BlockSpec((ATT_TQ, gw), lambda b, g, i: (b * nt + i, g)),
                  pl.BlockSpec((None, None, ncp, NSA_DH), lambda b, g, i: (b, g, 0, 0)),
                  pl.BlockSpec((None, None, ncp, NSA_DH), lambda b, g, i: (b, NSA_GROUPS + g, 0, 0)),
                  pl.BlockSpec((nb, ncp), lambda b, g, i: (0, 0))],
        out_specs=[pl.BlockSpec((ATT_TQ, gw), lambda b, g, i: (b * nt + i, g)),
                   pl.BlockSpec((None, None, ATT_TQ, nb), lambda b, g, i: (b, g, i, 0))],
        out_shape=[jax.ShapeDtypeStruct((n, NSA_Q), F32),
                   jax.ShapeDtypeStruct((bsz, NSA_GROUPS, seq, nb), MXU_DTYPE)],
        compiler_params=_params("parallel", "parallel", "parallel"),
        name="cmp_attention",
    )(nq, cmp_kv, cmp_kv, ov)


def _flash_step(q_ref, k_ref, v_ref, m_ref, l_ref, acc_ref, kt, valid):
    k = k_ref[pl.ds(pl.multiple_of(kt * ATT_TK, ATT_TK), ATT_TK), :]
    v = v_ref[pl.ds(pl.multiple_of(kt * ATT_TK, ATT_TK), ATT_TK), :]
    for h in range(NSA_HPG):
        s = _dot_nt(q_ref[:, h * NSA_DH:(h + 1) * NSA_DH], k) * (NSA_DH ** -0.5)
        s = jnp.where(valid, s, NEG_INF)
        m_old = m_ref[h]
        m_new = jnp.maximum(m_old, jnp.max(s, axis=-1, keepdims=True))
        alpha = jnp.exp(m_old - m_new)
        p = jnp.exp(s - m_new)
        l_ref[h] = alpha * l_ref[h] + jnp.sum(p, axis=-1, keepdims=True)
        acc_ref[h] = alpha * acc_ref[h] + _dot(p, v)
        m_ref[h] = m_new


def _flash_init(m_ref, l_ref, acc_ref):
    m_ref[...] = jnp.full(m_ref.shape, NEG_INF, F32)
    l_ref[...] = jnp.zeros_like(l_ref)
    acc_ref[...] = jnp.zeros_like(acc_ref)


def _flash_finish(o_ref, l_ref, acc_ref):
    for h in range(NSA_HPG):
        o_ref[:, h * NSA_DH:(h + 1) * NSA_DH] = acc_ref[h] / l_ref[h]


def _sel_attn_kernel(q_ref, k_ref, v_ref, sel_ref, o_ref, m_ref, l_ref, acc_ref, *, nb):
    qi = pl.program_id(2)
    _flash_init(m_ref, l_ref, acc_ref)
    tpos = qi * ATT_TQ + lax.broadcasted_iota(jnp.int32, (ATT_TQ, ATT_TK), 0)
    col = lax.broadcasted_iota(jnp.int32, (ATT_TQ, ATT_TK), 1)
    sel = sel_ref[...]
    blk_row = lax.broadcasted_iota(jnp.int32, (nb, ATT_TK), 0)
    blk_col = lax.broadcasted_iota(jnp.int32, (nb, ATT_TK), 1) // SEL_LEN

    def body(kt, carry):
        expand = (blk_row == blk_col + kt * (ATT_TK // SEL_LEN)).astype(MXU_DTYPE)
        picked = _dot(sel, expand) > 0.5
        valid = picked & (col + kt * ATT_TK <= tpos)
        _flash_step(q_ref, k_ref, v_ref, m_ref, l_ref, acc_ref, kt, valid)
        return carry

    lax.fori_loop(0, (qi + 1) * (ATT_TQ // ATT_TK), body, 0)
    _flash_finish(o_ref, l_ref, acc_ref)


def _win_attn_kernel(q_ref, k_ref, v_ref, o_ref, m_ref, l_ref, acc_ref):
    qi = pl.program_id(2)
    _flash_init(m_ref, l_ref, acc_ref)
    tpos = qi * ATT_TQ + lax.broadcasted_iota(jnp.int32, (ATT_TQ, ATT_TK), 0)
    col = lax.broadcasted_iota(jnp.int32, (ATT_TQ, ATT_TK), 1)
    last = (qi + 1) * (ATT_TQ // ATT_TK) - 1
    n_tiles = jnp.minimum(last + 1, (ATT_TQ + WINDOW) // ATT_TK)

    def body(step, carry):
        kt = last - step
        kpos = col + kt * ATT_TK
        valid = (kpos <= tpos) & (kpos > tpos - WINDOW)
        _flash_step(q_ref, k_ref, v_ref, m_ref, l_ref, acc_ref, kt, valid)
        return carry

    lax.fori_loop(0, n_tiles, body, 0)
    _flash_finish(o_ref, l_ref, acc_ref)


def _flash_attention(nq, kv, sel, bsz, seq):
    n = nq.shape[0]
    nt = seq // ATT_TQ
    nb = seq // SEL_LEN
    gw = NSA_HPG * NSA_DH
    q_spec = pl.BlockSpec((ATT_TQ, gw), lambda b, g, i: (b * nt + i, g))
    in_specs = [q_spec,
                pl.BlockSpec((seq, NSA_DH), lambda b, g, i: (b, g)),
                pl.BlockSpec((seq, NSA_DH), lambda b, g, i: (b, NSA_GROUPS + g))]
    args = [nq, kv, kv]
    if sel is None:
        body = _win_attn_kernel
        name = "window_attention"
    else:
        body = functools.partial(_sel_attn_kernel, nb=nb)
        name = "selected_attention"
        in_specs.append(pl.BlockSpec((None, None, ATT_TQ, nb), lambda b, g, i: (b, g, i, 0)))
        args.append(sel)
    return pl.pallas_call(
        body,
        grid=(bsz, NSA_GROUPS, nt),
        in_specs=in_specs,
        out_specs=q_spec,
        out_shape=jax.ShapeDtypeStruct((n, NSA_Q), F32),
        scratch_shapes=[pltpu.VMEM((NSA_HPG, ATT_TQ, 1), F32),
                        pltpu.VMEM((NSA_HPG, ATT_TQ, 1), F32),
                        pltpu.VMEM((NSA_HPG, ATT_TQ, NSA_DH), F32)],
        compiler_params=_params("parallel", "parallel", "parallel"),
        name=name,
    )(*args)


def _merge_kernel(x_ref, yr_ref, oc_ref, os_ref, ow_ref, ngl_ref, gab_ref, wr_ref, wn_ref, wo_ref, o_ref):
    tm = x_ref.shape[0]
    gates = jax.nn.sigmoid(ngl_ref[...])
    parts = []
    for h in range(NSA_HEADS):
        cols = slice(h * NSA_DH, (h + 1) * NSA_DH)

        def gate(br):
            return jnp.broadcast_to(gates[:, 3 * h + br:3 * h + br + 1], (tm, NSA_DH))

        parts.append(gate(0) * oc_ref[:, cols] + gate(1) * os_ref[:, cols] + gate(2) * ow_ref[:, cols])
    o_nsa = jnp.concatenate(parts, axis=1)
    y_ret = _dot(yr_ref[...], wr_ref[...])
    y_nsa = _dot(o_nsa, wn_ref[...])
    y = jax.nn.sigmoid(gab_ref[:, :D_MODEL]) * y_ret + jax.nn.sigmoid(gab_ref[:, D_MODEL:]) * y_nsa
    o_ref[...] = x_ref[...] + _dot(y, wo_ref[...])


def _merge(x2d, y_ret, o_cmp, o_sel, o_win, ngl, gab, w_ret_o, w_nsa_o, w_out):
    n = x2d.shape[0]
    tm = ROW_TM
    row = lambda i: (i, 0)
    const = lambda i: (0, 0)
    wide = pl.BlockSpec((tm, D_MODEL), row)
    wspec = pl.BlockSpec((D_MODEL, D_MODEL), const)
    return pl.pallas_call(
        _merge_kernel,
        grid=(n // tm,),
        in_specs=[wide, wide, wide, wide, wide,
                  pl.BlockSpec((tm, LANES), row),
                  pl.BlockSpec((tm, 2 * D_MODEL), row),
                  wspec, wspec, wspec],
        out_specs=wide,
        out_shape=jax.ShapeDtypeStruct((n, D_MODEL), F32),
        compiler_params=_params("parallel"),
        name="merge",
    )(x2d, y_ret, o_cmp, o_sel, o_win, ngl, gab, w_ret_o, w_nsa_o, w_out)


def _mem_kv_kernel(m_ref, g_ref, w_ref, o_ref):
    o_ref[...] = _dot(_rms(m_ref[...], g_ref[...]), w_ref[...]).astype(o_ref.dtype)


def _mem_kv(mem2d, g, w_xkv, bsz):
    nm = mem2d.shape[0] // bsz
    return pl.pallas_call(
        _mem_kv_kernel,
        grid=(bsz,),
        in_specs=[pl.BlockSpec((nm, D_MODEL), lambda b: (b, 0)),
                  pl.BlockSpec((1, D_MODEL), lambda b: (0, 0)),
                  pl.BlockSpec((D_MODEL, 2 * D_MODEL), lambda b: (0, 0))],
        out_specs=pl.BlockSpec((nm, 2 * D_MODEL), lambda b: (b, 0)),
        out_shape=jax.ShapeDtypeStruct((mem2d.shape[0], 2 * D_MODEL), MXU_DTYPE),
        compiler_params=_params("parallel"),
        name="mem_kv",
    )(mem2d, g, w_xkv)


def _cross_kernel(x_ref, kv_ref, gx_ref, wq_ref, wo_ref, gf_ref, wr_ref, br_ref, x2_ref, hf_ref, lg_ref):
    x = x_ref[...]
    q = _dot(_rms(x, gx_ref[...]), wq_ref[...])
    heads = []
    for h in range(X_HEADS):
        k = kv_ref[:, h * X_DH:(h + 1) * X_DH]
        v = kv_ref[:, D_MODEL + h * X_DH:D_MODEL + (h + 1) * X_DH]
        s = _dot_nt(q[:, h * X_DH:(h + 1) * X_DH], k) * (X_DH ** -0.5)
        e = jnp.exp(s - jnp.max(s, axis=-1, keepdims=True))
        p = e / jnp.sum(e, axis=-1, keepdims=True)
        heads.append(_dot(p, v))
    x2 = x + _dot(jnp.concatenate(heads, axis=1), wo_ref[...])
    x2_ref[...] = x2
    hf = _rms(x2, gf_ref[...])
    hf_ref[...] = hf.astype(hf_ref.dtype)
    h_hi, h_mid, _ = _split3(hf)
    w_hi = wr_ref[0]
    w_mid = wr_ref[1]
    lg_ref[...] = (jnp.dot(h_hi, w_hi, preferred_element_type=F32)
                   + (jnp.dot(h_mid, w_hi, preferred_element_type=F32)
                      + jnp.dot(h_hi, w_mid, preferred_element_type=F32))) + br_ref[...]


def _cross(x1, kv, gx, w_xq, w_xo, gf, w_router, b_router, bsz, seq):
    n = x1.shape[0]
    tm = ROW_TM
    nt = seq // tm
    nm = kv.shape[0] // bsz
    row = lambda i: (i, 0)
    const = lambda i: (0, 0)
    vec = pl.BlockSpec((1, D_MODEL), const)
    wspec = pl.BlockSpec((D_MODEL, D_MODEL), const)
    return pl.pallas_call(
        _cross_kernel,
        grid=(n // tm,),
        in_specs=[pl.BlockSpec((tm, D_MODEL), row),
                  pl.BlockSpec((nm, 2 * D_MODEL), lambda i: (i // nt, 0)),
                  vec, wspec, wspec, vec,
                  pl.BlockSpec((2, D_MODEL, LANES), lambda i: (0, 0, 0)),
                  pl.BlockSpec((1, LANES), const)],
        out_specs=[pl.BlockSpec((tm, D_MODEL), row),
                   pl.BlockSpec((tm, D_MODEL), row),
                   pl.BlockSpec((tm, LANES), row)],
        out_shape=[jax.ShapeDtypeStruct((n, D_MODEL), F32),
                   jax.ShapeDtypeStruct((n, D_MODEL), MXU_DTYPE),
                   jax.ShapeDtypeStruct((n, LANES), F32)],
        compiler_params=_params("parallel"),
        name="cross_attention",
    )(x1, kv, gx, w_xq, w_xo, gf, w_router, b_router)


def _expert_kernel(blk_e_ref, xb_ref, wt_ref, w1_ref, w3_ref, w2_ref, o_ref):
    del blk_e_ref
    xb = xb_ref[...]
    a = _dot(xb, w1_ref[...])
    hmid = a * jax.nn.sigmoid(a) * _dot(xb, w3_ref[...])
    o_ref[...] = _dot(hmid, w2_ref[...]) * wt_ref[...]


def _experts(blk_e, xb, wts, w1, w3, w2):
    cap = xb.shape[0]
    nblk = cap // MOE_BLOCK
    row = lambda i, e: (i, 0)
    grid_spec = pltpu.PrefetchScalarGridSpec(
        num_scalar_prefetch=1,
        grid=(nblk,),
        in_specs=[pl.BlockSpec((MOE_BLOCK, D_MODEL), row),
                  pl.BlockSpec((MOE_BLOCK, 1), row),
                  pl.BlockSpec((None, D_MODEL, D_EXPERT), lambda i, e: (e[i], 0, 0)),
                  pl.BlockSpec((None, D_MODEL, D_EXPERT), lambda i, e: (e[i], 0, 0)),
                  pl.BlockSpec((None, D_EXPERT, D_MODEL), lambda i, e: (e[i], 0, 0))],
        out_specs=pl.BlockSpec((MOE_BLOCK, D_MODEL), row),
    )
    return pl.pallas_call(
        _expert_kernel,
        grid_spec=grid_spec,
        out_shape=jax.ShapeDtypeStruct((cap, D_MODEL), F32),
        compiler_params=_params("arbitrary"),
        name="experts",
    )(blk_e, xb, wts, w1, w3, w2)


def _final_kernel(x_ref, y0_ref, y1_ref, g_ref, o_ref):
    o_ref[...] = _rms(x_ref[...] + (y0_ref[...] + y1_ref[...]), g_ref[...])


def _final(x2, y0, y1, g):
    n = x2.shape[0]
    tm = ROW_TM
    row = lambda i: (i, 0)
    wide = pl.BlockSpec((tm, D_MODEL), row)
    return pl.pallas_call(
        _final_kernel,
        grid=(n // tm,),
        in_specs=[wide, wide, wide, pl.BlockSpec((1, D_MODEL), lambda i: (0, 0))],
        out_specs=wide,
        out_shape=jax.ShapeDtypeStruct((n, D_MODEL), F32),
        compiler_params=_params("parallel"),
        name="final_norm",
    )(x2, y0, y1, g)


def _route(logits, b_tokens):
    n_tok = b_tokens
    lg = logits[:, :N_EGROUPS]
    pg = jax.nn.softmax(lg, axis=-1)
    grp = jnp.argmax(lg, axis=-1)
    g_gate = jnp.take_along_axis(pg, grp[:, None], axis=-1)[:, 0]
    le = logits[:, N_EGROUPS:N_EGROUPS + N_EXPERTS].reshape(n_tok, N_EGROUPS, EXP_PER_GROUP)
    le_g = jnp.take_along_axis(le, grp[:, None, None], axis=1)[:, 0]
    pe = jax.nn.softmax(le_g, axis=-1)
    top_p, top_i = lax.top_k(pe, EXP_TOPK)
    wts = g_gate[:, None] * top_p / jnp.sum(top_p, axis=-1, keepdims=True)
    eid = (grp[:, None] * EXP_PER_GROUP + top_i).reshape(-1).astype(jnp.int32)
    wflat = wts.reshape(-1)
    n_asg = n_tok * EXP_TOPK
    tok = (jnp.arange(n_asg) // EXP_TOPK).astype(jnp.int32)
    order = jnp.argsort(eid)
    se = eid[order]
    counts = jnp.bincount(eid, length=N_EXPERTS)
    padded = (counts + MOE_BLOCK - 1) // MOE_BLOCK * MOE_BLOCK
    starts = jnp.cumsum(counts) - counts
    pends = jnp.cumsum(padded)
    pstarts = pends - padded
    dest = (pstarts[se] + (jnp.arange(n_asg) - starts[se])).astype(jnp.int32)
    cap = ((n_asg + MOE_BLOCK - 1) // MOE_BLOCK + N_EXPERTS) * MOE_BLOCK
    nblk = cap // MOE_BLOCK
    buf_tok = jnp.full((cap,), n_tok, jnp.int32).at[dest].set(tok[order])
    buf_w = jnp.zeros((cap,), F32).at[dest].set(wflat[order])
    blk_e = jnp.minimum(jnp.searchsorted(pends, jnp.arange(nblk) * MOE_BLOCK, side='right'),
                        N_EXPERTS - 1).astype(jnp.int32)
    pos = jnp.zeros((n_asg,), jnp.int32).at[order].set(dest)
    return buf_tok, buf_w, blk_e, pos


def kernel(x, mem, norm_mix_g, w_in, w_ret_o, w_nsa_o, w_out, cmp_pe_k, cmp_w1_k, cmp_w2_k, cmp_pe_v,
           cmp_w1_v, cmp_w2_v, norm_x_g, norm_mem_g, w_xq, w_xkv, w_xo, norm_ffn_g, w_grp, b_grp, w_rt,
           b_rt, w_e1, w_e3, w_e2, norm_f_g):
    bsz, seq, _ = x.shape
    n = bsz * seq
    assert seq % PROJ_TM == 0 and seq % (2 * ATT_TQ) == 0 and w_in.shape[0] == 1
    cast = lambda a: a.astype(MXU_DTYPE)
    xc = x.reshape(n, D_MODEL)
    l = 0

    cos, sin = _rope_tables(seq)
    qk, rv, rg, nq, ckv, skv, wkv, gab, ngl = _proj(
        xc, norm_mix_g[l][None, :], _permute_w_in(w_in[l]), cos, sin, seq)
    y_ret = _retention(qk, rv, rg, bsz, seq)
    cmp_kv = _compress(ckv, jnp.stack([cmp_pe_k[l], cmp_pe_v[l]]),
                       cast(jnp.stack([cmp_w1_k[l], cmp_w1_v[l]])),
                       cast(jnp.stack([cmp_w2_k[l], cmp_w2_v[l]])), bsz, seq)
    o_cmp, sel = _cmp_attention(nq, cmp_kv, bsz, seq)
    o_sel = _flash_attention(nq, skv, sel, bsz, seq)
    o_win = _flash_attention(nq, wkv, None, bsz, seq)
    x1 = _merge(xc, y_ret, o_cmp, o_sel, o_win, ngl, gab, cast(w_ret_o[l]), cast(w_nsa_o[l]), cast(w_out[l]))

    kv = _mem_kv(mem.reshape(-1, D_MODEL), norm_mem_g[l][None, :], cast(w_xkv[l]), bsz)
    w_router = jnp.pad(jnp.concatenate([w_grp[l], w_rt[l]], axis=1),
                       ((0, 0), (0, LANES - N_EGROUPS - N_EXPERTS)))
    wr_hi = cast(w_router)
    wr_mid = cast(w_router - wr_hi.astype(F32))
    b_router = jnp.pad(jnp.concatenate([b_grp[l], b_rt[l]]), (0, LANES - N_EGROUPS - N_EXPERTS))[None, :]
    x2, hf, logits = _cross(x1, kv, norm_x_g[l][None, :], cast(w_xq[l]), cast(w_xo[l]),
                            norm_ffn_g[l][None, :], jnp.stack([wr_hi, wr_mid]), b_router, bsz, seq)

    buf_tok, buf_w, blk_e, pos = _route(logits, n)
    hp = jnp.concatenate([hf, jnp.zeros((1, D_MODEL), hf.dtype)], axis=0)
    ys = _experts(blk_e, hp[buf_tok], buf_w[:, None], cast(w_e1[l]), cast(w_e3[l]), cast(w_e2[l]))
    pos = pos.reshape(n, EXP_TOPK)
    out = _final(x2, ys[pos[:, 0]], ys[pos[:, 1]], norm_f_g[None, :])
    return out.reshape(bsz, seq, D_MODEL)
```

```python
import functools

import numpy as np
import jax
import jax.numpy as jnp
from jax import lax
from jax.experimental import pallas as pl
from jax.experimental.pallas import tpu as pltpu

MXU_DTYPE = jnp.bfloat16
F32 = jnp.float32

D_MODEL = 1024
N_MEM = 256
EPS = 1e-6
NEG_INF = -1e30
SEL_FORCE = 1e4

R_HEADS = 4
R_DK = 128
R_DV = 256
R_CHUNK = 128
ROPE_BASE = 10000.0

NSA_HEADS = 8
NSA_GROUPS = 2
NSA_HPG = NSA_HEADS // NSA_GROUPS
NSA_DH = 128
CMP_LEN = 32
CMP_STRIDE = 16
SEL_LEN = 64
SEL_TOPK = 16
WINDOW = 512

X_HEADS = 4
X_DH = D_MODEL // X_HEADS

N_EGROUPS = 4
EXP_PER_GROUP = 8
N_EXPERTS = N_EGROUPS * EXP_PER_GROUP
EXP_TOPK = 2
D_EXPERT = 512
MOE_BLOCK = 256

RET_QK = R_HEADS * R_DK
RET_V = R_HEADS * R_DV
NSA_Q = NSA_HEADS * NSA_DH
NSA_KV = NSA_GROUPS * NSA_DH
SPLITS = (RET_QK, RET_QK, RET_V, RET_V, NSA_Q, NSA_KV, NSA_KV, NSA_KV, NSA_KV, NSA_KV, NSA_KV,
          3 * NSA_HEADS, D_MODEL, D_MODEL)

LANES = 128
SUBLANES = 8
VMEM_LIMIT = 56 * 1024 * 1024

PROJ_TM = 512
RET_ROWS = 512
ATT_TQ = 256
ATT_TK = 256
ROW_TM = 512


def _params(*sem):
    return pltpu.CompilerParams(dimension_semantics=sem, vmem_limit_bytes=VMEM_LIMIT)


def _dot(a, b):
    return jnp.dot(a.astype(MXU_DTYPE), b.astype(MXU_DTYPE), preferred_element_type=F32)


def _dot_nt(a, b):
    return lax.dot_general(a.astype(MXU_DTYPE), b.astype(MXU_DTYPE), (((1,), (1,)), ((), ())),
                           preferred_element_type=F32)


def _dot_tn(a, b):
    return lax.dot_general(a.astype(MXU_DTYPE), b.astype(MXU_DTYPE), (((0,), (0,)), ((), ())),
                           preferred_element_type=F32)


def _split3(p):
    hi = p.astype(MXU_DTYPE)
    r1 = p - hi.astype(F32)
    mid = r1.astype(MXU_DTYPE)
    lo = (r1 - mid.astype(F32)).astype(MXU_DTYPE)
    return hi, mid, lo


def _rms(x, g):
    return x * lax.rsqrt(jnp.mean(x * x, axis=-1, keepdims=True) + EPS) * g


_C_RQK = 0
_C_RV = _C_RQK + 2 * RET_QK
_C_RG = _C_RV + RET_V
_C_NQ = _C_RG + RET_V
_C_CKV = _C_NQ + NSA_Q
_C_SKV = _C_CKV + 2 * NSA_KV
_C_WKV = _C_SKV + 2 * NSA_KV
_C_GAB = _C_WKV + 2 * NSA_KV
_C_NGL = _C_GAB + 2 * D_MODEL
_C_END = _C_NGL + LANES


def _proj_kernel(x_ref, g_ref, w_ref, cos_ref, sin_ref,
                 qk_ref, rv_ref, rg_ref, nq_ref, ckv_ref, skv_ref, wkv_ref, gab_ref, ngl_ref):
    hb = _rms(x_ref[...], g_ref[...]).astype(MXU_DTYPE)

    def mm(off, width):
        return jnp.dot(hb, w_ref[:, off:off + width], preferred_element_type=F32)

    cos = cos_ref[...]
    sin = sin_ref[...]
    qk = mm(_C_RQK, 2 * RET_QK)
    for i in range(2 * R_HEADS):
        t = qk[:, i * R_DK:(i + 1) * R_DK]
        r = t * cos + pltpu.roll(t, R_DK // 2, axis=1) * sin
        if i >= R_HEADS:
            r = r * (R_DK ** -0.5)
        qk_ref[:, i * R_DK:(i + 1) * R_DK] = r.astype(qk_ref.dtype)
    rv_ref[...] = mm(_C_RV, RET_V).astype(rv_ref.dtype)
    rg_ref[...] = mm(_C_RG, RET_V)
    nq_ref[...] = mm(_C_NQ, NSA_Q).astype(nq_ref.dtype)
    ckv_ref[...] = mm(_C_CKV, 2 * NSA_KV)
    skv_ref[...] = mm(_C_SKV, 2 * NSA_KV).astype(skv_ref.dtype)
    wkv_ref[...] = mm(_C_WKV, 2 * NSA_KV).astype(wkv_ref.dtype)
    gab_ref[:, :D_MODEL] = mm(_C_GAB, D_MODEL)
    gab_ref[:, D_MODEL:] = mm(_C_GAB + D_MODEL, D_MODEL)
    ngl_ref[...] = mm(_C_NGL, LANES)


def _permute_w_in(w_in):
    offs = np.cumsum((0,) + SPLITS)
    seg = [w_in[:, offs[i]:offs[i + 1]] for i in range(len(SPLITS))]
    rq, rk, rv, rg, nq, ck, cv, sk, sv, wk, wv, ngl, ga, gb = seg
    half = np.concatenate([np.arange(0, R_DK, 2), np.arange(1, R_DK, 2)])
    perm = (np.arange(R_HEADS)[:, None] * R_DK + half[None, :]).reshape(-1)
    ngl = jnp.pad(ngl, ((0, 0), (0, LANES - ngl.shape[1])))
    return jnp.concatenate([rq[:, perm], rk[:, perm], rv, rg, nq, ck, cv, sk, sv, wk, wv, ga, gb, ngl],
                           axis=1).astype(MXU_DTYPE)


def _rope_tables(seq):
    pos = jnp.arange(seq, dtype=F32)
    inv_freq = ROPE_BASE ** (-jnp.arange(0, R_DK, 2, dtype=F32) / R_DK)
    ang = pos[:, None] * inv_freq[None, :]
    cos = jnp.cos(ang)
    sin = jnp.sin(ang)
    return jnp.concatenate([cos, cos], axis=1), jnp.concatenate([-sin, sin], axis=1)


def _proj(x2d, g, w, cos, sin, seq):
    n = x2d.shape[0]
    tm = PROJ_TM
    nt = seq // tm
    row = lambda i: (i, 0)
    const = lambda i: (0, 0)
    widths = (2 * RET_QK, RET_V, RET_V, NSA_Q, 2 * NSA_KV, 2 * NSA_KV, 2 * NSA_KV, 2 * D_MODEL, LANES)
    dtypes = (MXU_DTYPE, MXU_DTYPE, F32, MXU_DTYPE, F32, MXU_DTYPE, MXU_DTYPE, F32, F32)
    return pl.pallas_call(
        _proj_kernel,
        grid=(n // tm,),
        in_specs=[pl.BlockSpec((tm, D_MODEL), row),
                  pl.BlockSpec((1, D_MODEL), const),
                  pl.BlockSpec((D_MODEL, _C_END), const, pipeline_mode=pl.Buffered(1)),
                  pl.BlockSpec((tm, R_DK), lambda i: (i % nt, 0)),
                  pl.BlockSpec((tm, R_DK), lambda i: (i % nt, 0))],
        out_specs=[pl.BlockSpec((tm, wd), row) for wd in widths],
        out_shape=[jax.ShapeDtypeStruct((n, wd), dt) for wd, dt in zip(widths, dtypes)],
        compiler_params=_params("parallel"),
        name="proj",
    )(x2d, g, w, cos, sin)


def _ret_kernel(qk_ref, v_ref, g_ref, decay_ref, zeta_ref, xi_ref, cd_ref, y_ref, state_ref):
    @pl.when(pl.program_id(1) == 0)
    def _():
        state_ref[...] = jnp.zeros_like(state_ref)

    for c in range(RET_ROWS // R_CHUNK):
        rows = slice(c * R_CHUNK, (c + 1) * R_CHUNK)
        for h in range(R_HEADS):
            q = qk_ref[rows, h * R_DK:(h + 1) * R_DK]
            k = qk_ref[rows, RET_QK + h * R_DK:RET_QK + (h + 1) * R_DK]
            v = v_ref[rows, h * R_DV:(h + 1) * R_DV]
            st = state_ref[h]
            scores = _dot_nt(q, k) * decay_ref[h]
            o = _dot(scores, v) + _dot(q, st) * xi_ref[h]
            o = o * lax.rsqrt(jnp.mean(o * o, axis=-1, keepdims=True) + EPS)
            g = g_ref[rows, h * R_DV:(h + 1) * R_DV]
            y_ref[rows, h * R_DV:(h + 1) * R_DV] = (g * jax.nn.sigmoid(g) * o).astype(y_ref.dtype)
            kz = k.astype(F32) * zeta_ref[h]
            state_ref[h] = st * cd_ref[h] + _dot_tn(kz, v)


def _retention(qk, rv, rg, bsz, seq):
    n = qk.shape[0]
    nt = seq // RET_ROWS
    log_g = jnp.log1p(-jnp.exp2(-5.0 - jnp.arange(R_HEADS, dtype=F32)))
    idx = jnp.arange(R_CHUNK, dtype=F32)
    diff = idx[:, None] - idx[None, :]
    decay = jnp.where(diff >= 0, jnp.exp(log_g[:, None, None] * jnp.maximum(diff, 0.0)), 0.0)
    zeta = jnp.exp(log_g[:, None] * (R_CHUNK - 1.0 - idx)[None, :])[:, :, None]
    xi = jnp.exp(log_g[:, None] * (idx + 1.0)[None, :])[:, :, None]
    cd = jnp.exp(log_g * R_CHUNK)[:, None, None]
    row = lambda b, i: (b * nt + i, 0)
    const3 = lambda b, i: (0, 0, 0)
    return pl.pallas_call(
        _ret_kernel,
        grid=(bsz, nt),
        in_specs=[pl.BlockSpec((RET_ROWS, 2 * RET_QK), row),
                  pl.BlockSpec((RET_ROWS, RET_V), row),
                  pl.BlockSpec((RET_ROWS, RET_V), row),
                  pl.BlockSpec((R_HEADS, R_CHUNK, R_CHUNK), const3),
                  pl.BlockSpec((R_HEADS, R_CHUNK, 1), const3),
                  pl.BlockSpec((R_HEADS, R_CHUNK, 1), const3),
                  pl.BlockSpec((R_HEADS, 1, 1), const3)],
        out_specs=pl.BlockSpec((RET_ROWS, RET_V), row),
        out_shape=jax.ShapeDtypeStruct((n, RET_V), MXU_DTYPE),
        scratch_shapes=[pltpu.VMEM((R_HEADS, R_DK, R_DV), F32)],
        compiler_params=_params("parallel", "arbitrary"),
        name="retention",
    )(qk, rv, rg, decay, zeta, xi, cd)


def _compress_kernel(x_ref, pe_ref, w1_ref, w2_ref, w2t_ref, o_ref, ot_ref, buf_ref, *, seq):
    ncp = seq // CMP_STRIDE
    buf_ref[0:seq, :] = x_ref[...]
    buf_ref[seq:seq + LANES, :] = jnp.zeros((LANES, NSA_DH), F32)
    acc = jnp.zeros((ncp, NSA_DH), F32)
    for l in range(CMP_LEN):
        xl = buf_ref[pl.ds(l, ncp, stride=CMP_STRIDE), :] + pe_ref[l:l + 1, :]
        acc = acc + _dot(xl, w1_ref[l])
    hid = jax.nn.gelu(acc)
    o_ref[...] = _dot(hid, w2_ref[...]).astype(o_ref.dtype)
    ot_ref[...] = _dot_nt(w2t_ref[...], hid).astype(ot_ref.dtype)


def _compress(ckv, pe, w1, w2, w2t, bsz, seq):
    ncp = seq // CMP_STRIDE
    nj = 2 * NSA_GROUPS
    wsel = lambda b, j: (j // NSA_GROUPS, 0, 0)
    return pl.pallas_call(
        functools.partial(_compress_kernel, seq=seq),
        grid=(bsz, nj),
        in_specs=[pl.BlockSpec((seq, NSA_DH), lambda b, j: (b, j)),
                  pl.BlockSpec((None, CMP_LEN, NSA_DH), wsel),
                  pl.BlockSpec((None, CMP_LEN, NSA_DH, NSA_DH), lambda b, j: (j // NSA_GROUPS, 0, 0, 0)),
                  pl.BlockSpec((None, NSA_DH, NSA_DH), wsel),
                  pl.BlockSpec((None, NSA_DH, NSA_DH), wsel)],
        out_specs=[pl.BlockSpec((None, None, ncp, NSA_DH), lambda b, j: (b, j, 0, 0)),
                   pl.BlockSpec((None, None, NSA_DH, ncp), lambda b, j: (b, j, 0, 0))],
        out_shape=[jax.ShapeDtypeStruct((bsz, nj, ncp, NSA_DH), MXU_DTYPE),
                   jax.ShapeDtypeStruct((bsz, nj, NSA_DH, ncp), MXU_DTYPE)],
        scratch_shapes=[pltpu.VMEM((seq + LANES, NSA_DH), F32)],
        compiler_params=_params("parallel", "parallel"),
        name="compress",
    )(ckv, pe, w1, w2, w2t)


def _cmp_attn_kernel(qt_ref, k_ref, vt_ref, ov_ref, o_ref, selt_ref, *, ncp, nb):
    tq = ATT_TQ
    t0 = pl.program_id(2) * tq
    t = t0 + lax.broadcasted_iota(jnp.int32, (ncp, tq), 1)
    n = lax.broadcasted_iota(jnp.int32, (ncp, tq), 0)
    valid = (n * CMP_STRIDE + (CMP_LEN - 1) <= t) & (n < ncp - 1)
    any_valid = (t0 + lax.broadcasted_iota(jnp.int32, (1, tq), 1)) >= CMP_LEN - 1
    valid = jnp.concatenate([valid] * NSA_HPG, axis=1)
    any_valid = jnp.concatenate([any_valid] * NSA_HPG, axis=1)
    s = _dot(k_ref[...], qt_ref[...]) * (NSA_DH ** -0.5)
    s = jnp.where(valid, s, NEG_INF)
    e = jnp.exp(s - jnp.max(s, axis=0, keepdims=True))
    p = e * jnp.where(any_valid, 1.0 / jnp.sum(e, axis=0, keepdims=True), 0.0)
    ot = _dot(vt_ref[...], p)
    for h in range(NSA_HPG):
        o_ref[:, h * NSA_DH:(h + 1) * NSA_DH] = ot[:, h * tq:(h + 1) * tq].T
    psum = sum(p[:, h * tq:(h + 1) * tq] for h in range(NSA_HPG))

    imp = sum(_dot(ov_ref[...], part) for part in _split3(psum))
    j = lax.broadcasted_iota(jnp.int32, (nb, tq), 0)
    tb = (t0 + lax.broadcasted_iota(jnp.int32, (nb, tq), 1)) // SEL_LEN
    forced = (j == 0) | (j == tb) | (j == tb - 1)
    imp = jnp.where(j > tb, -SEL_FORCE, jnp.where(forced, SEL_FORCE, imp))

    sub = SUBLANES
    grp = [imp[r * sub:(r + 1) * sub] for r in range(nb // sub)]
    cnt = [jnp.zeros((sub, tq), F32) for _ in grp]
    for i in range(nb):
        row = jnp.broadcast_to(imp[i:i + 1, :], (sub, tq))
        for r in range(nb // sub):
            if r * sub > i:
                beats = jnp.where(row >= grp[r], 1.0, 0.0)
            elif r * sub + sub - 1 < i:
                beats = jnp.where(row > grp[r], 1.0, 0.0)
            else:
                jr = r * sub + lax.broadcasted_iota(jnp.int32, (sub, tq), 0)
                beats = jnp.where(jr > i, jnp.where(row >= grp[r], 1.0, 0.0), jnp.where(row > grp[r], 1.0, 0.0))
            cnt[r] = cnt[r] + beats
    k_sel = min(SEL_TOPK, nb)
    for r in range(nb // sub):
        selt_ref[r * sub:(r + 1) * sub, :] = jnp.where(cnt[r] < k_sel, 1.0, 0.0)


def _cmp_attention(qt, cmp_k, cmp_vt, bsz, seq):
    n = bsz * seq
    ncp = seq // CMP_STRIDE
    nb = seq // SEL_LEN
    nt = seq // ATT_TQ
    gw = NSA_HPG * NSA_DH
    cstart = np.arange(ncp) * CMP_STRIDE
    jstart = np.arange(nb) * SEL_LEN
    ov = ((cstart[None, :] < jstart[:, None] + SEL_LEN) & (cstart[None, :] + CMP_LEN > jstart[:, None])
          & (np.arange(ncp)[None, :] < ncp - 1))
    ov = jnp.asarray(ov, MXU_DTYPE)
    return pl.pallas_call(
        functools.partial(_cmp_attn_kernel, ncp=ncp, nb=nb),
        grid=(bsz, NSA_GROUPS, nt),
        in_specs=[pl.BlockSpec((None, None, None, NSA_DH, NSA_HPG * ATT_TQ), lambda b, g, i: (b, g, i, 0, 0)),
                  pl.BlockSpec((None, None, ncp, NSA_DH), lambda b, g, i: (b, g, 0, 0)),
                  pl.BlockSpec((None, None, NSA_DH, ncp), lambda b, g, i: (b, NSA_GROUPS + g, 0, 0)),
                  pl.BlockSpec((nb, ncp), lambda b, g, i: (0, 0))],
        out_specs=[pl.BlockSpec((ATT_TQ, gw), lambda b, g, i: (b * nt + i, g)),
                   pl.BlockSpec((None, None, nb, ATT_TQ), lambda b, g, i: (b, g, 0, i))],
        out_shape=[jax.ShapeDtypeStruct((n, NSA_Q), F32),
                   jax.ShapeDtypeStruct((bsz, NSA_GROUPS, nb, seq), F32)],
        compiler_params=_params("parallel", "parallel", "parallel"),
        name="cmp_attention",
    )(qt, cmp_k, cmp_vt, ov)


_EXP2_SCALE = (NSA_DH ** -0.5) * float(np.log2(np.e))
_MASK_ALL = 1 << 28


def _flash_loop(qt_ref, k_ref, vt_ref, acc_ref, s_refs, n_tiles, tile_of, valid_of):
    width = NSA_HPG * ATT_TQ

    def scores(kt):
        k = k_ref[pl.ds(pl.multiple_of(kt * ATT_TK, ATT_TK), ATT_TK), :]
        return _dot(k, qt_ref[...])

    def half_step(step, s_cur, s_nxt, m, l):
        kt = tile_of(jnp.minimum(step, n_tiles - 1))
        s_nxt[...] = scores(tile_of(jnp.minimum(step + 1, n_tiles - 1)))
        valid = valid_of(kt, kt * ATT_TK + jnp.where(step < n_tiles, 0, _MASK_ALL))
        valid = jnp.concatenate([valid] * NSA_HPG, axis=1)
        s = jnp.where(valid, s_cur[...] * _EXP2_SCALE, NEG_INF)
        m_new = jnp.maximum(m, jnp.max(s, axis=0, keepdims=True))
        alpha = jnp.exp2(m - m_new)
        p = jnp.exp2(s - m_new)
        l = alpha * l + jnp.sum(p, axis=0, keepdims=True)
        acc_ref[...] = alpha * acc_ref[...] + _dot(vt_ref[kt], p)
        return m_new, l

    def body(i, carry):
        m, l = half_step(2 * i, s_refs[0], s_refs[1], *carry)
        return half_step(2 * i + 1, s_refs[1], s_refs[0], m, l)

    acc_ref[...] = jnp.zeros_like(acc_ref)
    s_refs[0][...] = scores(tile_of(0))
    init = (jnp.full((1, width), NEG_INF, F32), jnp.zeros((1, width), F32))
    _, l = lax.fori_loop(0, (n_tiles + 1) // 2, body, init)
    return l


def _flash_finish(o_ref, acc_ref, l):
    inv = 1.0 / l
    for h in range(NSA_HPG):
        cols = slice(h * ATT_TQ, (h + 1) * ATT_TQ)
        o_ref[:, h * NSA_DH:(h + 1) * NSA_DH] = (acc_ref[:, cols] * inv[:, cols]).T


def _sel_attn_kernel(qt_ref, k_ref, vt_ref, selt_ref, o_ref, acc_ref, s0_ref, s1_ref):
    qi = pl.program_id(2)
    tpos = qi * ATT_TQ + lax.broadcasted_iota(jnp.int32, (ATT_TK, ATT_TQ), 1)
    row = lax.broadcasted_iota(jnp.int32, (ATT_TK, ATT_TQ), 0)
    blocks_per_tile = ATT_TK // SEL_LEN

    def valid_of(kt, key0):
        picked = jnp.concatenate(
            [jnp.broadcast_to(selt_ref[pl.ds(kt * blocks_per_tile + jb, 1), :], (SEL_LEN, ATT_TQ))
             for jb in range(blocks_per_tile)], axis=0)
        return (picked > 0.5) & (row + key0 <= tpos)

    l = _flash_loop(qt_ref, k_ref, vt_ref, acc_ref, (s0_ref, s1_ref), (qi + 1) * (ATT_TQ // ATT_TK),
                    lambda step: step, valid_of)
    _flash_finish(o_ref, acc_ref, l)


def _win_attn_kernel(qt_ref, k_ref, vt_ref, o_ref, acc_ref, s0_ref, s1_ref):
    qi = pl.program_id(2)
    tpos = qi * ATT_TQ + lax.broadcasted_iota(jnp.int32, (ATT_TK, ATT_TQ), 1)
    row = lax.broadcasted_iota(jnp.int32, (ATT_TK, ATT_TQ), 0)
    last = (qi + 1) * (ATT_TQ // ATT_TK) - 1
    n_tiles = jnp.minimum(last + 1, (ATT_TQ + WINDOW) // ATT_TK)

    def valid_of(kt, key0):
        kpos = row + key0
        return (kpos <= tpos) & (kpos > tpos - WINDOW)

    l = _flash_loop(qt_ref, k_ref, vt_ref, acc_ref, (s0_ref, s1_ref), n_tiles,
                    lambda step: last - step, valid_of)
    _flash_finish(o_ref, acc_ref, l)


def _flash_attention(qt, kv, selt, bsz, seq):
    n = bsz * seq
    nt = seq // ATT_TQ
    nkt = seq // ATT_TK
    nb = seq // SEL_LEN
    gw = NSA_HPG * NSA_DH
    width = NSA_HPG * ATT_TQ
    vt = kv[:, NSA_KV:].reshape(bsz, nkt, ATT_TK, NSA_GROUPS, NSA_DH).transpose(0, 3, 1, 4, 2)
    in_specs = [pl.BlockSpec((None, None, None, NSA_DH, width), lambda b, g, i: (b, g, i, 0, 0)),
                pl.BlockSpec((seq, NSA_DH), lambda b, g, i: (b, g)),
                pl.BlockSpec((None, None, nkt, NSA_DH, ATT_TK), lambda b, g, i: (b, g, 0, 0, 0))]
    args = [qt, kv, vt]
    if selt is None:
        body = _win_attn_kernel
        name = "window_attention"
    else:
        body = _sel_attn_kernel
        name = "selected_attention"
        in_specs.append(pl.BlockSpec((None, None, nb, ATT_TQ), lambda b, g, i: (b, g, 0, i)))
        args.append(selt)
    return pl.pallas_call(
        body,
        grid=(bsz, NSA_GROUPS, nt),
        in_specs=in_specs,
        out_specs=pl.BlockSpec((ATT_TQ, gw), lambda b, g, i: (b * nt + i, g)),
        out_shape=jax.ShapeDtypeStruct((n, NSA_Q), F32),
        scratch_shapes=[pltpu.VMEM((NSA_DH, width), F32),
                        pltpu.VMEM((ATT_TK, width), F32),
                        pltpu.VMEM((ATT_TK, width), F32)],
        compiler_params=_params("parallel", "parallel", "parallel"),
        name=name,
    )(*args)


def _merge_kernel(x_ref, yr_ref, oc_ref, os_ref, ow_ref, ngl_ref, gab_ref, wr_ref, wn_ref, wo_ref, o_ref):
    tm = x_ref.shape[0]
    gates = jax.nn.sigmoid(ngl_ref[...])
    parts = []
    for h in range(NSA_HEADS):
        cols = slice(h * NSA_DH, (h + 1) * NSA_DH)

        def gate(br):
            return jnp.broadcast_to(gates[:, 3 * h + br:3 * h + br + 1], (tm, NSA_DH))

        parts.append(gate(0) * oc_ref[:, cols] + gate(1) * os_ref[:, cols] + gate(2) * ow_ref[:, cols])
    o_nsa = jnp.concatenate(parts, axis=1)
    y_ret = _dot(yr_ref[...], wr_ref[...])
    y_nsa = _dot(o_nsa, wn_ref[...])
    y = jax.nn.sigmoid(gab_ref[:, :D_MODEL]) * y_ret + jax.nn.sigmoid(gab_ref[:, D_MODEL:]) * y_nsa
    o_ref[...] = x_ref[...] + _dot(y, wo_ref[...])


def _merge(x2d, y_ret, o_cmp, o_sel, o_win, ngl, gab, w_ret_o, w_nsa_o, w_out):
    n = x2d.shape[0]
    tm = ROW_TM
    row = lambda i: (i, 0)
    const = lambda i: (0, 0)
    wide = pl.BlockSpec((tm, D_MODEL), row)
    wspec = pl.BlockSpec((D_MODEL, D_MODEL), const)
    return pl.pallas_call(
        _merge_kernel,
        grid=(n // tm,),
        in_specs=[wide, wide, wide, wide, wide,
                  pl.BlockSpec((tm, LANES), row),
                  pl.BlockSpec((tm, 2 * D_MODEL), row),
                  wspec, wspec, wspec],
        out_specs=wide,
        out_shape=jax.ShapeDtypeStruct((n, D_MODEL), F32),
        compiler_params=_params("parallel"),
        name="merge",
    )(x2d, y_ret, o_cmp, o_sel, o_win, ngl, gab, w_ret_o, w_nsa_o, w_out)


def _mem_kv_kernel(m_ref, g_ref, w_ref, o_ref):
    o_ref[...] = _dot(_rms(m_ref[...], g_ref[...]), w_ref[...]).astype(o_ref.dtype)


def _mem_kv(mem2d, g, w_xkv, bsz):
    nm = mem2d.shape[0] // bsz
    return pl.pallas_call(
        _mem_kv_kernel,
        grid=(bsz,),
        in_specs=[pl.BlockSpec((nm, D_MODEL), lambda b: (b, 0)),
                  pl.BlockSpec((1, D_MODEL), lambda b: (0, 0)),
                  pl.BlockSpec((D_MODEL, 2 * D_MODEL), lambda b: (0, 0))],
        out_specs=pl.BlockSpec((nm, 2 * D_MODEL), lambda b: (b, 0)),
        out_shape=jax.ShapeDtypeStruct((mem2d.shape[0], 2 * D_MODEL), MXU_DTYPE),
        compiler_params=_params("parallel"),
        name="mem_kv",
    )(mem2d, g, w_xkv)


def _cross_kernel(x_ref, kv_ref, gx_ref, wq_ref, wo_ref, gf_ref, wr_ref, br_ref, x2_ref, hf_ref, lg_ref):
    x = x_ref[...]
    q = _dot(_rms(x, gx_ref[...]), wq_ref[...])
    heads = []
    for h in range(X_HEADS):
        k = kv_ref[:, h * X_DH:(h + 1) * X_DH]
        v = kv_ref[:, D_MODEL + h * X_DH:D_MODEL + (h + 1) * X_DH]
        s = _dot_nt(q[:, h * X_DH:(h + 1) * X_DH], k) * (X_DH ** -0.5)
        e = jnp.exp(s - jnp.max(s, axis=-1, keepdims=True))
        p = e / jnp.sum(e, axis=-1, keepdims=True)
        heads.append(_dot(p, v))
    x2 = x + _dot(jnp.concatenate(heads, axis=1), wo_ref[...])
    x2_ref[...] = x2
    hf = _rms(x2, gf_ref[...])
    hf_ref[...] = hf.astype(hf_ref.dtype)
    h_hi, h_mid, _ = _split3(hf)
    w_hi = wr_ref[0]
    w_mid = wr_ref[1]
    lg_ref[...] = (jnp.dot(h_hi, w_hi, preferred_element_type=F32)
                   + (jnp.dot(h_mid, w_hi, preferred_element_type=F32)
                      + jnp.dot(h_hi, w_mid, preferred_element_type=F32))) + br_ref[...]


def _cross(x1, kv, gx, w_xq, w_xo, gf, w_router, b_router, bsz, seq):
    n = x1.shape[0]
    tm = ROW_TM
    nt = seq // tm
    nm = kv.shape[0] // bsz
    row = lambda i: (i, 0)
    const = lambda i: (0, 0)
    vec = pl.BlockSpec((1, D_MODEL), const)
    wspec = pl.BlockSpec((D_MODEL, D_MODEL), const)
    return pl.pallas_call(
        _cross_kernel,
        grid=(n // tm,),
        in_specs=[pl.BlockSpec((tm, D_MODEL), row),
                  pl.BlockSpec((nm, 2 * D_MODEL), lambda i: (i // nt, 0)),
                  vec, wspec, wspec, vec,
                  pl.BlockSpec((2, D_MODEL, LANES), lambda i: (0, 0, 0)),
                  pl.BlockSpec((1, LANES), const)],
        out_specs=[pl.BlockSpec((tm, D_MODEL), row),
                   pl.BlockSpec((tm, D_MODEL), row),
                   pl.BlockSpec((tm, LANES), row)],
        out_shape=[jax.ShapeDtypeStruct((n, D_MODEL), F32),
                   jax.ShapeDtypeStruct((n, D_MODEL), MXU_DTYPE),
                   jax.ShapeDtypeStruct((n, LANES), F32)],
        compiler_params=_params("parallel"),
        name="cross_attention",
    )(x1, kv, gx, w_xq, w_xo, gf, w_router, b_router)


def _expert_kernel(blk_e_ref, xb_ref, wt_ref, w1_ref, w3_ref, w2_ref, o_ref):
    del blk_e_ref
    xb = xb_ref[...]
    a = _dot(xb, w1_ref[...])
    hmid = a * jax.nn.sigmoid(a) * _dot(xb, w3_ref[...])
    o_ref[...] = _dot(hmid, w2_ref[...]) * wt_ref[...]


def _experts(blk_e, xb, wts, w1, w3, w2):
    cap = xb.shape[0]
    nblk = cap // MOE_BLOCK
    row = lambda i, e: (i, 0)
    grid_spec = pltpu.PrefetchScalarGridSpec(
        num_scalar_prefetch=1,
        grid=(nblk,),
        in_specs=[pl.BlockSpec((MOE_BLOCK, D_MODEL), row),
                  pl.BlockSpec((MOE_BLOCK, 1), row),
                  pl.BlockSpec((None, D_MODEL, D_EXPERT), lambda i, e: (e[i], 0, 0)),
                  pl.BlockSpec((None, D_MODEL, D_EXPERT), lambda i, e: (e[i], 0, 0)),
                  pl.BlockSpec((None, D_EXPERT, D_MODEL), lambda i, e: (e[i], 0, 0))],
        out_specs=pl.BlockSpec((MOE_BLOCK, D_MODEL), row),
    )
    return pl.pallas_call(
        _expert_kernel,
        grid_spec=grid_spec,
        out_shape=jax.ShapeDtypeStruct((cap, D_MODEL), F32),
        compiler_params=_params("arbitrary"),
        name="experts",
    )(blk_e, xb, wts, w1, w3, w2)


def _final_kernel(x_ref, y0_ref, y1_ref, g_ref, o_ref):
    o_ref[...] = _rms(x_ref[...] + (y0_ref[...] + y1_ref[...]), g_ref[...])


def _final(x2, y0, y1, g):
    n = x2.shape[0]
    tm = ROW_TM
    row = lambda i: (i, 0)
    wide = pl.BlockSpec((tm, D_MODEL), row)
    return pl.pallas_call(
        _final_kernel,
        grid=(n // tm,),
        in_specs=[wide, wide, wide, pl.BlockSpec((1, D_MODEL), lambda i: (0, 0))],
        out_specs=wide,
        out_shape=jax.ShapeDtypeStruct((n, D_MODEL), F32),
        compiler_params=_params("parallel"),
        name="final_norm",
    )(x2, y0, y1, g)


def _route(logits, n_tok):
    lg = logits[:, :N_EGROUPS]
    pg = jax.nn.softmax(lg, axis=-1)
    grp = jnp.argmax(lg, axis=-1)
    g_gate = jnp.take_along_axis(pg, grp[:, None], axis=-1)[:, 0]
    le = logits[:, N_EGROUPS:N_EGROUPS + N_EXPERTS].reshape(n_tok, N_EGROUPS, EXP_PER_GROUP)
    le_g = jnp.take_along_axis(le, grp[:, None, None], axis=1)[:, 0]
    pe = jax.nn.softmax(le_g, axis=-1)
    top_p, top_i = lax.top_k(pe, EXP_TOPK)
    wts = g_gate[:, None] * top_p / jnp.sum(top_p, axis=-1, keepdims=True)
    eid = (grp[:, None] * EXP_PER_GROUP + top_i).reshape(-1).astype(jnp.int32)
    wflat = wts.reshape(-1)
    n_asg = n_tok * EXP_TOPK
    iota = jnp.arange(n_asg, dtype=jnp.int32)
    se, order = lax.sort_key_val(eid, iota)
    counts = jnp.sum((eid[:, None] == jnp.arange(N_EXPERTS, dtype=jnp.int32)[None, :]).astype(jnp.int32), axis=0)
    padded = (counts + MOE_BLOCK - 1) // MOE_BLOCK * MOE_BLOCK
    starts = jnp.cumsum(counts) - counts
    pends = jnp.cumsum(padded)
    pstarts = pends - padded
    cap = ((n_asg + MOE_BLOCK - 1) // MOE_BLOCK + N_EXPERTS) * MOE_BLOCK
    nblk = cap // MOE_BLOCK
    blk_e = jnp.minimum(jnp.searchsorted(pends, jnp.arange(nblk) * MOE_BLOCK, side='right'),
                        N_EXPERTS - 1).astype(jnp.int32)
    e_row = jnp.repeat(blk_e, MOE_BLOCK)
    off = jnp.arange(cap, dtype=jnp.int32) - pstarts[e_row]
    live = off < counts[e_row]
    asg = order[jnp.clip(starts[e_row] + off, 0, n_asg - 1)]
    buf_tok = jnp.where(live, asg // EXP_TOPK, n_tok)
    buf_w = jnp.where(live, wflat[asg], 0.0)
    dest_sorted = pstarts[se] + (iota - starts[se])
    _, pos = lax.sort_key_val(order, dest_sorted)
    return buf_tok, buf_w, blk_e, pos


def kernel(x, mem, norm_mix_g, w_in, w_ret_o, w_nsa_o, w_out, cmp_pe_k, cmp_w1_k, cmp_w2_k, cmp_pe_v,
           cmp_w1_v, cmp_w2_v, norm_x_g, norm_mem_g, w_xq, w_xkv, w_xo, norm_ffn_g, w_grp, b_grp, w_rt,
           b_rt, w_e1, w_e3, w_e2, norm_f_g):
    bsz, seq, _ = x.shape
    n = bsz * seq
    assert seq % PROJ_TM == 0 and seq % (2 * ATT_TQ) == 0 and w_in.shape[0] == 1
    cast = lambda a: a.astype(MXU_DTYPE)
    xc = x.reshape(n, D_MODEL)
    l = 0

    cos, sin = _rope_tables(seq)
    qk, rv, rg, nq, ckv, skv, wkv, gab, ngl = _proj(
        xc, norm_mix_g[l][None, :], _permute_w_in(w_in[l]), cos, sin, seq)
    y_ret = _retention(qk, rv, rg, bsz, seq)
    w2 = jnp.stack([cmp_w2_k[l], cmp_w2_v[l]])
    cmp_k, cmp_vt = _compress(ckv, jnp.stack([cmp_pe_k[l], cmp_pe_v[l]]),
                              cast(jnp.stack([cmp_w1_k[l], cmp_w1_v[l]])),
                              cast(w2), cast(w2.transpose(0, 2, 1)), bsz, seq)
    qt = nq.reshape(bsz, seq // ATT_TQ, ATT_TQ, NSA_GROUPS, NSA_HPG, NSA_DH).transpose(0, 3, 1, 5, 4, 2)
    qt = qt.reshape(bsz, NSA_GROUPS, seq // ATT_TQ, NSA_DH, NSA_HPG * ATT_TQ)
    o_cmp, selt = _cmp_attention(qt, cmp_k, cmp_vt, bsz, seq)
    o_sel = _flash_attention(qt, skv, selt, bsz, seq)
    o_win = _flash_attention(qt, wkv, None, bsz, seq)
    x1 = _merge(xc, y_ret, o_cmp, o_sel, o_win, ngl, gab, cast(w_ret_o[l]), cast(w_nsa_o[l]), cast(w_out[l]))

    kv = _mem_kv(mem.reshape(-1, D_MODEL), norm_mem_g[l][None, :], cast(w_xkv[l]), bsz)
    w_router = jnp.pad(jnp.concatenate([w_grp[l], w_rt[l]], axis=1),
                       ((0, 0), (0, LANES - N_EGROUPS - N_EXPERTS)))
    wr_hi = cast(w_router)
    wr_mid = cast(w_router - wr_hi.astype(F32))
    b_router = jnp.pad(jnp.concatenate([b_grp[l], b_rt[l]]), (0, LANES - N_EGROUPS - N_EXPERTS))[None, :]
    x2, hf, logits = _cross(x1, kv, norm_x_g[l][None, :], cast(w_xq[l]), cast(w_xo[l]),
                            norm_ffn_g[l][None, :], jnp.stack([wr_hi, wr_mid]), b_router, bsz, seq)

    buf_tok, buf_w, blk_e, pos = _route(logits, n)
    hp = jnp.concatenate([hf, jnp.zeros((1, D_MODEL), hf.dtype)], axis=0)
    ys = _experts(blk_e, hp[buf_tok], buf_w[:, None], w_e1[l], w_e3[l], w_e2[l])
    pos = pos.reshape(n, EXP_TOPK)
    out = _final(x2, ys[pos[:, 0]], ys[pos[:, 1]], norm_f_g[None, :])
    return out.reshape(bsz, seq, D_MODEL)
```

```python
import functools

import numpy as np
import jax
import jax.numpy as jnp
from jax import lax
from jax.experimental import pallas as pl
from jax.experimental.pallas import tpu as pltpu

MXU_DTYPE = jnp.bfloat16
F32 = jnp.float32

D_MODEL = 1024
N_MEM = 256
EPS = 1e-6
NEG_INF = -1e30
SEL_FORCE = 1e4

R_HEADS = 4
R_DK = 128
R_DV = 256
R_CHUNK = 128
ROPE_BASE = 10000.0

NSA_HEADS = 8
NSA_GROUPS = 2
NSA_HPG = NSA_HEADS // NSA_GROUPS
NSA_DH = 128
CMP_LEN = 32
CMP_STRIDE = 16
SEL_LEN = 64
SEL_TOPK = 16
WINDOW = 512

X_HEADS = 4
X_DH = D_MODEL // X_HEADS

N_EGROUPS = 4
EXP_PER_GROUP = 8
N_EXPERTS = N_EGROUPS * EXP_PER_GROUP
EXP_TOPK = 2
D_EXPERT = 512
MOE_BLOCK = 256

RET_QK = R_HEADS * R_DK
RET_V = R_HEADS * R_DV
NSA_Q = NSA_HEADS * NSA_DH
NSA_KV = NSA_GROUPS * NSA_DH
SPLITS = (RET_QK, RET_QK, RET_V, RET_V, NSA_Q, NSA_KV, NSA_KV, NSA_KV, NSA_KV, NSA_KV, NSA_KV,
          3 * NSA_HEADS, D_MODEL, D_MODEL)

LANES = 128
SUBLANES = 8
VMEM_LIMIT = 56 * 1024 * 1024

PROJ_TM = 512
RET_ROWS = 512
ATT_TQ = 256
ATT_TK = 256
ROW_TM = 512


def _params(*sem):
    return pltpu.CompilerParams(dimension_semantics=sem, vmem_limit_bytes=VMEM_LIMIT)


def _dot(a, b):
    return jnp.dot(a.astype(MXU_DTYPE), b.astype(MXU_DTYPE), preferred_element_type=F32)


def _dot_nt(a, b):
    return lax.dot_general(a.astype(MXU_DTYPE), b.astype(MXU_DTYPE), (((1,), (1,)), ((), ())),
                           preferred_element_type=F32)


def _dot_tn(a, b):
    return lax.dot_general(a.astype(MXU_DTYPE), b.astype(MXU_DTYPE), (((0,), (0,)), ((), ())),
                           preferred_element_type=F32)


def _split3(p):
    hi = p.astype(MXU_DTYPE)
    r1 = p - hi.astype(F32)
    mid = r1.astype(MXU_DTYPE)
    lo = (r1 - mid.astype(F32)).astype(MXU_DTYPE)
    return hi, mid, lo


def _rms(x, g):
    return x * lax.rsqrt(jnp.mean(x * x, axis=-1, keepdims=True) + EPS) * g


_C_RQK = 0
_C_RV = _C_RQK + 2 * RET_QK
_C_RG = _C_RV + RET_V
_C_CKV = _C_RG + RET_V
_C_SK = _C_CKV + 2 * NSA_KV
_C_WK = _C_SK + NSA_KV
_C_GAB = _C_WK + NSA_KV
_C_NGL = _C_GAB + 2 * D_MODEL
_C_END = _C_NGL + LANES
_R_NQ = 0
_R_SV = _R_NQ + NSA_Q
_R_WV = _R_SV + NSA_KV
_R_END = _R_WV + NSA_KV


def _proj_kernel(x_ref, g_ref, w_ref, wt_ref, cos_ref, sin_up_ref, sin_dn_ref,
                 qk_ref, rv_ref, rg_ref, ckv_ref, sk_ref, wk_ref, gab_ref, ngl_ref, nqt_ref, svt_ref, wvt_ref):
    hb = _rms(x_ref[...], g_ref[...]).astype(MXU_DTYPE)
    tm = hb.shape[0]

    def mm(off, width):
        return jnp.dot(hb, w_ref[:, off:off + width], preferred_element_type=F32)

    def mm_t(off, height):
        return _dot_nt(wt_ref[off:off + height, :], hb)

    cos = cos_ref[...]
    sin_up = sin_up_ref[...]
    sin_dn = sin_dn_ref[...]
    qk = mm(_C_RQK, 2 * RET_QK)
    for i in range(2 * R_HEADS):
        t = qk[:, i * R_DK:(i + 1) * R_DK]
        r = t * cos + pltpu.roll(t, R_DK - 1, axis=1) * sin_up + pltpu.roll(t, 1, axis=1) * sin_dn
        if i >= R_HEADS:
            r = r * (R_DK ** -0.5)
        qk_ref[:, i * R_DK:(i + 1) * R_DK] = r.astype(qk_ref.dtype)
    rv_ref[...] = mm(_C_RV, RET_V).astype(rv_ref.dtype)
    rg_ref[...] = mm(_C_RG, RET_V)
    ckv_ref[...] = mm(_C_CKV, 2 * NSA_KV)
    sk_ref[...] = mm(_C_SK, NSA_KV).astype(sk_ref.dtype)
    wk_ref[...] = mm(_C_WK, NSA_KV).astype(wk_ref.dtype)
    gab_ref[:, :D_MODEL] = mm(_C_GAB, D_MODEL)
    gab_ref[:, D_MODEL:] = mm(_C_GAB + D_MODEL, D_MODEL)
    ngl_ref[...] = mm(_C_NGL, LANES)

    nqt = mm_t(_R_NQ, NSA_Q).astype(nqt_ref.dtype)
    for g in range(NSA_GROUPS):
        for j in range(tm // ATT_TQ):
            for hh in range(NSA_HPG):
                head = g * NSA_HPG + hh
                nqt_ref[g, j, :, hh * ATT_TQ:(hh + 1) * ATT_TQ] = (
                    nqt[head * NSA_DH:(head + 1) * NSA_DH, j * ATT_TQ:(j + 1) * ATT_TQ])
    for off, out_ref in ((_R_SV, svt_ref), (_R_WV, wvt_ref)):
        vt = mm_t(off, NSA_KV).astype(out_ref.dtype)
        for g in range(NSA_GROUPS):
            for j in range(tm // ATT_TK):
                out_ref[g, j] = vt[g * NSA_DH:(g + 1) * NSA_DH, j * ATT_TK:(j + 1) * ATT_TK]


def _split_w_in(w_in):
    offs = np.cumsum((0,) + SPLITS)
    rq, rk, rv, rg, nq, ck, cv, sk, sv, wk, wv, ngl, ga, gb = [
        w_in[:, offs[i]:offs[i + 1]] for i in range(len(SPLITS))]
    ngl = jnp.pad(ngl, ((0, 0), (0, LANES - ngl.shape[1])))
    w = jnp.concatenate([rq, rk, rv, rg, ck, cv, sk, wk, ga, gb, ngl], axis=1).astype(MXU_DTYPE)
    wt = jnp.concatenate([nq, sv, wv], axis=1).T.astype(MXU_DTYPE)
    return w, wt


def _rope_tables(seq):
    pos = jnp.arange(seq, dtype=F32)
    inv_freq = ROPE_BASE ** (-jnp.arange(0, R_DK, 2, dtype=F32) / R_DK)
    ang = pos[:, None] * inv_freq[None, :]
    cos = jnp.repeat(jnp.cos(ang), 2, axis=1)
    sin = jnp.repeat(jnp.sin(ang), 2, axis=1)
    even = (jnp.arange(R_DK) % 2 == 0)[None, :]
    return cos, jnp.where(even, -sin, 0.0), jnp.where(even, 0.0, sin)


def _proj(x2d, g, w, wt, tables, bsz, seq):
    n = x2d.shape[0]
    tm = PROJ_TM
    nt = seq // tm
    row = lambda i: (i, 0)
    const = lambda i: (0, 0)
    tile = lambda i: (i // nt, 0, i % nt, 0, 0)
    table = pl.BlockSpec((tm, R_DK), lambda i: (i % nt, 0))
    widths = (2 * RET_QK, RET_V, RET_V, 2 * NSA_KV, NSA_KV, NSA_KV, 2 * D_MODEL, LANES)
    dtypes = (MXU_DTYPE, MXU_DTYPE, F32, F32, MXU_DTYPE, MXU_DTYPE, F32, F32)
    width = NSA_HPG * ATT_TQ
    vt_shape = jax.ShapeDtypeStruct((bsz, NSA_GROUPS, seq // ATT_TK, NSA_DH, ATT_TK), MXU_DTYPE)
    vt_spec = pl.BlockSpec((None, NSA_GROUPS, tm // ATT_TK, NSA_DH, ATT_TK), tile)
    return pl.pallas_call(
        _proj_kernel,
        grid=(n // tm,),
        in_specs=[pl.BlockSpec((tm, D_MODEL), row),
                  pl.BlockSpec((1, D_MODEL), const),
                  pl.BlockSpec((D_MODEL, _C_END), const, pipeline_mode=pl.Buffered(1)),
                  pl.BlockSpec((_R_END, D_MODEL), const, pipeline_mode=pl.Buffered(1)),
                  table, table, table],
        out_specs=[pl.BlockSpec((tm, wd), row) for wd in widths] + [
            pl.BlockSpec((None, NSA_GROUPS, tm // ATT_TQ, NSA_DH, width), tile), vt_spec, vt_spec],
        out_shape=[jax.ShapeDtypeStruct((n, wd), dt) for wd, dt in zip(widths, dtypes)] + [
            jax.ShapeDtypeStruct((bsz, NSA_GROUPS, seq // ATT_TQ, NSA_DH, width), MXU_DTYPE), vt_shape, vt_shape],
        compiler_params=_params("parallel"),
        name="proj",
    )(x2d, g, w, wt, *tables)


def _ret_kernel(qk_ref, v_ref, g_ref, decay_ref, zeta_ref, xi_ref, cd_ref, y_ref, state_ref):
    @pl.when(pl.program_id(1) == 0)
    def _():
        state_ref[...] = jnp.zeros_like(state_ref)

    for c in range(RET_ROWS // R_CHUNK):
        rows = slice(c * R_CHUNK, (c + 1) * R_CHUNK)
        for h in range(R_HEADS):
            q = qk_ref[rows, h * R_DK:(h + 1) * R_DK]
            k = qk_ref[rows, RET_QK + h * R_DK:RET_QK + (h + 1) * R_DK]
            v = v_ref[rows, h * R_DV:(h + 1) * R_DV]
            st = state_ref[h]
            scores = _dot_nt(q, k) * decay_ref[h]
            o = _dot(scores, v) + _dot(q, st) * xi_ref[h]
            o = o * lax.rsqrt(jnp.mean(o * o, axis=-1, keepdims=True) + EPS)
            g = g_ref[rows, h * R_DV:(h + 1) * R_DV]
            y_ref[rows, h * R_DV:(h + 1) * R_DV] = (g * jax.nn.sigmoid(g) * o).astype(y_ref.dtype)
            kz = k.astype(F32) * zeta_ref[h]
            state_ref[h] = st * cd_ref[h] + _dot_tn(kz, v)


def _retention(qk, rv, rg, bsz, seq):
    n = qk.shape[0]
    nt = seq // RET_ROWS
    log_g = jnp.log1p(-jnp.exp2(-5.0 - jnp.arange(R_HEADS, dtype=F32)))
    idx = jnp.arange(R_CHUNK, dtype=F32)
    diff = idx[:, None] - idx[None, :]
    decay = jnp.where(diff >= 0, jnp.exp(log_g[:, None, None] * jnp.maximum(diff, 0.0)), 0.0)
    zeta = jnp.exp(log_g[:, None] * (R_CHUNK - 1.0 - idx)[None, :])[:, :, None]
    xi = jnp.exp(log_g[:, None] * (idx + 1.0)[None, :])[:, :, None]
    cd = jnp.exp(log_g * R_CHUNK)[:, None, None]
    row = lambda b, i: (b * nt + i, 0)
    const3 = lambda b, i: (0, 0, 0)
    return pl.pallas_call(
        _ret_kernel,
        grid=(bsz, nt),
        in_specs=[pl.BlockSpec((RET_ROWS, 2 * RET_QK), row),
                  pl.BlockSpec((RET_ROWS, RET_V), row),
                  pl.BlockSpec((RET_ROWS, RET_V), row),
                  pl.BlockSpec((R_HEADS, R_CHUNK, R_CHUNK), const3),
                  pl.BlockSpec((R_HEADS, R_CHUNK, 1), const3),
                  pl.BlockSpec((R_HEADS, R_CHUNK, 1), const3),
                  pl.BlockSpec((R_HEADS, 1, 1), const3)],
        out_specs=pl.BlockSpec((RET_ROWS, RET_V), row),
        out_shape=jax.ShapeDtypeStruct((n, RET_V), MXU_DTYPE),
        scratch_shapes=[pltpu.VMEM((R_HEADS, R_DK, R_DV), F32)],
        compiler_params=_params("parallel", "arbitrary"),
        name="retention",
    )(qk, rv, rg, decay, zeta, xi, cd)


def _compress_kernel(x_ref, pe_ref, w1_ref, w2_ref, w2t_ref, o_ref, ot_ref, buf_ref, *, seq):
    ncp = seq // CMP_STRIDE
    buf_ref[0:seq, :] = x_ref[...]
    buf_ref[seq:seq + LANES, :] = jnp.zeros((LANES, NSA_DH), F32)
    acc = jnp.zeros((ncp, NSA_DH), F32)
    for l in range(CMP_LEN):
        xl = buf_ref[pl.ds(l, ncp, stride=CMP_STRIDE), :] + pe_ref[l:l + 1, :]
        acc = acc + _dot(xl, w1_ref[l])
    hid = jax.nn.gelu(acc)
    o_ref[...] = _dot(hid, w2_ref[...]).astype(o_ref.dtype)
    ot_ref[...] = _dot_nt(w2t_ref[...], hid).astype(ot_ref.dtype)


def _compress(ckv, pe, w1, w2, w2t, bsz, seq):
    ncp = seq // CMP_STRIDE
    nj = 2 * NSA_GROUPS
    wsel = lambda b, j: (j // NSA_GROUPS, 0, 0)
    return pl.pallas_call(
        functools.partial(_compress_kernel, seq=seq),
        grid=(bsz, nj),
        in_specs=[pl.BlockSpec((seq, NSA_DH), lambda b, j: (b, j)),
                  pl.BlockSpec((None, CMP_LEN, NSA_DH), wsel),
                  pl.BlockSpec((None, CMP_LEN, NSA_DH, NSA_DH), lambda b, j: (j // NSA_GROUPS, 0, 0, 0)),
                  pl.BlockSpec((None, NSA_DH, NSA_DH), wsel),
                  pl.BlockSpec((None, NSA_DH, NSA_DH), wsel)],
        out_specs=[pl.BlockSpec((None, None, ncp, NSA_DH), lambda b, j: (b, j, 0, 0)),
                   pl.BlockSpec((None, None, NSA_DH, ncp), lambda b, j: (b, j, 0, 0))],
        out_shape=[jax.ShapeDtypeStruct((bsz, nj, ncp, NSA_DH), MXU_DTYPE),
                   jax.ShapeDtypeStruct((bsz, nj, NSA_DH, ncp), MXU_DTYPE)],
        scratch_shapes=[pltpu.VMEM((seq + LANES, NSA_DH), F32)],
        compiler_params=_params("parallel", "parallel"),
        name="compress",
    )(ckv, pe, w1, w2, w2t)


def _cmp_attn_kernel(qt_ref, k_ref, vt_ref, ov_ref, o_ref, selt_ref, *, ncp, nb):
    tq = ATT_TQ
    t0 = pl.program_id(2) * tq
    t = t0 + lax.broadcasted_iota(jnp.int32, (ncp, tq), 1)
    n = lax.broadcasted_iota(jnp.int32, (ncp, tq), 0)
    valid = (n * CMP_STRIDE + (CMP_LEN - 1) <= t) & (n < ncp - 1)
    any_valid = (t0 + lax.broadcasted_iota(jnp.int32, (1, tq), 1)) >= CMP_LEN - 1
    valid = jnp.concatenate([valid] * NSA_HPG, axis=1)
    any_valid = jnp.concatenate([any_valid] * NSA_HPG, axis=1)
    s = _dot(k_ref[...], qt_ref[...]) * (NSA_DH ** -0.5)
    s = jnp.where(valid, s, NEG_INF)
    e = jnp.exp(s - jnp.max(s, axis=0, keepdims=True))
    p = e * jnp.where(any_valid, 1.0 / jnp.sum(e, axis=0, keepdims=True), 0.0)
    ot = _dot(vt_ref[...], p)
    for h in range(NSA_HPG):
        o_ref[:, h * NSA_DH:(h + 1) * NSA_DH] = ot[:, h * tq:(h + 1) * tq].T
    psum = sum(p[:, h * tq:(h + 1) * tq] for h in range(NSA_HPG))

    imp = sum(_dot(ov_ref[...], part) for part in _split3(psum))
    j = lax.broadcasted_iota(jnp.int32, (nb, tq), 0)
    tb = (t0 + lax.broadcasted_iota(jnp.int32, (nb, tq), 1)) // SEL_LEN
    forced = (j == 0) | (j == tb) | (j == tb - 1)
    imp = jnp.where(j > tb, -SEL_FORCE, jnp.where(forced, SEL_FORCE, imp))

    sub = SUBLANES
    grp = [imp[r * sub:(r + 1) * sub] for r in range(nb // sub)]
    cnt = [jnp.zeros((sub, tq), F32) for _ in grp]
    for i in range(nb):
        row = jnp.broadcast_to(imp[i:i + 1, :], (sub, tq))
        for r in range(nb // sub):
            if r * sub > i:
                beats = jnp.where(row >= grp[r], 1.0, 0.0)
            elif r * sub + sub - 1 < i:
                beats = jnp.where(row > grp[r], 1.0, 0.0)
            else:
                jr = r * sub + lax.broadcasted_iota(jnp.int32, (sub, tq), 0)
                beats = jnp.where(jr > i, jnp.where(row >= grp[r], 1.0, 0.0), jnp.where(row > grp[r], 1.0, 0.0))
            cnt[r] = cnt[r] + beats
    k_sel = min(SEL_TOPK, nb)
    for r in range(nb // sub):
        selt_ref[r * sub:(r + 1) * sub, :] = jnp.where(cnt[r] < k_sel, 1.0, 0.0)


def _cmp_attention(qt, cmp_k, cmp_vt, bsz, seq):
    n = bsz * seq
    ncp = seq // CMP_STRIDE
    nb = seq // SEL_LEN
    nt = seq // ATT_TQ
    gw = NSA_HPG * NSA_DH
    cstart = np.arange(ncp) * CMP_STRIDE
    jstart = np.arange(nb) * SEL_LEN
    ov = ((cstart[None, :] < jstart[:, None] + SEL_LEN) & (cstart[None, :] + CMP_LEN > jstart[:, None])
          & (np.arange(ncp)[None, :] < ncp - 1))
    ov = jnp.asarray(ov, MXU_DTYPE)
    return pl.pallas_call(
        functools.partial(_cmp_attn_kernel, ncp=ncp, nb=nb),
        grid=(bsz, NSA_GROUPS, nt),
        in_specs=[pl.BlockSpec((None, None, None, NSA_DH, NSA_HPG * ATT_TQ), lambda b, g, i: (b, g, i, 0, 0)),
                  pl.BlockSpec((None, None, ncp, NSA_DH), lambda b, g, i: (b, g, 0, 0)),
                  pl.BlockSpec((None, None, NSA_DH, ncp), lambda b, g, i: (b, NSA_GROUPS + g, 0, 0)),
                  pl.BlockSpec((nb, ncp), lambda b, g, i: (0, 0))],
        out_specs=[pl.BlockSpec((ATT_TQ, gw), lambda b, g, i: (b * nt + i, g)),
                   pl.BlockSpec((None, None, nb, ATT_TQ), lambda b, g, i: (b, g, 0, i))],
        out_shape=[jax.ShapeDtypeStruct((n, NSA_Q), F32),
                   jax.ShapeDtypeStruct((bsz, NSA_GROUPS, nb, seq), F32)],
        compiler_params=_params("parallel", "parallel", "parallel"),
        name="cmp_attention",
    )(qt, cmp_k, cmp_vt, ov)


_EXP2_SCALE = (NSA_DH ** -0.5) * float(np.log2(np.e))
_MASK_ALL = 1 << 28


def _flash_loop(qt_ref, k_ref, vt_ref, acc_ref, s_refs, n_tiles, tile_of, valid_of):
    width = NSA_HPG * ATT_TQ

    def scores(kt):
        k = k_ref[pl.ds(pl.multiple_of(kt * ATT_TK, ATT_TK), ATT_TK), :]
        return _dot(k, qt_ref[...])

    def half_step(step, s_cur, s_nxt, m, l):
        kt = tile_of(jnp.minimum(step, n_tiles - 1))
        s_nxt[...] = scores(tile_of(jnp.minimum(step + 1, n_tiles - 1)))
        valid = valid_of(kt, kt * ATT_TK + jnp.where(step < n_tiles, 0, _MASK_ALL))
        valid = jnp.concatenate([valid] * NSA_HPG, axis=1)
        s = jnp.where(valid, s_cur[...] * _EXP2_SCALE, NEG_INF)
        m_new = jnp.maximum(m, jnp.max(s, axis=0, keepdims=True))
        alpha = jnp.exp2(m - m_new)
        p = jnp.exp2(s - m_new)
        l = alpha * l + jnp.sum(p, axis=0, keepdims=True)
        acc_ref[...] = alpha * acc_ref[...] + _dot(vt_ref[kt], p)
        return m_new, l

    def body(i, carry):
        m, l = half_step(2 * i, s_refs[0], s_refs[1], *carry)
        return half_step(2 * i + 1, s_refs[1], s_refs[0], m, l)

    acc_ref[...] = jnp.zeros_like(acc_ref)
    s_refs[0][...] = scores(tile_of(0))
    init = (jnp.full((1, width), NEG_INF, F32), jnp.zeros((1, width), F32))
    _, l = lax.fori_loop(0, (n_tiles + 1) // 2, body, init)
    return l


def _flash_finish(o_ref, acc_ref, l):
    inv = 1.0 / l
    for h in range(NSA_HPG):
        cols = slice(h * ATT_TQ, (h + 1) * ATT_TQ)
        o_ref[:, h * NSA_DH:(h + 1) * NSA_DH] = (acc_ref[:, cols] * inv[:, cols]).T


def _sel_attn_kernel(qt_ref, k_ref, vt_ref, selt_ref, o_ref, acc_ref, s0_ref, s1_ref):
    qi = pl.program_id(2)
    tpos = qi * ATT_TQ + lax.broadcasted_iota(jnp.int32, (ATT_TK, ATT_TQ), 1)
    row = lax.broadcasted_iota(jnp.int32, (ATT_TK, ATT_TQ), 0)
    blocks_per_tile = ATT_TK // SEL_LEN

    def valid_of(kt, key0):
        picked = jnp.concatenate(
            [jnp.broadcast_to(selt_ref[pl.ds(kt * blocks_per_tile + jb, 1), :], (SEL_LEN, ATT_TQ))
             for jb in range(blocks_per_tile)], axis=0)
        return (picked > 0.5) & (row + key0 <= tpos)

    l = _flash_loop(qt_ref, k_ref, vt_ref, acc_ref, (s0_ref, s1_ref), (qi + 1) * (ATT_TQ // ATT_TK),
                    lambda step: step, valid_of)
    _flash_finish(o_ref, acc_ref, l)


def _win_attn_kernel(qt_ref, k_ref, vt_ref, o_ref, acc_ref, s0_ref, s1_ref):
    qi = pl.program_id(2)
    tpos = qi * ATT_TQ + lax.broadcasted_iota(jnp.int32, (ATT_TK, ATT_TQ), 1)
    row = lax.broadcasted_iota(jnp.int32, (ATT_TK, ATT_TQ), 0)
    last = (qi + 1) * (ATT_TQ // ATT_TK) - 1
    n_tiles = jnp.minimum(last + 1, (ATT_TQ + WINDOW) // ATT_TK)

    def valid_of(kt, key0):
        kpos = row + key0
        return (kpos <= tpos) & (kpos > tpos - WINDOW)

    l = _flash_loop(qt_ref, k_ref, vt_ref, acc_ref, (s0_ref, s1_ref), n_tiles,
                    lambda step: last - step, valid_of)
    _flash_finish(o_ref, acc_ref, l)


def _flash_attention(qt, k, vt, selt, bsz, seq):
    n = bsz * seq
    nt = seq // ATT_TQ
    nkt = seq // ATT_TK
    nb = seq // SEL_LEN
    gw = NSA_HPG * NSA_DH
    width = NSA_HPG * ATT_TQ
    in_specs = [pl.BlockSpec((None, None, None, NSA_DH, width), lambda b, g, i: (b, g, i, 0, 0)),
                pl.BlockSpec((seq, NSA_DH), lambda b, g, i: (b, g)),
                pl.BlockSpec((None, None, nkt, NSA_DH, ATT_TK), lambda b, g, i: (b, g, 0, 0, 0))]
    args = [qt, k, vt]
    if selt is None:
        body = _win_attn_kernel
        name = "window_attention"
    else:
        body = _sel_attn_kernel
        name = "selected_attention"
        in_specs.append(pl.BlockSpec((None, None, nb, ATT_TQ), lambda b, g, i: (b, g, 0, i)))
        args.append(selt)
    return pl.pallas_call(
        body,
        grid=(bsz, NSA_GROUPS, nt),
        in_specs=in_specs,
        out_specs=pl.BlockSpec((ATT_TQ, gw), lambda b, g, i: (b * nt + i, g)),
        out_shape=jax.ShapeDtypeStruct((n, NSA_Q), F32),
        scratch_shapes=[pltpu.VMEM((NSA_DH, width), F32),
                        pltpu.VMEM((ATT_TK, width), F32),
                        pltpu.VMEM((ATT_TK, width), F32)],
        compiler_params=_params("parallel", "parallel", "parallel"),
        name=name,
    )(*args)


def _merge_kernel(x_ref, yr_ref, oc_ref, os_ref, ow_ref, ngl_ref, gab_ref, wr_ref, wn_ref, wo_ref, o_ref):
    tm = x_ref.shape[0]
    gates = jax.nn.sigmoid(ngl_ref[...])
    parts = []
    for h in range(NSA_HEADS):
        cols = slice(h * NSA_DH, (h + 1) * NSA_DH)

        def gate(br):
            return jnp.broadcast_to(gates[:, 3 * h + br:3 * h + br + 1], (tm, NSA_DH))

        parts.append(gate(0) * oc_ref[:, cols] + gate(1) * os_ref[:, cols] + gate(2) * ow_ref[:, cols])
    o_nsa = jnp.concatenate(parts, axis=1)
    y_ret = _dot(yr_ref[...], wr_ref[...])
    y_nsa = _dot(o_nsa, wn_ref[...])
    y = jax.nn.sigmoid(gab_ref[:, :D_MODEL]) * y_ret + jax.nn.sigmoid(gab_ref[:, D_MODEL:]) * y_nsa
    o_ref[...] = x_ref[...] + _dot(y, wo_ref[...])


def _merge(x2d, y_ret, o_cmp, o_sel, o_win, ngl, gab, w_ret_o, w_nsa_o, w_out):
    n = x2d.shape[0]
    tm = ROW_TM
    row = lambda i: (i, 0)
    const = lambda i: (0, 0)
    wide = pl.BlockSpec((tm, D_MODEL), row)
    wspec = pl.BlockSpec((D_MODEL, D_MODEL), const)
    return pl.pallas_call(
        _merge_kernel,
        grid=(n // tm,),
        in_specs=[wide, wide, wide, wide, wide,
                  pl.BlockSpec((tm, LANES), row),
                  pl.BlockSpec((tm, 2 * D_MODEL), row),
                  wspec, wspec, wspec],
        out_specs=wide,
        out_shape=jax.ShapeDtypeStruct((n, D_MODEL), F32),
        compiler_params=_params("parallel"),
        name="merge",
    )(x2d, y_ret, o_cmp, o_sel, o_win, ngl, gab, w_ret_o, w_nsa_o, w_out)


def _mem_kv_kernel(m_ref, g_ref, w_ref, o_ref):
    o_ref[...] = _dot(_rms(m_ref[...], g_ref[...]), w_ref[...]).astype(o_ref.dtype)


def _mem_kv(mem2d, g, w_xkv, bsz):
    nm = mem2d.shape[0] // bsz
    return pl.pallas_call(
        _mem_kv_kernel,
        grid=(bsz,),
        in_specs=[pl.BlockSpec((nm, D_MODEL), lambda b: (b, 0)),
                  pl.BlockSpec((1, D_MODEL), lambda b: (0, 0)),
                  pl.BlockSpec((D_MODEL, 2 * D_MODEL), lambda b: (0, 0))],
        out_specs=pl.BlockSpec((nm, 2 * D_MODEL), lambda b: (b, 0)),
        out_shape=jax.ShapeDtypeStruct((mem2d.shape[0], 2 * D_MODEL), MXU_DTYPE),
        compiler_params=_params("parallel"),
        name="mem_kv",
    )(mem2d, g, w_xkv)


def _pack_pairs(x):
    half = x.shape[1] // 2
    hi = lax.bitcast_convert_type(x[:, :half].astype(MXU_DTYPE).astype(F32), jnp.uint32)
    lo = lax.bitcast_convert_type(x[:, half:].astype(MXU_DTYPE).astype(F32), jnp.uint32)
    return (hi & jnp.uint32(0xFFFF0000)) | (lo >> 16)


def _unpack_pairs(u):
    hi = lax.bitcast_convert_type(u & jnp.uint32(0xFFFF0000), F32)
    lo = lax.bitcast_convert_type(u << 16, F32)
    return jnp.concatenate([hi, lo], axis=1)


def _cross_kernel(x_ref, kv_ref, gx_ref, wq_ref, wo_ref, gf_ref, wr_ref, br_ref, x2_ref, hf_ref, lg_ref):
    x = x_ref[...]
    q = _dot(_rms(x, gx_ref[...]), wq_ref[...])
    heads = []
    for h in range(X_HEADS):
        k = kv_ref[:, h * X_DH:(h + 1) * X_DH]
        v = kv_ref[:, D_MODEL + h * X_DH:D_MODEL + (h + 1) * X_DH]
        s = _dot_nt(q[:, h * X_DH:(h + 1) * X_DH], k) * (X_DH ** -0.5)
        e = jnp.exp(s - jnp.max(s, axis=-1, keepdims=True))
        p = e / jnp.sum(e, axis=-1, keepdims=True)
        heads.append(_dot(p, v))
    x2 = x + _dot(jnp.concatenate(heads, axis=1), wo_ref[...])
    x2_ref[...] = x2
    hf = _rms(x2, gf_ref[...])
    hf_ref[...] = _pack_pairs(hf)
    h_hi, h_mid, _ = _split3(hf)
    w_hi = wr_ref[0]
    w_mid = wr_ref[1]
    lg_ref[...] = (jnp.dot(h_hi, w_hi, preferred_element_type=F32)
                   + (jnp.dot(h_mid, w_hi, preferred_element_type=F32)
                      + jnp.dot(h_hi, w_mid, preferred_element_type=F32))) + br_ref[...]


def _cross(x1, kv, gx, w_xq, w_xo, gf, w_router, b_router, bsz, seq):
    n = x1.shape[0]
    tm = ROW_TM
    nt = seq // tm
    nm = kv.shape[0] // bsz
    row = lambda i: (i, 0)
    const = lambda i: (0, 0)
    vec = pl.BlockSpec((1, D_MODEL), const)
    wspec = pl.BlockSpec((D_MODEL, D_MODEL), const)
    return pl.pallas_call(
        _cross_kernel,
        grid=(n // tm,),
        in_specs=[pl.BlockSpec((tm, D_MODEL), row),
                  pl.BlockSpec((nm, 2 * D_MODEL), lambda i: (i // nt, 0)),
                  vec, wspec, wspec, vec,
                  pl.BlockSpec((2, D_MODEL, LANES), lambda i: (0, 0, 0)),
                  pl.BlockSpec((1, LANES), const)],
        out_specs=[pl.BlockSpec((tm, D_MODEL), row),
                   pl.BlockSpec((tm, D_MODEL // 2), row),
                   pl.BlockSpec((tm, LANES), row)],
        out_shape=[jax.ShapeDtypeStruct((n, D_MODEL), F32),
                   jax.ShapeDtypeStruct((n, D_MODEL // 2), jnp.uint32),
                   jax.ShapeDtypeStruct((n, LANES), F32)],
        compiler_params=_params("parallel"),
        name="cross_attention",
    )(x1, kv, gx, w_xq, w_xo, gf, w_router, b_router)


def _expert_kernel(blk_e_ref, xb_ref, wt_ref, w1_ref, w3_ref, w2_ref, o_ref):
    del blk_e_ref
    xb = _unpack_pairs(xb_ref[...]).astype(MXU_DTYPE)
    a = _dot(xb, w1_ref[...])
    hmid = a * jax.nn.sigmoid(a) * _dot(xb, w3_ref[...])
    o_ref[...] = _dot(hmid, w2_ref[...]) * wt_ref[...]


def _experts(blk_e, xb, wts, w1, w3, w2):
    cap = xb.shape[0]
    nblk = cap // MOE_BLOCK
    row = lambda i, e: (i, 0)
    grid_spec = pltpu.PrefetchScalarGridSpec(
        num_scalar_prefetch=1,
        grid=(nblk,),
        in_specs=[pl.BlockSpec((MOE_BLOCK, D_MODEL // 2), row),
                  pl.BlockSpec((MOE_BLOCK, 1), row),
                  pl.BlockSpec((None, D_MODEL, D_EXPERT), lambda i, e: (e[i], 0, 0)),
                  pl.BlockSpec((None, D_MODEL, D_EXPERT), lambda i, e: (e[i], 0, 0)),
                  pl.BlockSpec((None, D_EXPERT, D_MODEL), lambda i, e: (e[i], 0, 0))],
        out_specs=pl.BlockSpec((MOE_BLOCK, D_MODEL), row),
    )
    return pl.pallas_call(
        _expert_kernel,
        grid_spec=grid_spec,
        out_shape=jax.ShapeDtypeStruct((cap, D_MODEL), F32),
        compiler_params=_params("arbitrary"),
        name="experts",
    )(blk_e, xb, wts, w1, w3, w2)


def _final_kernel(x_ref, y0_ref, y1_ref, g_ref, o_ref):
    o_ref[...] = _rms(x_ref[...] + (y0_ref[...] + y1_ref[...]), g_ref[...])


def _final(x2, y0, y1, g):
    n = x2.shape[0]
    tm = ROW_TM
    row = lambda i: (i, 0)
    wide = pl.BlockSpec((tm, D_MODEL), row)
    return pl.pallas_call(
        _final_kernel,
        grid=(n // tm,),
        in_specs=[wide, wide, wide, pl.BlockSpec((1, D_MODEL), lambda i: (0, 0))],
        out_specs=wide,
        out_shape=jax.ShapeDtypeStruct((n, D_MODEL), F32),
        compiler_params=_params("parallel"),
        name="final_norm",
    )(x2, y0, y1, g)


def _route(logits, n_tok):
    lg = logits[:, :N_EGROUPS]
    grp = jnp.argmax(lg, axis=-1).astype(jnp.int32)
    g_gate = 1.0 / jnp.sum(jnp.exp(lg - jnp.max(lg, axis=-1, keepdims=True)), axis=-1)
    le = logits[:, N_EGROUPS:N_EGROUPS + N_EXPERTS].reshape(n_tok, N_EGROUPS, EXP_PER_GROUP)
    in_grp = grp[:, None] == jnp.arange(N_EGROUPS, dtype=jnp.int32)[None, :]
    le_g = jnp.sum(jnp.where(in_grp[:, :, None], le, 0.0), axis=1)
    pe = jax.nn.softmax(le_g, axis=-1)
    lane = jnp.arange(EXP_PER_GROUP, dtype=jnp.int32)[None, :]
    i0 = jnp.argmax(pe, axis=-1).astype(jnp.int32)
    rest = jnp.where(lane == i0[:, None], -1.0, pe)
    i1 = jnp.argmax(rest, axis=-1).astype(jnp.int32)
    top_p = jnp.stack([jnp.max(pe, axis=-1), jnp.max(rest, axis=-1)], axis=-1)
    wts = g_gate[:, None] * top_p / jnp.sum(top_p, axis=-1, keepdims=True)
    eid = (grp[:, None] * EXP_PER_GROUP + jnp.stack([i0, i1], axis=-1)).reshape(-1)
    wflat = wts.reshape(-1)
    n_asg = n_tok * EXP_TOPK
    iota = jnp.arange(n_asg, dtype=jnp.int32)
    se, order = lax.sort_key_val(eid, iota)
    counts = jnp.sum((jnp.arange(N_EXPERTS, dtype=jnp.int32)[:, None] == eid[None, :]).astype(jnp.int32), axis=1)
    padded = (counts + MOE_BLOCK - 1) // MOE_BLOCK * MOE_BLOCK
    starts = jnp.cumsum(counts) - counts
    pends = jnp.cumsum(padded)
    pstarts = pends - padded
    cap = ((n_asg + MOE_BLOCK - 1) // MOE_BLOCK + N_EXPERTS) * MOE_BLOCK
    nblk = cap // MOE_BLOCK
    blk_e = jnp.minimum(jnp.searchsorted(pends, jnp.arange(nblk) * MOE_BLOCK, side='right', method='compare_all'),
                        N_EXPERTS - 1).astype(jnp.int32)
    per_row = lambda a: jnp.repeat(a[blk_e], MOE_BLOCK)
    off = jnp.arange(cap, dtype=jnp.int32) - per_row(pstarts)
    live = off < per_row(counts)
    asg = order[jnp.clip(per_row(starts) + off, 0, n_asg - 1)]
    buf_tok = jnp.where(live, asg // EXP_TOPK, 0)
    buf_w = jnp.where(live, wflat[asg], 0.0)
    dest_sorted = iota + (pstarts - starts)[se]
    _, pos = lax.sort_key_val(order, dest_sorted)
    return buf_tok, buf_w, blk_e, pos


def kernel(x, mem, norm_mix_g, w_in, w_ret_o, w_nsa_o, w_out, cmp_pe_k, cmp_w1_k, cmp_w2_k, cmp_pe_v,
           cmp_w1_v, cmp_w2_v, norm_x_g, norm_mem_g, w_xq, w_xkv, w_xo, norm_ffn_g, w_grp, b_grp, w_rt,
           b_rt, w_e1, w_e3, w_e2, norm_f_g):
    bsz, seq, _ = x.shape
    n = bsz * seq
    assert seq % PROJ_TM == 0 and seq % (2 * ATT_TQ) == 0 and w_in.shape[0] == 1
    cast = lambda a: a.astype(MXU_DTYPE)
    xc = x.reshape(n, D_MODEL)
    l = 0

    w_main, w_t = _split_w_in(w_in[l])
    qk, rv, rg, ckv, sk, wk, gab, ngl, qt, svt, wvt = _proj(
        xc, norm_mix_g[l][None, :], w_main, w_t, _rope_tables(seq), bsz, seq)
    y_ret = _retention(qk, rv, rg, bsz, seq)
    w2 = jnp.stack([cmp_w2_k[l], cmp_w2_v[l]])
    cmp_k, cmp_vt = _compress(ckv, jnp.stack([cmp_pe_k[l], cmp_pe_v[l]]),
                              cast(jnp.stack([cmp_w1_k[l], cmp_w1_v[l]])),
                              cast(w2), cast(w2.transpose(0, 2, 1)), bsz, seq)
    o_cmp, selt = _cmp_attention(qt, cmp_k, cmp_vt, bsz, seq)
    o_sel = _flash_attention(qt, sk, svt, selt, bsz, seq)
    o_win = _flash_attention(qt, wk, wvt, None, bsz, seq)
    x1 = _merge(xc, y_ret, o_cmp, o_sel, o_win, ngl, gab, cast(w_ret_o[l]), cast(w_nsa_o[l]), cast(w_out[l]))

    kv = _mem_kv(mem.reshape(-1, D_MODEL), norm_mem_g[l][None, :], cast(w_xkv[l]), bsz)
    w_router = jnp.pad(jnp.concatenate([w_grp[l], w_rt[l]], axis=1),
                       ((0, 0), (0, LANES - N_EGROUPS - N_EXPERTS)))
    wr_hi = cast(w_router)
    wr_mid = cast(w_router - wr_hi.astype(F32))
    b_router = jnp.pad(jnp.concatenate([b_grp[l], b_rt[l]]), (0, LANES - N_EGROUPS - N_EXPERTS))[None, :]
    x2, hf, logits = _cross(x1, kv, norm_x_g[l][None, :], cast(w_xq[l]), cast(w_xo[l]),
                            norm_ffn_g[l][None, :], jnp.stack([wr_hi, wr_mid]), b_router, bsz, seq)

    buf_tok, buf_w, blk_e, pos = _route(logits, n)
    ys = _experts(blk_e, hf[buf_tok], buf_w[:, None], w_e1[l], w_e3[l], w_e2[l])
    pos = pos.reshape(n, EXP_TOPK)
    out = _final(x2, ys[pos[:, 0]], ys[pos[:, 1]], norm_f_g[None, :])
    return out.reshape(bsz, seq, D_MODEL)
```

```python
import functools

import numpy as np
import jax
import jax.numpy as jnp
from jax import lax
from jax.experimental import pallas as pl
from jax.experimental.pallas import tpu as pltpu

MXU_DTYPE = jnp.bfloat16
F32 = jnp.float32

D_MODEL = 1024
N_MEM = 256
EPS = 1e-6
NEG_INF = -1e30
SEL_FORCE = 1e4

R_HEADS = 4
R_DK = 128
R_DV = 256
R_CHUNK = 128
ROPE_BASE = 10000.0

NSA_HEADS = 8
NSA_GROUPS = 2
NSA_HPG = NSA_HEADS // NSA_GROUPS
NSA_DH = 128
CMP_LEN = 32
CMP_STRIDE = 16
SEL_LEN = 64
SEL_TOPK = 16
WINDOW = 512

X_HEADS = 4
X_DH = D_MODEL // X_HEADS

N_EGROUPS = 4
EXP_PER_GROUP = 8
N_EXPERTS = N_EGROUPS * EXP_PER_GROUP
EXP_TOPK = 2
D_EXPERT = 512
MOE_BLOCK = 256

RET_QK = R_HEADS * R_DK
RET_V = R_HEADS * R_DV
NSA_Q = NSA_HEADS * NSA_DH
NSA_KV = NSA_GROUPS * NSA_DH
SPLITS = (RET_QK, RET_QK, RET_V, RET_V, NSA_Q, NSA_KV, NSA_KV, NSA_KV, NSA_KV, NSA_KV, NSA_KV,
          3 * NSA_HEADS, D_MODEL, D_MODEL)

LANES = 128
SUBLANES = 8
VMEM_LIMIT = 56 * 1024 * 1024

PROJ_TM = 512
RET_ROWS = 512
ATT_TQ = 256
ATT_TK = 256
ROW_TM = 512


def _params(*sem):
    return pltpu.CompilerParams(dimension_semantics=sem, vmem_limit_bytes=VMEM_LIMIT)


def _dot(a, b):
    return jnp.dot(a.astype(MXU_DTYPE), b.astype(MXU_DTYPE), preferred_element_type=F32)


def _dot_nt(a, b):
    return lax.dot_general(a.astype(MXU_DTYPE), b.astype(MXU_DTYPE), (((1,), (1,)), ((), ())),
                           preferred_element_type=F32)


def _dot_tn(a, b):
    return lax.dot_general(a.astype(MXU_DTYPE), b.astype(MXU_DTYPE), (((0,), (0,)), ((), ())),
                           preferred_element_type=F32)


def _split3(p):
    hi = p.astype(MXU_DTYPE)
    r1 = p - hi.astype(F32)
    mid = r1.astype(MXU_DTYPE)
    lo = (r1 - mid.astype(F32)).astype(MXU_DTYPE)
    return hi, mid, lo


def _rms(x, g):
    return x * lax.rsqrt(jnp.mean(x * x, axis=-1, keepdims=True) + EPS) * g


_C_RQK = 0
_C_RV = _C_RQK + 2 * RET_QK
_C_RG = _C_RV + RET_V
_C_CKV = _C_RG + RET_V
_C_SK = _C_CKV + 2 * NSA_KV
_C_WK = _C_SK + NSA_KV
_C_NGL = _C_WK + NSA_KV
_C_END = _C_NGL + LANES
_R_NQ = 0
_R_SV = _R_NQ + NSA_Q
_R_WV = _R_SV + NSA_KV
_R_END = _R_WV + NSA_KV


def _proj_kernel(x_ref, g_ref, w_ref, wt_ref, cos_ref, sin_up_ref, sin_dn_ref,
                 qk_ref, rv_ref, rg_ref, ckv_ref, sk_ref, wk_ref, ngl_ref, nqt_ref, svt_ref, wvt_ref):
    hb = _rms(x_ref[...], g_ref[...]).astype(MXU_DTYPE)
    tm = hb.shape[0]

    def mm(off, width):
        return jnp.dot(hb, w_ref[:, off:off + width], preferred_element_type=F32)

    def mm_t(off, height):
        return _dot_nt(wt_ref[off:off + height, :], hb)

    cos = cos_ref[...]
    sin_up = sin_up_ref[...]
    sin_dn = sin_dn_ref[...]
    qk = mm(_C_RQK, 2 * RET_QK)
    for i in range(2 * R_HEADS):
        t = qk[:, i * R_DK:(i + 1) * R_DK]
        r = t * cos + pltpu.roll(t, R_DK - 1, axis=1) * sin_up + pltpu.roll(t, 1, axis=1) * sin_dn
        if i >= R_HEADS:
            r = r * (R_DK ** -0.5)
        qk_ref[:, i * R_DK:(i + 1) * R_DK] = r.astype(qk_ref.dtype)
    rv_ref[...] = mm(_C_RV, RET_V).astype(rv_ref.dtype)
    rg_ref[...] = mm(_C_RG, RET_V)
    ckv_ref[...] = mm(_C_CKV, 2 * NSA_KV)
    sk_ref[...] = mm(_C_SK, NSA_KV).astype(sk_ref.dtype)
    wk_ref[...] = mm(_C_WK, NSA_KV).astype(wk_ref.dtype)
    ngl_ref[...] = mm(_C_NGL, LANES)

    nqt = mm_t(_R_NQ, NSA_Q).astype(nqt_ref.dtype)
    for g in range(NSA_GROUPS):
        for j in range(tm // ATT_TQ):
            for hh in range(NSA_HPG):
                head = g * NSA_HPG + hh
                nqt_ref[g, j, :, hh * ATT_TQ:(hh + 1) * ATT_TQ] = (
                    nqt[head * NSA_DH:(head + 1) * NSA_DH, j * ATT_TQ:(j + 1) * ATT_TQ])
    for off, out_ref in ((_R_SV, svt_ref), (_R_WV, wvt_ref)):
        vt = mm_t(off, NSA_KV).astype(out_ref.dtype)
        for g in range(NSA_GROUPS):
            for j in range(tm // ATT_TK):
                out_ref[g, j] = vt[g * NSA_DH:(g + 1) * NSA_DH, j * ATT_TK:(j + 1) * ATT_TK]


def _split_w_in(w_in):
    offs = np.cumsum((0,) + SPLITS)
    rq, rk, rv, rg, nq, ck, cv, sk, sv, wk, wv, ngl, ga, gb = [
        w_in[:, offs[i]:offs[i + 1]] for i in range(len(SPLITS))]
    ngl = jnp.pad(ngl, ((0, 0), (0, LANES - ngl.shape[1])))
    w = jnp.concatenate([rq, rk, rv, rg, ck, cv, sk, wk, ngl], axis=1).astype(MXU_DTYPE)
    wt = jnp.concatenate([nq, sv, wv], axis=1).T.astype(MXU_DTYPE)
    w_gates = jnp.concatenate([ga, gb], axis=1).astype(MXU_DTYPE)
    return w, wt, w_gates


def _rope_tables(seq):
    pos = jnp.arange(seq, dtype=F32)
    inv_freq = ROPE_BASE ** (-jnp.arange(0, R_DK, 2, dtype=F32) / R_DK)
    ang = pos[:, None] * inv_freq[None, :]
    cos = jnp.repeat(jnp.cos(ang), 2, axis=1)
    sin = jnp.repeat(jnp.sin(ang), 2, axis=1)
    even = (jnp.arange(R_DK) % 2 == 0)[None, :]
    return cos, jnp.where(even, -sin, 0.0), jnp.where(even, 0.0, sin)


def _proj(x2d, g, w, wt, tables, bsz, seq):
    n = x2d.shape[0]
    tm = PROJ_TM
    nt = seq // tm
    row = lambda i: (i, 0)
    const = lambda i: (0, 0)
    tile = lambda i: (i // nt, 0, i % nt, 0, 0)
    table = pl.BlockSpec((tm, R_DK), lambda i: (i % nt, 0))
    widths = (2 * RET_QK, RET_V, RET_V, 2 * NSA_KV, NSA_KV, NSA_KV, LANES)
    dtypes = (MXU_DTYPE, MXU_DTYPE, F32, F32, MXU_DTYPE, MXU_DTYPE, F32)
    width = NSA_HPG * ATT_TQ
    vt_shape = jax.ShapeDtypeStruct((bsz, NSA_GROUPS, seq // ATT_TK, NSA_DH, ATT_TK), MXU_DTYPE)
    vt_spec = pl.BlockSpec((None, NSA_GROUPS, tm // ATT_TK, NSA_DH, ATT_TK), tile)
    return pl.pallas_call(
        _proj_kernel,
        grid=(n // tm,),
        in_specs=[pl.BlockSpec((tm, D_MODEL), row),
                  pl.BlockSpec((1, D_MODEL), const),
                  pl.BlockSpec((D_MODEL, _C_END), const, pipeline_mode=pl.Buffered(1)),
                  pl.BlockSpec((_R_END, D_MODEL), const, pipeline_mode=pl.Buffered(1)),
                  table, table, table],
        out_specs=[pl.BlockSpec((tm, wd), row) for wd in widths] + [
            pl.BlockSpec((None, NSA_GROUPS, tm // ATT_TQ, NSA_DH, width), tile), vt_spec, vt_spec],
        out_shape=[jax.ShapeDtypeStruct((n, wd), dt) for wd, dt in zip(widths, dtypes)] + [
            jax.ShapeDtypeStruct((bsz, NSA_GROUPS, seq // ATT_TQ, NSA_DH, width), MXU_DTYPE), vt_shape, vt_shape],
        compiler_params=_params("parallel"),
        name="proj",
    )(x2d, g, w, wt, *tables)


def _ret_kernel(qk_ref, v_ref, g_ref, decay_ref, zeta_ref, xi_ref, cd_ref, y_ref, state_ref):
    @pl.when(pl.program_id(1) == 0)
    def _():
        state_ref[...] = jnp.zeros_like(state_ref)

    for c in range(RET_ROWS // R_CHUNK):
        rows = slice(c * R_CHUNK, (c + 1) * R_CHUNK)
        for h in range(R_HEADS):
            q = qk_ref[rows, h * R_DK:(h + 1) * R_DK]
            k = qk_ref[rows, RET_QK + h * R_DK:RET_QK + (h + 1) * R_DK]
            v = v_ref[rows, h * R_DV:(h + 1) * R_DV]
            st = state_ref[h]
            scores = _dot_nt(q, k) * decay_ref[h]
            o = _dot(scores, v) + _dot(q, st) * xi_ref[h]
            o = o * lax.rsqrt(jnp.mean(o * o, axis=-1, keepdims=True) + EPS)
            g = g_ref[rows, h * R_DV:(h + 1) * R_DV]
            y_ref[rows, h * R_DV:(h + 1) * R_DV] = (g * jax.nn.sigmoid(g) * o).astype(y_ref.dtype)
            kz = k.astype(F32) * zeta_ref[h]
            state_ref[h] = st * cd_ref[h] + _dot_tn(kz, v)


def _retention(qk, rv, rg, bsz, seq):
    n = qk.shape[0]
    nt = seq // RET_ROWS
    log_g = jnp.log1p(-jnp.exp2(-5.0 - jnp.arange(R_HEADS, dtype=F32)))
    idx = jnp.arange(R_CHUNK, dtype=F32)
    diff = idx[:, None] - idx[None, :]
    decay = jnp.where(diff >= 0, jnp.exp(log_g[:, None, None] * jnp.maximum(diff, 0.0)), 0.0)
    zeta = jnp.exp(log_g[:, None] * (R_CHUNK - 1.0 - idx)[None, :])[:, :, None]
    xi = jnp.exp(log_g[:, None] * (idx + 1.0)[None, :])[:, :, None]
    cd = jnp.exp(log_g * R_CHUNK)[:, None, None]
    row = lambda b, i: (b * nt + i, 0)
    const3 = lambda b, i: (0, 0, 0)
    return pl.pallas_call(
        _ret_kernel,
        grid=(bsz, nt),
        in_specs=[pl.BlockSpec((RET_ROWS, 2 * RET_QK), row),
                  pl.BlockSpec((RET_ROWS, RET_V), row),
                  pl.BlockSpec((RET_ROWS, RET_V), row),
                  pl.BlockSpec((R_HEADS, R_CHUNK, R_CHUNK), const3),
                  pl.BlockSpec((R_HEADS, R_CHUNK, 1), const3),
                  pl.BlockSpec((R_HEADS, R_CHUNK, 1), const3),
                  pl.BlockSpec((R_HEADS, 1, 1), const3)],
        out_specs=pl.BlockSpec((RET_ROWS, RET_V), row),
        out_shape=jax.ShapeDtypeStruct((n, RET_V), MXU_DTYPE),
        scratch_shapes=[pltpu.VMEM((R_HEADS, R_DK, R_DV), F32)],
        compiler_params=_params("parallel", "arbitrary"),
        name="retention",
    )(qk, rv, rg, decay, zeta, xi, cd)


def _compress_kernel(x_ref, pe_ref, w1_ref, w2_ref, w2t_ref, o_ref, ot_ref, buf_ref, *, seq):
    ncp = seq // CMP_STRIDE
    buf_ref[0:seq, :] = x_ref[...]
    buf_ref[seq:seq + LANES, :] = jnp.zeros((LANES, NSA_DH), F32)
    acc = jnp.zeros((ncp, NSA_DH), F32)
    for l in range(CMP_LEN):
        xl = buf_ref[pl.ds(l, ncp, stride=CMP_STRIDE), :] + pe_ref[l:l + 1, :]
        acc = acc + _dot(xl, w1_ref[l])
    hid = jax.nn.gelu(acc)
    o_ref[...] = _dot(hid, w2_ref[...]).astype(o_ref.dtype)
    ot_ref[...] = _dot_nt(w2t_ref[...], hid).astype(ot_ref.dtype)


def _compress(ckv, pe, w1, w2, w2t, bsz, seq):
    ncp = seq // CMP_STRIDE
    nj = 2 * NSA_GROUPS
    wsel = lambda b, j: (j // NSA_GROUPS, 0, 0)
    return pl.pallas_call(
        functools.partial(_compress_kernel, seq=seq),
        grid=(bsz, nj),
        in_specs=[pl.BlockSpec((seq, NSA_DH), lambda b, j: (b, j)),
                  pl.BlockSpec((None, CMP_LEN, NSA_DH), wsel),
                  pl.BlockSpec((None, CMP_LEN, NSA_DH, NSA_DH), lambda b, j: (j // NSA_GROUPS, 0, 0, 0)),
                  pl.BlockSpec((None, NSA_DH, NSA_DH), wsel),
                  pl.BlockSpec((None, NSA_DH, NSA_DH), wsel)],
        out_specs=[pl.BlockSpec((None, None, ncp, NSA_DH), lambda b, j: (b, j, 0, 0)),
                   pl.BlockSpec((None, None, NSA_DH, ncp), lambda b, j: (b, j, 0, 0))],
        out_shape=[jax.ShapeDtypeStruct((bsz, nj, ncp, NSA_DH), MXU_DTYPE),
                   jax.ShapeDtypeStruct((bsz, nj, NSA_DH, ncp), MXU_DTYPE)],
        scratch_shapes=[pltpu.VMEM((seq + LANES, NSA_DH), F32)],
        compiler_params=_params("parallel", "parallel"),
        name="compress",
    )(ckv, pe, w1, w2, w2t)


def _cmp_attn_kernel(qt_ref, k_ref, vt_ref, ov_ref, o_ref, selt_ref, *, ncp, nb):
    tq = ATT_TQ
    t0 = pl.program_id(2) * tq
    t = t0 + lax.broadcasted_iota(jnp.int32, (ncp, tq), 1)
    n = lax.broadcasted_iota(jnp.int32, (ncp, tq), 0)
    valid = (n * CMP_STRIDE + (CMP_LEN - 1) <= t) & (n < ncp - 1)
    any_valid = (t0 + lax.broadcasted_iota(jnp.int32, (1, tq), 1)) >= CMP_LEN - 1
    valid = jnp.concatenate([valid] * NSA_HPG, axis=1)
    any_valid = jnp.concatenate([any_valid] * NSA_HPG, axis=1)
    s = _dot(k_ref[...], qt_ref[...]) * (NSA_DH ** -0.5)
    s = jnp.where(valid, s, NEG_INF)
    e = jnp.exp(s - jnp.max(s, axis=0, keepdims=True))
    p = e * jnp.where(any_valid, 1.0 / jnp.sum(e, axis=0, keepdims=True), 0.0)
    ot = _dot(vt_ref[...], p)
    for h in range(NSA_HPG):
        o_ref[:, h * NSA_DH:(h + 1) * NSA_DH] = ot[:, h * tq:(h + 1) * tq].T
    psum = sum(p[:, h * tq:(h + 1) * tq] for h in range(NSA_HPG))

    imp = sum(_dot(ov_ref[...], part) for part in _split3(psum))
    j = lax.broadcasted_iota(jnp.int32, (nb, tq), 0)
    tb = (t0 + lax.broadcasted_iota(jnp.int32, (nb, tq), 1)) // SEL_LEN
    forced = (j == 0) | (j == tb) | (j == tb - 1)
    imp = jnp.where(j > tb, -SEL_FORCE, jnp.where(forced, SEL_FORCE, imp))

    sub = SUBLANES
    grp = [imp[r * sub:(r + 1) * sub] for r in range(nb // sub)]
    cnt = [jnp.zeros((sub, tq), F32) for _ in grp]
    for i in range(nb):
        row = jnp.broadcast_to(imp[i:i + 1, :], (sub, tq))
        for r in range(nb // sub):
            if r * sub > i:
                beats = jnp.where(row >= grp[r], 1.0, 0.0)
            elif r * sub + sub - 1 < i:
                beats = jnp.where(row > grp[r], 1.0, 0.0)
            else:
                jr = r * sub + lax.broadcasted_iota(jnp.int32, (sub, tq), 0)
                beats = jnp.where(jr > i, jnp.where(row >= grp[r], 1.0, 0.0), jnp.where(row > grp[r], 1.0, 0.0))
            cnt[r] = cnt[r] + beats
    k_sel = min(SEL_TOPK, nb)
    for r in range(nb // sub):
        selt_ref[r * sub:(r + 1) * sub, :] = jnp.where(cnt[r] < k_sel, 1.0, 0.0)


def _cmp_attention(qt, cmp_k, cmp_vt, bsz, seq):
    n = bsz * seq
    ncp = seq // CMP_STRIDE
    nb = seq // SEL_LEN
    nt = seq // ATT_TQ
    gw = NSA_HPG * NSA_DH
    cstart = np.arange(ncp) * CMP_STRIDE
    jstart = np.arange(nb) * SEL_LEN
    ov = ((cstart[None, :] < jstart[:, None] + SEL_LEN) & (cstart[None, :] + CMP_LEN > jstart[:, None])
          & (np.arange(ncp)[None, :] < ncp - 1))
    ov = jnp.asarray(ov, MXU_DTYPE)
    return pl.pallas_call(
        functools.partial(_cmp_attn_kernel, ncp=ncp, nb=nb),
        grid=(bsz, NSA_GROUPS, nt),
        in_specs=[pl.BlockSpec((None, None, None, NSA_DH, NSA_HPG * ATT_TQ), lambda b, g, i: (b, g, i, 0, 0)),
                  pl.BlockSpec((None, None, ncp, NSA_DH), lambda b, g, i: (b, g, 0, 0)),
                  pl.BlockSpec((None, None, NSA_DH, ncp), lambda b, g, i: (b, NSA_GROUPS + g, 0, 0)),
                  pl.BlockSpec((nb, ncp), lambda b, g, i: (0, 0))],
        out_specs=[pl.BlockSpec((ATT_TQ, gw), lambda b, g, i: (b * nt + i, g)),
                   pl.BlockSpec((None, None, nb, ATT_TQ), lambda b, g, i: (b, g, 0, i))],
        out_shape=[jax.ShapeDtypeStruct((n, NSA_Q), F32),
                   jax.ShapeDtypeStruct((bsz, NSA_GROUPS, nb, seq), F32)],
        compiler_params=_params("parallel", "parallel", "parallel"),
        name="cmp_attention",
    )(qt, cmp_k, cmp_vt, ov)


_EXP2_SCALE = (NSA_DH ** -0.5) * float(np.log2(np.e))
_MASK_ALL = 1 << 28


def _flash_loop(qt_ref, k_ref, vt_ref, acc_ref, s_refs, n_tiles, tile_of, valid_of, max_steps=None):
    width = NSA_HPG * ATT_TQ

    def scores(kt):
        k = k_ref[pl.ds(pl.multiple_of(kt * ATT_TK, ATT_TK), ATT_TK), :]
        return _dot(k, qt_ref[...])

    def half_step(step, s_cur, s_nxt, m, l, prefetch=True):
        kt = tile_of(jnp.minimum(step, n_tiles - 1))
        if prefetch:
            s_nxt[...] = scores(tile_of(jnp.minimum(step + 1, n_tiles - 1)))
        valid = valid_of(kt, kt * ATT_TK + jnp.where(step < n_tiles, 0, _MASK_ALL))
        valid = jnp.concatenate([valid] * NSA_HPG, axis=1)
        s = jnp.where(valid, s_cur[...] * _EXP2_SCALE, NEG_INF)
        m_new = jnp.maximum(m, jnp.max(s, axis=0, keepdims=True))
        alpha = jnp.exp2(m - m_new)
        p = jnp.exp2(s - m_new)
        l = alpha * l + jnp.sum(p, axis=0, keepdims=True)
        acc_ref[...] = alpha * acc_ref[...] + _dot(vt_ref[kt], p)
        return m_new, l

    def body(i, carry):
        m, l = half_step(2 * i, s_refs[0], s_refs[1], *carry)
        return half_step(2 * i + 1, s_refs[1], s_refs[0], m, l)

    acc_ref[...] = jnp.zeros_like(acc_ref)
    s_refs[0][...] = scores(tile_of(0))
    carry = (jnp.full((1, width), NEG_INF, F32), jnp.zeros((1, width), F32))
    if max_steps is None:
        carry = lax.fori_loop(0, (n_tiles + 1) // 2, body, carry)
    else:
        for step in range(max_steps):
            carry = half_step(step, s_refs[step % 2], s_refs[(step + 1) % 2], *carry,
                              prefetch=step + 1 < max_steps)
    return carry[1]


def _flash_finish(o_ref, acc_ref, l):
    inv = 1.0 / l
    for h in range(NSA_HPG):
        cols = slice(h * ATT_TQ, (h + 1) * ATT_TQ)
        o_ref[:, h * NSA_DH:(h + 1) * NSA_DH] = (acc_ref[:, cols] * inv[:, cols]).T


def _sel_attn_kernel(qt_ref, k_ref, vt_ref, selt_ref, o_ref, acc_ref, s0_ref, s1_ref):
    qi = pl.program_id(2)
    tpos = qi * ATT_TQ + lax.broadcasted_iota(jnp.int32, (ATT_TK, ATT_TQ), 1)
    row = lax.broadcasted_iota(jnp.int32, (ATT_TK, ATT_TQ), 0)
    blocks_per_tile = ATT_TK // SEL_LEN

    def valid_of(kt, key0):
        picked = jnp.concatenate(
            [jnp.broadcast_to(selt_ref[pl.ds(kt * blocks_per_tile + jb, 1), :], (SEL_LEN, ATT_TQ))
             for jb in range(blocks_per_tile)], axis=0)
        return (picked > 0.5) & (row + key0 <= tpos)

    l = _flash_loop(qt_ref, k_ref, vt_ref, acc_ref, (s0_ref, s1_ref), (qi + 1) * (ATT_TQ // ATT_TK),
                    lambda step: step, valid_of)
    _flash_finish(o_ref, acc_ref, l)


def _win_attn_kernel(qt_ref, k_ref, vt_ref, o_ref, acc_ref, s0_ref, s1_ref):
    qi = pl.program_id(2)
    tpos = qi * ATT_TQ + lax.broadcasted_iota(jnp.int32, (ATT_TK, ATT_TQ), 1)
    row = lax.broadcasted_iota(jnp.int32, (ATT_TK, ATT_TQ), 0)
    last = (qi + 1) * (ATT_TQ // ATT_TK) - 1
    n_tiles = jnp.minimum(last + 1, (ATT_TQ + WINDOW) // ATT_TK)

    def valid_of(kt, key0):
        kpos = row + key0
        return (kpos <= tpos) & (kpos > tpos - WINDOW)

    l = _flash_loop(qt_ref, k_ref, vt_ref, acc_ref, (s0_ref, s1_ref), n_tiles,
                    lambda step: last - step, valid_of, max_steps=(ATT_TQ + WINDOW) // ATT_TK)
    _flash_finish(o_ref, acc_ref, l)


def _flash_attention(qt, k, vt, selt, bsz, seq):
    n = bsz * seq
    nt = seq // ATT_TQ
    nkt = seq // ATT_TK
    nb = seq // SEL_LEN
    gw = NSA_HPG * NSA_DH
    width = NSA_HPG * ATT_TQ
    in_specs = [pl.BlockSpec((None, None, None, NSA_DH, width), lambda b, g, i: (b, g, i, 0, 0)),
                pl.BlockSpec((seq, NSA_DH), lambda b, g, i: (b, g)),
                pl.BlockSpec((None, None, nkt, NSA_DH, ATT_TK), lambda b, g, i: (b, g, 0, 0, 0))]
    args = [qt, k, vt]
    if selt is None:
        body = _win_attn_kernel
        name = "window_attention"
    else:
        body = _sel_attn_kernel
        name = "selected_attention"
        in_specs.append(pl.BlockSpec((None, None, nb, ATT_TQ), lambda b, g, i: (b, g, 0, i)))
        args.append(selt)
    return pl.pallas_call(
        body,
        grid=(bsz, NSA_GROUPS, nt),
        in_specs=in_specs,
        out_specs=pl.BlockSpec((ATT_TQ, gw), lambda b, g, i: (b * nt + i, g)),
        out_shape=jax.ShapeDtypeStruct((n, NSA_Q), F32),
        scratch_shapes=[pltpu.VMEM((NSA_DH, width), F32),
                        pltpu.VMEM((ATT_TK, width), F32),
                        pltpu.VMEM((ATT_TK, width), F32)],
        compiler_params=_params("parallel", "parallel", "parallel"),
        name=name,
    )(*args)


def _merge_kernel(x_ref, yr_ref, oc_ref, os_ref, ow_ref, ngl_ref, g_ref, wg_ref, wr_ref, wn_ref, wo_ref, o_ref):
    tm = x_ref.shape[0]
    hb = _rms(x_ref[...], g_ref[...]).astype(MXU_DTYPE)
    ga = jnp.dot(hb, wg_ref[:, :D_MODEL], preferred_element_type=F32)
    gb = jnp.dot(hb, wg_ref[:, D_MODEL:], preferred_element_type=F32)
    gates = jax.nn.sigmoid(ngl_ref[...])
    parts = []
    for h in range(NSA_HEADS):
        cols = slice(h * NSA_DH, (h + 1) * NSA_DH)

        def gate(br):
            return jnp.broadcast_to(gates[:, 3 * h + br:3 * h + br + 1], (tm, NSA_DH))

        parts.append(gate(0) * oc_ref[:, cols] + gate(1) * os_ref[:, cols] + gate(2) * ow_ref[:, cols])
    o_nsa = jnp.concatenate(parts, axis=1)
    y_ret = _dot(yr_ref[...], wr_ref[...])
    y_nsa = _dot(o_nsa, wn_ref[...])
    y = jax.nn.sigmoid(ga) * y_ret + jax.nn.sigmoid(gb) * y_nsa
    o_ref[...] = x_ref[...] + _dot(y, wo_ref[...])


def _merge(x2d, y_ret, o_cmp, o_sel, o_win, ngl, g_mix, w_gates, w_ret_o, w_nsa_o, w_out):
    n = x2d.shape[0]
    tm = ROW_TM
    row = lambda i: (i, 0)
    const = lambda i: (0, 0)
    wide = pl.BlockSpec((tm, D_MODEL), row)
    wspec = pl.BlockSpec((D_MODEL, D_MODEL), const)
    return pl.pallas_call(
        _merge_kernel,
        grid=(n // tm,),
        in_specs=[wide, wide, wide, wide, wide,
                  pl.BlockSpec((tm, LANES), row),
                  pl.BlockSpec((1, D_MODEL), const),
                  pl.BlockSpec((D_MODEL, 2 * D_MODEL), const),
                  wspec, wspec, wspec],
        out_specs=wide,
        out_shape=jax.ShapeDtypeStruct((n, D_MODEL), F32),
        compiler_params=_params("parallel"),
        name="merge",
    )(x2d, y_ret, o_cmp, o_sel, o_win, ngl, g_mix, w_gates, w_ret_o, w_nsa_o, w_out)


def _mem_kv_kernel(m_ref, g_ref, w_ref, o_ref):
    o_ref[...] = _dot(_rms(m_ref[...], g_ref[...]), w_ref[...]).astype(o_ref.dtype)


def _mem_kv(mem2d, g, w_xkv, bsz):
    nm = mem2d.shape[0] // bsz
    return pl.pallas_call(
        _mem_kv_kernel,
        grid=(bsz,),
        in_specs=[pl.BlockSpec((nm, D_MODEL), lambda b: (b, 0)),
                  pl.BlockSpec((1, D_MODEL), lambda b: (0, 0)),
                  pl.BlockSpec((D_MODEL, 2 * D_MODEL), lambda b: (0, 0))],
        out_specs=pl.BlockSpec((nm, 2 * D_MODEL), lambda b: (b, 0)),
        out_shape=jax.ShapeDtypeStruct((mem2d.shape[0], 2 * D_MODEL), MXU_DTYPE),
        compiler_params=_params("parallel"),
        name="mem_kv",
    )(mem2d, g, w_xkv)


def _pack_pairs(x):
    half = x.shape[1] // 2
    hi = lax.bitcast_convert_type(x[:, :half].astype(MXU_DTYPE).astype(F32), jnp.uint32)
    lo = lax.bitcast_convert_type(x[:, half:].astype(MXU_DTYPE).astype(F32), jnp.uint32)
    return (hi & jnp.uint32(0xFFFF0000)) | (lo >> 16)


def _unpack_pairs(u):
    hi = lax.bitcast_convert_type(u & jnp.uint32(0xFFFF0000), F32)
    lo = lax.bitcast_convert_type(u << 16, F32)
    return jnp.concatenate([hi, lo], axis=1)


_ROUTER_E0 = 2 * SUBLANES
_ROUTER_ROWS = _ROUTER_E0 + N_EXPERTS


def _top2_route(lgt):
    sub = SUBLANES
    t = lgt.shape[1]
    rowid = lax.broadcasted_iota(jnp.int32, (sub, t), 0)
    first = lambda hit: jnp.min(jnp.where(hit, rowid, sub), axis=0, keepdims=True)
    lg = jnp.where(rowid < N_EGROUPS, lgt[0:sub], NEG_INF)
    gmax = jnp.max(lg, axis=0, keepdims=True)
    grp = first(lg == gmax)
    g_gate = 1.0 / jnp.sum(jnp.exp(lg - gmax), axis=0, keepdims=True)
    experts_of = lambda g: lgt[_ROUTER_E0 + g * EXP_PER_GROUP:_ROUTER_E0 + (g + 1) * EXP_PER_GROUP]
    le = experts_of(N_EGROUPS - 1)
    for g in range(N_EGROUPS - 2, -1, -1):
        le = jnp.where(grp == g, experts_of(g), le)
    ex = jnp.exp(le - jnp.max(le, axis=0, keepdims=True))
    pe = ex / jnp.sum(ex, axis=0, keepdims=True)
    p0 = jnp.max(pe, axis=0, keepdims=True)
    i0 = first(pe == p0)
    rest = jnp.where(rowid == i0, -1.0, pe)
    p1 = jnp.max(rest, axis=0, keepdims=True)
    i1 = first(rest == p1)
    den = p0 + p1
    base = grp * EXP_PER_GROUP
    return jnp.concatenate([(base + i0).astype(F32), (base + i1).astype(F32),
                            g_gate * p0 / den, g_gate * p1 / den], axis=0)


def _cross_kernel(x_ref, kv_ref, gx_ref, wq_ref, wo_ref, gf_ref, wr_ref, br_ref, x2_ref, hf_ref, rt_ref):
    x = x_ref[...]
    q = _dot(_rms(x, gx_ref[...]), wq_ref[...])
    heads = []
    for h in range(X_HEADS):
        k = kv_ref[:, h * X_DH:(h + 1) * X_DH]
        v = kv_ref[:, D_MODEL + h * X_DH:D_MODEL + (h + 1) * X_DH]
        s = _dot_nt(q[:, h * X_DH:(h + 1) * X_DH], k) * (X_DH ** -0.5)
        e = jnp.exp(s - jnp.max(s, axis=-1, keepdims=True))
        p = e / jnp.sum(e, axis=-1, keepdims=True)
        heads.append(_dot(p, v))
    x2 = x + _dot(jnp.concatenate(heads, axis=1), wo_ref[...])
    x2_ref[...] = x2
    hf = _rms(x2, gf_ref[...])
    hf_ref[...] = _pack_pairs(hf)
    h_hi, h_mid, _ = _split3(hf)
    w_hi = wr_ref[0]
    w_mid = wr_ref[1]
    lgt = (_dot_nt(w_hi, h_hi) + (_dot_nt(w_hi, h_mid) + _dot_nt(w_mid, h_hi))) + br_ref[...]
    rt_ref[...] = jnp.concatenate([_top2_route(lgt), jnp.zeros((SUBLANES - 4, lgt.shape[1]), F32)], axis=0)


def _cross(x1, kv, gx, w_xq, w_xo, gf, w_router, b_router, bsz, seq):
    n = x1.shape[0]
    tm = ROW_TM
    nt = seq // tm
    nm = kv.shape[0] // bsz
    row = lambda i: (i, 0)
    const = lambda i: (0, 0)
    vec = pl.BlockSpec((1, D_MODEL), const)
    wspec = pl.BlockSpec((D_MODEL, D_MODEL), const)
    return pl.pallas_call(
        _cross_kernel,
        grid=(n // tm,),
        in_specs=[pl.BlockSpec((tm, D_MODEL), row),
                  pl.BlockSpec((nm, 2 * D_MODEL), lambda i: (i // nt, 0)),
                  vec, wspec, wspec, vec,
                  pl.BlockSpec((2, _ROUTER_ROWS, D_MODEL), lambda i: (0, 0, 0)),
                  pl.BlockSpec((_ROUTER_ROWS, 1), const)],
        out_specs=[pl.BlockSpec((tm, D_MODEL), row),
                   pl.BlockSpec((tm, D_MODEL // 2), row),
                   pl.BlockSpec((SUBLANES, tm), lambda i: (0, i))],
        out_shape=[jax.ShapeDtypeStruct((n, D_MODEL), F32),
                   jax.ShapeDtypeStruct((n, D_MODEL // 2), jnp.uint32),
                   jax.ShapeDtypeStruct((SUBLANES, n), F32)],
        compiler_params=_params("parallel"),
        name="cross_attention",
    )(x1, kv, gx, w_xq, w_xo, gf, w_router, b_router)


def _expert_kernel(blk_ref, xb_ref, wt_ref, w1_ref, w3_ref, w2_ref, o_ref):
    n_used = blk_ref[pl.num_programs(0)]

    @pl.when(pl.program_id(0) < n_used)
    def _():
        xb = _unpack_pairs(xb_ref[...]).astype(MXU_DTYPE)
        a = _dot(xb, w1_ref[...])
        hmid = a * jax.nn.sigmoid(a) * _dot(xb, w3_ref[...])
        o_ref[...] = _dot(hmid, w2_ref[...]) * wt_ref[...]

    @pl.when(pl.program_id(0) >= n_used)
    def _():
        o_ref[...] = jnp.zeros_like(o_ref)


def _experts(blk, xb, wts, w1, w3, w2):
    cap = xb.shape[0]
    nblk = cap // MOE_BLOCK
    row = lambda i, e: (i, 0)
    grid_spec = pltpu.PrefetchScalarGridSpec(
        num_scalar_prefetch=1,
        grid=(nblk,),
        in_specs=[pl.BlockSpec((MOE_BLOCK, D_MODEL // 2), row),
                  pl.BlockSpec((MOE_BLOCK, 1), row),
                  pl.BlockSpec((None, D_MODEL, D_EXPERT), lambda i, e: (e[i], 0, 0)),
                  pl.BlockSpec((None, D_MODEL, D_EXPERT), lambda i, e: (e[i], 0, 0)),
                  pl.BlockSpec((None, D_EXPERT, D_MODEL), lambda i, e: (e[i], 0, 0))],
        out_specs=pl.BlockSpec((MOE_BLOCK, D_MODEL), row),
    )
    return pl.pallas_call(
        _expert_kernel,
        grid_spec=grid_spec,
        out_shape=jax.ShapeDtypeStruct((cap, D_MODEL), F32),
        compiler_params=_params("arbitrary"),
        name="experts",
    )(blk, xb, wts, w1, w3, w2)


def _final_kernel(x_ref, y0_ref, y1_ref, g_ref, o_ref):
    o_ref[...] = _rms(x_ref[...] + (y0_ref[...] + y1_ref[...]), g_ref[...])


def _final(x2, y0, y1, g):
    n = x2.shape[0]
    tm = ROW_TM
    row = lambda i: (i, 0)
    wide = pl.BlockSpec((tm, D_MODEL), row)
    return pl.pallas_call(
        _final_kernel,
        grid=(n // tm,),
        in_specs=[wide, wide, wide, pl.BlockSpec((1, D_MODEL), lambda i: (0, 0))],
        out_specs=wide,
        out_shape=jax.ShapeDtypeStruct((n, D_MODEL), F32),
        compiler_params=_params("parallel"),
        name="final_norm",
    )(x2, y0, y1, g)


def _route(routed, n_tok):
    eid = routed[0:EXP_TOPK].T.reshape(-1).astype(jnp.int32)
    wflat = routed[EXP_TOPK:2 * EXP_TOPK].T.reshape(-1)
    n_asg = n_tok * EXP_TOPK
    iota = jnp.arange(n_asg, dtype=jnp.int32)
    se, order = lax.sort_key_val(eid, iota)
    counts = jnp.sum((jnp.arange(N_EXPERTS, dtype=jnp.int32)[:, None] == eid[None, :]).astype(jnp.int32), axis=1)
    padded = (counts + MOE_BLOCK - 1) // MOE_BLOCK * MOE_BLOCK
    starts = jnp.cumsum(counts) - counts
    pends = jnp.cumsum(padded)
    pstarts = pends - padded
    cap = ((n_asg + MOE_BLOCK - 1) // MOE_BLOCK + N_EXPERTS) * MOE_BLOCK
    nblk = cap // MOE_BLOCK
    blk_e = jnp.minimum(jnp.searchsorted(pends, jnp.arange(nblk) * MOE_BLOCK, side='right', method='compare_all'),
                        N_EXPERTS - 1).astype(jnp.int32)
    per_row = lambda a: jnp.repeat(a[blk_e], MOE_BLOCK)
    off = jnp.arange(cap, dtype=jnp.int32) - per_row(pstarts)
    live = off < per_row(counts)
    asg = order[jnp.clip(per_row(starts) + off, 0, n_asg - 1)]
    buf_tok = jnp.where(live, asg // EXP_TOPK, 0)
    buf_w = jnp.where(live, wflat[asg], 0.0)
    dest_sorted = iota + (pstarts - starts)[se]
    _, pos = lax.sort_key_val(order, dest_sorted)
    n_used = (pends[-1] // MOE_BLOCK).astype(jnp.int32)
    return buf_tok, buf_w, jnp.concatenate([blk_e, n_used[None]]), pos


def kernel(x, mem, norm_mix_g, w_in, w_ret_o, w_nsa_o, w_out, cmp_pe_k, cmp_w1_k, cmp_w2_k, cmp_pe_v,
           cmp_w1_v, cmp_w2_v, norm_x_g, norm_mem_g, w_xq, w_xkv, w_xo, norm_ffn_g, w_grp, b_grp, w_rt,
           b_rt, w_e1, w_e3, w_e2, norm_f_g):
    bsz, seq, _ = x.shape
    n = bsz * seq
    assert seq % PROJ_TM == 0 and seq % (2 * ATT_TQ) == 0 and w_in.shape[0] == 1
    cast = lambda a: a.astype(MXU_DTYPE)
    xc = x.reshape(n, D_MODEL)
    l = 0

    w_main, w_t, w_gates = _split_w_in(w_in[l])
    qk, rv, rg, ckv, sk, wk, ngl, qt, svt, wvt = _proj(
        xc, norm_mix_g[l][None, :], w_main, w_t, _rope_tables(seq), bsz, seq)
    y_ret = _retention(qk, rv, rg, bsz, seq)
    w2 = jnp.stack([cmp_w2_k[l], cmp_w2_v[l]])
    cmp_k, cmp_vt = _compress(ckv, jnp.stack([cmp_pe_k[l], cmp_pe_v[l]]),
                              cast(jnp.stack([cmp_w1_k[l], cmp_w1_v[l]])),
                              cast(w2), cast(w2.transpose(0, 2, 1)), bsz, seq)
    o_cmp, selt = _cmp_attention(qt, cmp_k, cmp_vt, bsz, seq)
    o_sel = _flash_attention(qt, sk, svt, selt, bsz, seq)
    o_win = _flash_attention(qt, wk, wvt, None, bsz, seq)
    x1 = _merge(xc, y_ret, o_cmp, o_sel, o_win, ngl, norm_mix_g[l][None, :], w_gates,
                cast(w_ret_o[l]), cast(w_nsa_o[l]), cast(w_out[l]))

    kv = _mem_kv(mem.reshape(-1, D_MODEL), norm_mem_g[l][None, :], cast(w_xkv[l]), bsz)
    gap = _ROUTER_E0 - N_EGROUPS
    w_router = jnp.concatenate([w_grp[l].T, jnp.zeros((gap, D_MODEL), F32), w_rt[l].T], axis=0)
    wr_hi = cast(w_router)
    wr_mid = cast(w_router - wr_hi.astype(F32))
    b_router = jnp.concatenate([b_grp[l], jnp.zeros((gap,), F32), b_rt[l]])[:, None]
    x2, hf, routed = _cross(x1, kv, norm_x_g[l][None, :], cast(w_xq[l]), cast(w_xo[l]),
                            norm_ffn_g[l][None, :], jnp.stack([wr_hi, wr_mid]), b_router, bsz, seq)

    buf_tok, buf_w, blk, pos = _route(routed, n)
    ys = _experts(blk, hf[buf_tok], buf_w[:, None], w_e1[l], w_e3[l], w_e2[l])
    pos = pos.reshape(n, EXP_TOPK)
    out = _final(x2, ys[pos[:, 0]], ys[pos[:, 1]], norm_f_g[None, :])
    return out.reshape(bsz, seq, D_MODEL)
```

```python
import functools

import numpy as np
import jax
import jax.numpy as jnp
from jax import lax
from jax.experimental import pallas as pl
from jax.experimental.pallas import tpu as pltpu

MXU_DTYPE = jnp.bfloat16
F32 = jnp.float32

D_MODEL = 1024
N_MEM = 256
EPS = 1e-6
NEG_INF = -1e30
SEL_FORCE = 1e4

R_HEADS = 4
R_DK = 128
R_DV = 256
R_CHUNK = 128
ROPE_BASE = 10000.0

NSA_HEADS = 8
NSA_GROUPS = 2
NSA_HPG = NSA_HEADS // NSA_GROUPS
NSA_DH = 128
CMP_LEN = 32
CMP_STRIDE = 16
SEL_LEN = 64
SEL_TOPK = 16
WINDOW = 512

X_HEADS = 4
X_DH = D_MODEL // X_HEADS

N_EGROUPS = 4
EXP_PER_GROUP = 8
N_EXPERTS = N_EGROUPS * EXP_PER_GROUP
EXP_TOPK = 2
D_EXPERT = 512
MOE_BLOCK = 256

RET_QK = R_HEADS * R_DK
RET_V = R_HEADS * R_DV
NSA_Q = NSA_HEADS * NSA_DH
NSA_KV = NSA_GROUPS * NSA_DH
SPLITS = (RET_QK, RET_QK, RET_V, RET_V, NSA_Q, NSA_KV, NSA_KV, NSA_KV, NSA_KV, NSA_KV, NSA_KV,
          3 * NSA_HEADS, D_MODEL, D_MODEL)

_EXP2_SCALE = (NSA_DH ** -0.5) * float(np.log2(np.e))

LANES = 128
SUBLANES = 8
VMEM_LIMIT = 56 * 1024 * 1024

PROJ_TM = 512
RET_ROWS = 512
ATT_TQ = 256
ATT_TK = 256
ROW_TM = 512
MOE_SPLITS = 2


def _params(*sem):
    return pltpu.CompilerParams(dimension_semantics=sem, vmem_limit_bytes=VMEM_LIMIT)


def _dot(a, b):
    return jnp.dot(a.astype(MXU_DTYPE), b.astype(MXU_DTYPE), preferred_element_type=F32)


def _dot_nt(a, b):
    return lax.dot_general(a.astype(MXU_DTYPE), b.astype(MXU_DTYPE), (((1,), (1,)), ((), ())),
                           preferred_element_type=F32)


def _dot_tn(a, b):
    return lax.dot_general(a.astype(MXU_DTYPE), b.astype(MXU_DTYPE), (((0,), (0,)), ((), ())),
                           preferred_element_type=F32)


def _split3(p):
    hi = p.astype(MXU_DTYPE)
    r1 = p - hi.astype(F32)
    mid = r1.astype(MXU_DTYPE)
    lo = (r1 - mid.astype(F32)).astype(MXU_DTYPE)
    return hi, mid, lo


def _rms(x, g):
    return x * lax.rsqrt(jnp.mean(x * x, axis=-1, keepdims=True) + EPS) * g


_C_RQK = 0
_C_RV = _C_RQK + 2 * RET_QK
_C_RG = _C_RV + RET_V
_C_CKV = _C_RG + RET_V
_C_SK = _C_CKV + 2 * NSA_KV
_C_WK = _C_SK + NSA_KV
_C_NGL = _C_WK + NSA_KV
_C_END = _C_NGL + LANES
_R_NQ = 0
_R_SV = _R_NQ + NSA_Q
_R_WV = _R_SV + NSA_KV
_R_END = _R_WV + NSA_KV


def _proj_kernel(x_ref, g_ref, w_ref, wt_ref, cos_ref, sin_up_ref, sin_dn_ref,
                 qk_ref, rv_ref, rg_ref, ckv_ref, sk_ref, wk_ref, ngl_ref, nqt_ref, svt_ref, wvt_ref):
    hb = _rms(x_ref[...], g_ref[...]).astype(MXU_DTYPE)
    tm = hb.shape[0]

    def mm(off, width):
        return jnp.dot(hb, w_ref[:, off:off + width], preferred_element_type=F32)

    def mm_t(off, height):
        return _dot_nt(wt_ref[off:off + height, :], hb)

    cos = cos_ref[...]
    sin_up = sin_up_ref[...]
    sin_dn = sin_dn_ref[...]
    qk = mm(_C_RQK, 2 * RET_QK)
    for i in range(2 * R_HEADS):
        t = qk[:, i * R_DK:(i + 1) * R_DK]
        r = t * cos + pltpu.roll(t, R_DK - 1, axis=1) * sin_up + pltpu.roll(t, 1, axis=1) * sin_dn
        if i >= R_HEADS:
            r = r * (R_DK ** -0.5)
        qk_ref[:, i * R_DK:(i + 1) * R_DK] = r.astype(qk_ref.dtype)
    rv_ref[...] = mm(_C_RV, RET_V).astype(rv_ref.dtype)
    rg_ref[...] = mm(_C_RG, RET_V)
    ckv_ref[...] = mm(_C_CKV, 2 * NSA_KV)
    sk_ref[...] = mm(_C_SK, NSA_KV).astype(sk_ref.dtype)
    wk_ref[...] = mm(_C_WK, NSA_KV).astype(wk_ref.dtype)
    ngl_ref[...] = mm(_C_NGL, LANES)

    nqt = (mm_t(_R_NQ, NSA_Q) * _EXP2_SCALE).astype(nqt_ref.dtype)
    for g in range(NSA_GROUPS):
        for j in range(tm // ATT_TQ):
            for hh in range(NSA_HPG):
                head = g * NSA_HPG + hh
                nqt_ref[g, j, :, hh * ATT_TQ:(hh + 1) * ATT_TQ] = (
                    nqt[head * NSA_DH:(head + 1) * NSA_DH, j * ATT_TQ:(j + 1) * ATT_TQ])
    for off, out_ref in ((_R_SV, svt_ref), (_R_WV, wvt_ref)):
        vt = mm_t(off, NSA_KV).astype(out_ref.dtype)
        for g in range(NSA_GROUPS):
            for j in range(tm // ATT_TK):
                out_ref[g, j] = vt[g * NSA_DH:(g + 1) * NSA_DH, j * ATT_TK:(j + 1) * ATT_TK]


def _split_w_in(w_in):
    offs = np.cumsum((0,) + SPLITS)
    rq, rk, rv, rg, nq, ck, cv, sk, sv, wk, wv, ngl, ga, gb = [
        w_in[:, offs[i]:offs[i + 1]] for i in range(len(SPLITS))]
    ngl = jnp.pad(ngl, ((0, 0), (0, LANES - ngl.shape[1])))
    w = jnp.concatenate([rq, rk, rv, rg, ck, cv, sk, wk, ngl], axis=1).astype(MXU_DTYPE)
    wt = jnp.concatenate([nq, sv, wv], axis=1).T.astype(MXU_DTYPE)
    w_gates = jnp.concatenate([ga, gb], axis=1).astype(MXU_DTYPE)
    return w, wt, w_gates


def _rope_tables(seq):
    pos = jnp.arange(seq, dtype=F32)
    inv_freq = ROPE_BASE ** (-jnp.arange(0, R_DK, 2, dtype=F32) / R_DK)
    ang = pos[:, None] * inv_freq[None, :]
    cos = jnp.repeat(jnp.cos(ang), 2, axis=1)
    sin = jnp.repeat(jnp.sin(ang), 2, axis=1)
    even = (jnp.arange(R_DK) % 2 == 0)[None, :]
    return cos, jnp.where(even, -sin, 0.0), jnp.where(even, 0.0, sin)


def _proj(x2d, g, w, wt, tables, bsz, seq):
    n = x2d.shape[0]
    tm = PROJ_TM
    nt = seq // tm
    row = lambda i: (i, 0)
    const = lambda i: (0, 0)
    tile = lambda i: (i // nt, 0, i % nt, 0, 0)
    table = pl.BlockSpec((tm, R_DK), lambda i: (i % nt, 0))
    widths = (2 * RET_QK, RET_V, RET_V, 2 * NSA_KV, NSA_KV, NSA_KV, LANES)
    dtypes = (MXU_DTYPE, MXU_DTYPE, F32, F32, MXU_DTYPE, MXU_DTYPE, F32)
    width = NSA_HPG * ATT_TQ
    vt_shape = jax.ShapeDtypeStruct((bsz, NSA_GROUPS, seq // ATT_TK, NSA_DH, ATT_TK), MXU_DTYPE)
    vt_spec = pl.BlockSpec((None, NSA_GROUPS, tm // ATT_TK, NSA_DH, ATT_TK), tile)
    return pl.pallas_call(
        _proj_kernel,
        grid=(n // tm,),
        in_specs=[pl.BlockSpec((tm, D_MODEL), row),
                  pl.BlockSpec((1, D_MODEL), const),
                  pl.BlockSpec((D_MODEL, _C_END), const, pipeline_mode=pl.Buffered(1)),
                  pl.BlockSpec((_R_END, D_MODEL), const, pipeline_mode=pl.Buffered(1)),
                  table, table, table],
        out_specs=[pl.BlockSpec((tm, wd), row) for wd in widths] + [
            pl.BlockSpec((None, NSA_GROUPS, tm // ATT_TQ, NSA_DH, width), tile), vt_spec, vt_spec],
        out_shape=[jax.ShapeDtypeStruct((n, wd), dt) for wd, dt in zip(widths, dtypes)] + [
            jax.ShapeDtypeStruct((bsz, NSA_GROUPS, seq // ATT_TQ, NSA_DH, width), MXU_DTYPE), vt_shape, vt_shape],
        compiler_params=_params("parallel"),
        name="proj",
    )(x2d, g, w, wt, *tables)


def _ret_kernel(qk_ref, v_ref, g_ref, decay_ref, zeta_ref, xi_ref, cd_ref, y_ref, state_ref):
    @pl.when(pl.program_id(1) == 0)
    def _():
        state_ref[...] = jnp.zeros_like(state_ref)

    for c in range(RET_ROWS // R_CHUNK):
        rows = slice(c * R_CHUNK, (c + 1) * R_CHUNK)
        for h in range(R_HEADS):
            q = qk_ref[rows, h * R_DK:(h + 1) * R_DK]
            k = qk_ref[rows, RET_QK + h * R_DK:RET_QK + (h + 1) * R_DK]
            v = v_ref[rows, h * R_DV:(h + 1) * R_DV]
            st = state_ref[h]
            scores = _dot_nt(q, k) * decay_ref[h]
            o = _dot(scores, v) + _dot(q, st) * xi_ref[h]
            o = o * lax.rsqrt(jnp.mean(o * o, axis=-1, keepdims=True) + EPS)
            g = g_ref[rows, h * R_DV:(h + 1) * R_DV]
            y_ref[rows, h * R_DV:(h + 1) * R_DV] = (g * jax.nn.sigmoid(g) * o).astype(y_ref.dtype)
            kz = k.astype(F32) * zeta_ref[h]
            state_ref[h] = st * cd_ref[h] + _dot_tn(kz, v)


def _retention(qk, rv, rg, bsz, seq):
    n = qk.shape[0]
    nt = seq // RET_ROWS
    log_g = jnp.log1p(-jnp.exp2(-5.0 - jnp.arange(R_HEADS, dtype=F32)))
    idx = jnp.arange(R_CHUNK, dtype=F32)
    diff = idx[:, None] - idx[None, :]
    decay = jnp.where(diff >= 0, jnp.exp(log_g[:, None, None] * jnp.maximum(diff, 0.0)), 0.0)
    zeta = jnp.exp(log_g[:, None] * (R_CHUNK - 1.0 - idx)[None, :])[:, :, None]
    xi = jnp.exp(log_g[:, None] * (idx + 1.0)[None, :])[:, :, None]
    cd = jnp.exp(log_g * R_CHUNK)[:, None, None]
    row = lambda b, i: (b * nt + i, 0)
    const3 = lambda b, i: (0, 0, 0)
    return pl.pallas_call(
        _ret_kernel,
        grid=(bsz, nt),
        in_specs=[pl.BlockSpec((RET_ROWS, 2 * RET_QK), row),
                  pl.BlockSpec((RET_ROWS, RET_V), row),
                  pl.BlockSpec((RET_ROWS, RET_V), row),
                  pl.BlockSpec((R_HEADS, R_CHUNK, R_CHUNK), const3),
                  pl.BlockSpec((R_HEADS, R_CHUNK, 1), const3),
                  pl.BlockSpec((R_HEADS, R_CHUNK, 1), const3),
                  pl.BlockSpec((R_HEADS, 1, 1), const3)],
        out_specs=pl.BlockSpec((RET_ROWS, RET_V), row),
        out_shape=jax.ShapeDtypeStruct((n, RET_V), MXU_DTYPE),
        scratch_shapes=[pltpu.VMEM((R_HEADS, R_DK, R_DV), F32)],
        compiler_params=_params("parallel", "arbitrary"),
        name="retention",
    )(qk, rv, rg, decay, zeta, xi, cd)


def _compress_kernel(x_ref, pe_ref, w1_ref, w2_ref, w2t_ref, o_ref, ot_ref, buf_ref, *, seq):
    ncp = seq // CMP_STRIDE
    buf_ref[0:seq, :] = x_ref[...]
    buf_ref[seq:seq + LANES, :] = jnp.zeros((LANES, NSA_DH), F32)
    acc = jnp.zeros((ncp, NSA_DH), F32)
    for l in range(CMP_LEN):
        xl = buf_ref[pl.ds(l, ncp, stride=CMP_STRIDE), :] + pe_ref[l:l + 1, :]
        acc = acc + _dot(xl, w1_ref[l])
    hid = jax.nn.gelu(acc)
    o_ref[...] = _dot(hid, w2_ref[...]).astype(o_ref.dtype)
    ot_ref[...] = _dot_nt(w2t_ref[...], hid).astype(ot_ref.dtype)


def _compress(ckv, pe, w1, w2, w2t, bsz, seq):
    ncp = seq // CMP_STRIDE
    nj = 2 * NSA_GROUPS
    wsel = lambda b, j: (j // NSA_GROUPS, 0, 0)
    return pl.pallas_call(
        functools.partial(_compress_kernel, seq=seq),
        grid=(bsz, nj),
        in_specs=[pl.BlockSpec((seq, NSA_DH), lambda b, j: (b, j)),
                  pl.BlockSpec((None, CMP_LEN, NSA_DH), wsel),
                  pl.BlockSpec((None, CMP_LEN, NSA_DH, NSA_DH), lambda b, j: (j // NSA_GROUPS, 0, 0, 0)),
                  pl.BlockSpec((None, NSA_DH, NSA_DH), wsel),
                  pl.BlockSpec((None, NSA_DH, NSA_DH), wsel)],
        out_specs=[pl.BlockSpec((None, None, ncp, NSA_DH), lambda b, j: (b, j, 0, 0)),
                   pl.BlockSpec((None, None, NSA_DH, ncp), lambda b, j: (b, j, 0, 0))],
        out_shape=[jax.ShapeDtypeStruct((bsz, nj, ncp, NSA_DH), MXU_DTYPE),
                   jax.ShapeDtypeStruct((bsz, nj, NSA_DH, ncp), MXU_DTYPE)],
        scratch_shapes=[pltpu.VMEM((seq + LANES, NSA_DH), F32)],
        compiler_params=_params("parallel", "parallel"),
        name="compress",
    )(ckv, pe, w1, w2, w2t)


def _cmp_attn_kernel(qt_ref, k_ref, vt_ref, ov_ref, o_ref, selt_ref, *, ncp, nb):
    tq = ATT_TQ
    t0 = pl.program_id(2) * tq
    t = t0 + lax.broadcasted_iota(jnp.int32, (ncp, tq), 1)
    n = lax.broadcasted_iota(jnp.int32, (ncp, tq), 0)
    valid = (n * CMP_STRIDE + (CMP_LEN - 1) <= t) & (n < ncp - 1)
    any_valid = (t0 + lax.broadcasted_iota(jnp.int32, (1, tq), 1)) >= CMP_LEN - 1
    valid = jnp.concatenate([valid] * NSA_HPG, axis=1)
    any_valid = jnp.concatenate([any_valid] * NSA_HPG, axis=1)
    s = jnp.where(valid, _dot(k_ref[...], qt_ref[...]), NEG_INF)
    e = jnp.exp2(s - jnp.max(s, axis=0, keepdims=True))
    p = e * jnp.where(any_valid, 1.0 / jnp.sum(e, axis=0, keepdims=True), 0.0)
    ot = _dot(vt_ref[...], p)
    for h in range(NSA_HPG):
        o_ref[:, h * NSA_DH:(h + 1) * NSA_DH] = ot[:, h * tq:(h + 1) * tq].T
    psum = sum(p[:, h * tq:(h + 1) * tq] for h in range(NSA_HPG))

    imp = sum(_dot(ov_ref[...], part) for part in _split3(psum))
    j = lax.broadcasted_iota(jnp.int32, (nb, tq), 0)
    tb = (t0 + lax.broadcasted_iota(jnp.int32, (nb, tq), 1)) // SEL_LEN
    forced = (j == 0) | (j == tb) | (j == tb - 1)
    imp = jnp.where(j > tb, -SEL_FORCE, jnp.where(forced, SEL_FORCE, imp))

    sub = SUBLANES
    grp = [imp[r * sub:(r + 1) * sub] for r in range(nb // sub)]
    cnt = [jnp.zeros((sub, tq), F32) for _ in grp]
    for i in range(nb):
        row = jnp.broadcast_to(imp[i:i + 1, :], (sub, tq))
        for r in range(nb // sub):
            if r * sub > i:
                beats = jnp.where(row >= grp[r], 1.0, 0.0)
            elif r * sub + sub - 1 < i:
                beats = jnp.where(row > grp[r], 1.0, 0.0)
            else:
                jr = r * sub + lax.broadcasted_iota(jnp.int32, (sub, tq), 0)
                beats = jnp.where(jr > i, jnp.where(row >= grp[r], 1.0, 0.0), jnp.where(row > grp[r], 1.0, 0.0))
            cnt[r] = cnt[r] + beats
    k_sel = min(SEL_TOPK, nb)
    for r in range(nb // sub):
        selt_ref[r * sub:(r + 1) * sub, :] = jnp.where(cnt[r] < k_sel, 1.0, 0.0)


def _cmp_attention(qt, cmp_k, cmp_vt, bsz, seq):
    n = bsz * seq
    ncp = seq // CMP_STRIDE
    nb = seq // SEL_LEN
    nt = seq // ATT_TQ
    gw = NSA_HPG * NSA_DH
    cstart = np.arange(ncp) * CMP_STRIDE
    jstart = np.arange(nb) * SEL_LEN
    ov = ((cstart[None, :] < jstart[:, None] + SEL_LEN) & (cstart[None, :] + CMP_LEN > jstart[:, None])
          & (np.arange(ncp)[None, :] < ncp - 1))
    ov = jnp.asarray(ov, MXU_DTYPE)
    return pl.pallas_call(
        functools.partial(_cmp_attn_kernel, ncp=ncp, nb=nb),
        grid=(bsz, NSA_GROUPS, nt),
        in_specs=[pl.BlockSpec((None, None, None, NSA_DH, NSA_HPG * ATT_TQ), lambda b, g, i: (b, g, i, 0, 0)),
                  pl.BlockSpec((None, None, ncp, NSA_DH), lambda b, g, i: (b, g, 0, 0)),
                  pl.BlockSpec((None, None, NSA_DH, ncp), lambda b, g, i: (b, NSA_GROUPS + g, 0, 0)),
                  pl.BlockSpec((nb, ncp), lambda b, g, i: (0, 0))],
        out_specs=[pl.BlockSpec((ATT_TQ, gw), lambda b, g, i: (b * nt + i, g)),
                   pl.BlockSpec((None, None, nb, ATT_TQ), lambda b, g, i: (b, g, 0, i))],
        out_shape=[jax.ShapeDtypeStruct((n, NSA_Q), F32),
                   jax.ShapeDtypeStruct((bsz, NSA_GROUPS, nb, seq), F32)],
        compiler_params=_params("parallel", "parallel", "parallel"),
        name="cmp_attention",
    )(qt, cmp_k, cmp_vt, ov)


_MASK_ALL = 1 << 28


def _flash_loop(qt_ref, k_ref, vt_ref, acc_ref, s_refs, n_tiles, tile_of, valid_of, max_steps=None):
    width = NSA_HPG * ATT_TQ

    def scores(kt):
        k = k_ref[pl.ds(pl.multiple_of(kt * ATT_TK, ATT_TK), ATT_TK), :]
        return _dot(k, qt_ref[...])

    def half_step(step, s_cur, s_nxt, m, l, prefetch=True):
        kt = tile_of(jnp.minimum(step, n_tiles - 1))
        if prefetch:
            s_nxt[...] = scores(tile_of(jnp.minimum(step + 1, n_tiles - 1)))
        valid = valid_of(kt, kt * ATT_TK + jnp.where(step < n_tiles, 0, _MASK_ALL))
        valid = jnp.concatenate([valid] * NSA_HPG, axis=1)
        s = jnp.where(valid, s_cur[...], NEG_INF)
        m_new = jnp.maximum(m, jnp.max(s, axis=0, keepdims=True))
        alpha = jnp.exp2(m - m_new)
        p = jnp.exp2(s - m_new)
        l = alpha * l + jnp.sum(p, axis=0, keepdims=True)
        acc_ref[...] = alpha * acc_ref[...] + _dot(vt_ref[kt], p)
        return m_new, l

    def body(i, carry):
        m, l = half_step(2 * i, s_refs[0], s_refs[1], *carry)
        return half_step(2 * i + 1, s_refs[1], s_refs[0], m, l)

    acc_ref[...] = jnp.zeros_like(acc_ref)
    s_refs[0][...] = scores(tile_of(0))
    carry = (jnp.full((1, width), NEG_INF, F32), jnp.zeros((1, width), F32))
    if max_steps is None:
        carry = lax.fori_loop(0, (n_tiles + 1) // 2, body, carry)
    else:
        for step in range(max_steps):
            carry = half_step(step, s_refs[step % 2], s_refs[(step + 1) % 2], *carry,
                              prefetch=step + 1 < max_steps)
    return carry[1]


def _flash_finish(o_ref, acc_ref, l):
    inv = 1.0 / l
    for h in range(NSA_HPG):
        cols = slice(h * ATT_TQ, (h + 1) * ATT_TQ)
        o_ref[:, h * NSA_DH:(h + 1) * NSA_DH] = (acc_ref[:, cols] * inv[:, cols]).T


def _sel_attn_kernel(qt_ref, k_ref, vt_ref, selt_ref, o_ref, acc_ref, s0_ref, s1_ref):
    qi = pl.program_id(2)
    tpos = qi * ATT_TQ + lax.broadcasted_iota(jnp.int32, (ATT_TK, ATT_TQ), 1)
    row = lax.broadcasted_iota(jnp.int32, (ATT_TK, ATT_TQ), 0)
    blocks_per_tile = ATT_TK // SEL_LEN

    def valid_of(kt, key0):
        picked = jnp.concatenate(
            [jnp.broadcast_to(selt_ref[pl.ds(kt * blocks_per_tile + jb, 1), :], (SEL_LEN, ATT_TQ))
             for jb in range(blocks_per_tile)], axis=0)
        return (picked > 0.5) & (row + key0 <= tpos)

    l = _flash_loop(qt_ref, k_ref, vt_ref, acc_ref, (s0_ref, s1_ref), (qi + 1) * (ATT_TQ // ATT_TK),
                    lambda step: step, valid_of)
    _flash_finish(o_ref, acc_ref, l)


def _win_attn_kernel(qt_ref, k_ref, vt_ref, o_ref, acc_ref, s0_ref, s1_ref):
    qi = pl.program_id(2)
    tpos = qi * ATT_TQ + lax.broadcasted_iota(jnp.int32, (ATT_TK, ATT_TQ), 1)
    row = lax.broadcasted_iota(jnp.int32, (ATT_TK, ATT_TQ), 0)
    last = (qi + 1) * (ATT_TQ // ATT_TK) - 1
    n_tiles = jnp.minimum(last + 1, (ATT_TQ + WINDOW) // ATT_TK)

    def valid_of(kt, key0):
        kpos = row + key0
        return (kpos <= tpos) & (kpos > tpos - WINDOW)

    l = _flash_loop(qt_ref, k_ref, vt_ref, acc_ref, (s0_ref, s1_ref), n_tiles,
                    lambda step: last - step, valid_of, max_steps=(ATT_TQ + WINDOW) // ATT_TK)
    _flash_finish(o_ref, acc_ref, l)


def _flash_attention(qt, k, vt, selt, bsz, seq):
    n = bsz * seq
    nt = seq // ATT_TQ
    nkt = seq // ATT_TK
    nb = seq // SEL_LEN
    gw = NSA_HPG * NSA_DH
    width = NSA_HPG * ATT_TQ
    in_specs = [pl.BlockSpec((None, None, None, NSA_DH, width), lambda b, g, i: (b, g, i, 0, 0)),
                pl.BlockSpec((seq, NSA_DH), lambda b, g, i: (b, g)),
                pl.BlockSpec((None, None, nkt, NSA_DH, ATT_TK), lambda b, g, i: (b, g, 0, 0, 0))]
    args = [qt, k, vt]
    if selt is None:
        body = _win_attn_kernel
        name = "window_attention"
    else:
        body = _sel_attn_kernel
        name = "selected_attention"
        in_specs.append(pl.BlockSpec((None, None, nb, ATT_TQ), lambda b, g, i: (b, g, 0, i)))
        args.append(selt)
    return pl.pallas_call(
        body,
        grid=(bsz, NSA_GROUPS, nt),
        in_specs=in_specs,
        out_specs=pl.BlockSpec((ATT_TQ, gw), lambda b, g, i: (b * nt + i, g)),
        out_shape=jax.ShapeDtypeStruct((n, NSA_Q), F32),
        scratch_shapes=[pltpu.VMEM((NSA_DH, width), F32),
                        pltpu.VMEM((ATT_TK, width), F32),
                        pltpu.VMEM((ATT_TK, width), F32)],
        compiler_params=_params("parallel", "parallel", "parallel"),
        name=name,
    )(*args)


def _merge_kernel(x_ref, yr_ref, oc_ref, os_ref, ow_ref, ngl_ref, g_ref, wg_ref, wr_ref, wn_ref, wo_ref, o_ref):
    tm = x_ref.shape[0]
    hb = _rms(x_ref[...], g_ref[...]).astype(MXU_DTYPE)
    ga = jnp.dot(hb, wg_ref[:, :D_MODEL], preferred_element_type=F32)
    gb = jnp.dot(hb, wg_ref[:, D_MODEL:], preferred_element_type=F32)
    gates = jax.nn.sigmoid(ngl_ref[...])
    parts = []
    for h in range(NSA_HEADS):
        cols = slice(h * NSA_DH, (h + 1) * NSA_DH)

        def gate(br):
            return jnp.broadcast_to(gates[:, 3 * h + br:3 * h + br + 1], (tm, NSA_DH))

        parts.append(gate(0) * oc_ref[:, cols] + gate(1) * os_ref[:, cols] + gate(2) * ow_ref[:, cols])
    o_nsa = jnp.concatenate(parts, axis=1)
    y_ret = _dot(yr_ref[...], wr_ref[...])
    y_nsa = _dot(o_nsa, wn_ref[...])
    y = jax.nn.sigmoid(ga) * y_ret + jax.nn.sigmoid(gb) * y_nsa
    o_ref[...] = x_ref[...] + _dot(y, wo_ref[...])


def _merge(x2d, y_ret, o_cmp, o_sel, o_win, ngl, g_mix, w_gates, w_ret_o, w_nsa_o, w_out):
    n = x2d.shape[0]
    tm = ROW_TM
    row = lambda i: (i, 0)
    const = lambda i: (0, 0)
    wide = pl.BlockSpec((tm, D_MODEL), row)
    wspec = pl.BlockSpec((D_MODEL, D_MODEL), const)
    return pl.pallas_call(
        _merge_kernel,
        grid=(n // tm,),
        in_specs=[wide, wide, wide, wide, wide,
                  pl.BlockSpec((tm, LANES), row),
                  pl.BlockSpec((1, D_MODEL), const),
                  pl.BlockSpec((D_MODEL, 2 * D_MODEL), const),
                  wspec, wspec, wspec],
        out_specs=wide,
        out_shape=jax.ShapeDtypeStruct((n, D_MODEL), F32),
        compiler_params=_params("parallel"),
        name="merge",
    )(x2d, y_ret, o_cmp, o_sel, o_win, ngl, g_mix, w_gates, w_ret_o, w_nsa_o, w_out)


def _mem_kv_kernel(m_ref, g_ref, w_ref, o_ref):
    o_ref[...] = _dot(_rms(m_ref[...], g_ref[...]), w_ref[...]).astype(o_ref.dtype)


def _mem_kv(mem2d, g, w_xkv, bsz):
    nm = mem2d.shape[0] // bsz
    return pl.pallas_call(
        _mem_kv_kernel,
        grid=(bsz,),
        in_specs=[pl.BlockSpec((nm, D_MODEL), lambda b: (b, 0)),
                  pl.BlockSpec((1, D_MODEL), lambda b: (0, 0)),
                  pl.BlockSpec((D_MODEL, 2 * D_MODEL), lambda b: (0, 0))],
        out_specs=pl.BlockSpec((nm, 2 * D_MODEL), lambda b: (b, 0)),
        out_shape=jax.ShapeDtypeStruct((mem2d.shape[0], 2 * D_MODEL), MXU_DTYPE),
        compiler_params=_params("parallel"),
        name="mem_kv",
    )(mem2d, g, w_xkv)


def _pack_pairs(x):
    half = x.shape[1] // 2
    hi = lax.bitcast_convert_type(x[:, :half].astype(MXU_DTYPE).astype(F32), jnp.uint32)
    lo = lax.bitcast_convert_type(x[:, half:].astype(MXU_DTYPE).astype(F32), jnp.uint32)
    return (hi & jnp.uint32(0xFFFF0000)) | (lo >> 16)


def _unpack_pairs(u):
    hi = lax.bitcast_convert_type(u & jnp.uint32(0xFFFF0000), F32)
    lo = lax.bitcast_convert_type(u << 16, F32)
    return jnp.concatenate([hi, lo], axis=1)


_ROUTER_E0 = 2 * SUBLANES
_ROUTER_ROWS = _ROUTER_E0 + N_EXPERTS


def _top2_route(lgt):
    sub = SUBLANES
    t = lgt.shape[1]
    rowid = lax.broadcasted_iota(jnp.int32, (sub, t), 0)
    first = lambda hit: jnp.min(jnp.where(hit, rowid, sub), axis=0, keepdims=True)
    lg = jnp.where(rowid < N_EGROUPS, lgt[0:sub], NEG_INF)
    gmax = jnp.max(lg, axis=0, keepdims=True)
    grp = first(lg == gmax)
    g_gate = 1.0 / jnp.sum(jnp.exp(lg - gmax), axis=0, keepdims=True)
    experts_of = lambda g: lgt[_ROUTER_E0 + g * EXP_PER_GROUP:_ROUTER_E0 + (g + 1) * EXP_PER_GROUP]
    le = experts_of(N_EGROUPS - 1)
    for g in range(N_EGROUPS - 2, -1, -1):
        le = jnp.where(grp == g, experts_of(g), le)
    ex = jnp.exp(le - jnp.max(le, axis=0, keepdims=True))
    pe = ex / jnp.sum(ex, axis=0, keepdims=True)
    p0 = jnp.max(pe, axis=0, keepdims=True)
    i0 = first(pe == p0)
    rest = jnp.where(rowid == i0, -1.0, pe)
    p1 = jnp.max(rest, axis=0, keepdims=True)
    i1 = first(rest == p1)
    den = p0 + p1
    base = grp * EXP_PER_GROUP
    return jnp.concatenate([(base + i0).astype(F32), (base + i1).astype(F32),
                            g_gate * p0 / den, g_gate * p1 / den], axis=0)


def _cross_kernel(x_ref, kv_ref, gx_ref, wq_ref, wo_ref, gf_ref, wr_ref, br_ref, x2_ref, hf_ref, rt_ref):
    x = x_ref[...]
    q = _dot(_rms(x, gx_ref[...]), wq_ref[...])
    heads = []
    for h in range(X_HEADS):
        k = kv_ref[:, h * X_DH:(h + 1) * X_DH]
        v = kv_ref[:, D_MODEL + h * X_DH:D_MODEL + (h + 1) * X_DH]
        s = _dot_nt(q[:, h * X_DH:(h + 1) * X_DH], k) * (X_DH ** -0.5)
        e = jnp.exp(s - jnp.max(s, axis=-1, keepdims=True))
        p = e / jnp.sum(e, axis=-1, keepdims=True)
        heads.append(_dot(p, v))
    x2 = x + _dot(jnp.concatenate(heads, axis=1), wo_ref[...])
    x2_ref[...] = x2
    hf = _rms(x2, gf_ref[...])
    hf_ref[...] = _pack_pairs(hf)
    h_hi, h_mid, _ = _split3(hf)
    w_hi = wr_ref[0]
    w_mid = wr_ref[1]
    lgt = (_dot_nt(w_hi, h_hi) + (_dot_nt(w_hi, h_mid) + _dot_nt(w_mid, h_hi))) + br_ref[...]
    rt_ref[...] = jnp.concatenate([_top2_route(lgt), jnp.zeros((SUBLANES - 4, lgt.shape[1]), F32)], axis=0)


def _cross(x1, kv, gx, w_xq, w_xo, gf, w_router, b_router, bsz, seq):
    n = x1.shape[0]
    tm = ROW_TM
    nt = seq // tm
    nm = kv.shape[0] // bsz
    row = lambda i: (i, 0)
    const = lambda i: (0, 0)
    vec = pl.BlockSpec((1, D_MODEL), const)
    wspec = pl.BlockSpec((D_MODEL, D_MODEL), const)
    return pl.pallas_call(
        _cross_kernel,
        grid=(n // tm,),
        in_specs=[pl.BlockSpec((tm, D_MODEL), row),
                  pl.BlockSpec((nm, 2 * D_MODEL), lambda i: (i // nt, 0)),
                  vec, wspec, wspec, vec,
                  pl.BlockSpec((2, _ROUTER_ROWS, D_MODEL), lambda i: (0, 0, 0)),
                  pl.BlockSpec((_ROUTER_ROWS, 1), const)],
        out_specs=[pl.BlockSpec((tm, D_MODEL), row),
                   pl.BlockSpec((tm, D_MODEL // 2), row),
                   pl.BlockSpec((SUBLANES, tm), lambda i: (0, i))],
        out_shape=[jax.ShapeDtypeStruct((n, D_MODEL), F32),
                   jax.ShapeDtypeStruct((n, D_MODEL // 2), jnp.uint32),
                   jax.ShapeDtypeStruct((SUBLANES, n), F32)],
        compiler_params=_params("parallel"),
        name="cross_attention",
    )(x1, kv, gx, w_xq, w_xo, gf, w_router, b_router)


def _expert_kernel(blk_ref, xb_ref, w1_ref, w3_ref, w2_ref, o_ref):
    n_used = blk_ref[pl.num_programs(0)]

    @pl.when(pl.program_id(0) < n_used)
    def _():
        xb = _unpack_pairs(xb_ref[...]).astype(MXU_DTYPE)
        a = _dot(xb, w1_ref[...])
        hmid = a * jax.nn.sigmoid(a) * _dot(xb, w3_ref[...])
        o_ref[...] = _dot(hmid, w2_ref[...])

    @pl.when(pl.program_id(0) >= n_used)
    def _():
        o_ref[...] = jnp.zeros_like(o_ref)


def _experts(blk, xb, w1, w3, w2):
    cap = xb.shape[0]
    nblk = cap // MOE_BLOCK
    row = lambda i, e: (i, 0)
    grid_spec = pltpu.PrefetchScalarGridSpec(
        num_scalar_prefetch=1,
        grid=(nblk,),
        in_specs=[pl.BlockSpec((MOE_BLOCK, D_MODEL // 2), row),
                  pl.BlockSpec((None, D_MODEL, D_EXPERT), lambda i, e: (e[i], 0, 0)),
                  pl.BlockSpec((None, D_MODEL, D_EXPERT), lambda i, e: (e[i], 0, 0)),
                  pl.BlockSpec((None, D_EXPERT, D_MODEL), lambda i, e: (e[i], 0, 0))],
        out_specs=pl.BlockSpec((MOE_BLOCK, D_MODEL), row),
    )
    return pl.pallas_call(
        _expert_kernel,
        grid_spec=grid_spec,
        out_shape=jax.ShapeDtypeStruct((cap, D_MODEL), F32),
        compiler_params=_params("arbitrary"),
        name="experts",
    )(blk, xb, w1, w3, w2)


def _final_kernel(x_ref, w_ref, g_ref, *refs):
    o_ref = refs[-1]
    tiles_per_split = pl.num_programs(0) // MOE_SPLITS
    for s in range(MOE_SPLITS):
        @pl.when(pl.program_id(0) // tiles_per_split == s)
        def _(s=s):
            moe = w_ref[:, 0:1] * refs[2 * s][...] + w_ref[:, 1:2] * refs[2 * s + 1][...]
            o_ref[...] = _rms(x_ref[...] + moe, g_ref[...])


def _final(x2, wts, g, ys):
    n = x2.shape[0]
    tm = ROW_TM
    tiles_per_split = n // tm // MOE_SPLITS
    row = lambda i: (i, 0)
    wide = pl.BlockSpec((tm, D_MODEL), row)

    def split_spec(s):
        return pl.BlockSpec((tm, D_MODEL), lambda i: (jnp.clip(i - s * tiles_per_split, 0, tiles_per_split - 1), 0))

    return pl.pallas_call(
        _final_kernel,
        grid=(n // tm,),
        in_specs=[wide, pl.BlockSpec((tm, EXP_TOPK), row), pl.BlockSpec((1, D_MODEL), lambda i: (0, 0))] + [
            split_spec(s) for s in range(MOE_SPLITS) for _ in range(EXP_TOPK)],
        out_specs=wide,
        out_shape=jax.ShapeDtypeStruct((n, D_MODEL), F32),
        compiler_params=_params("parallel"),
        name="final_norm",
    )(x2, wts, g, *[y for pair in ys for y in pair])


def _route(eid, tok0):
    n_tok = eid.shape[1]
    eid = eid.reshape(-1)
    n_asg = eid.shape[0]
    iota = jnp.arange(n_asg, dtype=jnp.int32)
    se, order = lax.sort_key_val(eid, iota)
    counts = jnp.sum((jnp.arange(N_EXPERTS, dtype=jnp.int32)[:, None] == eid[None, :]).astype(jnp.int32), axis=1)
    padded = (counts + MOE_BLOCK - 1) // MOE_BLOCK * MOE_BLOCK
    starts = jnp.cumsum(counts) - counts
    pends = jnp.cumsum(padded)
    pstarts = pends - padded
    cap = ((n_asg + MOE_BLOCK - 1) // MOE_BLOCK + N_EXPERTS) * MOE_BLOCK
    nblk = cap // MOE_BLOCK
    blk_e = jnp.minimum(jnp.searchsorted(pends, jnp.arange(nblk) * MOE_BLOCK, side='right', method='compare_all'),
                        N_EXPERTS - 1).astype(jnp.int32)
    per_row = lambda a: jnp.repeat(a[blk_e], MOE_BLOCK)
    off = jnp.arange(cap, dtype=jnp.int32) - per_row(pstarts)
    asg = order[jnp.clip(per_row(starts) + off, 0, n_asg - 1)]
    buf_tok = tok0 + jnp.where(off < per_row(counts), asg % n_tok, 0)
    dest_sorted = iota + (pstarts - starts)[se]
    _, pos = lax.sort_key_val(order, dest_sorted)
    n_used = (pends[-1] // MOE_BLOCK).astype(jnp.int32)
    return buf_tok, jnp.concatenate([blk_e, n_used[None]]), pos.reshape(EXP_TOPK, n_tok)


def kernel(x, mem, norm_mix_g, w_in, w_ret_o, w_nsa_o, w_out, cmp_pe_k, cmp_w1_k, cmp_w2_k, cmp_pe_v,
           cmp_w1_v, cmp_w2_v, norm_x_g, norm_mem_g, w_xq, w_xkv, w_xo, norm_ffn_g, w_grp, b_grp, w_rt,
           b_rt, w_e1, w_e3, w_e2, norm_f_g):
    bsz, seq, _ = x.shape
    n = bsz * seq
    assert seq % PROJ_TM == 0 and seq % (2 * ATT_TQ) == 0 and w_in.shape[0] == 1
    assert n % (ROW_TM * MOE_SPLITS) == 0
    cast = lambda a: a.astype(MXU_DTYPE)
    xc = x.reshape(n, D_MODEL)
    l = 0

    w_main, w_t, w_gates = _split_w_in(w_in[l])
    qk, rv, rg, ckv, sk, wk, ngl, qt, svt, wvt = _proj(
        xc, norm_mix_g[l][None, :], w_main, w_t, _rope_tables(seq), bsz, seq)
    y_ret = _retention(qk, rv, rg, bsz, seq)
    w2 = jnp.stack([cmp_w2_k[l], cmp_w2_v[l]])
    cmp_k, cmp_vt = _compress(ckv, jnp.stack([cmp_pe_k[l], cmp_pe_v[l]]),
                              cast(jnp.stack([cmp_w1_k[l], cmp_w1_v[l]])),
                              cast(w2), cast(w2.transpose(0, 2, 1)), bsz, seq)
    o_cmp, selt = _cmp_attention(qt, cmp_k, cmp_vt, bsz, seq)
    o_sel = _flash_attention(qt, sk, svt, selt, bsz, seq)
    o_win = _flash_attention(qt, wk, wvt, None, bsz, seq)
    x1 = _merge(xc, y_ret, o_cmp, o_sel, o_win, ngl, norm_mix_g[l][None, :], w_gates,
                cast(w_ret_o[l]), cast(w_nsa_o[l]), cast(w_out[l]))

    kv = _mem_kv(mem.reshape(-1, D_MODEL), norm_mem_g[l][None, :], cast(w_xkv[l]), bsz)
    gap = _ROUTER_E0 - N_EGROUPS
    w_router = jnp.concatenate([w_grp[l].T, jnp.zeros((gap, D_MODEL), F32), w_rt[l].T], axis=0)
    wr_hi = cast(w_router)
    wr_mid = cast(w_router - wr_hi.astype(F32))
    b_router = jnp.concatenate([b_grp[l], jnp.zeros((gap,), F32), b_rt[l]])[:, None]
    x2, hf, routed = _cross(x1, kv, norm_x_g[l][None, :], cast(w_xq[l]), cast(w_xo[l]),
                            norm_ffn_g[l][None, :], jnp.stack([wr_hi, wr_mid]), b_router, bsz, seq)

    eid = routed[0:EXP_TOPK].astype(jnp.int32)
    wts = routed[EXP_TOPK:2 * EXP_TOPK].T
    per_split = n // MOE_SPLITS
    ys = []
    for s in range(MOE_SPLITS):
        buf_tok, blk, pos = _route(eid[:, s * per_split:(s + 1) * per_split], s * per_split)
        y = _experts(blk, hf[buf_tok], w_e1[l], w_e3[l], w_e2[l])
        ys.append([y[pos[j]] for j in range(EXP_TOPK)])
    out = _final(x2, wts, norm_f_g[None, :], ys)
    return out.reshape(bsz, seq, D_MODEL)
```

```python
import functools

import numpy as np
import jax
import jax.numpy as jnp
from jax import lax
from jax.experimental import pallas as pl
from jax.experimental.pallas import tpu as pltpu

MXU_DTYPE = jnp.bfloat16
F32 = jnp.float32

D_MODEL = 1024
N_MEM = 256
EPS = 1e-6
NEG_INF = -1e30
SEL_FORCE = 1e4

R_HEADS = 4
R_DK = 128
R_DV = 256
R_CHUNK = 128
ROPE_BASE = 10000.0

NSA_HEADS = 8
NSA_GROUPS = 2
NSA_HPG = NSA_HEADS // NSA_GROUPS
NSA_DH = 128
CMP_LEN = 32
CMP_STRIDE = 16
SEL_LEN = 64
SEL_TOPK = 16
WINDOW = 512

X_HEADS = 4
X_DH = D_MODEL // X_HEADS

N_EGROUPS = 4
EXP_PER_GROUP = 8
N_EXPERTS = N_EGROUPS * EXP_PER_GROUP
EXP_TOPK = 2
D_EXPERT = 512
MOE_BLOCK = 256

RET_QK = R_HEADS * R_DK
RET_V = R_HEADS * R_DV
NSA_Q = NSA_HEADS * NSA_DH
NSA_KV = NSA_GROUPS * NSA_DH
SPLITS = (RET_QK, RET_QK, RET_V, RET_V, NSA_Q, NSA_KV, NSA_KV, NSA_KV, NSA_KV, NSA_KV, NSA_KV,
          3 * NSA_HEADS, D_MODEL, D_MODEL)

_EXP2_SCALE = (NSA_DH ** -0.5) * float(np.log2(np.e))

LANES = 128
SUBLANES = 8
VMEM_LIMIT = 56 * 1024 * 1024

PROJ_TM = 512
RET_ROWS = 512
ATT_TQ = 256
ATT_TK = 256
ROW_TM = 512
MOE_SPLITS = 2


def _params(*sem):
    return pltpu.CompilerParams(dimension_semantics=sem, vmem_limit_bytes=VMEM_LIMIT)


def _dot(a, b):
    return jnp.dot(a.astype(MXU_DTYPE), b.astype(MXU_DTYPE), preferred_element_type=F32)


def _dot_nt(a, b):
    return lax.dot_general(a.astype(MXU_DTYPE), b.astype(MXU_DTYPE), (((1,), (1,)), ((), ())),
                           preferred_element_type=F32)


def _dot_tn(a, b):
    return lax.dot_general(a.astype(MXU_DTYPE), b.astype(MXU_DTYPE), (((0,), (0,)), ((), ())),
                           preferred_element_type=F32)


def _split3(p):
    hi = p.astype(MXU_DTYPE)
    r1 = p - hi.astype(F32)
    mid = r1.astype(MXU_DTYPE)
    lo = (r1 - mid.astype(F32)).astype(MXU_DTYPE)
    return hi, mid, lo


def _rms(x, g):
    return x * lax.rsqrt(jnp.mean(x * x, axis=-1, keepdims=True) + EPS) * g


_C_RQK = 0
_C_RV = _C_RQK + 2 * RET_QK
_C_RG = _C_RV + RET_V
_C_CKV = _C_RG + RET_V
_C_SK = _C_CKV + 2 * NSA_KV
_C_WK = _C_SK + NSA_KV
_C_NGL = _C_WK + NSA_KV
_C_END = _C_NGL + LANES
_R_NQ = 0
_R_SV = _R_NQ + NSA_Q
_R_WV = _R_SV + NSA_KV
_R_END = _R_WV + NSA_KV


def _proj_kernel(x_ref, g_ref, w_ref, wt_ref, cos_ref, sin_up_ref, sin_dn_ref,
                 qk_ref, rv_ref, rg_ref, ckv_ref, sk_ref, wk_ref, ngl_ref, nqt_ref, svt_ref, wvt_ref):
    hb = _rms(x_ref[...], g_ref[...]).astype(MXU_DTYPE)
    tm = hb.shape[0]

    def mm(off, width):
        return jnp.dot(hb, w_ref[:, off:off + width], preferred_element_type=F32)

    def mm_t(off, height):
        return _dot_nt(wt_ref[off:off + height, :], hb)

    cos = cos_ref[...]
    sin_up = sin_up_ref[...]
    sin_dn = sin_dn_ref[...]
    qk = mm(_C_RQK, 2 * RET_QK)
    for i in range(2 * R_HEADS):
        t = qk[:, i * R_DK:(i + 1) * R_DK]
        r = t * cos + pltpu.roll(t, R_DK - 1, axis=1) * sin_up + pltpu.roll(t, 1, axis=1) * sin_dn
        if i >= R_HEADS:
            r = r * (R_DK ** -0.5)
        qk_ref[:, i * R_DK:(i + 1) * R_DK] = r.astype(qk_ref.dtype)
    rv_ref[...] = mm(_C_RV, RET_V).astype(rv_ref.dtype)
    rg_ref[...] = mm(_C_RG, RET_V)
    ckv_ref[...] = mm(_C_CKV, 2 * NSA_KV)
    sk_ref[...] = mm(_C_SK, NSA_KV).astype(sk_ref.dtype)
    wk_ref[...] = mm(_C_WK, NSA_KV).astype(wk_ref.dtype)
    ngl_ref[...] = mm(_C_NGL, LANES)

    nqt = (mm_t(_R_NQ, NSA_Q) * _EXP2_SCALE).astype(nqt_ref.dtype)
    for g in range(NSA_GROUPS):
        for j in range(tm // ATT_TQ):
            for hh in range(NSA_HPG):
                head = g * NSA_HPG + hh
                nqt_ref[g, j, :, hh * ATT_TQ:(hh + 1) * ATT_TQ] = (
                    nqt[head * NSA_DH:(head + 1) * NSA_DH, j * ATT_TQ:(j + 1) * ATT_TQ])
    for off, out_ref in ((_R_SV, svt_ref), (_R_WV, wvt_ref)):
        vt = mm_t(off, NSA_KV).astype(out_ref.dtype)
        for g in range(NSA_GROUPS):
            for j in range(tm // ATT_TK):
                out_ref[g, j] = vt[g * NSA_DH:(g + 1) * NSA_DH, j * ATT_TK:(j + 1) * ATT_TK]


def _split_w_in(w_in):
    offs = np.cumsum((0,) + SPLITS)
    rq, rk, rv, rg, nq, ck, cv, sk, sv, wk, wv, ngl, ga, gb = [
        w_in[:, offs[i]:offs[i + 1]] for i in range(len(SPLITS))]
    ngl = jnp.pad(ngl, ((0, 0), (0, LANES - ngl.shape[1])))
    w = jnp.concatenate([rq, rk, rv, rg, ck, cv, sk, wk, ngl], axis=1).astype(MXU_DTYPE)
    wt = jnp.concatenate([nq, sv, wv], axis=1).T.astype(MXU_DTYPE)
    w_gates = jnp.concatenate([ga, gb], axis=1).astype(MXU_DTYPE)
    return w, wt, w_gates


def _rope_tables(seq):
    pos = jnp.arange(seq, dtype=F32)
    inv_freq = ROPE_BASE ** (-jnp.arange(0, R_DK, 2, dtype=F32) / R_DK)
    ang = pos[:, None] * inv_freq[None, :]
    cos = jnp.repeat(jnp.cos(ang), 2, axis=1)
    sin = jnp.repeat(jnp.sin(ang), 2, axis=1)
    even = (jnp.arange(R_DK) % 2 == 0)[None, :]
    return cos, jnp.where(even, -sin, 0.0), jnp.where(even, 0.0, sin)


def _proj(x2d, g, w, wt, tables, bsz, seq):
    n = x2d.shape[0]
    tm = PROJ_TM
    nt = seq // tm
    row = lambda i: (i, 0)
    const = lambda i: (0, 0)
    tile = lambda i: (i // nt, 0, i % nt, 0, 0)
    table = pl.BlockSpec((tm, R_DK), lambda i: (i % nt, 0))
    widths = (2 * RET_QK, RET_V, RET_V, 2 * NSA_KV, NSA_KV, NSA_KV, LANES)
    dtypes = (MXU_DTYPE, MXU_DTYPE, F32, F32, MXU_DTYPE, MXU_DTYPE, F32)
    width = NSA_HPG * ATT_TQ
    vt_shape = jax.ShapeDtypeStruct((bsz, NSA_GROUPS, seq // ATT_TK, NSA_DH, ATT_TK), MXU_DTYPE)
    vt_spec = pl.BlockSpec((None, NSA_GROUPS, tm // ATT_TK, NSA_DH, ATT_TK), tile)
    return pl.pallas_call(
        _proj_kernel,
        grid=(n // tm,),
        in_specs=[pl.BlockSpec((tm, D_MODEL), row),
                  pl.BlockSpec((1, D_MODEL), const),
                  pl.BlockSpec((D_MODEL, _C_END), const, pipeline_mode=pl.Buffered(1)),
                  pl.BlockSpec((_R_END, D_MODEL), const, pipeline_mode=pl.Buffered(1)),
                  table, table, table],
        out_specs=[pl.BlockSpec((tm, wd), row) for wd in widths] + [
            pl.BlockSpec((None, NSA_GROUPS, tm // ATT_TQ, NSA_DH, width), tile), vt_spec, vt_spec],
        out_shape=[jax.ShapeDtypeStruct((n, wd), dt) for wd, dt in zip(widths, dtypes)] + [
            jax.ShapeDtypeStruct((bsz, NSA_GROUPS, seq // ATT_TQ, NSA_DH, width), MXU_DTYPE), vt_shape, vt_shape],
        compiler_params=_params("parallel"),
        name="proj",
    )(x2d, g, w, wt, *tables)


def _ret_kernel(qk_ref, v_ref, g_ref, decay_ref, zeta_ref, xi_ref, cd_ref, y_ref, state_ref):
    @pl.when(pl.program_id(1) == 0)
    def _():
        state_ref[...] = jnp.zeros_like(state_ref)

    for c in range(RET_ROWS // R_CHUNK):
        rows = slice(c * R_CHUNK, (c + 1) * R_CHUNK)
        for h in range(R_HEADS):
            q = qk_ref[rows, h * R_DK:(h + 1) * R_DK]
            k = qk_ref[rows, RET_QK + h * R_DK:RET_QK + (h + 1) * R_DK]
            v = v_ref[rows, h * R_DV:(h + 1) * R_DV]
            st = state_ref[h]
            scores = _dot_nt(q, k) * decay_ref[h]
            o = _dot(scores, v) + _dot(q, st) * xi_ref[h]
            o = o * lax.rsqrt(jnp.mean(o * o, axis=-1, keepdims=True) + EPS)
            g = g_ref[rows, h * R_DV:(h + 1) * R_DV]
            y_ref[rows, h * R_DV:(h + 1) * R_DV] = (g * jax.nn.sigmoid(g) * o).astype(y_ref.dtype)
            kz = k.astype(F32) * zeta_ref[h]
            state_ref[h] = st * cd_ref[h] + _dot_tn(kz, v)


def _retention(qk, rv, rg, bsz, seq):
    n = qk.shape[0]
    nt = seq // RET_ROWS
    log_g = jnp.log1p(-jnp.exp2(-5.0 - jnp.arange(R_HEADS, dtype=F32)))
    idx = jnp.arange(R_CHUNK, dtype=F32)
    diff = idx[:, None] - idx[None, :]
    decay = jnp.where(diff >= 0, jnp.exp(log_g[:, None, None] * jnp.maximum(diff, 0.0)), 0.0)
    zeta = jnp.exp(log_g[:, None] * (R_CHUNK - 1.0 - idx)[None, :])[:, :, None]
    xi = jnp.exp(log_g[:, None] * (idx + 1.0)[None, :])[:, :, None]
    cd = jnp.exp(log_g * R_CHUNK)[:, None, None]
    row = lambda b, i: (b * nt + i, 0)
    const3 = lambda b, i: (0, 0, 0)
    return pl.pallas_call(
        _ret_kernel,
        grid=(bsz, nt),
        in_specs=[pl.BlockSpec((RET_ROWS, 2 * RET_QK), row),
                  pl.BlockSpec((RET_ROWS, RET_V), row),
                  pl.BlockSpec((RET_ROWS, RET_V), row),
                  pl.BlockSpec((R_HEADS, R_CHUNK, R_CHUNK), const3),
                  pl.BlockSpec((R_HEADS, R_CHUNK, 1), const3),
                  pl.BlockSpec((R_HEADS, R_CHUNK, 1), const3),
                  pl.BlockSpec((R_HEADS, 1, 1), const3)],
        out_specs=pl.BlockSpec((RET_ROWS, RET_V), row),
        out_shape=jax.ShapeDtypeStruct((n, RET_V), MXU_DTYPE),
        scratch_shapes=[pltpu.VMEM((R_HEADS, R_DK, R_DV), F32)],
        compiler_params=_params("parallel", "arbitrary"),
        name="retention",
    )(qk, rv, rg, decay, zeta, xi, cd)


def _compress_kernel(x_ref, pe_ref, w1_ref, w2_ref, w2t_ref, o_ref, ot_ref, buf_ref, *, seq):
    ncp = seq // CMP_STRIDE
    buf_ref[0:seq, :] = x_ref[...]
    buf_ref[seq:seq + LANES, :] = jnp.zeros((LANES, NSA_DH), F32)
    acc = jnp.zeros((ncp, NSA_DH), F32)
    for l in range(CMP_LEN):
        xl = buf_ref[pl.ds(l, ncp, stride=CMP_STRIDE), :] + pe_ref[l:l + 1, :]
        acc = acc + _dot(xl, w1_ref[l])
    hid = jax.nn.gelu(acc)
    o_ref[...] = _dot(hid, w2_ref[...]).astype(o_ref.dtype)
    ot_ref[...] = _dot_nt(w2t_ref[...], hid).astype(ot_ref.dtype)


def _compress(ckv, pe, w1, w2, w2t, bsz, seq):
    ncp = seq // CMP_STRIDE
    nj = 2 * NSA_GROUPS
    wsel = lambda b, j: (j // NSA_GROUPS, 0, 0)
    return pl.pallas_call(
        functools.partial(_compress_kernel, seq=seq),
        grid=(bsz, nj),
        in_specs=[pl.BlockSpec((seq, NSA_DH), lambda b, j: (b, j)),
                  pl.BlockSpec((None, CMP_LEN, NSA_DH), wsel),
                  pl.BlockSpec((None, CMP_LEN, NSA_DH, NSA_DH), lambda b, j: (j // NSA_GROUPS, 0, 0, 0)),
                  pl.BlockSpec((None, NSA_DH, NSA_DH), wsel),
                  pl.BlockSpec((None, NSA_DH, NSA_DH), wsel)],
        out_specs=[pl.BlockSpec((None, None, ncp, NSA_DH), lambda b, j: (b, j, 0, 0)),
                   pl.BlockSpec((None, None, NSA_DH, ncp), lambda b, j: (b, j, 0, 0))],
        out_shape=[jax.ShapeDtypeStruct((bsz, nj, ncp, NSA_DH), MXU_DTYPE),
                   jax.ShapeDtypeStruct((bsz, nj, NSA_DH, ncp), MXU_DTYPE)],
        scratch_shapes=[pltpu.VMEM((seq + LANES, NSA_DH), F32)],
        compiler_params=_params("parallel", "parallel"),
        name="compress",
    )(ckv, pe, w1, w2, w2t)


def _cmp_attn_kernel(qt_ref, k_ref, vt_ref, ov_ref, o_ref, selt_ref, *, ncp, nb):
    tq = ATT_TQ
    t0 = pl.program_id(2) * tq
    t = t0 + lax.broadcasted_iota(jnp.int32, (ncp, tq), 1)
    n = lax.broadcasted_iota(jnp.int32, (ncp, tq), 0)
    valid = (n * CMP_STRIDE + (CMP_LEN - 1) <= t) & (n < ncp - 1)
    any_valid = (t0 + lax.broadcasted_iota(jnp.int32, (1, tq), 1)) >= CMP_LEN - 1
    valid = jnp.concatenate([valid] * NSA_HPG, axis=1)
    any_valid = jnp.concatenate([any_valid] * NSA_HPG, axis=1)
    s = jnp.where(valid, _dot(k_ref[...], qt_ref[...]), NEG_INF)
    e = jnp.exp2(s - jnp.max(s, axis=0, keepdims=True))
    p = e * jnp.where(any_valid, 1.0 / jnp.sum(e, axis=0, keepdims=True), 0.0)
    ot = _dot(vt_ref[...], p)
    for h in range(NSA_HPG):
        o_ref[:, h * NSA_DH:(h + 1) * NSA_DH] = ot[:, h * tq:(h + 1) * tq].T
    psum = sum(p[:, h * tq:(h + 1) * tq] for h in range(NSA_HPG))

    imp = sum(_dot(ov_ref[...], part) for part in _split3(psum))
    j = lax.broadcasted_iota(jnp.int32, (nb, tq), 0)
    tb = (t0 + lax.broadcasted_iota(jnp.int32, (nb, tq), 1)) // SEL_LEN
    forced = (j == 0) | (j == tb) | (j == tb - 1)
    imp = jnp.where(j > tb, -SEL_FORCE, jnp.where(forced, SEL_FORCE, imp))

    sub = SUBLANES
    grp = [imp[r * sub:(r + 1) * sub] for r in range(nb // sub)]
    cnt = [jnp.zeros((sub, tq), F32) for _ in grp]
    for i in range(nb):
        row = jnp.broadcast_to(imp[i:i + 1, :], (sub, tq))
        for r in range(nb // sub):
            if r * sub > i:
                beats = jnp.where(row >= grp[r], 1.0, 0.0)
            elif r * sub + sub - 1 < i:
                beats = jnp.where(row > grp[r], 1.0, 0.0)
            else:
                jr = r * sub + lax.broadcasted_iota(jnp.int32, (sub, tq), 0)
                beats = jnp.where(jr > i, jnp.where(row >= grp[r], 1.0, 0.0), jnp.where(row > grp[r], 1.0, 0.0))
            cnt[r] = cnt[r] + beats
    k_sel = min(SEL_TOPK, nb)
    for r in range(nb // sub):
        selt_ref[r * sub:(r + 1) * sub, :] = jnp.where(cnt[r] < k_sel, 1.0, 0.0)


def _cmp_attention(qt, cmp_k, cmp_vt, bsz, seq):
    n = bsz * seq
    ncp = seq // CMP_STRIDE
    nb = seq // SEL_LEN
    nt = seq // ATT_TQ
    gw = NSA_HPG * NSA_DH
    cstart = np.arange(ncp) * CMP_STRIDE
    jstart = np.arange(nb) * SEL_LEN
    ov = ((cstart[None, :] < jstart[:, None] + SEL_LEN) & (cstart[None, :] + CMP_LEN > jstart[:, None])
          & (np.arange(ncp)[None, :] < ncp - 1))
    ov = jnp.asarray(ov, MXU_DTYPE)
    return pl.pallas_call(
        functools.partial(_cmp_attn_kernel, ncp=ncp, nb=nb),
        grid=(bsz, NSA_GROUPS, nt),
        in_specs=[pl.BlockSpec((None, None, None, NSA_DH, NSA_HPG * ATT_TQ), lambda b, g, i: (b, g, i, 0, 0)),
                  pl.BlockSpec((None, None, ncp, NSA_DH), lambda b, g, i: (b, g, 0, 0)),
                  pl.BlockSpec((None, None, NSA_DH, ncp), lambda b, g, i: (b, NSA_GROUPS + g, 0, 0)),
                  pl.BlockSpec((nb, ncp), lambda b, g, i: (0, 0))],
        out_specs=[pl.BlockSpec((ATT_TQ, gw), lambda b, g, i: (b * nt + i, g)),
                   pl.BlockSpec((None, None, nb, ATT_TQ), lambda b, g, i: (b, g, 0, i))],
        out_shape=[jax.ShapeDtypeStruct((n, NSA_Q), F32),
                   jax.ShapeDtypeStruct((bsz, NSA_GROUPS, nb, seq), F32)],
        compiler_params=_params("parallel", "parallel", "parallel"),
        name="cmp_attention",
    )(qt, cmp_k, cmp_vt, ov)


_MASK_ALL = 1 << 28


def _flash_loop(qt_ref, k_ref, vt_ref, acc_ref, s_refs, n_tiles, tile_of, valid_of, max_steps=None):
    width = NSA_HPG * ATT_TQ

    def scores(kt):
        k = k_ref[pl.ds(pl.multiple_of(kt * ATT_TK, ATT_TK), ATT_TK), :]
        return _dot(k, qt_ref[...])

    def half_step(step, s_cur, s_nxt, m, l, prefetch=True):
        kt = tile_of(jnp.minimum(step, n_tiles - 1))
        if prefetch:
            s_nxt[...] = scores(tile_of(jnp.minimum(step + 1, n_tiles - 1)))
        valid = valid_of(kt, kt * ATT_TK + jnp.where(step < n_tiles, 0, _MASK_ALL))
        valid = jnp.concatenate([valid] * NSA_HPG, axis=1)
        s = jnp.where(valid, s_cur[...], NEG_INF)
        m_new = jnp.maximum(m, jnp.max(s, axis=0, keepdims=True))
        alpha = jnp.exp2(m - m_new)
        p = jnp.exp2(s - m_new)
        l = alpha * l + jnp.sum(p, axis=0, keepdims=True)
        acc_ref[...] = alpha * acc_ref[...] + _dot(vt_ref[kt], p)
        return m_new, l

    def body(i, carry):
        m, l = half_step(2 * i, s_refs[0], s_refs[1], *carry)
        return half_step(2 * i + 1, s_refs[1], s_refs[0], m, l)

    acc_ref[...] = jnp.zeros_like(acc_ref)
    s_refs[0][...] = scores(tile_of(0))
    carry = (jnp.full((1, width), NEG_INF, F32), jnp.zeros((1, width), F32))
    if max_steps is None:
        carry = lax.fori_loop(0, (n_tiles + 1) // 2, body, carry)
    else:
        for step in range(max_steps):
            carry = half_step(step, s_refs[step % 2], s_refs[(step + 1) % 2], *carry,
                              prefetch=step + 1 < max_steps)
    return carry[1]


def _flash_finish(o_ref, acc_ref, l):
    inv = 1.0 / l
    for h in range(NSA_HPG):
        cols = slice(h * ATT_TQ, (h + 1) * ATT_TQ)
        o_ref[:, h * NSA_DH:(h + 1) * NSA_DH] = (acc_ref[:, cols] * inv[:, cols]).T


def _sel_attn_kernel(qt_ref, k_ref, vt_ref, selt_ref, o_ref, acc_ref, s0_ref, s1_ref):
    qi = pl.program_id(2)
    tpos = qi * ATT_TQ + lax.broadcasted_iota(jnp.int32, (ATT_TK, ATT_TQ), 1)
    row = lax.broadcasted_iota(jnp.int32, (ATT_TK, ATT_TQ), 0)
    blocks_per_tile = ATT_TK // SEL_LEN

    def valid_of(kt, key0):
        picked = jnp.concatenate(
            [jnp.broadcast_to(selt_ref[pl.ds(kt * blocks_per_tile + jb, 1), :], (SEL_LEN, ATT_TQ))
             for jb in range(blocks_per_tile)], axis=0)
        return (picked > 0.5) & (row + key0 <= tpos)

    l = _flash_loop(qt_ref, k_ref, vt_ref, acc_ref, (s0_ref, s1_ref), (qi + 1) * (ATT_TQ // ATT_TK),
                    lambda step: step, valid_of)
    _flash_finish(o_ref, acc_ref, l)


def _win_attn_kernel(qt_ref, k_ref, vt_ref, o_ref, acc_ref, s0_ref, s1_ref):
    qi = pl.program_id(2)
    tpos = qi * ATT_TQ + lax.broadcasted_iota(jnp.int32, (ATT_TK, ATT_TQ), 1)
    row = lax.broadcasted_iota(jnp.int32, (ATT_TK, ATT_TQ), 0)
    last = (qi + 1) * (ATT_TQ // ATT_TK) - 1
    n_tiles = jnp.minimum(last + 1, (ATT_TQ + WINDOW) // ATT_TK)

    def valid_of(kt, key0):
        kpos = row + key0
        return (kpos <= tpos) & (kpos > tpos - WINDOW)

    l = _flash_loop(qt_ref, k_ref, vt_ref, acc_ref, (s0_ref, s1_ref), n_tiles,
                    lambda step: last - step, valid_of, max_steps=(ATT_TQ + WINDOW) // ATT_TK)
    _flash_finish(o_ref, acc_ref, l)


def _flash_attention(qt, k, vt, selt, bsz, seq):
    n = bsz * seq
    nt = seq // ATT_TQ
    nkt = seq // ATT_TK
    nb = seq // SEL_LEN
    gw = NSA_HPG * NSA_DH
    width = NSA_HPG * ATT_TQ
    in_specs = [pl.BlockSpec((None, None, None, NSA_DH, width), lambda b, g, i: (b, g, i, 0, 0)),
                pl.BlockSpec((seq, NSA_DH), lambda b, g, i: (b, g)),
                pl.BlockSpec((None, None, nkt, NSA_DH, ATT_TK), lambda b, g, i: (b, g, 0, 0, 0))]
    args = [qt, k, vt]
    if selt is None:
        body = _win_attn_kernel
        name = "window_attention"
    else:
        body = _sel_attn_kernel
        name = "selected_attention"
        in_specs.append(pl.BlockSpec((None, None, nb, ATT_TQ), lambda b, g, i: (b, g, 0, i)))
        args.append(selt)
    return pl.pallas_call(
        body,
        grid=(bsz, NSA_GROUPS, nt),
        in_specs=in_specs,
        out_specs=pl.BlockSpec((ATT_TQ, gw), lambda b, g, i: (b * nt + i, g)),
        out_shape=jax.ShapeDtypeStruct((n, NSA_Q), F32),
        scratch_shapes=[pltpu.VMEM((NSA_DH, width), F32),
                        pltpu.VMEM((ATT_TK, width), F32),
                        pltpu.VMEM((ATT_TK, width), F32)],
        compiler_params=_params("parallel", "parallel", "parallel"),
        name=name,
    )(*args)


def _merge_kernel(x_ref, yr_ref, oc_ref, os_ref, ow_ref, ngl_ref, g_ref, wg_ref, wr_ref, wn_ref, wo_ref, o_ref):
    tm = x_ref.shape[0]
    hb = _rms(x_ref[...], g_ref[...]).astype(MXU_DTYPE)
    ga = jnp.dot(hb, wg_ref[:, :D_MODEL], preferred_element_type=F32)
    gb = jnp.dot(hb, wg_ref[:, D_MODEL:], preferred_element_type=F32)
    gates = jax.nn.sigmoid(ngl_ref[...])
    parts = []
    for h in range(NSA_HEADS):
        cols = slice(h * NSA_DH, (h + 1) * NSA_DH)

        def gate(br):
            return jnp.broadcast_to(gates[:, 3 * h + br:3 * h + br + 1], (tm, NSA_DH))

        parts.append(gate(0) * oc_ref[:, cols] + gate(1) * os_ref[:, cols] + gate(2) * ow_ref[:, cols])
    o_nsa = jnp.concatenate(parts, axis=1)
    y_ret = _dot(yr_ref[...], wr_ref[...])
    y_nsa = _dot(o_nsa, wn_ref[...])
    y = jax.nn.sigmoid(ga) * y_ret + jax.nn.sigmoid(gb) * y_nsa
    o_ref[...] = x_ref[...] + _dot(y, wo_ref[...])


def _merge(x2d, y_ret, o_cmp, o_sel, o_win, ngl, g_mix, w_gates, w_ret_o, w_nsa_o, w_out):
    n = x2d.shape[0]
    tm = ROW_TM
    row = lambda i: (i, 0)
    const = lambda i: (0, 0)
    wide = pl.BlockSpec((tm, D_MODEL), row)
    wspec = pl.BlockSpec((D_MODEL, D_MODEL), const)
    return pl.pallas_call(
        _merge_kernel,
        grid=(n // tm,),
        in_specs=[wide, wide, wide, wide, wide,
                  pl.BlockSpec((tm, LANES), row),
                  pl.BlockSpec((1, D_MODEL), const),
                  pl.BlockSpec((D_MODEL, 2 * D_MODEL), const),
                  wspec, wspec, wspec],
        out_specs=wide,
        out_shape=jax.ShapeDtypeStruct((n, D_MODEL), F32),
        compiler_params=_params("parallel"),
        name="merge",
    )(x2d, y_ret, o_cmp, o_sel, o_win, ngl, g_mix, w_gates, w_ret_o, w_nsa_o, w_out)


def _mem_kv_kernel(m_ref, g_ref, w_ref, o_ref):
    o_ref[...] = _dot(_rms(m_ref[...], g_ref[...]), w_ref[...]).astype(o_ref.dtype)


def _mem_kv(mem2d, g, w_xkv, bsz):
    nm = mem2d.shape[0] // bsz
    return pl.pallas_call(
        _mem_kv_kernel,
        grid=(bsz,),
        in_specs=[pl.BlockSpec((nm, D_MODEL), lambda b: (b, 0)),
                  pl.BlockSpec((1, D_MODEL), lambda b: (0, 0)),
                  pl.BlockSpec((D_MODEL, 2 * D_MODEL), lambda b: (0, 0))],
        out_specs=pl.BlockSpec((nm, 2 * D_MODEL), lambda b: (b, 0)),
        out_shape=jax.ShapeDtypeStruct((mem2d.shape[0], 2 * D_MODEL), MXU_DTYPE),
        compiler_params=_params("parallel"),
        name="mem_kv",
    )(mem2d, g, w_xkv)


_ROUTER_E0 = 2 * SUBLANES
_ROUTER_ROWS = _ROUTER_E0 + N_EXPERTS


def _top2_route(lgt):
    sub = SUBLANES
    t = lgt.shape[1]
    rowid = lax.broadcasted_iota(jnp.int32, (sub, t), 0)
    first = lambda hit: jnp.min(jnp.where(hit, rowid, sub), axis=0, keepdims=True)
    lg = jnp.where(rowid < N_EGROUPS, lgt[0:sub], NEG_INF)
    gmax = jnp.max(lg, axis=0, keepdims=True)
    grp = first(lg == gmax)
    g_gate = 1.0 / jnp.sum(jnp.exp(lg - gmax), axis=0, keepdims=True)
    experts_of = lambda g: lgt[_ROUTER_E0 + g * EXP_PER_GROUP:_ROUTER_E0 + (g + 1) * EXP_PER_GROUP]
    le = experts_of(N_EGROUPS - 1)
    for g in range(N_EGROUPS - 2, -1, -1):
        le = jnp.where(grp == g, experts_of(g), le)
    ex = jnp.exp(le - jnp.max(le, axis=0, keepdims=True))
    pe = ex / jnp.sum(ex, axis=0, keepdims=True)
    p0 = jnp.max(pe, axis=0, keepdims=True)
    i0 = first(pe == p0)
    rest = jnp.where(rowid == i0, -1.0, pe)
    p1 = jnp.max(rest, axis=0, keepdims=True)
    i1 = first(rest == p1)
    den = p0 + p1
    base = grp * EXP_PER_GROUP
    return jnp.concatenate([(base + i0).astype(F32), (base + i1).astype(F32),
                            g_gate * p0 / den, g_gate * p1 / den], axis=0)


def _cross_kernel(x_ref, kv_ref, gx_ref, wq_ref, wo_ref, gf_ref, wr_ref, br_ref, x2_ref, hf_ref, rt_ref):
    x = x_ref[...]
    q = _dot(_rms(x, gx_ref[...]), wq_ref[...])
    heads = []
    for h in range(X_HEADS):
        k = kv_ref[:, h * X_DH:(h + 1) * X_DH]
        v = kv_ref[:, D_MODEL + h * X_DH:D_MODEL + (h + 1) * X_DH]
        s = _dot_nt(q[:, h * X_DH:(h + 1) * X_DH], k) * (X_DH ** -0.5)
        e = jnp.exp(s - jnp.max(s, axis=-1, keepdims=True))
        p = e / jnp.sum(e, axis=-1, keepdims=True)
        heads.append(_dot(p, v))
    x2 = x + _dot(jnp.concatenate(heads, axis=1), wo_ref[...])
    x2_ref[...] = x2
    hf = _rms(x2, gf_ref[...])
    hf_ref[...] = hf
    h_hi, h_mid, _ = _split3(hf)
    w_hi = wr_ref[0]
    w_mid = wr_ref[1]
    lgt = (_dot_nt(w_hi, h_hi) + (_dot_nt(w_hi, h_mid) + _dot_nt(w_mid, h_hi))) + br_ref[...]
    rt_ref[...] = jnp.concatenate([_top2_route(lgt), jnp.zeros((SUBLANES - 4, lgt.shape[1]), F32)], axis=0)


def _cross(x1, kv, gx, w_xq, w_xo, gf, w_router, b_router, bsz, seq):
    n = x1.shape[0]
    tm = ROW_TM
    nt = seq // tm
    nm = kv.shape[0] // bsz
    row = lambda i: (i, 0)
    const = lambda i: (0, 0)
    vec = pl.BlockSpec((1, D_MODEL), const)
    wspec = pl.BlockSpec((D_MODEL, D_MODEL), const)
    return pl.pallas_call(
        _cross_kernel,
        grid=(n // tm,),
        in_specs=[pl.BlockSpec((tm, D_MODEL), row),
                  pl.BlockSpec((nm, 2 * D_MODEL), lambda i: (i // nt, 0)),
                  vec, wspec, wspec, vec,
                  pl.BlockSpec((2, _ROUTER_ROWS, D_MODEL), lambda i: (0, 0, 0)),
                  pl.BlockSpec((_ROUTER_ROWS, 1), const)],
        out_specs=[pl.BlockSpec((tm, D_MODEL), row),
                   pl.BlockSpec((tm, D_MODEL), row),
                   pl.BlockSpec((SUBLANES, tm), lambda i: (0, i))],
        out_shape=[jax.ShapeDtypeStruct((n, D_MODEL), F32),
                   jax.ShapeDtypeStruct((n, D_MODEL), F32),
                   jax.ShapeDtypeStruct((SUBLANES, n), F32)],
        compiler_params=_params("parallel"),
        name="cross_attention",
    )(x1, kv, gx, w_xq, w_xo, gf, w_router, b_router)


def _expert_kernel(blk_ref, xb_ref, w1_ref, w3_ref, w2_ref, o_ref):
    n_used = blk_ref[pl.num_programs(0)]

    @pl.when(pl.program_id(0) < n_used)
    def _():
        xb = xb_ref[...].astype(MXU_DTYPE)
        a = _dot(xb, w1_ref[...])
        hmid = a * jax.nn.sigmoid(a) * _dot(xb, w3_ref[...])
        o_ref[...] = _dot(hmid, w2_ref[...])

    @pl.when(pl.program_id(0) >= n_used)
    def _():
        o_ref[...] = jnp.zeros_like(o_ref)


def _experts(blk, xb, w1, w3, w2):
    cap = xb.shape[0]
    nblk = cap // MOE_BLOCK
    row = lambda i, e: (i, 0)
    grid_spec = pltpu.PrefetchScalarGridSpec(
        num_scalar_prefetch=1,
        grid=(nblk,),
        in_specs=[pl.BlockSpec((MOE_BLOCK, D_MODEL), row),
                  pl.BlockSpec((None, D_MODEL, D_EXPERT), lambda i, e: (e[i], 0, 0)),
                  pl.BlockSpec((None, D_MODEL, D_EXPERT), lambda i, e: (e[i], 0, 0)),
                  pl.BlockSpec((None, D_EXPERT, D_MODEL), lambda i, e: (e[i], 0, 0))],
        out_specs=pl.BlockSpec((MOE_BLOCK, D_MODEL), row),
    )
    return pl.pallas_call(
        _expert_kernel,
        grid_spec=grid_spec,
        out_shape=jax.ShapeDtypeStruct((cap, D_MODEL), F32),
        compiler_params=_params("arbitrary"),
        name="experts",
    )(blk, xb, w1, w3, w2)


def _final_kernel(x_ref, w_ref, g_ref, *refs):
    o_ref = refs[-1]
    tiles_per_split = pl.num_programs(0) // MOE_SPLITS
    for s in range(MOE_SPLITS):
        @pl.when(pl.program_id(0) // tiles_per_split == s)
        def _(s=s):
            moe = w_ref[:, 0:1] * refs[2 * s][...] + w_ref[:, 1:2] * refs[2 * s + 1][...]
            o_ref[...] = _rms(x_ref[...] + moe, g_ref[...])


def _final(x2, wts, g, ys):
    n = x2.shape[0]
    tm = ROW_TM
    tiles_per_split = n // tm // MOE_SPLITS
    row = lambda i: (i, 0)
    wide = pl.BlockSpec((tm, D_MODEL), row)

    def split_spec(s):
        return pl.BlockSpec((tm, D_MODEL), lambda i: (jnp.clip(i - s * tiles_per_split, 0, tiles_per_split - 1), 0))

    return pl.pallas_call(
        _final_kernel,
        grid=(n // tm,),
        in_specs=[wide, pl.BlockSpec((tm, EXP_TOPK), row), pl.BlockSpec((1, D_MODEL), lambda i: (0, 0))] + [
            split_spec(s) for s in range(MOE_SPLITS) for _ in range(EXP_TOPK)],
        out_specs=wide,
        out_shape=jax.ShapeDtypeStruct((n, D_MODEL), F32),
        compiler_params=_params("parallel"),
        name="final_norm",
    )(x2, wts, g, *[y for pair in ys for y in pair])


def _route(eid, tok0):
    n_tok = eid.shape[1]
    eid = eid.reshape(-1)
    n_asg = eid.shape[0]
    iota = jnp.arange(n_asg, dtype=jnp.int32)
    se, order = lax.sort_key_val(eid, iota)
    counts = jnp.sum((jnp.arange(N_EXPERTS, dtype=jnp.int32)[:, None] == eid[None, :]).astype(jnp.int32), axis=1)
    padded = (counts + MOE_BLOCK - 1) // MOE_BLOCK * MOE_BLOCK
    starts = jnp.cumsum(counts) - counts
    pends = jnp.cumsum(padded)
    pstarts = pends - padded
    cap = ((n_asg + MOE_BLOCK - 1) // MOE_BLOCK + N_EXPERTS) * MOE_BLOCK
    nblk = cap // MOE_BLOCK
    blk_e = jnp.minimum(jnp.searchsorted(pends, jnp.arange(nblk) * MOE_BLOCK, side='right', method='compare_all'),
                        N_EXPERTS - 1).astype(jnp.int32)
    per_row = lambda a: jnp.repeat(a[blk_e], MOE_BLOCK)
    off = jnp.arange(cap, dtype=jnp.int32) - per_row(pstarts)
    asg = order[jnp.clip(per_row(starts) + off, 0, n_asg - 1)]
    buf_tok = tok0 + jnp.where(off < per_row(counts), asg % n_tok, 0)
    dest_sorted = iota + (pstarts - starts)[se]
    _, pos = lax.sort_key_val(order, dest_sorted)
    n_used = (pends[-1] // MOE_BLOCK).astype(jnp.int32)
    return buf_tok, jnp.concatenate([blk_e, n_used[None]]), pos.reshape(EXP_TOPK, n_tok)


def kernel(x, mem, norm_mix_g, w_in, w_ret_o, w_nsa_o, w_out, cmp_pe_k, cmp_w1_k, cmp_w2_k, cmp_pe_v,
           cmp_w1_v, cmp_w2_v, norm_x_g, norm_mem_g, w_xq, w_xkv, w_xo, norm_ffn_g, w_grp, b_grp, w_rt,
           b_rt, w_e1, w_e3, w_e2, norm_f_g):
    bsz, seq, _ = x.shape
    n = bsz * seq
    assert seq % PROJ_TM == 0 and seq % (2 * ATT_TQ) == 0 and w_in.shape[0] == 1
    assert n % (ROW_TM * MOE_SPLITS) == 0
    cast = lambda a: a.astype(MXU_DTYPE)
    xc = x.reshape(n, D_MODEL)
    l = 0

    w_main, w_t, w_gates = _split_w_in(w_in[l])
    qk, rv, rg, ckv, sk, wk, ngl, qt, svt, wvt = _proj(
        xc, norm_mix_g[l][None, :], w_main, w_t, _rope_tables(seq), bsz, seq)
    y_ret = _retention(qk, rv, rg, bsz, seq)
    w2 = jnp.stack([cmp_w2_k[l], cmp_w2_v[l]])
    cmp_k, cmp_vt = _compress(ckv, jnp.stack([cmp_pe_k[l], cmp_pe_v[l]]),
                              cast(jnp.stack([cmp_w1_k[l], cmp_w1_v[l]])),
                              cast(w2), cast(w2.transpose(0, 2, 1)), bsz, seq)
    o_cmp, selt = _cmp_attention(qt, cmp_k, cmp_vt, bsz, seq)
    o_sel = _flash_attention(qt, sk, svt, selt, bsz, seq)
    o_win = _flash_attention(qt, wk, wvt, None, bsz, seq)
    x1 = _merge(xc, y_ret, o_cmp, o_sel, o_win, ngl, norm_mix_g[l][None, :], w_gates,
                cast(w_ret_o[l]), cast(w_nsa_o[l]), cast(w_out[l]))

    kv = _mem_kv(mem.reshape(-1, D_MODEL), norm_mem_g[l][None, :], cast(w_xkv[l]), bsz)
    gap = _ROUTER_E0 - N_EGROUPS
    w_router = jnp.concatenate([w_grp[l].T, jnp.zeros((gap, D_MODEL), F32), w_rt[l].T], axis=0)
    wr_hi = cast(w_router)
    wr_mid = cast(w_router - wr_hi.astype(F32))
    b_router = jnp.concatenate([b_grp[l], jnp.zeros((gap,), F32), b_rt[l]])[:, None]
    x2, hf, routed = _cross(x1, kv, norm_x_g[l][None, :], cast(w_xq[l]), cast(w_xo[l]),
                            norm_ffn_g[l][None, :], jnp.stack([wr_hi, wr_mid]), b_router, bsz, seq)

    eid = routed[0:EXP_TOPK].astype(jnp.int32)
    wts = routed[EXP_TOPK:2 * EXP_TOPK].T
    per_split = n // MOE_SPLITS
    ys = []
    for s in range(MOE_SPLITS):
        buf_tok, blk, pos = _route(eid[:, s * per_split:(s + 1) * per_split], s * per_split)
        y = _experts(blk, hf[buf_tok], w_e1[l], w_e3[l], w_e2[l])
        ys.append([y[pos[j]] for j in range(EXP_TOPK)])
    out = _final(x2, wts, norm_f_g[None, :], ys)
    return out.reshape(bsz, seq, D_MODEL)
```

```python
import functools

import numpy as np
import jax
import jax.numpy as jnp
from jax import lax
from jax.experimental import pallas as pl
from jax.experimental.pallas import tpu as pltpu

MXU_DTYPE = jnp.bfloat16
F32 = jnp.float32

D_MODEL = 1024
N_MEM = 256
EPS = 1e-6
NEG_INF = -1e30
SEL_FORCE = 1e4

R_HEADS = 4
R_DK = 128
R_DV = 256
R_CHUNK = 128
ROPE_BASE = 10000.0

NSA_HEADS = 8
NSA_GROUPS = 2
NSA_HPG = NSA_HEADS // NSA_GROUPS
NSA_DH = 128
CMP_LEN = 32
CMP_STRIDE = 16
SEL_LEN = 64
SEL_TOPK = 16
WINDOW = 512

X_HEADS = 4
X_DH = D_MODEL // X_HEADS

N_EGROUPS = 4
EXP_PER_GROUP = 8
N_EXPERTS = N_EGROUPS * EXP_PER_GROUP
EXP_TOPK = 2
D_EXPERT = 512
MOE_BLOCK = 256

RET_QK = R_HEADS * R_DK
RET_V = R_HEADS * R_DV
NSA_Q = NSA_HEADS * NSA_DH
NSA_KV = NSA_GROUPS * NSA_DH
SPLITS = (RET_QK, RET_QK, RET_V, RET_V, NSA_Q, NSA_KV, NSA_KV, NSA_KV, NSA_KV, NSA_KV, NSA_KV,
          3 * NSA_HEADS, D_MODEL, D_MODEL)

_EXP2_SCALE = (NSA_DH ** -0.5) * float(np.log2(np.e))

LANES = 128
SUBLANES = 8
VMEM_LIMIT = 56 * 1024 * 1024

PROJ_TM = 512
RET_ROWS = 512
ATT_TQ = 256
ATT_TK = 256
ROW_TM = 512
MOE_SPLITS = 2
VT_ROWS = NSA_DH + 2 * SUBLANES


def _params(*sem):
    return pltpu.CompilerParams(dimension_semantics=sem, vmem_limit_bytes=VMEM_LIMIT)


def _dot(a, b):
    return jnp.dot(a.astype(MXU_DTYPE), b.astype(MXU_DTYPE), preferred_element_type=F32)


def _dot_nt(a, b):
    return lax.dot_general(a.astype(MXU_DTYPE), b.astype(MXU_DTYPE), (((1,), (1,)), ((), ())),
                           preferred_element_type=F32)


def _dot_tn(a, b):
    return lax.dot_general(a.astype(MXU_DTYPE), b.astype(MXU_DTYPE), (((0,), (0,)), ((), ())),
                           preferred_element_type=F32)


def _split3(p):
    hi = p.astype(MXU_DTYPE)
    r1 = p - hi.astype(F32)
    mid = r1.astype(MXU_DTYPE)
    lo = (r1 - mid.astype(F32)).astype(MXU_DTYPE)
    return hi, mid, lo


def _rms(x, g):
    return x * lax.rsqrt(jnp.mean(x * x, axis=-1, keepdims=True) + EPS) * g


_C_RQK = 0
_C_RV = _C_RQK + 2 * RET_QK
_C_RG = _C_RV + RET_V
_C_CKV = _C_RG + RET_V
_C_SK = _C_CKV + 2 * NSA_KV
_C_WK = _C_SK + NSA_KV
_C_NGL = _C_WK + NSA_KV
_C_END = _C_NGL + LANES
_R_NQ = 0
_R_SV = _R_NQ + NSA_Q
_R_WV = _R_SV + NSA_KV
_R_END = _R_WV + NSA_KV


def _proj_kernel(x_ref, g_ref, w_ref, wt_ref, cos_ref, sin_up_ref, sin_dn_ref,
                 qk_ref, rv_ref, rg_ref, ckv_ref, sk_ref, wk_ref, ngl_ref, nqt_ref, svt_ref, wvt_ref):
    hb = _rms(x_ref[...], g_ref[...]).astype(MXU_DTYPE)
    tm = hb.shape[0]

    def mm(off, width):
        return jnp.dot(hb, w_ref[:, off:off + width], preferred_element_type=F32)

    def mm_t(off, height):
        return _dot_nt(wt_ref[off:off + height, :], hb)

    cos = cos_ref[...]
    sin_up = sin_up_ref[...]
    sin_dn = sin_dn_ref[...]
    qk = mm(_C_RQK, 2 * RET_QK)
    for i in range(2 * R_HEADS):
        t = qk[:, i * R_DK:(i + 1) * R_DK]
        r = t * cos + pltpu.roll(t, R_DK - 1, axis=1) * sin_up + pltpu.roll(t, 1, axis=1) * sin_dn
        if i >= R_HEADS:
            r = r * (R_DK ** -0.5)
        qk_ref[:, i * R_DK:(i + 1) * R_DK] = r.astype(qk_ref.dtype)
    rv_ref[...] = mm(_C_RV, RET_V).astype(rv_ref.dtype)
    rg_ref[...] = mm(_C_RG, RET_V)
    ckv_ref[...] = mm(_C_CKV, 2 * NSA_KV)
    sk_ref[...] = mm(_C_SK, NSA_KV).astype(sk_ref.dtype)
    wk_ref[...] = mm(_C_WK, NSA_KV).astype(wk_ref.dtype)
    ngl_ref[...] = mm(_C_NGL, LANES)

    nqt = (mm_t(_R_NQ, NSA_Q) * _EXP2_SCALE).astype(nqt_ref.dtype)
    for g in range(NSA_GROUPS):
        for j in range(tm // ATT_TQ):
            for hh in range(NSA_HPG):
                head = g * NSA_HPG + hh
                nqt_ref[g, j, :, hh * ATT_TQ:(hh + 1) * ATT_TQ] = (
                    nqt[head * NSA_DH:(head + 1) * NSA_DH, j * ATT_TQ:(j + 1) * ATT_TQ])
    for off, out_ref in ((_R_SV, svt_ref), (_R_WV, wvt_ref)):
        vt = mm_t(off, NSA_KV).astype(out_ref.dtype)
        for g in range(NSA_GROUPS):
            for j in range(tm // ATT_TK):
                out_ref[g, j, 0:NSA_DH] = vt[g * NSA_DH:(g + 1) * NSA_DH, j * ATT_TK:(j + 1) * ATT_TK]
                out_ref[g, j, NSA_DH:VT_ROWS] = jnp.ones((VT_ROWS - NSA_DH, ATT_TK), out_ref.dtype)


def _split_w_in(w_in):
    offs = np.cumsum((0,) + SPLITS)
    rq, rk, rv, rg, nq, ck, cv, sk, sv, wk, wv, ngl, ga, gb = [
        w_in[:, offs[i]:offs[i + 1]] for i in range(len(SPLITS))]
    ngl = jnp.pad(ngl, ((0, 0), (0, LANES - ngl.shape[1])))
    w = jnp.concatenate([rq, rk, rv, rg, ck, cv, sk, wk, ngl], axis=1).astype(MXU_DTYPE)
    wt = jnp.concatenate([nq, sv, wv], axis=1).T.astype(MXU_DTYPE)
    w_gates = jnp.concatenate([ga, gb], axis=1).astype(MXU_DTYPE)
    return w, wt, w_gates


def _rope_tables(seq):
    pos = jnp.arange(seq, dtype=F32)
    inv_freq = ROPE_BASE ** (-jnp.arange(0, R_DK, 2, dtype=F32) / R_DK)
    ang = pos[:, None] * inv_freq[None, :]
    cos = jnp.repeat(jnp.cos(ang), 2, axis=1)
    sin = jnp.repeat(jnp.sin(ang), 2, axis=1)
    even = (jnp.arange(R_DK) % 2 == 0)[None, :]
    return cos, jnp.where(even, -sin, 0.0), jnp.where(even, 0.0, sin)


def _proj(x2d, g, w, wt, tables, bsz, seq):
    n = x2d.shape[0]
    tm = PROJ_TM
    nt = seq // tm
    row = lambda i: (i, 0)
    const = lambda i: (0, 0)
    tile = lambda i: (i // nt, 0, i % nt, 0, 0)
    table = pl.BlockSpec((tm, R_DK), lambda i: (i % nt, 0))
    widths = (2 * RET_QK, RET_V, RET_V, 2 * NSA_KV, NSA_KV, NSA_KV, LANES)
    dtypes = (MXU_DTYPE, MXU_DTYPE, F32, F32, MXU_DTYPE, MXU_DTYPE, F32)
    width = NSA_HPG * ATT_TQ
    vt_shape = jax.ShapeDtypeStruct((bsz, NSA_GROUPS, seq // ATT_TK, VT_ROWS, ATT_TK), MXU_DTYPE)
    vt_spec = pl.BlockSpec((None, NSA_GROUPS, tm // ATT_TK, VT_ROWS, ATT_TK), tile)
    return pl.pallas_call(
        _proj_kernel,
        grid=(n // tm,),
        in_specs=[pl.BlockSpec((tm, D_MODEL), row),
                  pl.BlockSpec((1, D_MODEL), const),
                  pl.BlockSpec((D_MODEL, _C_END), const, pipeline_mode=pl.Buffered(1)),
                  pl.BlockSpec((_R_END, D_MODEL), const, pipeline_mode=pl.Buffered(1)),
                  table, table, table],
        out_specs=[pl.BlockSpec((tm, wd), row) for wd in widths] + [
            pl.BlockSpec((None, NSA_GROUPS, tm // ATT_TQ, NSA_DH, width), tile), vt_spec, vt_spec],
        out_shape=[jax.ShapeDtypeStruct((n, wd), dt) for wd, dt in zip(widths, dtypes)] + [
            jax.ShapeDtypeStruct((bsz, NSA_GROUPS, seq // ATT_TQ, NSA_DH, width), MXU_DTYPE), vt_shape, vt_shape],
        compiler_params=_params("parallel"),
        name="proj",
    )(x2d, g, w, wt, *tables)


def _ret_kernel(qk_ref, v_ref, g_ref, decay_ref, zeta_ref, xi_ref, cd_ref, y_ref, state_ref):
    @pl.when(pl.program_id(1) == 0)
    def _():
        state_ref[...] = jnp.zeros_like(state_ref)

    for c in range(RET_ROWS // R_CHUNK):
        rows = slice(c * R_CHUNK, (c + 1) * R_CHUNK)
        for h in range(R_HEADS):
            q = qk_ref[rows, h * R_DK:(h + 1) * R_DK]
            k = qk_ref[rows, RET_QK + h * R_DK:RET_QK + (h + 1) * R_DK]
            v = v_ref[rows, h * R_DV:(h + 1) * R_DV]
            st = state_ref[h]
            scores = _dot_nt(q, k) * decay_ref[h]
            o = _dot(scores, v) + _dot(q, st) * xi_ref[h]
            o = o * lax.rsqrt(jnp.mean(o * o, axis=-1, keepdims=True) + EPS)
            g = g_ref[rows, h * R_DV:(h + 1) * R_DV]
            y_ref[rows, h * R_DV:(h + 1) * R_DV] = (g * jax.nn.sigmoid(g) * o).astype(y_ref.dtype)
            kz = k.astype(F32) * zeta_ref[h]
            state_ref[h] = st * cd_ref[h] + _dot_tn(kz, v)


def _retention(qk, rv, rg, bsz, seq):
    n = qk.shape[0]
    nt = seq // RET_ROWS
    log_g = jnp.log1p(-jnp.exp2(-5.0 - jnp.arange(R_HEADS, dtype=F32)))
    idx = jnp.arange(R_CHUNK, dtype=F32)
    diff = idx[:, None] - idx[None, :]
    decay = jnp.where(diff >= 0, jnp.exp(log_g[:, None, None] * jnp.maximum(diff, 0.0)), 0.0)
    zeta = jnp.exp(log_g[:, None] * (R_CHUNK - 1.0 - idx)[None, :])[:, :, None]
    xi = jnp.exp(log_g[:, None] * (idx + 1.0)[None, :])[:, :, None]
    cd = jnp.exp(log_g * R_CHUNK)[:, None, None]
    row = lambda b, i: (b * nt + i, 0)
    const3 = lambda b, i: (0, 0, 0)
    return pl.pallas_call(
        _ret_kernel,
        grid=(bsz, nt),
        in_specs=[pl.BlockSpec((RET_ROWS, 2 * RET_QK), row),
                  pl.BlockSpec((RET_ROWS, RET_V), row),
                  pl.BlockSpec((RET_ROWS, RET_V), row),
                  pl.BlockSpec((R_HEADS, R_CHUNK, R_CHUNK), const3),
                  pl.BlockSpec((R_HEADS, R_CHUNK, 1), const3),
                  pl.BlockSpec((R_HEADS, R_CHUNK, 1), const3),
                  pl.BlockSpec((R_HEADS, 1, 1), const3)],
        out_specs=pl.BlockSpec((RET_ROWS, RET_V), row),
        out_shape=jax.ShapeDtypeStruct((n, RET_V), MXU_DTYPE),
        scratch_shapes=[pltpu.VMEM((R_HEADS, R_DK, R_DV), F32)],
        compiler_params=_params("parallel", "arbitrary"),
        name="retention",
    )(qk, rv, rg, decay, zeta, xi, cd)


def _compress_kernel(x_ref, pe_ref, w1_ref, w2_ref, w2t_ref, o_ref, ot_ref, buf_ref, *, seq):
    ncp = seq // CMP_STRIDE
    buf_ref[0:seq, :] = x_ref[...]
    buf_ref[seq:seq + LANES, :] = jnp.zeros((LANES, NSA_DH), F32)
    acc = jnp.zeros((ncp, NSA_DH), F32)
    for l in range(CMP_LEN):
        xl = buf_ref[pl.ds(l, ncp, stride=CMP_STRIDE), :] + pe_ref[l:l + 1, :]
        acc = acc + _dot(xl, w1_ref[l])
    hid = jax.nn.gelu(acc)
    o_ref[...] = _dot(hid, w2_ref[...]).astype(o_ref.dtype)
    ot_ref[...] = _dot_nt(w2t_ref[...], hid).astype(ot_ref.dtype)


def _compress(ckv, pe, w1, w2, w2t, bsz, seq):
    ncp = seq // CMP_STRIDE
    nj = 2 * NSA_GROUPS
    wsel = lambda b, j: (j // NSA_GROUPS, 0, 0)
    return pl.pallas_call(
        functools.partial(_compress_kernel, seq=seq),
        grid=(bsz, nj),
        in_specs=[pl.BlockSpec((seq, NSA_DH), lambda b, j: (b, j)),
                  pl.BlockSpec((None, CMP_LEN, NSA_DH), wsel),
                  pl.BlockSpec((None, CMP_LEN, NSA_DH, NSA_DH), lambda b, j: (j // NSA_GROUPS, 0, 0, 0)),
                  pl.BlockSpec((None, NSA_DH, NSA_DH), wsel),
                  pl.BlockSpec((None, NSA_DH, NSA_DH), wsel)],
        out_specs=[pl.BlockSpec((None, None, ncp, NSA_DH), lambda b, j: (b, j, 0, 0)),
                   pl.BlockSpec((None, None, NSA_DH, ncp), lambda b, j: (b, j, 0, 0))],
        out_shape=[jax.ShapeDtypeStruct((bsz, nj, ncp, NSA_DH), MXU_DTYPE),
                   jax.ShapeDtypeStruct((bsz, nj, NSA_DH, ncp), MXU_DTYPE)],
        scratch_shapes=[pltpu.VMEM((seq + LANES, NSA_DH), F32)],
        compiler_params=_params("parallel", "parallel"),
        name="compress",
    )(ckv, pe, w1, w2, w2t)


def _cmp_attn_kernel(qt_ref, k_ref, vt_ref, ov_ref, o_ref, selt_ref, *, ncp, nb):
    tq = ATT_TQ
    t0 = pl.program_id(2) * tq
    t = t0 + lax.broadcasted_iota(jnp.int32, (ncp, tq), 1)
    n = lax.broadcasted_iota(jnp.int32, (ncp, tq), 0)
    valid = (n * CMP_STRIDE + (CMP_LEN - 1) <= t) & (n < ncp - 1)
    any_valid = (t0 + lax.broadcasted_iota(jnp.int32, (1, tq), 1)) >= CMP_LEN - 1
    valid = jnp.concatenate([valid] * NSA_HPG, axis=1)
    any_valid = jnp.concatenate([any_valid] * NSA_HPG, axis=1)
    s = jnp.where(valid, _dot(k_ref[...], qt_ref[...]), NEG_INF)
    e = jnp.exp2(s - jnp.max(s, axis=0, keepdims=True))
    p = e * jnp.where(any_valid, 1.0 / jnp.sum(e, axis=0, keepdims=True), 0.0)
    ot = _dot(vt_ref[...], p)
    for h in range(NSA_HPG):
        o_ref[:, h * NSA_DH:(h + 1) * NSA_DH] = ot[:, h * tq:(h + 1) * tq].T
    psum = sum(p[:, h * tq:(h + 1) * tq] for h in range(NSA_HPG))

    imp = sum(_dot(ov_ref[...], part) for part in _split3(psum))
    j = lax.broadcasted_iota(jnp.int32, (nb, tq), 0)
    tb = (t0 + lax.broadcasted_iota(jnp.int32, (nb, tq), 1)) // SEL_LEN
    forced = (j == 0) | (j == tb) | (j == tb - 1)
    imp = jnp.where(j > tb, -SEL_FORCE, jnp.where(forced, SEL_FORCE, imp))

    sub = SUBLANES
    grp = [imp[r * sub:(r + 1) * sub] for r in range(nb // sub)]
    cnt = [jnp.zeros((sub, tq), F32) for _ in grp]
    for i in range(nb):
        row = jnp.broadcast_to(imp[i:i + 1, :], (sub, tq))
        for r in range(nb // sub):
            if r * sub > i:
                beats = jnp.where(row >= grp[r], 1.0, 0.0)
            elif r * sub + sub - 1 < i:
                beats = jnp.where(row > grp[r], 1.0, 0.0)
            else:
                jr = r * sub + lax.broadcasted_iota(jnp.int32, (sub, tq), 0)
                beats = jnp.where(jr > i, jnp.where(row >= grp[r], 1.0, 0.0), jnp.where(row > grp[r], 1.0, 0.0))
            cnt[r] = cnt[r] + beats
    k_sel = min(SEL_TOPK, nb)
    for r in range(nb // sub):
        selt_ref[r * sub:(r + 1) * sub, :] = jnp.where(cnt[r] < k_sel, 1.0, 0.0)


def _cmp_attention(qt, cmp_k, cmp_vt, bsz, seq):
    n = bsz * seq
    ncp = seq // CMP_STRIDE
    nb = seq // SEL_LEN
    nt = seq // ATT_TQ
    gw = NSA_HPG * NSA_DH
    cstart = np.arange(ncp) * CMP_STRIDE
    jstart = np.arange(nb) * SEL_LEN
    ov = ((cstart[None, :] < jstart[:, None] + SEL_LEN) & (cstart[None, :] + CMP_LEN > jstart[:, None])
          & (np.arange(ncp)[None, :] < ncp - 1))
    ov = jnp.asarray(ov, MXU_DTYPE)
    return pl.pallas_call(
        functools.partial(_cmp_attn_kernel, ncp=ncp, nb=nb),
        grid=(bsz, NSA_GROUPS, nt),
        in_specs=[pl.BlockSpec((None, None, None, NSA_DH, NSA_HPG * ATT_TQ), lambda b, g, i: (b, g, i, 0, 0)),
                  pl.BlockSpec((None, None, ncp, NSA_DH), lambda b, g, i: (b, g, 0, 0)),
                  pl.BlockSpec((None, None, NSA_DH, ncp), lambda b, g, i: (b, NSA_GROUPS + g, 0, 0)),
                  pl.BlockSpec((nb, ncp), lambda b, g, i: (0, 0))],
        out_specs=[pl.BlockSpec((ATT_TQ, gw), lambda b, g, i: (b * nt + i, g)),
                   pl.BlockSpec((None, None, nb, ATT_TQ), lambda b, g, i: (b, g, 0, i))],
        out_shape=[jax.ShapeDtypeStruct((n, NSA_Q), F32),
                   jax.ShapeDtypeStruct((bsz, NSA_GROUPS, nb, seq), F32)],
        compiler_params=_params("parallel", "parallel", "parallel"),
        name="cmp_attention",
    )(qt, cmp_k, cmp_vt, ov)


_MASK_ALL = 1 << 28


def _flash_loop(qt_ref, k_ref, vt_ref, acc_ref, s_refs, n_tiles, tile_of, valid_of, max_steps=None):
    width = NSA_HPG * ATT_TQ

    def scores(kt):
        k = k_ref[pl.ds(pl.multiple_of(kt * ATT_TK, ATT_TK), ATT_TK), :]
        return _dot(k, qt_ref[...])

    def half_step(step, s_cur, s_nxt, m, prefetch=True):
        kt = tile_of(jnp.minimum(step, n_tiles - 1))
        if prefetch:
            s_nxt[...] = scores(tile_of(jnp.minimum(step + 1, n_tiles - 1)))
        valid = valid_of(kt, kt * ATT_TK + jnp.where(step < n_tiles, 0, _MASK_ALL))
        valid = jnp.concatenate([valid] * NSA_HPG, axis=1)
        s = jnp.where(valid, s_cur[...], NEG_INF)
        m_new = jnp.maximum(m, jnp.max(s, axis=0, keepdims=True))
        p = jnp.exp2(s - m_new)
        acc_ref[...] = jnp.exp2(m - m_new) * acc_ref[...] + _dot(vt_ref[kt], p)
        return m_new

    def body(i, m):
        m = half_step(2 * i, s_refs[0], s_refs[1], m)
        return half_step(2 * i + 1, s_refs[1], s_refs[0], m)

    acc_ref[...] = jnp.zeros_like(acc_ref)
    s_refs[0][...] = scores(tile_of(0))
    m = jnp.full((1, width), NEG_INF, F32)
    if max_steps is None:
        lax.fori_loop(0, (n_tiles + 1) // 2, body, m)
    else:
        for step in range(max_steps):
            m = half_step(step, s_refs[step % 2], s_refs[(step + 1) % 2], m, prefetch=step + 1 < max_steps)


def _flash_finish(o_ref, acc_ref):
    inv = 1.0 / acc_ref[NSA_DH:NSA_DH + 1, :]
    for h in range(NSA_HPG):
        cols = slice(h * ATT_TQ, (h + 1) * ATT_TQ)
        o_ref[:, h * NSA_DH:(h + 1) * NSA_DH] = (acc_ref[0:NSA_DH, cols] * inv[:, cols]).T


def _sel_attn_kernel(qt_ref, k_ref, vt_ref, selt_ref, o_ref, acc_ref, s0_ref, s1_ref):
    qi = pl.program_id(2)
    tpos = qi * ATT_TQ + lax.broadcasted_iota(jnp.int32, (ATT_TK, ATT_TQ), 1)
    row = lax.broadcasted_iota(jnp.int32, (ATT_TK, ATT_TQ), 0)
    blocks_per_tile = ATT_TK // SEL_LEN

    def valid_of(kt, key0):
        picked = jnp.concatenate(
            [jnp.broadcast_to(selt_ref[pl.ds(kt * blocks_per_tile + jb, 1), :], (SEL_LEN, ATT_TQ))
             for jb in range(blocks_per_tile)], axis=0)
        return (picked > 0.5) & (row + key0 <= tpos)

    _flash_loop(qt_ref, k_ref, vt_ref, acc_ref, (s0_ref, s1_ref), (qi + 1) * (ATT_TQ // ATT_TK),
                lambda step: step, valid_of)
    _flash_finish(o_ref, acc_ref)


def _win_attn_kernel(qt_ref, k_ref, vt_ref, o_ref, acc_ref, s0_ref, s1_ref):
    qi = pl.program_id(2)
    tpos = qi * ATT_TQ + lax.broadcasted_iota(jnp.int32, (ATT_TK, ATT_TQ), 1)
    row = lax.broadcasted_iota(jnp.int32, (ATT_TK, ATT_TQ), 0)
    last = (qi + 1) * (ATT_TQ // ATT_TK) - 1
    n_tiles = jnp.minimum(last + 1, (ATT_TQ + WINDOW) // ATT_TK)

    def valid_of(kt, key0):
        kpos = row + key0
        return (kpos <= tpos) & (kpos > tpos - WINDOW)

    _flash_loop(qt_ref, k_ref, vt_ref, acc_ref, (s0_ref, s1_ref), n_tiles,
                lambda step: last - step, valid_of, max_steps=(ATT_TQ + WINDOW) // ATT_TK)
    _flash_finish(o_ref, acc_ref)


def _flash_attention(qt, k, vt, selt, bsz, seq):
    n = bsz * seq
    nt = seq // ATT_TQ
    nkt = seq // ATT_TK
    nb = seq // SEL_LEN
    gw = NSA_HPG * NSA_DH
    width = NSA_HPG * ATT_TQ
    in_specs = [pl.BlockSpec((None, None, None, NSA_DH, width), lambda b, g, i: (b, g, i, 0, 0)),
                pl.BlockSpec((seq, NSA_DH), lambda b, g, i: (b, g)),
                pl.BlockSpec((None, None, nkt, VT_ROWS, ATT_TK), lambda b, g, i: (b, g, 0, 0, 0))]
    args = [qt, k, vt]
    if selt is None:
        body = _win_attn_kernel
        name = "window_attention"
    else:
        body = _sel_attn_kernel
        name = "selected_attention"
        in_specs.append(pl.BlockSpec((None, None, nb, ATT_TQ), lambda b, g, i: (b, g, 0, i)))
        args.append(selt)
    return pl.pallas_call(
        body,
        grid=(bsz, NSA_GROUPS, nt),
        in_specs=in_specs,
        out_specs=pl.BlockSpec((ATT_TQ, gw), lambda b, g, i: (b * nt + i, g)),
        out_shape=jax.ShapeDtypeStruct((n, NSA_Q), F32),
        scratch_shapes=[pltpu.VMEM((VT_ROWS, width), F32),
                        pltpu.VMEM((ATT_TK, width), F32),
                        pltpu.VMEM((ATT_TK, width), F32)],
        compiler_params=_params("parallel", "parallel", "parallel"),
        name=name,
    )(*args)


def _merge_kernel(x_ref, yr_ref, oc_ref, os_ref, ow_ref, ngl_ref, g_ref, wg_ref, wr_ref, wn_ref, wo_ref, o_ref):
    tm = x_ref.shape[0]
    hb = _rms(x_ref[...], g_ref[...]).astype(MXU_DTYPE)
    ga = jnp.dot(hb, wg_ref[:, :D_MODEL], preferred_element_type=F32)
    gb = jnp.dot(hb, wg_ref[:, D_MODEL:], preferred_element_type=F32)
    gates = jax.nn.sigmoid(ngl_ref[...])
    parts = []
    for h in range(NSA_HEADS):
        cols = slice(h * NSA_DH, (h + 1) * NSA_DH)

        def gate(br):
            return jnp.broadcast_to(gates[:, 3 * h + br:3 * h + br + 1], (tm, NSA_DH))

        parts.append(gate(0) * oc_ref[:, cols] + gate(1) * os_ref[:, cols] + gate(2) * ow_ref[:, cols])
    o_nsa = jnp.concatenate(parts, axis=1)
    y_ret = _dot(yr_ref[...], wr_ref[...])
    y_nsa = _dot(o_nsa, wn_ref[...])
    y = jax.nn.sigmoid(ga) * y_ret + jax.nn.sigmoid(gb) * y_nsa
    o_ref[...] = x_ref[...] + _dot(y, wo_ref[...])


def _merge(x2d, y_ret, o_cmp, o_sel, o_win, ngl, g_mix, w_gates, w_ret_o, w_nsa_o, w_out):
    n = x2d.shape[0]
    tm = ROW_TM
    row = lambda i: (i, 0)
    const = lambda i: (0, 0)
    wide = pl.BlockSpec((tm, D_MODEL), row)
    wspec = pl.BlockSpec((D_MODEL, D_MODEL), const)
    return pl.pallas_call(
        _merge_kernel,
        grid=(n // tm,),
        in_specs=[wide, wide, wide, wide, wide,
                  pl.BlockSpec((tm, LANES), row),
                  pl.BlockSpec((1, D_MODEL), const),
                  pl.BlockSpec((D_MODEL, 2 * D_MODEL), const),
                  wspec, wspec, wspec],
        out_specs=wide,
        out_shape=jax.ShapeDtypeStruct((n, D_MODEL), F32),
        compiler_params=_params("parallel"),
        name="merge",
    )(x2d, y_ret, o_cmp, o_sel, o_win, ngl, g_mix, w_gates, w_ret_o, w_nsa_o, w_out)


def _mem_kv_kernel(m_ref, g_ref, w_ref, o_ref):
    o_ref[...] = _dot(_rms(m_ref[...], g_ref[...]), w_ref[...]).astype(o_ref.dtype)


def _mem_kv(mem2d, g, w_xkv, bsz):
    nm = mem2d.shape[0] // bsz
    return pl.pallas_call(
        _mem_kv_kernel,
        grid=(bsz,),
        in_specs=[pl.BlockSpec((nm, D_MODEL), lambda b: (b, 0)),
                  pl.BlockSpec((1, D_MODEL), lambda b: (0, 0)),
                  pl.BlockSpec((D_MODEL, 2 * D_MODEL), lambda b: (0, 0))],
        out_specs=pl.BlockSpec((nm, 2 * D_MODEL), lambda b: (b, 0)),
        out_shape=jax.ShapeDtypeStruct((mem2d.shape[0], 2 * D_MODEL), MXU_DTYPE),
        compiler_params=_params("parallel"),
        name="mem_kv",
    )(mem2d, g, w_xkv)


def _pack_pairs(x):
    half = x.shape[1] // 2
    hi = lax.bitcast_convert_type(x[:, :half].astype(MXU_DTYPE).astype(F32), jnp.uint32)
    lo = lax.bitcast_convert_type(x[:, half:].astype(MXU_DTYPE).astype(F32), jnp.uint32)
    return (hi & jnp.uint32(0xFFFF0000)) | (lo >> 16)


def _unpack_pairs(u):
    hi = lax.bitcast_convert_type(u & jnp.uint32(0xFFFF0000), F32)
    lo = lax.bitcast_convert_type(u << 16, F32)
    return jnp.concatenate([hi, lo], axis=1)


_ROUTER_E0 = 2 * SUBLANES
_ROUTER_ROWS = _ROUTER_E0 + N_EXPERTS


def _top2_route(lgt):
    sub = SUBLANES
    t = lgt.shape[1]
    rowid = lax.broadcasted_iota(jnp.int32, (sub, t), 0)
    first = lambda hit: jnp.min(jnp.where(hit, rowid, sub), axis=0, keepdims=True)
    lg = jnp.where(rowid < N_EGROUPS, lgt[0:sub], NEG_INF)
    gmax = jnp.max(lg, axis=0, keepdims=True)
    grp = first(lg == gmax)
    g_gate = 1.0 / jnp.sum(jnp.exp(lg - gmax), axis=0, keepdims=True)
    experts_of = lambda g: lgt[_ROUTER_E0 + g * EXP_PER_GROUP:_ROUTER_E0 + (g + 1) * EXP_PER_GROUP]
    le = experts_of(N_EGROUPS - 1)
    for g in range(N_EGROUPS - 2, -1, -1):
        le = jnp.where(grp == g, experts_of(g), le)
    ex = jnp.exp(le - jnp.max(le, axis=0, keepdims=True))
    pe = ex / jnp.sum(ex, axis=0, keepdims=True)
    p0 = jnp.max(pe, axis=0, keepdims=True)
    i0 = first(pe == p0)
    rest = jnp.where(rowid == i0, -1.0, pe)
    p1 = jnp.max(rest, axis=0, keepdims=True)
    i1 = first(rest == p1)
    den = p0 + p1
    base = grp * EXP_PER_GROUP
    return jnp.concatenate([(base + i0).astype(F32), (base + i1).astype(F32),
                            g_gate * p0 / den, g_gate * p1 / den], axis=0)


def _cross_kernel(x_ref, kv_ref, gx_ref, wq_ref, wo_ref, gf_ref, wr_ref, br_ref, x2_ref, hf_ref, rt_ref):
    x = x_ref[...]
    q = _dot(_rms(x, gx_ref[...]), wq_ref[...])
    heads = []
    for h in range(X_HEADS):
        k = kv_ref[:, h * X_DH:(h + 1) * X_DH]
        v = kv_ref[:, D_MODEL + h * X_DH:D_MODEL + (h + 1) * X_DH]
        s = _dot_nt(q[:, h * X_DH:(h + 1) * X_DH], k) * (X_DH ** -0.5)
        e = jnp.exp(s - jnp.max(s, axis=-1, keepdims=True))
        p = e / jnp.sum(e, axis=-1, keepdims=True)
        heads.append(_dot(p, v))
    x2 = x + _dot(jnp.concatenate(heads, axis=1), wo_ref[...])
    x2_ref[...] = x2
    hf = _rms(x2, gf_ref[...])
    hf_ref[...] = _pack_pairs(hf)
    h_hi, h_mid, _ = _split3(hf)
    w_hi = wr_ref[0]
    w_mid = wr_ref[1]
    lgt = (_dot_nt(w_hi, h_hi) + (_dot_nt(w_hi, h_mid) + _dot_nt(w_mid, h_hi))) + br_ref[...]
    rt_ref[...] = jnp.concatenate([_top2_route(lgt), jnp.zeros((SUBLANES - 4, lgt.shape[1]), F32)], axis=0)


def _cross(x1, kv, gx, w_xq, w_xo, gf, w_router, b_router, bsz, seq):
    n = x1.shape[0]
    tm = ROW_TM
    nt = seq // tm
    nm = kv.shape[0] // bsz
    row = lambda i: (i, 0)
    const = lambda i: (0, 0)
    vec = pl.BlockSpec((1, D_MODEL), const)
    wspec = pl.BlockSpec((D_MODEL, D_MODEL), const)
    return pl.pallas_call(
        _cross_kernel,
        grid=(n // tm,),
        in_specs=[pl.BlockSpec((tm, D_MODEL), row),
                  pl.BlockSpec((nm, 2 * D_MODEL), lambda i: (i // nt, 0)),
                  vec, wspec, wspec, vec,
                  pl.BlockSpec((2, _ROUTER_ROWS, D_MODEL), lambda i: (0, 0, 0)),
                  pl.BlockSpec((_ROUTER_ROWS, 1), const)],
        out_specs=[pl.BlockSpec((tm, D_MODEL), row),
                   pl.BlockSpec((tm, D_MODEL // 2), row),
                   pl.BlockSpec((SUBLANES, tm), lambda i: (0, i))],
        out_shape=[jax.ShapeDtypeStruct((n, D_MODEL), F32),
                   jax.ShapeDtypeStruct((n, D_MODEL // 2), jnp.uint32),
                   jax.ShapeDtypeStruct((SUBLANES, n), F32)],
        compiler_params=_params("parallel"),
        name="cross_attention",
    )(x1, kv, gx, w_xq, w_xo, gf, w_router, b_router)


def _expert_kernel(blk_ref, xb_ref, w1_ref, w3_ref, w2_ref, o_ref):
    n_used = blk_ref[pl.num_programs(0)]

    @pl.when(pl.program_id(0) < n_used)
    def _():
        xb = _unpack_pairs(xb_ref[...]).astype(MXU_DTYPE)
        a = _dot(xb, w1_ref[...])
        hmid = a * jax.nn.sigmoid(a) * _dot(xb, w3_ref[...])
        o_ref[...] = _dot(hmid, w2_ref[...])

    @pl.when(pl.program_id(0) >= n_used)
    def _():
        o_ref[...] = jnp.zeros_like(o_ref)


def _experts(blk, xb, w1, w3, w2):
    cap = xb.shape[0]
    nblk = cap // MOE_BLOCK
    row = lambda i, e: (i, 0)
    grid_spec = pltpu.PrefetchScalarGridSpec(
        num_scalar_prefetch=1,
        grid=(nblk,),
        in_specs=[pl.BlockSpec((MOE_BLOCK, D_MODEL // 2), row),
                  pl.BlockSpec((None, D_MODEL, D_EXPERT), lambda i, e: (e[i], 0, 0)),
                  pl.BlockSpec((None, D_MODEL, D_EXPERT), lambda i, e: (e[i], 0, 0)),
                  pl.BlockSpec((None, D_EXPERT, D_MODEL), lambda i, e: (e[i], 0, 0))],
        out_specs=pl.BlockSpec((MOE_BLOCK, D_MODEL), row),
    )
    return pl.pallas_call(
        _expert_kernel,
        grid_spec=grid_spec,
        out_shape=jax.ShapeDtypeStruct((cap, D_MODEL), F32),
        compiler_params=_params("arbitrary"),
        name="experts",
    )(blk, xb, w1, w3, w2)


def _final_kernel(x_ref, w_ref, g_ref, *refs):
    o_ref = refs[-1]
    tiles_per_split = pl.num_programs(0) // MOE_SPLITS
    for s in range(MOE_SPLITS):
        @pl.when(pl.program_id(0) // tiles_per_split == s)
        def _(s=s):
            moe = w_ref[:, 0:1] * refs[2 * s][...] + w_ref[:, 1:2] * refs[2 * s + 1][...]
            o_ref[...] = _rms(x_ref[...] + moe, g_ref[...])


def _final(x2, wts, g, ys):
    n = x2.shape[0]
    tm = ROW_TM
    tiles_per_split = n // tm // MOE_SPLITS
    row = lambda i: (i, 0)
    wide = pl.BlockSpec((tm, D_MODEL), row)

    def split_spec(s):
        return pl.BlockSpec((tm, D_MODEL), lambda i: (jnp.clip(i - s * tiles_per_split, 0, tiles_per_split - 1), 0))

    return pl.pallas_call(
        _final_kernel,
        grid=(n // tm,),
        in_specs=[wide, pl.BlockSpec((tm, EXP_TOPK), row), pl.BlockSpec((1, D_MODEL), lambda i: (0, 0))] + [
            split_spec(s) for s in range(MOE_SPLITS) for _ in range(EXP_TOPK)],
        out_specs=wide,
        out_shape=jax.ShapeDtypeStruct((n, D_MODEL), F32),
        compiler_params=_params("parallel"),
        name="final_norm",
    )(x2, wts, g, *[y for pair in ys for y in pair])


def _route(eid, tok0):
    n_tok = eid.shape[1]
    eid = eid.reshape(-1)
    n_asg = eid.shape[0]
    iota = jnp.arange(n_asg, dtype=jnp.int32)
    se, order = lax.sort_key_val(eid, iota)
    counts = jnp.sum((jnp.arange(N_EXPERTS, dtype=jnp.int32)[:, None] == eid[None, :]).astype(jnp.int32), axis=1)
    padded = (counts + MOE_BLOCK - 1) // MOE_BLOCK * MOE_BLOCK
    starts = jnp.cumsum(counts) - counts
    pends = jnp.cumsum(padded)
    pstarts = pends - padded
    cap = ((n_asg + MOE_BLOCK - 1) // MOE_BLOCK + N_EXPERTS) * MOE_BLOCK
    nblk = cap // MOE_BLOCK
    blk_e = jnp.minimum(jnp.searchsorted(pends, jnp.arange(nblk) * MOE_BLOCK, side='right', method='compare_all'),
                        N_EXPERTS - 1).astype(jnp.int32)
    per_row = lambda a: jnp.repeat(a[blk_e], MOE_BLOCK)
    row = jnp.arange(cap, dtype=jnp.int32)
    off = row - per_row(pstarts)
    asg = order[jnp.clip(per_row(starts) + off, 0, n_asg - 1)]
    buf_tok = tok0 + jnp.where(off < per_row(counts), asg % n_tok, row % n_tok)
    dest_sorted = iota + (pstarts - starts)[se]
    _, pos = lax.sort_key_val(order, dest_sorted)
    n_used = (pends[-1] // MOE_BLOCK).astype(jnp.int32)
    return buf_tok, jnp.concatenate([blk_e, n_used[None]]), pos.reshape(EXP_TOPK, n_tok)


def kernel(x, mem, norm_mix_g, w_in, w_ret_o, w_nsa_o, w_out, cmp_pe_k, cmp_w1_k, cmp_w2_k, cmp_pe_v,
           cmp_w1_v, cmp_w2_v, norm_x_g, norm_mem_g, w_xq, w_xkv, w_xo, norm_ffn_g, w_grp, b_grp, w_rt,
           b_rt, w_e1, w_e3, w_e2, norm_f_g):
    bsz, seq, _ = x.shape
    n = bsz * seq
    assert seq % PROJ_TM == 0 and seq % (2 * ATT_TQ) == 0 and w_in.shape[0] == 1
    assert n % (ROW_TM * MOE_SPLITS) == 0
    cast = lambda a: a.astype(MXU_DTYPE)
    xc = x.reshape(n, D_MODEL)
    l = 0

    w_main, w_t, w_gates = _split_w_in(w_in[l])
    qk, rv, rg, ckv, sk, wk, ngl, qt, svt, wvt = _proj(
        xc, norm_mix_g[l][None, :], w_main, w_t, _rope_tables(seq), bsz, seq)
    y_ret = _retention(qk, rv, rg, bsz, seq)
    w2 = jnp.stack([cmp_w2_k[l], cmp_w2_v[l]])
    cmp_k, cmp_vt = _compress(ckv, jnp.stack([cmp_pe_k[l], cmp_pe_v[l]]),
                              cast(jnp.stack([cmp_w1_k[l], cmp_w1_v[l]])),
                              cast(w2), cast(w2.transpose(0, 2, 1)), bsz, seq)
    o_cmp, selt = _cmp_attention(qt, cmp_k, cmp_vt, bsz, seq)
    o_sel = _flash_attention(qt, sk, svt, selt, bsz, seq)
    o_win = _flash_attention(qt, wk, wvt, None, bsz, seq)
    x1 = _merge(xc, y_ret, o_cmp, o_sel, o_win, ngl, norm_mix_g[l][None, :], w_gates,
                cast(w_ret_o[l]), cast(w_nsa_o[l]), cast(w_out[l]))

    kv = _mem_kv(mem.reshape(-1, D_MODEL), norm_mem_g[l][None, :], cast(w_xkv[l]), bsz)
    gap = _ROUTER_E0 - N_EGROUPS
    w_router = jnp.concatenate([w_grp[l].T, jnp.zeros((gap, D_MODEL), F32), w_rt[l].T], axis=0)
    wr_hi = cast(w_router)
    wr_mid = cast(w_router - wr_hi.astype(F32))
    b_router = jnp.concatenate([b_grp[l], jnp.zeros((gap,), F32), b_rt[l]])[:, None]
    x2, hf, routed = _cross(x1, kv, norm_x_g[l][None, :], cast(w_xq[l]), cast(w_xo[l]),
                            norm_ffn_g[l][None, :], jnp.stack([wr_hi, wr_mid]), b_router, bsz, seq)

    eid = routed[0:EXP_TOPK].astype(jnp.int32)
    wts = routed[EXP_TOPK:2 * EXP_TOPK].T
    per_split = n // MOE_SPLITS
    ys = []
    for s in range(MOE_SPLITS):
        buf_tok, blk, pos = _route(eid[:, s * per_split:(s + 1) * per_split], s * per_split)
        y = _experts(blk, hf[buf_tok], w_e1[l], w_e3[l], w_e2[l])
        ys.append([y[pos[j]] for j in range(EXP_TOPK)])
    out = _final(x2, wts, norm_f_g[None, :], ys)
    return out.reshape(bsz, seq, D_MODEL)
```

```python
import functools

import numpy as np
import jax
import jax.numpy as jnp
from jax import lax
from jax.experimental import pallas as pl
from jax.experimental.pallas import tpu as pltpu

MXU_DTYPE = jnp.bfloat16
F32 = jnp.float32

D_MODEL = 1024
N_MEM = 256
EPS = 1e-6
NEG_INF = -1e30
SEL_FORCE = 1e4

R_HEADS = 4
R_DK = 128
R_DV = 256
R_CHUNK = 128
ROPE_BASE = 10000.0

NSA_HEADS = 8
NSA_GROUPS = 2
NSA_HPG = NSA_HEADS // NSA_GROUPS
NSA_DH = 128
CMP_LEN = 32
CMP_STRIDE = 16
SEL_LEN = 64
SEL_TOPK = 16
WINDOW = 512

X_HEADS = 4
X_DH = D_MODEL // X_HEADS

N_EGROUPS = 4
EXP_PER_GROUP = 8
N_EXPERTS = N_EGROUPS * EXP_PER_GROUP
EXP_TOPK = 2
D_EXPERT = 512
MOE_BLOCK = 256

RET_QK = R_HEADS * R_DK
RET_V = R_HEADS * R_DV
NSA_Q = NSA_HEADS * NSA_DH
NSA_KV = NSA_GROUPS * NSA_DH
SPLITS = (RET_QK, RET_QK, RET_V, RET_V, NSA_Q, NSA_KV, NSA_KV, NSA_KV, NSA_KV, NSA_KV, NSA_KV,
          3 * NSA_HEADS, D_MODEL, D_MODEL)

_EXP2_SCALE = (NSA_DH ** -0.5) * float(np.log2(np.e))

LANES = 128
SUBLANES = 8
VMEM_LIMIT = 56 * 1024 * 1024

PROJ_TM = 512
RET_ROWS = 512
ATT_TQ = 256
ATT_TK = 256
ROW_TM = 512
MOE_SPLITS = 2
VT_ROWS = NSA_DH + 2 * SUBLANES
SEL_KW = 2 * NSA_DH
assert ATT_TQ == ATT_TK and SEL_LEN * 2 * SUBLANES >= ATT_TK


def _params(*sem):
    return pltpu.CompilerParams(dimension_semantics=sem, vmem_limit_bytes=VMEM_LIMIT)


def _dot(a, b):
    return jnp.dot(a.astype(MXU_DTYPE), b.astype(MXU_DTYPE), preferred_element_type=F32)


def _dot_nt(a, b):
    return lax.dot_general(a.astype(MXU_DTYPE), b.astype(MXU_DTYPE), (((1,), (1,)), ((), ())),
                           preferred_element_type=F32)


def _dot_tn(a, b):
    return lax.dot_general(a.astype(MXU_DTYPE), b.astype(MXU_DTYPE), (((0,), (0,)), ((), ())),
                           preferred_element_type=F32)


def _split3(p):
    hi = p.astype(MXU_DTYPE)
    r1 = p - hi.astype(F32)
    mid = r1.astype(MXU_DTYPE)
    lo = (r1 - mid.astype(F32)).astype(MXU_DTYPE)
    return hi, mid, lo


def _rms(x, g):
    return x * lax.rsqrt(jnp.mean(x * x, axis=-1, keepdims=True) + EPS) * g


_C_RQK = 0
_C_RV = _C_RQK + 2 * RET_QK
_C_RG = _C_RV + RET_V
_C_CKV = _C_RG + RET_V
_C_SK = _C_CKV + 2 * NSA_KV
_C_WK = _C_SK + NSA_KV
_C_NGL = _C_WK + NSA_KV
_C_END = _C_NGL + LANES
_R_NQ = 0
_R_SV = _R_NQ + NSA_Q
_R_WV = _R_SV + NSA_KV
_R_END = _R_WV + NSA_KV


def _proj_kernel(x_ref, g_ref, w_ref, wt_ref, cos_ref, sin_up_ref, sin_dn_ref,
                 qk_ref, rv_ref, rg_ref, ckv_ref, sk_ref, wk_ref, ngl_ref, nqt_ref, svt_ref, wvt_ref):
    hb = _rms(x_ref[...], g_ref[...]).astype(MXU_DTYPE)
    tm = hb.shape[0]

    def mm(off, width):
        return jnp.dot(hb, w_ref[:, off:off + width], preferred_element_type=F32)

    def mm_t(off, height):
        return _dot_nt(wt_ref[off:off + height, :], hb)

    cos = cos_ref[...]
    sin_up = sin_up_ref[...]
    sin_dn = sin_dn_ref[...]
    qk = mm(_C_RQK, 2 * RET_QK)
    for i in range(2 * R_HEADS):
        t = qk[:, i * R_DK:(i + 1) * R_DK]
        r = t * cos + pltpu.roll(t, R_DK - 1, axis=1) * sin_up + pltpu.roll(t, 1, axis=1) * sin_dn
        if i >= R_HEADS:
            r = r * (R_DK ** -0.5)
        qk_ref[:, i * R_DK:(i + 1) * R_DK] = r.astype(qk_ref.dtype)
    rv_ref[...] = mm(_C_RV, RET_V).astype(rv_ref.dtype)
    rg_ref[...] = mm(_C_RG, RET_V)
    ckv_ref[...] = mm(_C_CKV, 2 * NSA_KV)
    sk = mm(_C_SK, NSA_KV).astype(sk_ref.dtype)
    blk = (lax.broadcasted_iota(jnp.int32, (tm, NSA_DH), 0) % ATT_TK) // SEL_LEN
    onehot = jnp.where(lax.broadcasted_iota(jnp.int32, (tm, NSA_DH), 1) == blk, 1.0, 0.0).astype(sk_ref.dtype)
    for g in range(NSA_GROUPS):
        sk_ref[:, g * SEL_KW:g * SEL_KW + NSA_DH] = sk[:, g * NSA_DH:(g + 1) * NSA_DH]
        sk_ref[:, g * SEL_KW + NSA_DH:(g + 1) * SEL_KW] = onehot
    wk_ref[...] = mm(_C_WK, NSA_KV).astype(wk_ref.dtype)
    ngl_ref[...] = mm(_C_NGL, LANES)

    nqt = (mm_t(_R_NQ, NSA_Q) * _EXP2_SCALE).astype(nqt_ref.dtype)
    for g in range(NSA_GROUPS):
        for j in range(tm // ATT_TQ):
            for hh in range(NSA_HPG):
                head = g * NSA_HPG + hh
                nqt_ref[g, j, :, hh * ATT_TQ:(hh + 1) * ATT_TQ] = (
                    nqt[head * NSA_DH:(head + 1) * NSA_DH, j * ATT_TQ:(j + 1) * ATT_TQ])
    for off, out_ref in ((_R_SV, svt_ref), (_R_WV, wvt_ref)):
        vt = mm_t(off, NSA_KV).astype(out_ref.dtype)
        for g in range(NSA_GROUPS):
            for j in range(tm // ATT_TK):
                out_ref[g, j, 0:NSA_DH] = vt[g * NSA_DH:(g + 1) * NSA_DH, j * ATT_TK:(j + 1) * ATT_TK]
                out_ref[g, j, NSA_DH:VT_ROWS] = jnp.ones((VT_ROWS - NSA_DH, ATT_TK), out_ref.dtype)


def _split_w_in(w_in):
    offs = np.cumsum((0,) + SPLITS)
    rq, rk, rv, rg, nq, ck, cv, sk, sv, wk, wv, ngl, ga, gb = [
        w_in[:, offs[i]:offs[i + 1]] for i in range(len(SPLITS))]
    ngl = jnp.pad(ngl, ((0, 0), (0, LANES - ngl.shape[1])))
    w = jnp.concatenate([rq, rk, rv, rg, ck, cv, sk, wk, ngl], axis=1).astype(MXU_DTYPE)
    wt = jnp.concatenate([nq, sv, wv], axis=1).T.astype(MXU_DTYPE)
    w_gates = jnp.concatenate([ga, gb], axis=1).astype(MXU_DTYPE)
    return w, wt, w_gates


def _rope_tables(seq):
    pos = jnp.arange(seq, dtype=F32)
    inv_freq = ROPE_BASE ** (-jnp.arange(0, R_DK, 2, dtype=F32) / R_DK)
    ang = pos[:, None] * inv_freq[None, :]
    cos = jnp.repeat(jnp.cos(ang), 2, axis=1)
    sin = jnp.repeat(jnp.sin(ang), 2, axis=1)
    even = (jnp.arange(R_DK) % 2 == 0)[None, :]
    return cos, jnp.where(even, -sin, 0.0), jnp.where(even, 0.0, sin)


def _proj(x2d, g, w, wt, tables, bsz, seq):
    n = x2d.shape[0]
    tm = PROJ_TM
    nt = seq // tm
    row = lambda i: (i, 0)
    const = lambda i: (0, 0)
    tile = lambda i: (i // nt, 0, i % nt, 0, 0)
    table = pl.BlockSpec((tm, R_DK), lambda i: (i % nt, 0))
    widths = (2 * RET_QK, RET_V, RET_V, 2 * NSA_KV, NSA_GROUPS * SEL_KW, NSA_KV, LANES)
    dtypes = (MXU_DTYPE, MXU_DTYPE, F32, F32, MXU_DTYPE, MXU_DTYPE, F32)
    width = NSA_HPG * ATT_TQ
    vt_shape = jax.ShapeDtypeStruct((bsz, NSA_GROUPS, seq // ATT_TK, VT_ROWS, ATT_TK), MXU_DTYPE)
    vt_spec = pl.BlockSpec((None, NSA_GROUPS, tm // ATT_TK, VT_ROWS, ATT_TK), tile)
    return pl.pallas_call(
        _proj_kernel,
        grid=(n // tm,),
        in_specs=[pl.BlockSpec((tm, D_MODEL), row),
                  pl.BlockSpec((1, D_MODEL), const),
                  pl.BlockSpec((D_MODEL, _C_END), const, pipeline_mode=pl.Buffered(1)),
                  pl.BlockSpec((_R_END, D_MODEL), const, pipeline_mode=pl.Buffered(1)),
                  table, table, table],
        out_specs=[pl.BlockSpec((tm, wd), row) for wd in widths] + [
            pl.BlockSpec((None, NSA_GROUPS, tm // ATT_TQ, NSA_DH, width), tile), vt_spec, vt_spec],
        out_shape=[jax.ShapeDtypeStruct((n, wd), dt) for wd, dt in zip(widths, dtypes)] + [
            jax.ShapeDtypeStruct((bsz, NSA_GROUPS, seq // ATT_TQ, NSA_DH, width), MXU_DTYPE), vt_shape, vt_shape],
        compiler_params=_params("parallel"),
        name="proj",
    )(x2d, g, w, wt, *tables)


def _ret_kernel(qk_ref, v_ref, g_ref, decay_ref, zeta_ref, xi_ref, cd_ref, y_ref, state_ref):
    @pl.when(pl.program_id(1) == 0)
    def _():
        state_ref[...] = jnp.zeros_like(state_ref)

    for c in range(RET_ROWS // R_CHUNK):
        rows = slice(c * R_CHUNK, (c + 1) * R_CHUNK)
        for h in range(R_HEADS):
            q = qk_ref[rows, h * R_DK:(h + 1) * R_DK]
            k = qk_ref[rows, RET_QK + h * R_DK:RET_QK + (h + 1) * R_DK]
            v = v_ref[rows, h * R_DV:(h + 1) * R_DV]
            st = state_ref[h]
            scores = _dot_nt(q, k) * decay_ref[h]
            o = _dot(scores, v) + _dot(q, st) * xi_ref[h]
            o = o * lax.rsqrt(jnp.mean(o * o, axis=-1, keepdims=True) + EPS)
            g = g_ref[rows, h * R_DV:(h + 1) * R_DV]
            y_ref[rows, h * R_DV:(h + 1) * R_DV] = (g * jax.nn.sigmoid(g) * o).astype(y_ref.dtype)
            kz = k.astype(F32) * zeta_ref[h]
            state_ref[h] = st * cd_ref[h] + _dot_tn(kz, v)


def _retention(qk, rv, rg, bsz, seq):
    n = qk.shape[0]
    nt = seq // RET_ROWS
    log_g = jnp.log1p(-jnp.exp2(-5.0 - jnp.arange(R_HEADS, dtype=F32)))
    idx = jnp.arange(R_CHUNK, dtype=F32)
    diff = idx[:, None] - idx[None, :]
    decay = jnp.where(diff >= 0, jnp.exp(log_g[:, None, None] * jnp.maximum(diff, 0.0)), 0.0)
    zeta = jnp.exp(log_g[:, None] * (R_CHUNK - 1.0 - idx)[None, :])[:, :, None]
    xi = jnp.exp(log_g[:, None] * (idx + 1.0)[None, :])[:, :, None]
    cd = jnp.exp(log_g * R_CHUNK)[:, None, None]
    row = lambda b, i: (b * nt + i, 0)
    const3 = lambda b, i: (0, 0, 0)
    return pl.pallas_call(
        _ret_kernel,
        grid=(bsz, nt),
        in_specs=[pl.BlockSpec((RET_ROWS, 2 * RET_QK), row),
                  pl.BlockSpec((RET_ROWS, RET_V), row),
                  pl.BlockSpec((RET_ROWS, RET_V), row),
                  pl.BlockSpec((R_HEADS, R_CHUNK, R_CHUNK), const3),
                  pl.BlockSpec((R_HEADS, R_CHUNK, 1), const3),
                  pl.BlockSpec((R_HEADS, R_CHUNK, 1), const3),
                  pl.BlockSpec((R_HEADS, 1, 1), const3)],
        out_specs=pl.BlockSpec((RET_ROWS, RET_V), row),
        out_shape=jax.ShapeDtypeStruct((n, RET_V), MXU_DTYPE),
        scratch_shapes=[pltpu.VMEM((R_HEADS, R_DK, R_DV), F32)],
        compiler_params=_params("parallel", "arbitrary"),
        name="retention",
    )(qk, rv, rg, decay, zeta, xi, cd)


def _compress_kernel(x_ref, pe_ref, w1_ref, w2_ref, w2t_ref, o_ref, ot_ref, buf_ref, *, seq):
    ncp = seq // CMP_STRIDE
    buf_ref[0:seq, :] = x_ref[...]
    buf_ref[seq:seq + LANES, :] = jnp.zeros((LANES, NSA_DH), F32)
    acc = jnp.zeros((ncp, NSA_DH), F32)
    for l in range(CMP_LEN):
        xl = buf_ref[pl.ds(l, ncp, stride=CMP_STRIDE), :] + pe_ref[l:l + 1, :]
        acc = acc + _dot(xl, w1_ref[l])
    hid = jax.nn.gelu(acc)
    o_ref[...] = _dot(hid, w2_ref[...]).astype(o_ref.dtype)
    ot_ref[...] = _dot_nt(w2t_ref[...], hid).astype(ot_ref.dtype)


def _compress(ckv, pe, w1, w2, w2t, bsz, seq):
    ncp = seq // CMP_STRIDE
    nj = 2 * NSA_GROUPS
    wsel = lambda b, j: (j // NSA_GROUPS, 0, 0)
    return pl.pallas_call(
        functools.partial(_compress_kernel, seq=seq),
        grid=(bsz, nj),
        in_specs=[pl.BlockSpec((seq, NSA_DH), lambda b, j: (b, j)),
                  pl.BlockSpec((None, CMP_LEN, NSA_DH), wsel),
                  pl.BlockSpec((None, CMP_LEN, NSA_DH, NSA_DH), lambda b, j: (j // NSA_GROUPS, 0, 0, 0)),
                  pl.BlockSpec((None, NSA_DH, NSA_DH), wsel),
                  pl.BlockSpec((None, NSA_DH, NSA_DH), wsel)],
        out_specs=[pl.BlockSpec((None, None, ncp, NSA_DH), lambda b, j: (b, j, 0, 0)),
                   pl.BlockSpec((None, None, NSA_DH, ncp), lambda b, j: (b, j, 0, 0))],
        out_shape=[jax.ShapeDtypeStruct((bsz, nj, ncp, NSA_DH), MXU_DTYPE),
                   jax.ShapeDtypeStruct((bsz, nj, NSA_DH, ncp), MXU_DTYPE)],
        scratch_shapes=[pltpu.VMEM((seq + LANES, NSA_DH), F32)],
        compiler_params=_params("parallel", "parallel"),
        name="compress",
    )(ckv, pe, w1, w2, w2t)


def _cmp_attn_kernel(qt_ref, k_ref, vt_ref, ov_ref, o_ref, selt_ref, *, ncp, nb):
    tq = ATT_TQ
    t0 = pl.program_id(2) * tq
    t = t0 + lax.broadcasted_iota(jnp.int32, (ncp, tq), 1)
    n = lax.broadcasted_iota(jnp.int32, (ncp, tq), 0)
    valid = (n * CMP_STRIDE + (CMP_LEN - 1) <= t) & (n < ncp - 1)
    any_valid = (t0 + lax.broadcasted_iota(jnp.int32, (1, tq), 1)) >= CMP_LEN - 1
    valid = jnp.concatenate([valid] * NSA_HPG, axis=1)
    any_valid = jnp.concatenate([any_valid] * NSA_HPG, axis=1)
    s = jnp.where(valid, _dot(k_ref[...], qt_ref[...]), NEG_INF)
    e = jnp.exp2(s - jnp.max(s, axis=0, keepdims=True))
    p = e * jnp.where(any_valid, 1.0 / jnp.sum(e, axis=0, keepdims=True), 0.0)
    ot = _dot(vt_ref[...], p)
    for h in range(NSA_HPG):
        o_ref[:, h * NSA_DH:(h + 1) * NSA_DH] = ot[:, h * tq:(h + 1) * tq].T
    psum = sum(p[:, h * tq:(h + 1) * tq] for h in range(NSA_HPG))

    imp = sum(_dot(ov_ref[...], part) for part in _split3(psum))
    j = lax.broadcasted_iota(jnp.int32, (nb, tq), 0)
    tb = (t0 + lax.broadcasted_iota(jnp.int32, (nb, tq), 1)) // SEL_LEN
    forced = (j == 0) | (j == tb) | (j == tb - 1)
    imp = jnp.where(j > tb, -SEL_FORCE, jnp.where(forced, SEL_FORCE, imp))

    sub = SUBLANES
    grp = [imp[r * sub:(r + 1) * sub] for r in range(nb // sub)]
    cnt = [jnp.zeros((sub, tq), F32) for _ in grp]
    for i in range(nb):
        row = jnp.broadcast_to(imp[i:i + 1, :], (sub, tq))
        for r in range(nb // sub):
            if r * sub > i:
                beats = jnp.where(row >= grp[r], 1.0, 0.0)
            elif r * sub + sub - 1 < i:
                beats = jnp.where(row > grp[r], 1.0, 0.0)
            else:
                jr = r * sub + lax.broadcasted_iota(jnp.int32, (sub, tq), 0)
                beats = jnp.where(jr > i, jnp.where(row >= grp[r], 1.0, 0.0), jnp.where(row > grp[r], 1.0, 0.0))
            cnt[r] = cnt[r] + beats
    k_sel = min(SEL_TOPK, nb)
    for r in range(nb // sub):
        selt_ref[r * sub:(r + 1) * sub, :] = jnp.where(cnt[r] < k_sel, 1.0, 0.0)


def _cmp_attention(qt, cmp_k, cmp_vt, bsz, seq):
    n = bsz * seq
    ncp = seq // CMP_STRIDE
    nb = seq // SEL_LEN
    nt = seq // ATT_TQ
    gw = NSA_HPG * NSA_DH
    cstart = np.arange(ncp) * CMP_STRIDE
    jstart = np.arange(nb) * SEL_LEN
    ov = ((cstart[None, :] < jstart[:, None] + SEL_LEN) & (cstart[None, :] + CMP_LEN > jstart[:, None])
          & (np.arange(ncp)[None, :] < ncp - 1))
    ov = jnp.asarray(ov, MXU_DTYPE)
    return pl.pallas_call(
        functools.partial(_cmp_attn_kernel, ncp=ncp, nb=nb),
        grid=(bsz, NSA_GROUPS, nt),
        in_specs=[pl.BlockSpec((None, None, None, NSA_DH, NSA_HPG * ATT_TQ), lambda b, g, i: (b, g, i, 0, 0)),
                  pl.BlockSpec((None, None, ncp, NSA_DH), lambda b, g, i: (b, g, 0, 0)),
                  pl.BlockSpec((None, None, NSA_DH, ncp), lambda b, g, i: (b, NSA_GROUPS + g, 0, 0)),
                  pl.BlockSpec((nb, ncp), lambda b, g, i: (0, 0))],
        out_specs=[pl.BlockSpec((ATT_TQ, gw), lambda b, g, i: (b * nt + i, g)),
                   pl.BlockSpec((None, None, nb, ATT_TQ), lambda b, g, i: (b, g, 0, i))],
        out_shape=[jax.ShapeDtypeStruct((n, NSA_Q), F32),
                   jax.ShapeDtypeStruct((bsz, NSA_GROUPS, nb, seq), F32)],
        compiler_params=_params("parallel", "parallel", "parallel"),
        name="cmp_attention",
    )(qt, cmp_k, cmp_vt, ov)


_MASK_ALL = 1 << 28


def _softmax_step(s, m, vt, acc_ref):
    m_new = jnp.maximum(m, jnp.max(s, axis=0, keepdims=True))
    p = jnp.exp2(s - m_new)
    acc_ref[...] = jnp.exp2(m - m_new) * acc_ref[...] + _dot(vt, p)
    return m_new


def _flash_loop(qt_ref, k_ref, vt_ref, acc_ref, s_refs, n_tiles, tile_of, valid_of, max_steps=None):
    width = NSA_HPG * ATT_TQ

    def scores(kt):
        k = k_ref[pl.ds(pl.multiple_of(kt * ATT_TK, ATT_TK), ATT_TK), :]
        return _dot(k, qt_ref[...])

    def half_step(step, s_cur, s_nxt, m, prefetch=True):
        kt = tile_of(jnp.minimum(step, n_tiles - 1))
        if prefetch:
            s_nxt[...] = scores(tile_of(jnp.minimum(step + 1, n_tiles - 1)))
        valid = valid_of(kt, kt * ATT_TK + jnp.where(step < n_tiles, 0, _MASK_ALL))
        valid = jnp.concatenate([valid] * NSA_HPG, axis=1)
        return _softmax_step(jnp.where(valid, s_cur[...], NEG_INF), m, vt_ref[kt], acc_ref)

    def body(i, m):
        m = half_step(2 * i, s_refs[0], s_refs[1], m)
        return half_step(2 * i + 1, s_refs[1], s_refs[0], m)

    acc_ref[...] = jnp.zeros_like(acc_ref)
    s_refs[0][...] = scores(tile_of(0))
    m = jnp.full((1, width), NEG_INF, F32)
    if max_steps is None:
        lax.fori_loop(0, (n_tiles + 1) // 2, body, m)
    else:
        for step in range(max_steps):
            m = half_step(step, s_refs[step % 2], s_refs[(step + 1) % 2], m, prefetch=step + 1 < max_steps)


def _flash_finish(o_ref, acc_ref):
    inv = 1.0 / acc_ref[NSA_DH:NSA_DH + 1, :]
    for h in range(NSA_HPG):
        cols = slice(h * ATT_TQ, (h + 1) * ATT_TQ)
        o_ref[:, h * NSA_DH:(h + 1) * NSA_DH] = (acc_ref[0:NSA_DH, cols] * inv[:, cols]).T


def _sel_attn_kernel(qt_ref, k_ref, vt_ref, selt_ref, o_ref, acc_ref, s0_ref, s1_ref, qa_ref):
    qi = pl.program_id(2)
    width = NSA_HPG * ATT_TQ
    blocks_per_tile = ATT_TK // SEL_LEN
    bias_rows = 2 * SUBLANES
    n_loop = qi * (ATT_TQ // ATT_TK)
    n_steps = (n_loop + 1) // 2 * 2
    tile_of = lambda step: jnp.where(step < n_loop, step, qi)
    rowid = lax.broadcasted_iota(jnp.int32, (bias_rows, ATT_TQ), 0)
    tri = (lax.broadcasted_iota(jnp.int32, (ATT_TK, ATT_TQ), 0) <=
           lax.broadcasted_iota(jnp.int32, (ATT_TK, ATT_TQ), 1))
    tri = jnp.concatenate([tri] * NSA_HPG, axis=1)

    qa_ref[0:NSA_DH, :] = qt_ref[...]
    qa_ref[NSA_DH + bias_rows:, :] = jnp.zeros((qa_ref.shape[0] - NSA_DH - bias_rows, width), qa_ref.dtype)

    def scores(step):
        kt = tile_of(step)
        threshold = jnp.where((step < n_loop) | (step == n_steps), 0.5, 2.0)
        bias = jnp.zeros((bias_rows, ATT_TQ), F32)
        for jb in range(blocks_per_tile):
            picked = selt_ref[pl.ds(kt * blocks_per_tile + jb, 1), :] > threshold
            bias = jnp.where(rowid == jb, jnp.where(picked, 0.0, NEG_INF), bias)
        qa_ref[NSA_DH:NSA_DH + bias_rows, :] = jnp.concatenate([bias] * NSA_HPG, axis=1).astype(qa_ref.dtype)
        k = k_ref[pl.ds(pl.multiple_of(kt * ATT_TK, ATT_TK), ATT_TK), :]
        return _dot(k, qa_ref[...])

    def half_step(step, s_cur, s_nxt, m, diagonal=False):
        if not diagonal:
            s_nxt[...] = scores(step + 1)
        s = jnp.where(tri, s_cur[...], NEG_INF) if diagonal else s_cur[...]
        return _softmax_step(s, m, vt_ref[tile_of(step)], acc_ref)

    def body(i, m):
        m = half_step(2 * i, s0_ref, s1_ref, m)
        return half_step(2 * i + 1, s1_ref, s0_ref, m)

    acc_ref[...] = jnp.zeros_like(acc_ref)
    s0_ref[...] = scores(0)
    m = lax.fori_loop(0, n_steps // 2, body, jnp.full((1, width), NEG_INF, F32))
    half_step(n_steps, s0_ref, s1_ref, m, diagonal=True)
    _flash_finish(o_ref, acc_ref)


def _win_attn_kernel(qt_ref, k_ref, vt_ref, o_ref, acc_ref, s0_ref, s1_ref):
    qi = pl.program_id(2)
    tpos = qi * ATT_TQ + lax.broadcasted_iota(jnp.int32, (ATT_TK, ATT_TQ), 1)
    row = lax.broadcasted_iota(jnp.int32, (ATT_TK, ATT_TQ), 0)
    last = (qi + 1) * (ATT_TQ // ATT_TK) - 1
    n_tiles = jnp.minimum(last + 1, (ATT_TQ + WINDOW) // ATT_TK)

    def valid_of(kt, key0):
        kpos = row + key0
        return (kpos <= tpos) & (kpos > tpos - WINDOW)

    _flash_loop(qt_ref, k_ref, vt_ref, acc_ref, (s0_ref, s1_ref), n_tiles,
                lambda step: last - step, valid_of, max_steps=(ATT_TQ + WINDOW) // ATT_TK)
    _flash_finish(o_ref, acc_ref)


def _flash_attention(qt, k, vt, selt, bsz, seq):
    n = bsz * seq
    nt = seq // ATT_TQ
    nkt = seq // ATT_TK
    nb = seq // SEL_LEN
    gw = NSA_HPG * NSA_DH
    width = NSA_HPG * ATT_TQ
    kw = k.shape[1] // NSA_GROUPS
    in_specs = [pl.BlockSpec((None, None, None, NSA_DH, width), lambda b, g, i: (b, g, i, 0, 0)),
                pl.BlockSpec((seq, kw), lambda b, g, i: (b, g)),
                pl.BlockSpec((None, None, nkt, VT_ROWS, ATT_TK), lambda b, g, i: (b, g, 0, 0, 0))]
    args = [qt, k, vt]
    scratch = [pltpu.VMEM((VT_ROWS, width), F32),
               pltpu.VMEM((ATT_TK, width), F32),
               pltpu.VMEM((ATT_TK, width), F32)]
    if selt is None:
        body = _win_attn_kernel
        name = "window_attention"
    else:
        body = _sel_attn_kernel
        name = "selected_attention"
        in_specs.append(pl.BlockSpec((None, None, nb, ATT_TQ), lambda b, g, i: (b, g, 0, i)))
        args.append(selt)
        scratch.append(pltpu.VMEM((kw, width), MXU_DTYPE))
    return pl.pallas_call(
        body,
        grid=(bsz, NSA_GROUPS, nt),
        in_specs=in_specs,
        out_specs=pl.BlockSpec((ATT_TQ, gw), lambda b, g, i: (b * nt + i, g)),
        out_shape=jax.ShapeDtypeStruct((n, NSA_Q), F32),
        scratch_shapes=scratch,
        compiler_params=_params("parallel", "parallel", "parallel"),
        name=name,
    )(*args)


def _merge_kernel(x_ref, yr_ref, oc_ref, os_ref, ow_ref, ngl_ref, g_ref, wg_ref, wr_ref, wn_ref, wo_ref, o_ref):
    tm = x_ref.shape[0]
    hb = _rms(x_ref[...], g_ref[...]).astype(MXU_DTYPE)
    ga = jnp.dot(hb, wg_ref[:, :D_MODEL], preferred_element_type=F32)
    gb = jnp.dot(hb, wg_ref[:, D_MODEL:], preferred_element_type=F32)
    gates = jax.nn.sigmoid(ngl_ref[...])
    parts = []
    for h in range(NSA_HEADS):
        cols = slice(h * NSA_DH, (h + 1) * NSA_DH)

        def gate(br):
            return jnp.broadcast_to(gates[:, 3 * h + br:3 * h + br + 1], (tm, NSA_DH))

        parts.append(gate(0) * oc_ref[:, cols] + gate(1) * os_ref[:, cols] + gate(2) * ow_ref[:, cols])
    o_nsa = jnp.concatenate(parts, axis=1)
    y_ret = _dot(yr_ref[...], wr_ref[...])
    y_nsa = _dot(o_nsa, wn_ref[...])
    y = jax.nn.sigmoid(ga) * y_ret + jax.nn.sigmoid(gb) * y_nsa
    o_ref[...] = x_ref[...] + _dot(y, wo_ref[...])


def _merge(x2d, y_ret, o_cmp, o_sel, o_win, ngl, g_mix, w_gates, w_ret_o, w_nsa_o, w_out):
    n = x2d.shape[0]
    tm = ROW_TM
    row = lambda i: (i, 0)
    const = lambda i: (0, 0)
    wide = pl.BlockSpec((tm, D_MODEL), row)
    wspec = pl.BlockSpec((D_MODEL, D_MODEL), const)
    return pl.pallas_call(
        _merge_kernel,
        grid=(n // tm,),
        in_specs=[wide, wide, wide, wide, wide,
                  pl.BlockSpec((tm, LANES), row),
                  pl.BlockSpec((1, D_MODEL), const),
                  pl.BlockSpec((D_MODEL, 2 * D_MODEL), const),
                  wspec, wspec, wspec],
        out_specs=wide,
        out_shape=jax.ShapeDtypeStruct((n, D_MODEL), F32),
        compiler_params=_params("parallel"),
        name="merge",
    )(x2d, y_ret, o_cmp, o_sel, o_win, ngl, g_mix, w_gates, w_ret_o, w_nsa_o, w_out)


def _mem_kv_kernel(m_ref, g_ref, w_ref, o_ref):
    o_ref[...] = _dot(_rms(m_ref[...], g_ref[...]), w_ref[...]).astype(o_ref.dtype)


def _mem_kv(mem2d, g, w_xkv, bsz):
    nm = mem2d.shape[0] // bsz
    return pl.pallas_call(
        _mem_kv_kernel,
        grid=(bsz,),
        in_specs=[pl.BlockSpec((nm, D_MODEL), lambda b: (b, 0)),
                  pl.BlockSpec((1, D_MODEL), lambda b: (0, 0)),
                  pl.BlockSpec((D_MODEL, 2 * D_MODEL), lambda b: (0, 0))],
        out_specs=pl.BlockSpec((nm, 2 * D_MODEL), lambda b: (b, 0)),
        out_shape=jax.ShapeDtypeStruct((mem2d.shape[0], 2 * D_MODEL), MXU_DTYPE),
        compiler_params=_params("parallel"),
        name="mem_kv",
    )(mem2d, g, w_xkv)


def _pack_pairs(x):
    half = x.shape[1] // 2
    hi = lax.bitcast_convert_type(x[:, :half].astype(MXU_DTYPE).astype(F32), jnp.uint32)
    lo = lax.bitcast_convert_type(x[:, half:].astype(MXU_DTYPE).astype(F32), jnp.uint32)
    return (hi & jnp.uint32(0xFFFF0000)) | (lo >> 16)


def _unpack_pairs(u):
    hi = lax.bitcast_convert_type(u & jnp.uint32(0xFFFF0000), F32)
    lo = lax.bitcast_convert_type(u << 16, F32)
    return jnp.concatenate([hi, lo], axis=1)


_ROUTER_E0 = 2 * SUBLANES
_ROUTER_ROWS = _ROUTER_E0 + N_EXPERTS


def _top2_route(lgt):
    sub = SUBLANES
    t = lgt.shape[1]
    rowid = lax.broadcasted_iota(jnp.int32, (sub, t), 0)
    first = lambda hit: jnp.min(jnp.where(hit, rowid, sub), axis=0, keepdims=True)
    lg = jnp.where(rowid < N_EGROUPS, lgt[0:sub], NEG_INF)
    gmax = jnp.max(lg, axis=0, keepdims=True)
    grp = first(lg == gmax)
    g_gate = 1.0 / jnp.sum(jnp.exp(lg - gmax), axis=0, keepdims=True)
    experts_of = lambda g: lgt[_ROUTER_E0 + g * EXP_PER_GROUP:_ROUTER_E0 + (g + 1) * EXP_PER_GROUP]
    le = experts_of(N_EGROUPS - 1)
    for g in range(N_EGROUPS - 2, -1, -1):
        le = jnp.where(grp == g, experts_of(g), le)
    ex = jnp.exp(le - jnp.max(le, axis=0, keepdims=True))
    pe = ex / jnp.sum(ex, axis=0, keepdims=True)
    p0 = jnp.max(pe, axis=0, keepdims=True)
    i0 = first(pe == p0)
    rest = jnp.where(rowid == i0, -1.0, pe)
    p1 = jnp.max(rest, axis=0, keepdims=True)
    i1 = first(rest == p1)
    den = p0 + p1
    base = grp * EXP_PER_GROUP
    return jnp.concatenate([(base + i0).astype(F32), (base + i1).astype(F32),
                            g_gate * p0 / den, g_gate * p1 / den], axis=0)


def _cross_kernel(x_ref, kv_ref, gx_ref, wq_ref, wo_ref, gf_ref, wr_ref, br_ref, x2_ref, hf_ref, rt_ref):
    x = x_ref[...]
    q = _dot(_rms(x, gx_ref[...]), wq_ref[...])
    heads = []
    for h in range(X_HEADS):
        k = kv_ref[:, h * X_DH:(h + 1) * X_DH]
        v = kv_ref[:, D_MODEL + h * X_DH:D_MODEL + (h + 1) * X_DH]
        s = _dot_nt(q[:, h * X_DH:(h + 1) * X_DH], k) * (X_DH ** -0.5)
        e = jnp.exp(s - jnp.max(s, axis=-1, keepdims=True))
        p = e / jnp.sum(e, axis=-1, keepdims=True)
        heads.append(_dot(p, v))
    x2 = x + _dot(jnp.concatenate(heads, axis=1), wo_ref[...])
    x2_ref[...] = x2
    hf = _rms(x2, gf_ref[...])
    hf_ref[...] = _pack_pairs(hf)
    h_hi, h_mid, _ = _split3(hf)
    w_hi = wr_ref[0]
    w_mid = wr_ref[1]
    lgt = (_dot_nt(w_hi, h_hi) + (_dot_nt(w_hi, h_mid) + _dot_nt(w_mid, h_hi))) + br_ref[...]
    rt_ref[...] = jnp.concatenate([_top2_route(lgt), jnp.zeros((SUBLANES - 4, lgt.shape[1]), F32)], axis=0)


def _cross(x1, kv, gx, w_xq, w_xo, gf, w_router, b_router, bsz, seq):
    n = x1.shape[0]
    tm = ROW_TM
    nt = seq // tm
    nm = kv.shape[0] // bsz
    row = lambda i: (i, 0)
    const = lambda i: (0, 0)
    vec = pl.BlockSpec((1, D_MODEL), const)
    wspec = pl.BlockSpec((D_MODEL, D_MODEL), const)
    return pl.pallas_call(
        _cross_kernel,
        grid=(n // tm,),
        in_specs=[pl.BlockSpec((tm, D_MODEL), row),
                  pl.BlockSpec((nm, 2 * D_MODEL), lambda i: (i // nt, 0)),
                  vec, wspec, wspec, vec,
                  pl.BlockSpec((2, _ROUTER_ROWS, D_MODEL), lambda i: (0, 0, 0)),
                  pl.BlockSpec((_ROUTER_ROWS, 1), const)],
        out_specs=[pl.BlockSpec((tm, D_MODEL), row),
                   pl.BlockSpec((tm, D_MODEL // 2), row),
                   pl.BlockSpec((SUBLANES, tm), lambda i: (0, i))],
        out_shape=[jax.ShapeDtypeStruct((n, D_MODEL), F32),
                   jax.ShapeDtypeStruct((n, D_MODEL // 2), jnp.uint32),
                   jax.ShapeDtypeStruct((SUBLANES, n), F32)],
        compiler_params=_params("parallel"),
        name="cross_attention",
    )(x1, kv, gx, w_xq, w_xo, gf, w_router, b_router)


def _expert_kernel(blk_ref, xb_ref, w1_ref, w3_ref, w2_ref, o_ref):
    n_used = blk_ref[pl.num_programs(0)]

    @pl.when(pl.program_id(0) < n_used)
    def _():
        xb = _unpack_pairs(xb_ref[...]).astype(MXU_DTYPE)
        a = _dot(xb, w1_ref[...])
        hmid = a * jax.nn.sigmoid(a) * _dot(xb, w3_ref[...])
        o_ref[...] = _dot(hmid, w2_ref[...])

    @pl.when(pl.program_id(0) >= n_used)
    def _():
        o_ref[...] = jnp.zeros_like(o_ref)


def _experts(blk, xb, w1, w3, w2):
    cap = xb.shape[0]
    nblk = cap // MOE_BLOCK
    row = lambda i, e: (i, 0)
    by_expert = lambda i, e: (e[i], 0, 0)
    grid_spec = pltpu.PrefetchScalarGridSpec(
        num_scalar_prefetch=1,
        grid=(nblk,),
        in_specs=[pl.BlockSpec((MOE_BLOCK, D_MODEL // 2), row),
                  pl.BlockSpec((None, D_MODEL, D_EXPERT), by_expert),
                  pl.BlockSpec((None, D_MODEL, D_EXPERT), by_expert),
                  pl.BlockSpec((None, D_EXPERT, D_MODEL), by_expert)],
        out_specs=pl.BlockSpec((MOE_BLOCK, D_MODEL), row),
    )
    return pl.pallas_call(
        _expert_kernel,
        grid_spec=grid_spec,
        out_shape=jax.ShapeDtypeStruct((cap, D_MODEL), F32),
        compiler_params=_params("arbitrary"),
        name="experts",
    )(blk, xb, w1, w3, w2)


def _final_kernel(x_ref, w_ref, g_ref, *refs):
    o_ref = refs[-1]
    tiles_per_split = pl.num_programs(0) // MOE_SPLITS
    for s in range(MOE_SPLITS):
        @pl.when(pl.program_id(0) // tiles_per_split == s)
        def _(s=s):
            moe = w_ref[:, 0:1] * refs[2 * s][...] + w_ref[:, 1:2] * refs[2 * s + 1][...]
            o_ref[...] = _rms(x_ref[...] + moe, g_ref[...])


def _final(x2, wts, g, ys):
    n = x2.shape[0]
    tm = ROW_TM
    tiles_per_split = n // tm // MOE_SPLITS
    row = lambda i: (i, 0)
    wide = pl.BlockSpec((tm, D_MODEL), row)

    def split_spec(s):
        return pl.BlockSpec((tm, D_MODEL), lambda i: (jnp.clip(i - s * tiles_per_split, 0, tiles_per_split - 1), 0))

    return pl.pallas_call(
        _final_kernel,
        grid=(n // tm,),
        in_specs=[wide, pl.BlockSpec((tm, EXP_TOPK), row), pl.BlockSpec((1, D_MODEL), lambda i: (0, 0))] + [
            split_spec(s) for s in range(MOE_SPLITS) for _ in range(EXP_TOPK)],
        out_specs=wide,
        out_shape=jax.ShapeDtypeStruct((n, D_MODEL), F32),
        compiler_params=_params("parallel"),
        name="final_norm",
    )(x2, wts, g, *[y for pair in ys for y in pair])


def _route(eid, tok0):
    n_tok = eid.shape[1]
    eid = eid.reshape(-1)
    n_asg = eid.shape[0]
    iota = jnp.arange(n_asg, dtype=jnp.int32)
    se, order = lax.sort_key_val(eid, iota)
    counts = jnp.sum((jnp.arange(N_EXPERTS, dtype=jnp.int32)[:, None] == eid[None, :]).astype(jnp.int32), axis=1)
    padded = (counts + MOE_BLOCK - 1) // MOE_BLOCK * MOE_BLOCK
    starts = jnp.cumsum(counts) - counts
    pends = jnp.cumsum(padded)
    pstarts = pends - padded
    cap = ((n_asg + MOE_BLOCK - 1) // MOE_BLOCK + N_EXPERTS) * MOE_BLOCK
    nblk = cap // MOE_BLOCK
    blk_e = jnp.minimum(jnp.searchsorted(pends, jnp.arange(nblk) * MOE_BLOCK, side='right', method='compare_all'),
                        N_EXPERTS - 1).astype(jnp.int32)
    per_row = lambda a: jnp.repeat(a[blk_e], MOE_BLOCK)
    row = jnp.arange(cap, dtype=jnp.int32)
    off = row - per_row(pstarts)
    asg = order[jnp.clip(per_row(starts) + off, 0, n_asg - 1)]
    buf_tok = tok0 + jnp.where(off < per_row(counts), asg % n_tok, row % n_tok)
    dest_sorted = iota + (pstarts - starts)[se]
    _, pos = lax.sort_key_val(order, dest_sorted)
    n_used = (pends[-1] // MOE_BLOCK).astype(jnp.int32)
    return buf_tok, jnp.concatenate([blk_e, n_used[None]]), pos.reshape(EXP_TOPK, n_tok)


def kernel(x, mem, norm_mix_g, w_in, w_ret_o, w_nsa_o, w_out, cmp_pe_k, cmp_w1_k, cmp_w2_k, cmp_pe_v,
           cmp_w1_v, cmp_w2_v, norm_x_g, norm_mem_g, w_xq, w_xkv, w_xo, norm_ffn_g, w_grp, b_grp, w_rt,
           b_rt, w_e1, w_e3, w_e2, norm_f_g):
    bsz, seq, _ = x.shape
    n = bsz * seq
    assert seq % PROJ_TM == 0 and seq % (2 * ATT_TQ) == 0 and w_in.shape[0] == 1
    assert n % (ROW_TM * MOE_SPLITS) == 0
    cast = lambda a: a.astype(MXU_DTYPE)
    xc = x.reshape(n, D_MODEL)
    l = 0

    w_main, w_t, w_gates = _split_w_in(w_in[l])
    qk, rv, rg, ckv, sk, wk, ngl, qt, svt, wvt = _proj(
        xc, norm_mix_g[l][None, :], w_main, w_t, _rope_tables(seq), bsz, seq)
    y_ret = _retention(qk, rv, rg, bsz, seq)
    w2 = jnp.stack([cmp_w2_k[l], cmp_w2_v[l]])
    cmp_k, cmp_vt = _compress(ckv, jnp.stack([cmp_pe_k[l], cmp_pe_v[l]]),
                              cast(jnp.stack([cmp_w1_k[l], cmp_w1_v[l]])),
                              cast(w2), cast(w2.transpose(0, 2, 1)), bsz, seq)
    o_cmp, selt = _cmp_attention(qt, cmp_k, cmp_vt, bsz, seq)
    o_sel = _flash_attention(qt, sk, svt, selt, bsz, seq)
    o_win = _flash_attention(qt, wk, wvt, None, bsz, seq)
    x1 = _merge(xc, y_ret, o_cmp, o_sel, o_win, ngl, norm_mix_g[l][None, :], w_gates,
                cast(w_ret_o[l]), cast(w_nsa_o[l]), cast(w_out[l]))

    kv = _mem_kv(mem.reshape(-1, D_MODEL), norm_mem_g[l][None, :], cast(w_xkv[l]), bsz)
    gap = _ROUTER_E0 - N_EGROUPS
    w_router = jnp.concatenate([w_grp[l].T, jnp.zeros((gap, D_MODEL), F32), w_rt[l].T], axis=0)
    wr_hi = cast(w_router)
    wr_mid = cast(w_router - wr_hi.astype(F32))
    b_router = jnp.concatenate([b_grp[l], jnp.zeros((gap,), F32), b_rt[l]])[:, None]
    x2, hf, routed = _cross(x1, kv, norm_x_g[l][None, :], cast(w_xq[l]), cast(w_xo[l]),
                            norm_ffn_g[l][None, :], jnp.stack([wr_hi, wr_mid]), b_router, bsz, seq)

    eid = routed[0:EXP_TOPK].astype(jnp.int32)
    wts = routed[EXP_TOPK:2 * EXP_TOPK].T
    per_split = n // MOE_SPLITS
    ys = []
    for s in range(MOE_SPLITS):
        buf_tok, blk, pos = _route(eid[:, s * per_split:(s + 1) * per_split], s * per_split)
        y = _experts(blk, hf[buf_tok], w_e1[l], w_e3[l], w_e2[l])
        ys.append([y[pos[j]] for j in range(EXP_TOPK)])
    out = _final(x2, wts, norm_f_g[None, :], ys)
    return out.reshape(bsz, seq, D_MODEL)
```

```python
import functools

import numpy as np
import jax
import jax.numpy as jnp
from jax import lax
from jax.experimental import pallas as pl
from jax.experimental.pallas import tpu as pltpu

MXU_DTYPE = jnp.bfloat16
F32 = jnp.float32

D_MODEL = 1024
N_MEM = 256
EPS = 1e-6
NEG_INF = -1e30
SEL_FORCE = 1e4

R_HEADS = 4
R_DK = 128
R_DV = 256
R_CHUNK = 128
ROPE_BASE = 10000.0

NSA_HEADS = 8
NSA_GROUPS = 2
NSA_HPG = NSA_HEADS // NSA_GROUPS
NSA_DH = 128
CMP_LEN = 32
CMP_STRIDE = 16
SEL_LEN = 64
SEL_TOPK = 16
WINDOW = 512

X_HEADS = 4
X_DH = D_MODEL // X_HEADS

N_EGROUPS = 4
EXP_PER_GROUP = 8
N_EXPERTS = N_EGROUPS * EXP_PER_GROUP
EXP_TOPK = 2
D_EXPERT = 512
MOE_BLOCK = 256

RET_QK = R_HEADS * R_DK
RET_V = R_HEADS * R_DV
NSA_Q = NSA_HEADS * NSA_DH
NSA_KV = NSA_GROUPS * NSA_DH
SPLITS = (RET_QK, RET_QK, RET_V, RET_V, NSA_Q, NSA_KV, NSA_KV, NSA_KV, NSA_KV, NSA_KV, NSA_KV,
          3 * NSA_HEADS, D_MODEL, D_MODEL)

_EXP2_SCALE = (NSA_DH ** -0.5) * float(np.log2(np.e))

LANES = 128
SUBLANES = 8
VMEM_LIMIT = 56 * 1024 * 1024

PROJ_TM = 512
RET_ROWS = 512
ATT_TQ = 256
ATT_TK = 256
ROW_TM = 512
MOE_SPLITS = 2
VT_ROWS = NSA_DH + 2 * SUBLANES
SEL_KW = 2 * NSA_DH
assert ATT_TQ == ATT_TK and SEL_LEN * 2 * SUBLANES >= ATT_TK and WINDOW == 2 * ATT_TK


def _params(*sem):
    return pltpu.CompilerParams(dimension_semantics=sem, vmem_limit_bytes=VMEM_LIMIT)


def _dot(a, b):
    return jnp.dot(a.astype(MXU_DTYPE), b.astype(MXU_DTYPE), preferred_element_type=F32)


def _dot_nt(a, b):
    return lax.dot_general(a.astype(MXU_DTYPE), b.astype(MXU_DTYPE), (((1,), (1,)), ((), ())),
                           preferred_element_type=F32)


def _dot_tn(a, b):
    return lax.dot_general(a.astype(MXU_DTYPE), b.astype(MXU_DTYPE), (((0,), (0,)), ((), ())),
                           preferred_element_type=F32)


def _split3(p):
    hi = p.astype(MXU_DTYPE)
    r1 = p - hi.astype(F32)
    mid = r1.astype(MXU_DTYPE)
    lo = (r1 - mid.astype(F32)).astype(MXU_DTYPE)
    return hi, mid, lo


def _rms(x, g):
    return x * lax.rsqrt(jnp.mean(x * x, axis=-1, keepdims=True) + EPS) * g


_C_RQK = 0
_C_RV = _C_RQK + 2 * RET_QK
_C_RG = _C_RV + RET_V
_C_CKV = _C_RG + RET_V
_C_SK = _C_CKV + 2 * NSA_KV
_C_WK = _C_SK + NSA_KV
_C_NGL = _C_WK + NSA_KV
_C_END = _C_NGL + LANES
_R_NQ = 0
_R_SV = _R_NQ + NSA_Q
_R_WV = _R_SV + NSA_KV
_R_END = _R_WV + NSA_KV


def _proj_kernel(x_ref, g_ref, w_ref, wt_ref, cos_ref, sin_up_ref, sin_dn_ref,
                 qk_ref, rv_ref, rg_ref, ckv_ref, sk_ref, wk_ref, ngl_ref, nqt_ref, svt_ref, wvt_ref):
    hb = _rms(x_ref[...], g_ref[...]).astype(MXU_DTYPE)
    tm = hb.shape[0]

    def mm(off, width):
        return jnp.dot(hb, w_ref[:, off:off + width], preferred_element_type=F32)

    def mm_t(off, height):
        return _dot_nt(wt_ref[off:off + height, :], hb)

    cos = cos_ref[...]
    sin_up = sin_up_ref[...]
    sin_dn = sin_dn_ref[...]
    qk = mm(_C_RQK, 2 * RET_QK)
    for i in range(2 * R_HEADS):
        t = qk[:, i * R_DK:(i + 1) * R_DK]
        r = t * cos + pltpu.roll(t, R_DK - 1, axis=1) * sin_up + pltpu.roll(t, 1, axis=1) * sin_dn
        if i >= R_HEADS:
            r = r * (R_DK ** -0.5)
        qk_ref[:, i * R_DK:(i + 1) * R_DK] = r.astype(qk_ref.dtype)
    rv_ref[...] = mm(_C_RV, RET_V).astype(rv_ref.dtype)
    rg_ref[...] = mm(_C_RG, RET_V)
    ckv_ref[...] = mm(_C_CKV, 2 * NSA_KV)
    sk = mm(_C_SK, NSA_KV).astype(sk_ref.dtype)
    blk = (lax.broadcasted_iota(jnp.int32, (tm, NSA_DH), 0) % ATT_TK) // SEL_LEN
    onehot = jnp.where(lax.broadcasted_iota(jnp.int32, (tm, NSA_DH), 1) == blk, 1.0, 0.0).astype(sk_ref.dtype)
    for g in range(NSA_GROUPS):
        sk_ref[:, g * SEL_KW:g * SEL_KW + NSA_DH] = sk[:, g * NSA_DH:(g + 1) * NSA_DH]
        sk_ref[:, g * SEL_KW + NSA_DH:(g + 1) * SEL_KW] = onehot
    wk_ref[...] = mm(_C_WK, NSA_KV).astype(wk_ref.dtype)
    ngl_ref[...] = mm(_C_NGL, LANES)

    nqt = (mm_t(_R_NQ, NSA_Q) * _EXP2_SCALE).astype(nqt_ref.dtype)
    for g in range(NSA_GROUPS):
        for j in range(tm // ATT_TQ):
            for hh in range(NSA_HPG):
                head = g * NSA_HPG + hh
                nqt_ref[g, j, :, hh * ATT_TQ:(hh + 1) * ATT_TQ] = (
                    nqt[head * NSA_DH:(head + 1) * NSA_DH, j * ATT_TQ:(j + 1) * ATT_TQ])
    for off, out_ref in ((_R_SV, svt_ref), (_R_WV, wvt_ref)):
        vt = mm_t(off, NSA_KV).astype(out_ref.dtype)
        for g in range(NSA_GROUPS):
            for j in range(tm // ATT_TK):
                out_ref[g, j, 0:NSA_DH] = vt[g * NSA_DH:(g + 1) * NSA_DH, j * ATT_TK:(j + 1) * ATT_TK]
                out_ref[g, j, NSA_DH:VT_ROWS] = jnp.ones((VT_ROWS - NSA_DH, ATT_TK), out_ref.dtype)


def _split_w_in(w_in):
    offs = np.cumsum((0,) + SPLITS)
    rq, rk, rv, rg, nq, ck, cv, sk, sv, wk, wv, ngl, ga, gb = [
        w_in[:, offs[i]:offs[i + 1]] for i in range(len(SPLITS))]
    ngl = jnp.pad(ngl, ((0, 0), (0, LANES - ngl.shape[1])))
    w = jnp.concatenate([rq, rk, rv, rg, ck, cv, sk, wk, ngl], axis=1).astype(MXU_DTYPE)
    wt = jnp.concatenate([nq, sv, wv], axis=1).T.astype(MXU_DTYPE)
    w_gates = jnp.concatenate([ga, gb], axis=1).astype(MXU_DTYPE)
    return w, wt, w_gates


def _rope_tables(seq):
    pos = jnp.arange(seq, dtype=F32)
    inv_freq = ROPE_BASE ** (-jnp.arange(0, R_DK, 2, dtype=F32) / R_DK)
    ang = pos[:, None] * inv_freq[None, :]
    cos = jnp.repeat(jnp.cos(ang), 2, axis=1)
    sin = jnp.repeat(jnp.sin(ang), 2, axis=1)
    even = (jnp.arange(R_DK) % 2 == 0)[None, :]
    return cos, jnp.where(even, -sin, 0.0), jnp.where(even, 0.0, sin)


def _proj(x2d, g, w, wt, tables, bsz, seq):
    n = x2d.shape[0]
    tm = PROJ_TM
    nt = seq // tm
    row = lambda i: (i, 0)
    const = lambda i: (0, 0)
    tile = lambda i: (i // nt, 0, i % nt, 0, 0)
    table = pl.BlockSpec((tm, R_DK), lambda i: (i % nt, 0))
    widths = (2 * RET_QK, RET_V, RET_V, 2 * NSA_KV, NSA_GROUPS * SEL_KW, NSA_KV, LANES)
    dtypes = (MXU_DTYPE, MXU_DTYPE, F32, F32, MXU_DTYPE, MXU_DTYPE, F32)
    width = NSA_HPG * ATT_TQ
    vt_shape = jax.ShapeDtypeStruct((bsz, NSA_GROUPS, seq // ATT_TK, VT_ROWS, ATT_TK), MXU_DTYPE)
    vt_spec = pl.BlockSpec((None, NSA_GROUPS, tm // ATT_TK, VT_ROWS, ATT_TK), tile)
    return pl.pallas_call(
        _proj_kernel,
        grid=(n // tm,),
        in_specs=[pl.BlockSpec((tm, D_MODEL), row),
                  pl.BlockSpec((1, D_MODEL), const),
                  pl.BlockSpec((D_MODEL, _C_END), const, pipeline_mode=pl.Buffered(1)),
                  pl.BlockSpec((_R_END, D_MODEL), const, pipeline_mode=pl.Buffered(1)),
                  table, table, table],
        out_specs=[pl.BlockSpec((tm, wd), row) for wd in widths] + [
            pl.BlockSpec((None, NSA_GROUPS, tm // ATT_TQ, NSA_DH, width), tile), vt_spec, vt_spec],
        out_shape=[jax.ShapeDtypeStruct((n, wd), dt) for wd, dt in zip(widths, dtypes)] + [
            jax.ShapeDtypeStruct((bsz, NSA_GROUPS, seq // ATT_TQ, NSA_DH, width), MXU_DTYPE), vt_shape, vt_shape],
        compiler_params=_params("parallel"),
        name="proj",
    )(x2d, g, w, wt, *tables)


def _ret_kernel(qk_ref, v_ref, g_ref, decay_ref, zeta_ref, xi_ref, cd_ref, y_ref, state_ref):
    @pl.when(pl.program_id(1) == 0)
    def _():
        state_ref[...] = jnp.zeros_like(state_ref)

    for c in range(RET_ROWS // R_CHUNK):
        rows = slice(c * R_CHUNK, (c + 1) * R_CHUNK)
        for h in range(R_HEADS):
            q = qk_ref[rows, h * R_DK:(h + 1) * R_DK]
            k = qk_ref[rows, RET_QK + h * R_DK:RET_QK + (h + 1) * R_DK]
            v = v_ref[rows, h * R_DV:(h + 1) * R_DV]
            st = state_ref[h]
            scores = _dot_nt(q, k) * decay_ref[h]
            o = _dot(scores, v) + _dot(q, st) * xi_ref[h]
            o = o * lax.rsqrt(jnp.mean(o * o, axis=-1, keepdims=True) + EPS)
            g = g_ref[rows, h * R_DV:(h + 1) * R_DV]
            y_ref[rows, h * R_DV:(h + 1) * R_DV] = (g * jax.nn.sigmoid(g) * o).astype(y_ref.dtype)
            kz = k.astype(F32) * zeta_ref[h]
            state_ref[h] = st * cd_ref[h] + _dot_tn(kz, v)


def _retention(qk, rv, rg, bsz, seq):
    n = qk.shape[0]
    nt = seq // RET_ROWS
    log_g = jnp.log1p(-jnp.exp2(-5.0 - jnp.arange(R_HEADS, dtype=F32)))
    idx = jnp.arange(R_CHUNK, dtype=F32)
    diff = idx[:, None] - idx[None, :]
    decay = jnp.where(diff >= 0, jnp.exp(log_g[:, None, None] * jnp.maximum(diff, 0.0)), 0.0)
    zeta = jnp.exp(log_g[:, None] * (R_CHUNK - 1.0 - idx)[None, :])[:, :, None]
    xi = jnp.exp(log_g[:, None] * (idx + 1.0)[None, :])[:, :, None]
    cd = jnp.exp(log_g * R_CHUNK)[:, None, None]
    row = lambda b, i: (b * nt + i, 0)
    const3 = lambda b, i: (0, 0, 0)
    return pl.pallas_call(
        _ret_kernel,
        grid=(bsz, nt),
        in_specs=[pl.BlockSpec((RET_ROWS, 2 * RET_QK), row),
                  pl.BlockSpec((RET_ROWS, RET_V), row),
                  pl.BlockSpec((RET_ROWS, RET_V), row),
                  pl.BlockSpec((R_HEADS, R_CHUNK, R_CHUNK), const3),
                  pl.BlockSpec((R_HEADS, R_CHUNK, 1), const3),
                  pl.BlockSpec((R_HEADS, R_CHUNK, 1), const3),
                  pl.BlockSpec((R_HEADS, 1, 1), const3)],
        out_specs=pl.BlockSpec((RET_ROWS, RET_V), row),
        out_shape=jax.ShapeDtypeStruct((n, RET_V), MXU_DTYPE),
        scratch_shapes=[pltpu.VMEM((R_HEADS, R_DK, R_DV), F32)],
        compiler_params=_params("parallel", "arbitrary"),
        name="retention",
    )(qk, rv, rg, decay, zeta, xi, cd)


def _compress_kernel(x_ref, pe_ref, w1_ref, w2_ref, w2t_ref, o_ref, ot_ref, buf_ref, *, seq):
    ncp = seq // CMP_STRIDE
    buf_ref[0:seq, :] = x_ref[...]
    buf_ref[seq:seq + LANES, :] = jnp.zeros((LANES, NSA_DH), F32)
    acc = jnp.zeros((ncp, NSA_DH), F32)
    for l in range(CMP_LEN):
        xl = buf_ref[pl.ds(l, ncp, stride=CMP_STRIDE), :] + pe_ref[l:l + 1, :]
        acc = acc + _dot(xl, w1_ref[l])
    hid = jax.nn.gelu(acc)
    o_ref[...] = _dot(hid, w2_ref[...]).astype(o_ref.dtype)
    ot_ref[...] = _dot_nt(w2t_ref[...], hid).astype(ot_ref.dtype)


def _compress(ckv, pe, w1, w2, w2t, bsz, seq):
    ncp = seq // CMP_STRIDE
    nj = 2 * NSA_GROUPS
    wsel = lambda b, j: (j // NSA_GROUPS, 0, 0)
    return pl.pallas_call(
        functools.partial(_compress_kernel, seq=seq),
        grid=(bsz, nj),
        in_specs=[pl.BlockSpec((seq, NSA_DH), lambda b, j: (b, j)),
                  pl.BlockSpec((None, CMP_LEN, NSA_DH), wsel),
                  pl.BlockSpec((None, CMP_LEN, NSA_DH, NSA_DH), lambda b, j: (j // NSA_GROUPS, 0, 0, 0)),
                  pl.BlockSpec((None, NSA_DH, NSA_DH), wsel),
                  pl.BlockSpec((None, NSA_DH, NSA_DH), wsel)],
        out_specs=[pl.BlockSpec((None, None, ncp, NSA_DH), lambda b, j: (b, j, 0, 0)),
                   pl.BlockSpec((None, None, NSA_DH, ncp), lambda b, j: (b, j, 0, 0))],
        out_shape=[jax.ShapeDtypeStruct((bsz, nj, ncp, NSA_DH), MXU_DTYPE),
                   jax.ShapeDtypeStruct((bsz, nj, NSA_DH, ncp), MXU_DTYPE)],
        scratch_shapes=[pltpu.VMEM((seq + LANES, NSA_DH), F32)],
        compiler_params=_params("parallel", "parallel"),
        name="compress",
    )(ckv, pe, w1, w2, w2t)


def _cmp_attn_kernel(qt_ref, k_ref, vt_ref, ov_ref, o_ref, selt_ref, p_ref, *, ncp, nb):
    tq = ATT_TQ
    qi = pl.program_id(2)
    t0 = qi * tq
    any_valid = (t0 + lax.broadcasted_iota(jnp.int32, (1, tq), 1)) >= CMP_LEN - 1
    any_valid = jnp.concatenate([any_valid] * NSA_HPG, axis=1)

    def probabilities(rows):
        t = t0 + lax.broadcasted_iota(jnp.int32, (rows, tq), 1)
        n = lax.broadcasted_iota(jnp.int32, (rows, tq), 0)
        valid = (n * CMP_STRIDE + (CMP_LEN - 1) <= t) & (n < ncp - 1)
        valid = jnp.concatenate([valid] * NSA_HPG, axis=1)
        s = jnp.where(valid, _dot(k_ref[0:rows, :], qt_ref[...]), NEG_INF)
        e = jnp.exp2(s - jnp.max(s, axis=0, keepdims=True))
        p_ref[0:rows, :] = e * jnp.where(any_valid, 1.0 / jnp.sum(e, axis=0, keepdims=True), 0.0)
        if rows < ncp:
            p_ref[rows:, :] = jnp.zeros((ncp - rows, NSA_HPG * tq), F32)

    n_buckets = 4
    per_bucket = ncp // n_buckets
    visible = (t0 + tq - CMP_LEN) // CMP_STRIDE + 1
    bucket = jnp.minimum((visible + per_bucket - 1) // per_bucket, n_buckets)
    for c in range(1, n_buckets + 1):
        pl.when(bucket == c)(functools.partial(probabilities, c * per_bucket))

    p = p_ref[...]
    ot = _dot(vt_ref[...], p)
    for h in range(NSA_HPG):
        o_ref[:, h * NSA_DH:(h + 1) * NSA_DH] = ot[:, h * tq:(h + 1) * tq].T
    psum = sum(p[:, h * tq:(h + 1) * tq] for h in range(NSA_HPG))

    imp = sum(_dot(ov_ref[...], part) for part in _split3(psum))
    j = lax.broadcasted_iota(jnp.int32, (nb, tq), 0)
    tb = (t0 + lax.broadcasted_iota(jnp.int32, (nb, tq), 1)) // SEL_LEN
    forced = (j == 0) | (j == tb) | (j == tb - 1)
    imp = jnp.where(j > tb, -SEL_FORCE, jnp.where(forced, SEL_FORCE, imp))

    sub = SUBLANES
    grp = [imp[r * sub:(r + 1) * sub] for r in range(nb // sub)]
    cnt = [jnp.zeros((sub, tq), F32) for _ in grp]
    for i in range(nb):
        row = jnp.broadcast_to(imp[i:i + 1, :], (sub, tq))
        for r in range(nb // sub):
            if r * sub > i:
                beats = jnp.where(row >= grp[r], 1.0, 0.0)
            elif r * sub + sub - 1 < i:
                beats = jnp.where(row > grp[r], 1.0, 0.0)
            else:
                jr = r * sub + lax.broadcasted_iota(jnp.int32, (sub, tq), 0)
                beats = jnp.where(jr > i, jnp.where(row >= grp[r], 1.0, 0.0), jnp.where(row > grp[r], 1.0, 0.0))
            cnt[r] = cnt[r] + beats
    k_sel = min(SEL_TOPK, nb)
    for r in range(nb // sub):
        selt_ref[r * sub:(r + 1) * sub, :] = jnp.where(cnt[r] < k_sel, 1.0, 0.0)


def _cmp_attention(qt, cmp_k, cmp_vt, bsz, seq):
    n = bsz * seq
    ncp = seq // CMP_STRIDE
    nb = seq // SEL_LEN
    nt = seq // ATT_TQ
    gw = NSA_HPG * NSA_DH
    cstart = np.arange(ncp) * CMP_STRIDE
    jstart = np.arange(nb) * SEL_LEN
    ov = ((cstart[None, :] < jstart[:, None] + SEL_LEN) & (cstart[None, :] + CMP_LEN > jstart[:, None])
          & (np.arange(ncp)[None, :] < ncp - 1))
    ov = jnp.asarray(ov, MXU_DTYPE)
    return pl.pallas_call(
        functools.partial(_cmp_attn_kernel, ncp=ncp, nb=nb),
        grid=(bsz, NSA_GROUPS, nt),
        in_specs=[pl.BlockSpec((None, None, None, NSA_DH, NSA_HPG * ATT_TQ), lambda b, g, i: (b, g, i, 0, 0)),
                  pl.BlockSpec((None, None, ncp, NSA_DH), lambda b, g, i: (b, g, 0, 0)),
                  pl.BlockSpec((None, None, NSA_DH, ncp), lambda b, g, i: (b, NSA_GROUPS + g, 0, 0)),
                  pl.BlockSpec((nb, ncp), lambda b, g, i: (0, 0))],
        out_specs=[pl.BlockSpec((ATT_TQ, gw), lambda b, g, i: (b * nt + i, g)),
                   pl.BlockSpec((None, None, nb, ATT_TQ), lambda b, g, i: (b, g, 0, i))],
        out_shape=[jax.ShapeDtypeStruct((n, NSA_Q), F32),
                   jax.ShapeDtypeStruct((bsz, NSA_GROUPS, nb, seq), F32)],
        scratch_shapes=[pltpu.VMEM((ncp, NSA_HPG * ATT_TQ), F32)],
        compiler_params=_params("parallel", "parallel", "parallel"),
        name="cmp_attention",
    )(qt, cmp_k, cmp_vt, ov)


def _softmax_step(s, m, vt, acc_ref):
    m_new = jnp.maximum(m, jnp.max(s, axis=0, keepdims=True))
    p = jnp.exp2(s - m_new)
    acc_ref[...] = jnp.exp2(m - m_new) * acc_ref[...] + _dot(vt, p)
    return m_new


def _flash_finish(o_ref, acc_ref):
    inv = 1.0 / acc_ref[NSA_DH:NSA_DH + 1, :]
    for h in range(NSA_HPG):
        cols = slice(h * ATT_TQ, (h + 1) * ATT_TQ)
        o_ref[:, h * NSA_DH:(h + 1) * NSA_DH] = (acc_ref[0:NSA_DH, cols] * inv[:, cols]).T


def _sel_attn_kernel(qt_ref, k_ref, vt_ref, selt_ref, o_ref, acc_ref, s0_ref, s1_ref, qa_ref):
    qi = pl.program_id(2)
    width = NSA_HPG * ATT_TQ
    blocks_per_tile = ATT_TK // SEL_LEN
    bias_rows = 2 * SUBLANES
    n_loop = qi * (ATT_TQ // ATT_TK)
    n_steps = (n_loop + 1) // 2 * 2
    tile_of = lambda step: jnp.where(step < n_loop, step, qi)
    rowid = lax.broadcasted_iota(jnp.int32, (bias_rows, ATT_TQ), 0)
    tri = (lax.broadcasted_iota(jnp.int32, (ATT_TK, ATT_TQ), 0) <=
           lax.broadcasted_iota(jnp.int32, (ATT_TK, ATT_TQ), 1))
    tri = jnp.concatenate([tri] * NSA_HPG, axis=1)

    qa_ref[0:NSA_DH, :] = qt_ref[...]
    qa_ref[NSA_DH + bias_rows:, :] = jnp.zeros((qa_ref.shape[0] - NSA_DH - bias_rows, width), qa_ref.dtype)

    def scores(step):
        kt = tile_of(step)
        threshold = jnp.where((step < n_loop) | (step == n_steps), 0.5, 2.0)
        bias = jnp.zeros((bias_rows, ATT_TQ), F32)
        for jb in range(blocks_per_tile):
            picked = selt_ref[pl.ds(kt * blocks_per_tile + jb, 1), :] > threshold
            bias = jnp.where(rowid == jb, jnp.where(picked, 0.0, NEG_INF), bias)
        qa_ref[NSA_DH:NSA_DH + bias_rows, :] = jnp.concatenate([bias] * NSA_HPG, axis=1).astype(qa_ref.dtype)
        k = k_ref[pl.ds(pl.multiple_of(kt * ATT_TK, ATT_TK), ATT_TK), :]
        return _dot(k, qa_ref[...])

    def half_step(step, s_cur, s_nxt, m, diagonal=False):
        if not diagonal:
            s_nxt[...] = scores(step + 1)
        s = jnp.where(tri, s_cur[...], NEG_INF) if diagonal else s_cur[...]
        return _softmax_step(s, m, vt_ref[tile_of(step)], acc_ref)

    def body(i, m):
        m = half_step(2 * i, s0_ref, s1_ref, m)
        return half_step(2 * i + 1, s1_ref, s0_ref, m)

    acc_ref[...] = jnp.zeros_like(acc_ref)
    s0_ref[...] = scores(0)
    m = lax.fori_loop(0, n_steps // 2, body, jnp.full((1, width), NEG_INF, F32))
    half_step(n_steps, s0_ref, s1_ref, m, diagonal=True)
    _flash_finish(o_ref, acc_ref)


def _win_attn_kernel(qt_ref, k_ref, vt_ref, o_ref, acc_ref, s0_ref, s1_ref, m_ref):
    qi = pl.program_id(2)
    row = lax.broadcasted_iota(jnp.int32, (ATT_TK, ATT_TQ), 0)
    col = lax.broadcasted_iota(jnp.int32, (ATT_TK, ATT_TQ), 1)
    causal = jnp.concatenate([row <= col] * NSA_HPG, axis=1)
    window_tail = jnp.concatenate([row > col] * NSA_HPG, axis=1)

    def scores(kt):
        k = k_ref[pl.ds(pl.multiple_of(kt * ATT_TK, ATT_TK), ATT_TK), :]
        return _dot(k, qt_ref[...])

    acc_ref[...] = jnp.zeros_like(acc_ref)
    s0_ref[...] = scores(qi)
    s1_ref[...] = scores(jnp.maximum(qi - 1, 0))
    m_ref[...] = _softmax_step(jnp.where(causal, s0_ref[...], NEG_INF), jnp.full(m_ref.shape, NEG_INF, F32),
                               vt_ref[qi], acc_ref)

    @pl.when(qi >= 1)
    def _():
        s0_ref[...] = scores(jnp.maximum(qi - 2, 0))
        m_ref[...] = _softmax_step(s1_ref[...], m_ref[...], vt_ref[qi - 1], acc_ref)

    @pl.when(qi >= 2)
    def _():
        _softmax_step(jnp.where(window_tail, s0_ref[...], NEG_INF), m_ref[...], vt_ref[qi - 2], acc_ref)

    _flash_finish(o_ref, acc_ref)


def _flash_attention(qt, k, vt, selt, bsz, seq):
    n = bsz * seq
    nt = seq // ATT_TQ
    nkt = seq // ATT_TK
    nb = seq // SEL_LEN
    gw = NSA_HPG * NSA_DH
    width = NSA_HPG * ATT_TQ
    kw = k.shape[1] // NSA_GROUPS
    in_specs = [pl.BlockSpec((None, None, None, NSA_DH, width), lambda b, g, i: (b, g, i, 0, 0)),
                pl.BlockSpec((seq, kw), lambda b, g, i: (b, g)),
                pl.BlockSpec((None, None, nkt, VT_ROWS, ATT_TK), lambda b, g, i: (b, g, 0, 0, 0))]
    args = [qt, k, vt]
    scratch = [pltpu.VMEM((VT_ROWS, width), F32),
               pltpu.VMEM((ATT_TK, width), F32),
               pltpu.VMEM((ATT_TK, width), F32)]
    if selt is None:
        body = _win_attn_kernel
        name = "window_attention"
        scratch.append(pltpu.VMEM((1, width), F32))
    else:
        body = _sel_attn_kernel
        name = "selected_attention"
        in_specs.append(pl.BlockSpec((None, None, nb, ATT_TQ), lambda b, g, i: (b, g, 0, i)))
        args.append(selt)
        scratch.append(pltpu.VMEM((kw, width), MXU_DTYPE))
    return pl.pallas_call(
        body,
        grid=(bsz, NSA_GROUPS, nt),
        in_specs=in_specs,
        out_specs=pl.BlockSpec((ATT_TQ, gw), lambda b, g, i: (b * nt + i, g)),
        out_shape=jax.ShapeDtypeStruct((n, NSA_Q), F32),
        scratch_shapes=scratch,
        compiler_params=_params("parallel", "parallel", "parallel"),
        name=name,
    )(*args)


def _merge_kernel(x_ref, yr_ref, oc_ref, os_ref, ow_ref, ngl_ref, g_ref, wg_ref, wr_ref, wn_ref, wo_ref, o_ref):
    tm = x_ref.shape[0]
    hb = _rms(x_ref[...], g_ref[...]).astype(MXU_DTYPE)
    ga = jnp.dot(hb, wg_ref[:, :D_MODEL], preferred_element_type=F32)
    gb = jnp.dot(hb, wg_ref[:, D_MODEL:], preferred_element_type=F32)
    gates = jax.nn.sigmoid(ngl_ref[...])
    parts = []
    for h in range(NSA_HEADS):
        cols = slice(h * NSA_DH, (h + 1) * NSA_DH)

        def gate(br):
            return jnp.broadcast_to(gates[:, 3 * h + br:3 * h + br + 1], (tm, NSA_DH))

        parts.append(gate(0) * oc_ref[:, cols] + gate(1) * os_ref[:, cols] + gate(2) * ow_ref[:, cols])
    o_nsa = jnp.concatenate(parts, axis=1)
    y_ret = _dot(yr_ref[...], wr_ref[...])
    y_nsa = _dot(o_nsa, wn_ref[...])
    y = jax.nn.sigmoid(ga) * y_ret + jax.nn.sigmoid(gb) * y_nsa
    o_ref[...] = x_ref[...] + _dot(y, wo_ref[...])


def _merge(x2d, y_ret, o_cmp, o_sel, o_win, ngl, g_mix, w_gates, w_ret_o, w_nsa_o, w_out):
    n = x2d.shape[0]
    tm = ROW_TM
    row = lambda i: (i, 0)
    const = lambda i: (0, 0)
    wide = pl.BlockSpec((tm, D_MODEL), row)
    wspec = pl.BlockSpec((D_MODEL, D_MODEL), const)
    return pl.pallas_call(
        _merge_kernel,
        grid=(n // tm,),
        in_specs=[wide, wide, wide, wide, wide,
                  pl.BlockSpec((tm, LANES), row),
                  pl.BlockSpec((1, D_MODEL), const),
                  pl.BlockSpec((D_MODEL, 2 * D_MODEL), const),
                  wspec, wspec, wspec],
        out_specs=wide,
        out_shape=jax.ShapeDtypeStruct((n, D_MODEL), F32),
        compiler_params=_params("parallel"),
        name="merge",
    )(x2d, y_ret, o_cmp, o_sel, o_win, ngl, g_mix, w_gates, w_ret_o, w_nsa_o, w_out)


def _mem_kv_kernel(m_ref, g_ref, w_ref, o_ref):
    o_ref[...] = _dot(_rms(m_ref[...], g_ref[...]), w_ref[...]).astype(o_ref.dtype)


def _mem_kv(mem2d, g, w_xkv, bsz):
    nm = mem2d.shape[0] // bsz
    return pl.pallas_call(
        _mem_kv_kernel,
        grid=(bsz,),
        in_specs=[pl.BlockSpec((nm, D_MODEL), lambda b: (b, 0)),
                  pl.BlockSpec((1, D_MODEL), lambda b: (0, 0)),
                  pl.BlockSpec((D_MODEL, 2 * D_MODEL), lambda b: (0, 0))],
        out_specs=pl.BlockSpec((nm, 2 * D_MODEL), lambda b: (b, 0)),
        out_shape=jax.ShapeDtypeStruct((mem2d.shape[0], 2 * D_MODEL), MXU_DTYPE),
        compiler_params=_params("parallel"),
        name="mem_kv",
    )(mem2d, g, w_xkv)


def _pack_pairs(x):
    half = x.shape[1] // 2
    hi = lax.bitcast_convert_type(x[:, :half].astype(MXU_DTYPE).astype(F32), jnp.uint32)
    lo = lax.bitcast_convert_type(x[:, half:].astype(MXU_DTYPE).astype(F32), jnp.uint32)
    return (hi & jnp.uint32(0xFFFF0000)) | (lo >> 16)


def _unpack_pairs(u):
    hi = lax.bitcast_convert_type(u & jnp.uint32(0xFFFF0000), F32)
    lo = lax.bitcast_convert_type(u << 16, F32)
    return jnp.concatenate([hi, lo], axis=1)


_ROUTER_E0 = 2 * SUBLANES
_ROUTER_ROWS = _ROUTER_E0 + N_EXPERTS


def _top2_route(lgt):
    sub = SUBLANES
    t = lgt.shape[1]
    rowid = lax.broadcasted_iota(jnp.int32, (sub, t), 0)
    first = lambda hit: jnp.min(jnp.where(hit, rowid, sub), axis=0, keepdims=True)
    lg = jnp.where(rowid < N_EGROUPS, lgt[0:sub], NEG_INF)
    gmax = jnp.max(lg, axis=0, keepdims=True)
    grp = first(lg == gmax)
    g_gate = 1.0 / jnp.sum(jnp.exp(lg - gmax), axis=0, keepdims=True)
    experts_of = lambda g: lgt[_ROUTER_E0 + g * EXP_PER_GROUP:_ROUTER_E0 + (g + 1) * EXP_PER_GROUP]
    le = experts_of(N_EGROUPS - 1)
    for g in range(N_EGROUPS - 2, -1, -1):
        le = jnp.where(grp == g, experts_of(g), le)
    ex = jnp.exp(le - jnp.max(le, axis=0, keepdims=True))
    pe = ex / jnp.sum(ex, axis=0, keepdims=True)
    p0 = jnp.max(pe, axis=0, keepdims=True)
    i0 = first(pe == p0)
    rest = jnp.where(rowid == i0, -1.0, pe)
    p1 = jnp.max(rest, axis=0, keepdims=True)
    i1 = first(rest == p1)
    den = p0 + p1
    base = grp * EXP_PER_GROUP
    return jnp.concatenate([(base + i0).astype(F32), (base + i1).astype(F32),
                            g_gate * p0 / den, g_gate * p1 / den], axis=0)


def _cross_kernel(x_ref, kv_ref, gx_ref, wq_ref, wo_ref, gf_ref, wr_ref, br_ref, x2_ref, hf_ref, rt_ref):
    x = x_ref[...]
    q = _dot(_rms(x, gx_ref[...]), wq_ref[...])
    heads = []
    for h in range(X_HEADS):
        k = kv_ref[:, h * X_DH:(h + 1) * X_DH]
        v = kv_ref[:, D_MODEL + h * X_DH:D_MODEL + (h + 1) * X_DH]
        s = _dot_nt(q[:, h * X_DH:(h + 1) * X_DH], k) * (X_DH ** -0.5)
        e = jnp.exp(s - jnp.max(s, axis=-1, keepdims=True))
        p = e / jnp.sum(e, axis=-1, keepdims=True)
        heads.append(_dot(p, v))
    x2 = x + _dot(jnp.concatenate(heads, axis=1), wo_ref[...])
    x2_ref[...] = x2
    hf = _rms(x2, gf_ref[...])
    hf_ref[...] = _pack_pairs(hf)
    h_hi, h_mid, _ = _split3(hf)
    w_hi = wr_ref[0]
    w_mid = wr_ref[1]
    lgt = (_dot_nt(w_hi, h_hi) + (_dot_nt(w_hi, h_mid) + _dot_nt(w_mid, h_hi))) + br_ref[...]
    rt_ref[...] = jnp.concatenate([_top2_route(lgt), jnp.zeros((SUBLANES - 4, lgt.shape[1]), F32)], axis=0)


def _cross(x1, kv, gx, w_xq, w_xo, gf, w_router, b_router, bsz, seq):
    n = x1.shape[0]
    tm = ROW_TM
    nt = seq // tm
    nm = kv.shape[0] // bsz
    row = lambda i: (i, 0)
    const = lambda i: (0, 0)
    vec = pl.BlockSpec((1, D_MODEL), const)
    wspec = pl.BlockSpec((D_MODEL, D_MODEL), const)
    return pl.pallas_call(
        _cross_kernel,
        grid=(n // tm,),
        in_specs=[pl.BlockSpec((tm, D_MODEL), row),
                  pl.BlockSpec((nm, 2 * D_MODEL), lambda i: (i // nt, 0)),
                  vec, wspec, wspec, vec,
                  pl.BlockSpec((2, _ROUTER_ROWS, D_MODEL), lambda i: (0, 0, 0)),
                  pl.BlockSpec((_ROUTER_ROWS, 1), const)],
        out_specs=[pl.BlockSpec((tm, D_MODEL), row),
                   pl.BlockSpec((tm, D_MODEL // 2), row),
                   pl.BlockSpec((SUBLANES, tm), lambda i: (0, i))],
        out_shape=[jax.ShapeDtypeStruct((n, D_MODEL), F32),
                   jax.ShapeDtypeStruct((n, D_MODEL // 2), jnp.uint32),
                   jax.ShapeDtypeStruct((SUBLANES, n), F32)],
        compiler_params=_params("parallel"),
        name="cross_attention",
    )(x1, kv, gx, w_xq, w_xo, gf, w_router, b_router)


def _expert_kernel(blk_ref, xb_ref, w1_ref, w3_ref, w2_ref, o_ref):
    n_used = blk_ref[pl.num_programs(0)]

    @pl.when(pl.program_id(0) < n_used)
    def _():
        xb = _unpack_pairs(xb_ref[...]).astype(MXU_DTYPE)
        a = _dot(xb, w1_ref[...])
        hmid = a * jax.nn.sigmoid(a) * _dot(xb, w3_ref[...])
        o_ref[...] = _dot(hmid, w2_ref[...])

    @pl.when(pl.program_id(0) >= n_used)
    def _():
        o_ref[...] = jnp.zeros_like(o_ref)


def _experts(blk, xb, w1, w3, w2):
    cap = xb.shape[0]
    nblk = cap // MOE_BLOCK
    row = lambda i, e: (i, 0)
    by_expert = lambda i, e: (e[i], 0, 0)
    grid_spec = pltpu.PrefetchScalarGridSpec(
        num_scalar_prefetch=1,
        grid=(nblk,),
        in_specs=[pl.BlockSpec((MOE_BLOCK, D_MODEL // 2), row),
                  pl.BlockSpec((None, D_MODEL, D_EXPERT), by_expert),
                  pl.BlockSpec((None, D_MODEL, D_EXPERT), by_expert),
                  pl.BlockSpec((None, D_EXPERT, D_MODEL), by_expert)],
        out_specs=pl.BlockSpec((MOE_BLOCK, D_MODEL), row),
    )
    return pl.pallas_call(
        _expert_kernel,
        grid_spec=grid_spec,
        out_shape=jax.ShapeDtypeStruct((cap, D_MODEL), F32),
        compiler_params=_params("arbitrary"),
        name="experts",
    )(blk, xb, w1, w3, w2)


def _final_kernel(x_ref, w_ref, g_ref, *refs):
    o_ref = refs[-1]
    tiles_per_split = pl.num_programs(0) // MOE_SPLITS
    for s in range(MOE_SPLITS):
        @pl.when(pl.program_id(0) // tiles_per_split == s)
        def _(s=s):
            moe = w_ref[:, 0:1] * refs[2 * s][...] + w_ref[:, 1:2] * refs[2 * s + 1][...]
            o_ref[...] = _rms(x_ref[...] + moe, g_ref[...])


def _final(x2, wts, g, ys):
    n = x2.shape[0]
    tm = ROW_TM
    tiles_per_split = n // tm // MOE_SPLITS
    row = lambda i: (i, 0)
    wide = pl.BlockSpec((tm, D_MODEL), row)

    def split_spec(s):
        return pl.BlockSpec((tm, D_MODEL), lambda i: (jnp.clip(i - s * tiles_per_split, 0, tiles_per_split - 1), 0))

    return pl.pallas_call(
        _final_kernel,
        grid=(n // tm,),
        in_specs=[wide, pl.BlockSpec((tm, EXP_TOPK), row), pl.BlockSpec((1, D_MODEL), lambda i: (0, 0))] + [
            split_spec(s) for s in range(MOE_SPLITS) for _ in range(EXP_TOPK)],
        out_specs=wide,
        out_shape=jax.ShapeDtypeStruct((n, D_MODEL), F32),
        compiler_params=_params("parallel"),
        name="final_norm",
    )(x2, wts, g, *[y for pair in ys for y in pair])


def _route(eid, tok0):
    n_tok = eid.shape[1]
    eid = eid.reshape(-1)
    n_asg = eid.shape[0]
    iota = jnp.arange(n_asg, dtype=jnp.int32)
    se, order = lax.sort_key_val(eid, iota)
    counts = jnp.sum((jnp.arange(N_EXPERTS, dtype=jnp.int32)[:, None] == eid[None, :]).astype(jnp.int32), axis=1)
    padded = (counts + MOE_BLOCK - 1) // MOE_BLOCK * MOE_BLOCK
    starts = jnp.cumsum(counts) - counts
    pends = jnp.cumsum(padded)
    pstarts = pends - padded
    cap = ((n_asg + MOE_BLOCK - 1) // MOE_BLOCK + N_EXPERTS) * MOE_BLOCK
    nblk = cap // MOE_BLOCK
    blk_e = jnp.minimum(jnp.searchsorted(pends, jnp.arange(nblk) * MOE_BLOCK, side='right', method='compare_all'),
                        N_EXPERTS - 1).astype(jnp.int32)
    per_row = lambda a: jnp.repeat(a[blk_e], MOE_BLOCK)
    row = jnp.arange(cap, dtype=jnp.int32)
    off = row - per_row(pstarts)
    asg = order[jnp.clip(per_row(starts) + off, 0, n_asg - 1)]
    buf_tok = tok0 + jnp.where(off < per_row(counts), asg % n_tok, row % n_tok)
    dest_sorted = iota + (pstarts - starts)[se]
    _, pos = lax.sort_key_val(order, dest_sorted)
    n_used = (pends[-1] // MOE_BLOCK).astype(jnp.int32)
    return buf_tok, jnp.concatenate([blk_e, n_used[None]]), pos.reshape(EXP_TOPK, n_tok)


def kernel(x, mem, norm_mix_g, w_in, w_ret_o, w_nsa_o, w_out, cmp_pe_k, cmp_w1_k, cmp_w2_k, cmp_pe_v,
           cmp_w1_v, cmp_w2_v, norm_x_g, norm_mem_g, w_xq, w_xkv, w_xo, norm_ffn_g, w_grp, b_grp, w_rt,
           b_rt, w_e1, w_e3, w_e2, norm_f_g):
    bsz, seq, _ = x.shape
    n = bsz * seq
    assert seq % PROJ_TM == 0 and seq % (2 * ATT_TQ) == 0 and w_in.shape[0] == 1
    assert n % (ROW_TM * MOE_SPLITS) == 0
    cast = lambda a: a.astype(MXU_DTYPE)
    xc = x.reshape(n, D_MODEL)
    l = 0

    w_main, w_t, w_gates = _split_w_in(w_in[l])
    qk, rv, rg, ckv, sk, wk, ngl, qt, svt, wvt = _proj(
        xc, norm_mix_g[l][None, :], w_main, w_t, _rope_tables(seq), bsz, seq)
    y_ret = _retention(qk, rv, rg, bsz, seq)
    w2 = jnp.stack([cmp_w2_k[l], cmp_w2_v[l]])
    cmp_k, cmp_vt = _compress(ckv, jnp.stack([cmp_pe_k[l], cmp_pe_v[l]]),
                              cast(jnp.stack([cmp_w1_k[l], cmp_w1_v[l]])),
                              cast(w2), cast(w2.transpose(0, 2, 1)), bsz, seq)
    o_cmp, selt = _cmp_attention(qt, cmp_k, cmp_vt, bsz, seq)
    o_sel = _flash_attention(qt, sk, svt, selt, bsz, seq)
    o_win = _flash_attention(qt, wk, wvt, None, bsz, seq)
    x1 = _merge(xc, y_ret, o_cmp, o_sel, o_win, ngl, norm_mix_g[l][None, :], w_gates,
                cast(w_ret_o[l]), cast(w_nsa_o[l]), cast(w_out[l]))

    kv = _mem_kv(mem.reshape(-1, D_MODEL), norm_mem_g[l][None, :], cast(w_xkv[l]), bsz)
    gap = _ROUTER_E0 - N_EGROUPS
    w_router = jnp.concatenate([w_grp[l].T, jnp.zeros((gap, D_MODEL), F32), w_rt[l].T], axis=0)
    wr_hi = cast(w_router)
    wr_mid = cast(w_router - wr_hi.astype(F32))
    b_router = jnp.concatenate([b_grp[l], jnp.zeros((gap,), F32), b_rt[l]])[:, None]
    x2, hf, routed = _cross(x1, kv, norm_x_g[l][None, :], cast(w_xq[l]), cast(w_xo[l]),
                            norm_ffn_g[l][None, :], jnp.stack([wr_hi, wr_mid]), b_router, bsz, seq)

    eid = routed[0:EXP_TOPK].astype(jnp.int32)
    wts = routed[EXP_TOPK:2 * EXP_TOPK].T
    per_split = n // MOE_SPLITS
    ys = []
    for s in range(MOE_SPLITS):
        buf_tok, blk, pos = _route(eid[:, s * per_split:(s + 1) * per_split], s * per_split)
        y = _experts(blk, hf[buf_tok], w_e1[l], w_e3[l], w_e2[l])
        ys.append([y[pos[j]] for j in range(EXP_TOPK)])
    out = _final(x2, wts, norm_f_g[None, :], ys)
    return out.reshape(bsz, seq, D_MODEL)
```

```python
import functools

import numpy as np
import jax
import jax.numpy as jnp
from jax import lax
from jax.experimental import pallas as pl
from jax.experimental.pallas import tpu as pltpu

MXU_DTYPE = jnp.bfloat16
F32 = jnp.float32

D_MODEL = 1024
N_MEM = 256
EPS = 1e-6
NEG_INF = -1e30
SEL_FORCE = 1e4

R_HEADS = 4
R_DK = 128
R_DV = 256
R_CHUNK = 128
ROPE_BASE = 10000.0

NSA_HEADS = 8
NSA_GROUPS = 2
NSA_HPG = NSA_HEADS // NSA_GROUPS
NSA_DH = 128
CMP_LEN = 32
CMP_STRIDE = 16
SEL_LEN = 64
SEL_TOPK = 16
WINDOW = 512

X_HEADS = 4
X_DH = D_MODEL // X_HEADS

N_EGROUPS = 4
EXP_PER_GROUP = 8
N_EXPERTS = N_EGROUPS * EXP_PER_GROUP
EXP_TOPK = 2
D_EXPERT = 512
MOE_BLOCK = 256

RET_QK = R_HEADS * R_DK
RET_V = R_HEADS * R_DV
NSA_Q = NSA_HEADS * NSA_DH
NSA_KV = NSA_GROUPS * NSA_DH
SPLITS = (RET_QK, RET_QK, RET_V, RET_V, NSA_Q, NSA_KV, NSA_KV, NSA_KV, NSA_KV, NSA_KV, NSA_KV,
          3 * NSA_HEADS, D_MODEL, D_MODEL)

_EXP2_SCALE = (NSA_DH ** -0.5) * float(np.log2(np.e))

LANES = 128
SUBLANES = 8
VMEM_LIMIT = 56 * 1024 * 1024

PROJ_TM = 512
RET_ROWS = 512
ATT_TQ = 256
ATT_TK = 256
ROW_TM = 512
MOE_SPLITS = 2
VT_ROWS = NSA_DH + 2 * SUBLANES
SEL_KW = 2 * NSA_DH
assert ATT_TQ == ATT_TK and SEL_LEN * 2 * SUBLANES >= ATT_TK and WINDOW == 2 * ATT_TK


def _params(*sem):
    return pltpu.CompilerParams(dimension_semantics=sem, vmem_limit_bytes=VMEM_LIMIT)


def _dot(a, b):
    return jnp.dot(a.astype(MXU_DTYPE), b.astype(MXU_DTYPE), preferred_element_type=F32)


def _dot_nt(a, b):
    return lax.dot_general(a.astype(MXU_DTYPE), b.astype(MXU_DTYPE), (((1,), (1,)), ((), ())),
                           preferred_element_type=F32)


def _dot_tn(a, b):
    return lax.dot_general(a.astype(MXU_DTYPE), b.astype(MXU_DTYPE), (((0,), (0,)), ((), ())),
                           preferred_element_type=F32)


def _split3(p):
    hi = p.astype(MXU_DTYPE)
    r1 = p - hi.astype(F32)
    mid = r1.astype(MXU_DTYPE)
    lo = (r1 - mid.astype(F32)).astype(MXU_DTYPE)
    return hi, mid, lo


def _rms(x, g):
    return x * lax.rsqrt(jnp.mean(x * x, axis=-1, keepdims=True) + EPS) * g


_C_RQK = 0
_C_RV = _C_RQK + 2 * RET_QK
_C_RG = _C_RV + RET_V
_C_CKV = _C_RG + RET_V
_C_SK = _C_CKV + 2 * NSA_KV
_C_WK = _C_SK + NSA_KV
_C_NGL = _C_WK + NSA_KV
_C_END = _C_NGL + LANES
_R_NQ = 0
_R_SV = _R_NQ + NSA_Q
_R_WV = _R_SV + NSA_KV
_R_END = _R_WV + NSA_KV


def _proj_kernel(x_ref, g_ref, w_ref, wt_ref, cos_ref, sin_up_ref, sin_dn_ref,
                 qk_ref, rv_ref, rg_ref, ckv_ref, sk_ref, wk_ref, ngl_ref, nqt_ref, svt_ref, wvt_ref):
    hb = _rms(x_ref[...], g_ref[...]).astype(MXU_DTYPE)
    tm = hb.shape[0]

    def mm(off, width):
        return jnp.dot(hb, w_ref[:, off:off + width], preferred_element_type=F32)

    def mm_t(off, height):
        return _dot_nt(wt_ref[off:off + height, :], hb)

    cos = cos_ref[...]
    sin_up = sin_up_ref[...]
    sin_dn = sin_dn_ref[...]
    qk = mm(_C_RQK, 2 * RET_QK)
    for i in range(2 * R_HEADS):
        t = qk[:, i * R_DK:(i + 1) * R_DK]
        r = t * cos + pltpu.roll(t, R_DK - 1, axis=1) * sin_up + pltpu.roll(t, 1, axis=1) * sin_dn
        if i >= R_HEADS:
            r = r * (R_DK ** -0.5)
        qk_ref[:, i * R_DK:(i + 1) * R_DK] = r.astype(qk_ref.dtype)
    rv_ref[...] = mm(_C_RV, RET_V).astype(rv_ref.dtype)
    rg_ref[...] = mm(_C_RG, RET_V)
    ckv_ref[...] = mm(_C_CKV, 2 * NSA_KV)
    sk = mm(_C_SK, NSA_KV).astype(sk_ref.dtype)
    blk = (lax.broadcasted_iota(jnp.int32, (tm, NSA_DH), 0) % ATT_TK) // SEL_LEN
    onehot = jnp.where(lax.broadcasted_iota(jnp.int32, (tm, NSA_DH), 1) == blk, 1.0, 0.0).astype(sk_ref.dtype)
    for g in range(NSA_GROUPS):
        sk_ref[:, g * SEL_KW:g * SEL_KW + NSA_DH] = sk[:, g * NSA_DH:(g + 1) * NSA_DH]
        sk_ref[:, g * SEL_KW + NSA_DH:(g + 1) * SEL_KW] = onehot
    wk_ref[...] = mm(_C_WK, NSA_KV).astype(wk_ref.dtype)
    ngl_ref[...] = mm(_C_NGL, LANES)

    nqt = (mm_t(_R_NQ, NSA_Q) * _EXP2_SCALE).astype(nqt_ref.dtype)
    for g in range(NSA_GROUPS):
        for j in range(tm // ATT_TQ):
            for hh in range(NSA_HPG):
                head = g * NSA_HPG + hh
                nqt_ref[g, j, :, hh * ATT_TQ:(hh + 1) * ATT_TQ] = (
                    nqt[head * NSA_DH:(head + 1) * NSA_DH, j * ATT_TQ:(j + 1) * ATT_TQ])
    for off, out_ref in ((_R_SV, svt_ref), (_R_WV, wvt_ref)):
        vt = mm_t(off, NSA_KV).astype(out_ref.dtype)
        for g in range(NSA_GROUPS):
            for j in range(tm // ATT_TK):
                out_ref[g, j, 0:NSA_DH] = vt[g * NSA_DH:(g + 1) * NSA_DH, j * ATT_TK:(j + 1) * ATT_TK]
                out_ref[g, j, NSA_DH:VT_ROWS] = jnp.ones((VT_ROWS - NSA_DH, ATT_TK), out_ref.dtype)


def _split_w_in(w_in):
    offs = np.cumsum((0,) + SPLITS)
    rq, rk, rv, rg, nq, ck, cv, sk, sv, wk, wv, ngl, ga, gb = [
        w_in[:, offs[i]:offs[i + 1]] for i in range(len(SPLITS))]
    ngl = jnp.pad(ngl, ((0, 0), (0, LANES - ngl.shape[1])))
    w = jnp.concatenate([rq, rk, rv, rg, ck, cv, sk, wk, ngl], axis=1).astype(MXU_DTYPE)
    wt = jnp.concatenate([nq, sv, wv], axis=1).T.astype(MXU_DTYPE)
    w_gates = jnp.concatenate([ga, gb], axis=1).astype(MXU_DTYPE)
    return w, wt, w_gates


def _rope_tables(seq):
    pos = jnp.arange(seq, dtype=F32)
    inv_freq = ROPE_BASE ** (-jnp.arange(0, R_DK, 2, dtype=F32) / R_DK)
    ang = pos[:, None] * inv_freq[None, :]
    cos = jnp.repeat(jnp.cos(ang), 2, axis=1)
    sin = jnp.repeat(jnp.sin(ang), 2, axis=1)
    even = (jnp.arange(R_DK) % 2 == 0)[None, :]
    return cos, jnp.where(even, -sin, 0.0), jnp.where(even, 0.0, sin)


def _proj(x2d, g, w, wt, tables, bsz, seq):
    n = x2d.shape[0]
    tm = PROJ_TM
    nt = seq // tm
    row = lambda i: (i, 0)
    const = lambda i: (0, 0)
    tile = lambda i: (i // nt, 0, i % nt, 0, 0)
    table = pl.BlockSpec((tm, R_DK), lambda i: (i % nt, 0))
    widths = (2 * RET_QK, RET_V, RET_V, 2 * NSA_KV, NSA_GROUPS * SEL_KW, NSA_KV, LANES)
    dtypes = (MXU_DTYPE, MXU_DTYPE, F32, F32, MXU_DTYPE, MXU_DTYPE, F32)
    width = NSA_HPG * ATT_TQ
    vt_shape = jax.ShapeDtypeStruct((bsz, NSA_GROUPS, seq // ATT_TK, VT_ROWS, ATT_TK), MXU_DTYPE)
    vt_spec = pl.BlockSpec((None, NSA_GROUPS, tm // ATT_TK, VT_ROWS, ATT_TK), tile)
    return pl.pallas_call(
        _proj_kernel,
        grid=(n // tm,),
        in_specs=[pl.BlockSpec((tm, D_MODEL), row),
                  pl.BlockSpec((1, D_MODEL), const),
                  pl.BlockSpec((D_MODEL, _C_END), const, pipeline_mode=pl.Buffered(1)),
                  pl.BlockSpec((_R_END, D_MODEL), const, pipeline_mode=pl.Buffered(1)),
                  table, table, table],
        out_specs=[pl.BlockSpec((tm, wd), row) for wd in widths] + [
            pl.BlockSpec((None, NSA_GROUPS, tm // ATT_TQ, NSA_DH, width), tile), vt_spec, vt_spec],
        out_shape=[jax.ShapeDtypeStruct((n, wd), dt) for wd, dt in zip(widths, dtypes)] + [
            jax.ShapeDtypeStruct((bsz, NSA_GROUPS, seq // ATT_TQ, NSA_DH, width), MXU_DTYPE), vt_shape, vt_shape],
        compiler_params=_params("parallel"),
        name="proj",
    )(x2d, g, w, wt, *tables)


def _ret_kernel(qk_ref, v_ref, g_ref, decay_ref, zeta_ref, xi_ref, cd_ref, y_ref, state_ref):
    @pl.when(pl.program_id(1) == 0)
    def _():
        state_ref[...] = jnp.zeros_like(state_ref)

    for c in range(RET_ROWS // R_CHUNK):
        rows = slice(c * R_CHUNK, (c + 1) * R_CHUNK)
        for h in range(R_HEADS):
            q = qk_ref[rows, h * R_DK:(h + 1) * R_DK]
            k = qk_ref[rows, RET_QK + h * R_DK:RET_QK + (h + 1) * R_DK]
            v = v_ref[rows, h * R_DV:(h + 1) * R_DV]
            st = state_ref[h]
            scores = _dot_nt(q, k) * decay_ref[h]
            o = _dot(scores, v) + _dot(q, st) * xi_ref[h]
            o = o * lax.rsqrt(jnp.mean(o * o, axis=-1, keepdims=True) + EPS)
            g = g_ref[rows, h * R_DV:(h + 1) * R_DV]
            y_ref[rows, h * R_DV:(h + 1) * R_DV] = (g * jax.nn.sigmoid(g) * o).astype(y_ref.dtype)
            kz = k.astype(F32) * zeta_ref[h]
            state_ref[h] = st * cd_ref[h] + _dot_tn(kz, v)


def _retention(qk, rv, rg, bsz, seq):
    n = qk.shape[0]
    nt = seq // RET_ROWS
    log_g = jnp.log1p(-jnp.exp2(-5.0 - jnp.arange(R_HEADS, dtype=F32)))
    idx = jnp.arange(R_CHUNK, dtype=F32)
    diff = idx[:, None] - idx[None, :]
    decay = jnp.where(diff >= 0, jnp.exp(log_g[:, None, None] * jnp.maximum(diff, 0.0)), 0.0)
    zeta = jnp.exp(log_g[:, None] * (R_CHUNK - 1.0 - idx)[None, :])[:, :, None]
    xi = jnp.exp(log_g[:, None] * (idx + 1.0)[None, :])[:, :, None]
    cd = jnp.exp(log_g * R_CHUNK)[:, None, None]
    row = lambda b, i: (b * nt + i, 0)
    const3 = lambda b, i: (0, 0, 0)
    return pl.pallas_call(
        _ret_kernel,
        grid=(bsz, nt),
        in_specs=[pl.BlockSpec((RET_ROWS, 2 * RET_QK), row),
                  pl.BlockSpec((RET_ROWS, RET_V), row),
                  pl.BlockSpec((RET_ROWS, RET_V), row),
                  pl.BlockSpec((R_HEADS, R_CHUNK, R_CHUNK), const3),
                  pl.BlockSpec((R_HEADS, R_CHUNK, 1), const3),
                  pl.BlockSpec((R_HEADS, R_CHUNK, 1), const3),
                  pl.BlockSpec((R_HEADS, 1, 1), const3)],
        out_specs=pl.BlockSpec((RET_ROWS, RET_V), row),
        out_shape=jax.ShapeDtypeStruct((n, RET_V), MXU_DTYPE),
        scratch_shapes=[pltpu.VMEM((R_HEADS, R_DK, R_DV), F32)],
        compiler_params=_params("parallel", "arbitrary"),
        name="retention",
    )(qk, rv, rg, decay, zeta, xi, cd)


def _compress_kernel(x_ref, pe_ref, w1_ref, w2_ref, w2t_ref, o_ref, ot_ref, buf_ref, *, seq):
    ncp = seq // CMP_STRIDE
    buf_ref[0:seq, :] = x_ref[...]
    buf_ref[seq:seq + LANES, :] = jnp.zeros((LANES, NSA_DH), F32)
    acc = jnp.zeros((ncp, NSA_DH), F32)
    for l in range(CMP_LEN):
        xl = buf_ref[pl.ds(l, ncp, stride=CMP_STRIDE), :] + pe_ref[l:l + 1, :]
        acc = acc + _dot(xl, w1_ref[l])
    hid = jax.nn.gelu(acc)
    o_ref[...] = _dot(hid, w2_ref[...]).astype(o_ref.dtype)
    ot_ref[...] = _dot_nt(w2t_ref[...], hid).astype(ot_ref.dtype)


def _compress(ckv, pe, w1, w2, w2t, bsz, seq):
    ncp = seq // CMP_STRIDE
    nj = 2 * NSA_GROUPS
    wsel = lambda b, j: (j // NSA_GROUPS, 0, 0)
    return pl.pallas_call(
        functools.partial(_compress_kernel, seq=seq),
        grid=(bsz, nj),
        in_specs=[pl.BlockSpec((seq, NSA_DH), lambda b, j: (b, j)),
                  pl.BlockSpec((None, CMP_LEN, NSA_DH), wsel),
                  pl.BlockSpec((None, CMP_LEN, NSA_DH, NSA_DH), lambda b, j: (j // NSA_GROUPS, 0, 0, 0)),
                  pl.BlockSpec((None, NSA_DH, NSA_DH), wsel),
                  pl.BlockSpec((None, NSA_DH, NSA_DH), wsel)],
        out_specs=[pl.BlockSpec((None, None, ncp, NSA_DH), lambda b, j: (b, j, 0, 0)),
                   pl.BlockSpec((None, None, NSA_DH, ncp), lambda b, j: (b, j, 0, 0))],
        out_shape=[jax.ShapeDtypeStruct((bsz, nj, ncp, NSA_DH), MXU_DTYPE),
                   jax.ShapeDtypeStruct((bsz, nj, NSA_DH, ncp), MXU_DTYPE)],
        scratch_shapes=[pltpu.VMEM((seq + LANES, NSA_DH), F32)],
        compiler_params=_params("parallel", "parallel"),
        name="compress",
    )(ckv, pe, w1, w2, w2t)


def _cmp_attn_kernel(qt_ref, k_ref, vt_ref, ov_ref, o_ref, selt_ref, p_ref, *, ncp, nb):
    tq = ATT_TQ
    qi = pl.program_id(2)
    t0 = qi * tq
    any_valid = (t0 + lax.broadcasted_iota(jnp.int32, (1, tq), 1)) >= CMP_LEN - 1
    any_valid = jnp.concatenate([any_valid] * NSA_HPG, axis=1)

    def probabilities(rows):
        t = t0 + lax.broadcasted_iota(jnp.int32, (rows, tq), 1)
        n = lax.broadcasted_iota(jnp.int32, (rows, tq), 0)
        valid = (n * CMP_STRIDE + (CMP_LEN - 1) <= t) & (n < ncp - 1)
        valid = jnp.concatenate([valid] * NSA_HPG, axis=1)
        s = jnp.where(valid, _dot(k_ref[0:rows, :], qt_ref[...]), NEG_INF)
        e = jnp.exp2(s - jnp.max(s, axis=0, keepdims=True))
        p_ref[0:rows, :] = e * jnp.where(any_valid, 1.0 / jnp.sum(e, axis=0, keepdims=True), 0.0)
        if rows < ncp:
            p_ref[rows:, :] = jnp.zeros((ncp - rows, NSA_HPG * tq), F32)

    n_buckets = 4
    per_bucket = ncp // n_buckets
    visible = (t0 + tq - CMP_LEN) // CMP_STRIDE + 1
    bucket = jnp.minimum((visible + per_bucket - 1) // per_bucket, n_buckets)
    for c in range(1, n_buckets + 1):
        pl.when(bucket == c)(functools.partial(probabilities, c * per_bucket))

    p = p_ref[...]
    ot = _dot(vt_ref[...], p)
    for h in range(NSA_HPG):
        o_ref[:, h * NSA_DH:(h + 1) * NSA_DH] = ot[:, h * tq:(h + 1) * tq].T
    psum = sum(p[:, h * tq:(h + 1) * tq] for h in range(NSA_HPG))

    imp = sum(_dot(ov_ref[...], part) for part in _split3(psum))
    j = lax.broadcasted_iota(jnp.int32, (nb, tq), 0)
    tb = (t0 + lax.broadcasted_iota(jnp.int32, (nb, tq), 1)) // SEL_LEN
    forced = (j == 0) | (j == tb) | (j == tb - 1)
    imp = jnp.where(j > tb, -SEL_FORCE, jnp.where(forced, SEL_FORCE, imp))

    sub = SUBLANES
    grp = [imp[r * sub:(r + 1) * sub] for r in range(nb // sub)]
    cnt = [jnp.zeros((sub, tq), F32) for _ in grp]
    for i in range(nb):
        row = jnp.broadcast_to(imp[i:i + 1, :], (sub, tq))
        for r in range(nb // sub):
            if r * sub > i:
                beats = jnp.where(row >= grp[r], 1.0, 0.0)
            elif r * sub + sub - 1 < i:
                beats = jnp.where(row > grp[r], 1.0, 0.0)
            else:
                jr = r * sub + lax.broadcasted_iota(jnp.int32, (sub, tq), 0)
                beats = jnp.where(jr > i, jnp.where(row >= grp[r], 1.0, 0.0), jnp.where(row > grp[r], 1.0, 0.0))
            cnt[r] = cnt[r] + beats
    k_sel = min(SEL_TOPK, nb)
    for r in range(nb // sub):
        selt_ref[r * sub:(r + 1) * sub, :] = jnp.where(cnt[r] < k_sel, 1.0, 0.0)


def _cmp_attention(qt, cmp_k, cmp_vt, bsz, seq):
    n = bsz * seq
    ncp = seq // CMP_STRIDE
    nb = seq // SEL_LEN
    nt = seq // ATT_TQ
    gw = NSA_HPG * NSA_DH
    cstart = np.arange(ncp) * CMP_STRIDE
    jstart = np.arange(nb) * SEL_LEN
    ov = ((cstart[None, :] < jstart[:, None] + SEL_LEN) & (cstart[None, :] + CMP_LEN > jstart[:, None])
          & (np.arange(ncp)[None, :] < ncp - 1))
    ov = jnp.asarray(ov, MXU_DTYPE)
    return pl.pallas_call(
        functools.partial(_cmp_attn_kernel, ncp=ncp, nb=nb),
        grid=(bsz, NSA_GROUPS, nt),
        in_specs=[pl.BlockSpec((None, None, None, NSA_DH, NSA_HPG * ATT_TQ), lambda b, g, i: (b, g, i, 0, 0)),
                  pl.BlockSpec((None, None, ncp, NSA_DH), lambda b, g, i: (b, g, 0, 0)),
                  pl.BlockSpec((None, None, NSA_DH, ncp), lambda b, g, i: (b, NSA_GROUPS + g, 0, 0)),
                  pl.BlockSpec((nb, ncp), lambda b, g, i: (0, 0))],
        out_specs=[pl.BlockSpec((ATT_TQ, gw), lambda b, g, i: (b * nt + i, g)),
                   pl.BlockSpec((None, None, nb, ATT_TQ), lambda b, g, i: (b, g, 0, i))],
        out_shape=[jax.ShapeDtypeStruct((n, NSA_Q), F32),
                   jax.ShapeDtypeStruct((bsz, NSA_GROUPS, nb, seq), F32)],
        scratch_shapes=[pltpu.VMEM((ncp, NSA_HPG * ATT_TQ), F32)],
        compiler_params=_params("parallel", "parallel", "parallel"),
        name="cmp_attention",
    )(qt, cmp_k, cmp_vt, ov)


def _softmax_step(s, m, vt, acc_ref):
    m_new = jnp.maximum(m, jnp.max(s, axis=0, keepdims=True))
    p = jnp.exp2(s - m_new)
    acc_ref[...] = jnp.exp2(m - m_new) * acc_ref[...] + _dot(vt, p)
    return m_new


def _flash_finish(o_ref, acc_ref, col0=0):
    inv = 1.0 / acc_ref[NSA_DH:NSA_DH + 1, :]
    for h in range(NSA_HPG):
        cols = slice(h * ATT_TQ, (h + 1) * ATT_TQ)
        o_ref[:, col0 + h * NSA_DH:col0 + (h + 1) * NSA_DH] = (acc_ref[0:NSA_DH, cols] * inv[:, cols]).T


def _sel_attn_kernel(qt_ref, k_ref, vt_ref, selt_ref, o_ref, acc_ref, s_ref, qa_ref):
    qi = pl.program_id(1)
    width = NSA_HPG * ATT_TQ
    groups = range(NSA_GROUPS)
    blocks_per_tile = ATT_TK // SEL_LEN
    bias_rows = 2 * SUBLANES
    n_loop = qi * (ATT_TQ // ATT_TK)
    n_steps = (n_loop + 1) // 2 * 2
    tile_of = lambda step: jnp.where(step < n_loop, step, qi)
    rowid = lax.broadcasted_iota(jnp.int32, (bias_rows, ATT_TQ), 0)
    tri = (lax.broadcasted_iota(jnp.int32, (ATT_TK, ATT_TQ), 0) <=
           lax.broadcasted_iota(jnp.int32, (ATT_TK, ATT_TQ), 1))
    tri = jnp.concatenate([tri] * NSA_HPG, axis=1)

    def scores(g, step):
        kt = tile_of(step)
        threshold = jnp.where((step < n_loop) | (step == n_steps), 0.5, 2.0)
        bias = jnp.zeros((bias_rows, ATT_TQ), F32)
        for jb in range(blocks_per_tile):
            picked = selt_ref[g, pl.ds(kt * blocks_per_tile + jb, 1), :] > threshold
            bias = jnp.where(rowid == jb, jnp.where(picked, 0.0, NEG_INF), bias)
        qa_ref[g, NSA_DH:NSA_DH + bias_rows, :] = jnp.concatenate([bias] * NSA_HPG, axis=1).astype(qa_ref.dtype)
        k = k_ref[pl.ds(pl.multiple_of(kt * ATT_TK, ATT_TK), ATT_TK), g * SEL_KW:(g + 1) * SEL_KW]
        return _dot(k, qa_ref[g])

    def half_step(g, step, cur, nxt, m, diagonal=False):
        if not diagonal:
            s_ref[g, nxt] = scores(g, step + 1)
        s = jnp.where(tri, s_ref[g, cur], NEG_INF) if diagonal else s_ref[g, cur]
        return _softmax_step(s, m, vt_ref[g, tile_of(step)], acc_ref.at[g])

    def body(i, ms):
        ms = [half_step(g, 2 * i, 0, 1, ms[g]) for g in groups]
        return tuple(half_step(g, 2 * i + 1, 1, 0, ms[g]) for g in groups)

    for g in groups:
        qa_ref[g, 0:NSA_DH, :] = qt_ref[g]
        qa_ref[g, NSA_DH + bias_rows:, :] = jnp.zeros((SEL_KW - NSA_DH - bias_rows, width), qa_ref.dtype)
        acc_ref[g] = jnp.zeros(acc_ref.shape[1:], F32)
        s_ref[g, 0] = scores(g, 0)
    ms = lax.fori_loop(0, n_steps // 2, body, tuple(jnp.full((1, width), NEG_INF, F32) for _ in groups))
    for g in groups:
        half_step(g, n_steps, 0, 1, ms[g], diagonal=True)
        _flash_finish(o_ref, acc_ref.at[g], g * NSA_HPG * NSA_DH)


def _selected_attention(qt, k, vt, selt, bsz, seq):
    n = bsz * seq
    nt = seq // ATT_TQ
    nkt = seq // ATT_TK
    nb = seq // SEL_LEN
    width = NSA_HPG * ATT_TQ
    return pl.pallas_call(
        _sel_attn_kernel,
        grid=(bsz, nt),
        in_specs=[pl.BlockSpec((None, NSA_GROUPS, None, NSA_DH, width), lambda b, i: (b, 0, i, 0, 0)),
                  pl.BlockSpec((seq, NSA_GROUPS * SEL_KW), lambda b, i: (b, 0)),
                  pl.BlockSpec((None, NSA_GROUPS, nkt, VT_ROWS, ATT_TK), lambda b, i: (b, 0, 0, 0, 0)),
                  pl.BlockSpec((None, NSA_GROUPS, nb, ATT_TQ), lambda b, i: (b, 0, 0, i))],
        out_specs=pl.BlockSpec((ATT_TQ, NSA_Q), lambda b, i: (b * nt + i, 0)),
        out_shape=jax.ShapeDtypeStruct((n, NSA_Q), F32),
        scratch_shapes=[pltpu.VMEM((NSA_GROUPS, VT_ROWS, width), F32),
                        pltpu.VMEM((NSA_GROUPS, 2, ATT_TK, width), F32),
                        pltpu.VMEM((NSA_GROUPS, SEL_KW, width), MXU_DTYPE)],
        compiler_params=_params("parallel", "parallel"),
        name="selected_attention",
    )(qt, k, vt, selt)


def _win_attn_kernel(qt_ref, k_ref, vt_ref, o_ref, acc_ref, s0_ref, s1_ref, m_ref):
    qi = pl.program_id(1)
    groups = range(NSA_GROUPS)
    row = lax.broadcasted_iota(jnp.int32, (ATT_TK, ATT_TQ), 0)
    col = lax.broadcasted_iota(jnp.int32, (ATT_TK, ATT_TQ), 1)
    causal = jnp.concatenate([row <= col] * NSA_HPG, axis=1)
    window_tail = jnp.concatenate([row > col] * NSA_HPG, axis=1)

    def scores(g, kt):
        k = k_ref[pl.ds(pl.multiple_of(kt * ATT_TK, ATT_TK), ATT_TK), g * NSA_DH:(g + 1) * NSA_DH]
        return _dot(k, qt_ref[g])

    for g in groups:
        acc_ref[g] = jnp.zeros(acc_ref.shape[1:], F32)
        s0_ref[g] = scores(g, qi)
    for g in groups:
        s1_ref[g] = scores(g, jnp.maximum(qi - 1, 0))
        m_ref[g] = _softmax_step(jnp.where(causal, s0_ref[g], NEG_INF), jnp.full(m_ref.shape[1:], NEG_INF, F32),
                                 vt_ref[g, qi], acc_ref.at[g])

    @pl.when(qi >= 1)
    def _():
        for g in groups:
            s0_ref[g] = scores(g, jnp.maximum(qi - 2, 0))
            m_ref[g] = _softmax_step(s1_ref[g], m_ref[g], vt_ref[g, qi - 1], acc_ref.at[g])

    @pl.when(qi >= 2)
    def _():
        for g in groups:
            _softmax_step(jnp.where(window_tail, s0_ref[g], NEG_INF), m_ref[g], vt_ref[g, qi - 2], acc_ref.at[g])

    for g in groups:
        _flash_finish(o_ref, acc_ref.at[g], g * NSA_HPG * NSA_DH)


def _window_attention(qt, k, vt, bsz, seq):
    n = bsz * seq
    nt = seq // ATT_TQ
    nkt = seq // ATT_TK
    width = NSA_HPG * ATT_TQ
    return pl.pallas_call(
        _win_attn_kernel,
        grid=(bsz, nt),
        in_specs=[pl.BlockSpec((None, NSA_GROUPS, None, NSA_DH, width), lambda b, i: (b, 0, i, 0, 0)),
                  pl.BlockSpec((seq, NSA_KV), lambda b, i: (b, 0)),
                  pl.BlockSpec((None, NSA_GROUPS, nkt, VT_ROWS, ATT_TK), lambda b, i: (b, 0, 0, 0, 0))],
        out_specs=pl.BlockSpec((ATT_TQ, NSA_Q), lambda b, i: (b * nt + i, 0)),
        out_shape=jax.ShapeDtypeStruct((n, NSA_Q), F32),
        scratch_shapes=[pltpu.VMEM((NSA_GROUPS, VT_ROWS, width), F32),
                        pltpu.VMEM((NSA_GROUPS, ATT_TK, width), F32),
                        pltpu.VMEM((NSA_GROUPS, ATT_TK, width), F32),
                        pltpu.VMEM((NSA_GROUPS, 1, width), F32)],
        compiler_params=_params("parallel", "parallel"),
        name="window_attention",
    )(qt, k, vt)


def _merge_kernel(x_ref, yr_ref, oc_ref, os_ref, ow_ref, ngl_ref, g_ref, wg_ref, wr_ref, wn_ref, wo_ref, o_ref):
    tm = x_ref.shape[0]
    hb = _rms(x_ref[...], g_ref[...]).astype(MXU_DTYPE)
    ga = jnp.dot(hb, wg_ref[:, :D_MODEL], preferred_element_type=F32)
    gb = jnp.dot(hb, wg_ref[:, D_MODEL:], preferred_element_type=F32)
    gates = jax.nn.sigmoid(ngl_ref[...])
    parts = []
    for h in range(NSA_HEADS):
        cols = slice(h * NSA_DH, (h + 1) * NSA_DH)

        def gate(br):
            return jnp.broadcast_to(gates[:, 3 * h + br:3 * h + br + 1], (tm, NSA_DH))

        parts.append(gate(0) * oc_ref[:, cols] + gate(1) * os_ref[:, cols] + gate(2) * ow_ref[:, cols])
    o_nsa = jnp.concatenate(parts, axis=1)
    y_ret = _dot(yr_ref[...], wr_ref[...])
    y_nsa = _dot(o_nsa, wn_ref[...])
    y = jax.nn.sigmoid(ga) * y_ret + jax.nn.sigmoid(gb) * y_nsa
    o_ref[...] = x_ref[...] + _dot(y, wo_ref[...])


def _merge(x2d, y_ret, o_cmp, o_sel, o_win, ngl, g_mix, w_gates, w_ret_o, w_nsa_o, w_out):
    n = x2d.shape[0]
    tm = ROW_TM
    row = lambda i: (i, 0)
    const = lambda i: (0, 0)
    wide = pl.BlockSpec((tm, D_MODEL), row)
    wspec = pl.BlockSpec((D_MODEL, D_MODEL), const)
    return pl.pallas_call(
        _merge_kernel,
        grid=(n // tm,),
        in_specs=[wide, wide, wide, wide, wide,
                  pl.BlockSpec((tm, LANES), row),
                  pl.BlockSpec((1, D_MODEL), const),
                  pl.BlockSpec((D_MODEL, 2 * D_MODEL), const),
                  wspec, wspec, wspec],
        out_specs=wide,
        out_shape=jax.ShapeDtypeStruct((n, D_MODEL), F32),
        compiler_params=_params("parallel"),
        name="merge",
    )(x2d, y_ret, o_cmp, o_sel, o_win, ngl, g_mix, w_gates, w_ret_o, w_nsa_o, w_out)


def _mem_kv_kernel(m_ref, g_ref, w_ref, o_ref):
    o_ref[...] = _dot(_rms(m_ref[...], g_ref[...]), w_ref[...]).astype(o_ref.dtype)


def _mem_kv(mem2d, g, w_xkv, bsz):
    nm = mem2d.shape[0] // bsz
    return pl.pallas_call(
        _mem_kv_kernel,
        grid=(bsz,),
        in_specs=[pl.BlockSpec((nm, D_MODEL), lambda b: (b, 0)),
                  pl.BlockSpec((1, D_MODEL), lambda b: (0, 0)),
                  pl.BlockSpec((D_MODEL, 2 * D_MODEL), lambda b: (0, 0))],
        out_specs=pl.BlockSpec((nm, 2 * D_MODEL), lambda b: (b, 0)),
        out_shape=jax.ShapeDtypeStruct((mem2d.shape[0], 2 * D_MODEL), MXU_DTYPE),
        compiler_params=_params("parallel"),
        name="mem_kv",
    )(mem2d, g, w_xkv)


def _pack_pairs(x):
    half = x.shape[1] // 2
    hi = lax.bitcast_convert_type(x[:, :half].astype(MXU_DTYPE).astype(F32), jnp.uint32)
    lo = lax.bitcast_convert_type(x[:, half:].astype(MXU_DTYPE).astype(F32), jnp.uint32)
    return (hi & jnp.uint32(0xFFFF0000)) | (lo >> 16)


def _unpack_pairs(u):
    hi = lax.bitcast_convert_type(u & jnp.uint32(0xFFFF0000), F32)
    lo = lax.bitcast_convert_type(u << 16, F32)
    return jnp.concatenate([hi, lo], axis=1)


_ROUTER_E0 = 2 * SUBLANES
_ROUTER_ROWS = _ROUTER_E0 + N_EXPERTS


def _top2_route(lgt):
    sub = SUBLANES
    t = lgt.shape[1]
    rowid = lax.broadcasted_iota(jnp.int32, (sub, t), 0)
    first = lambda hit: jnp.min(jnp.where(hit, rowid, sub), axis=0, keepdims=True)
    lg = jnp.where(rowid < N_EGROUPS, lgt[0:sub], NEG_INF)
    gmax = jnp.max(lg, axis=0, keepdims=True)
    grp = first(lg == gmax)
    g_gate = 1.0 / jnp.sum(jnp.exp(lg - gmax), axis=0, keepdims=True)
    experts_of = lambda g: lgt[_ROUTER_E0 + g * EXP_PER_GROUP:_ROUTER_E0 + (g + 1) * EXP_PER_GROUP]
    le = experts_of(N_EGROUPS - 1)
    for g in range(N_EGROUPS - 2, -1, -1):
        le = jnp.where(grp == g, experts_of(g), le)
    ex = jnp.exp(le - jnp.max(le, axis=0, keepdims=True))
    pe = ex / jnp.sum(ex, axis=0, keepdims=True)
    p0 = jnp.max(pe, axis=0, keepdims=True)
    i0 = first(pe == p0)
    rest = jnp.where(rowid == i0, -1.0, pe)
    p1 = jnp.max(rest, axis=0, keepdims=True)
    i1 = first(rest == p1)
    den = p0 + p1
    base = grp * EXP_PER_GROUP
    return jnp.concatenate([(base + i0).astype(F32), (base + i1).astype(F32),
                            g_gate * p0 / den, g_gate * p1 / den], axis=0)


def _cross_kernel(x_ref, kv_ref, gx_ref, wq_ref, wo_ref, gf_ref, wr_ref, br_ref, x2_ref, hf_ref, rt_ref):
    x = x_ref[...]
    q = _dot(_rms(x, gx_ref[...]), wq_ref[...])
    heads = []
    for h in range(X_HEADS):
        k = kv_ref[:, h * X_DH:(h + 1) * X_DH]
        v = kv_ref[:, D_MODEL + h * X_DH:D_MODEL + (h + 1) * X_DH]
        s = _dot_nt(q[:, h * X_DH:(h + 1) * X_DH], k) * (X_DH ** -0.5)
        e = jnp.exp(s - jnp.max(s, axis=-1, keepdims=True))
        p = e / jnp.sum(e, axis=-1, keepdims=True)
        heads.append(_dot(p, v))
    x2 = x + _dot(jnp.concatenate(heads, axis=1), wo_ref[...])
    x2_ref[...] = x2
    hf = _rms(x2, gf_ref[...])
    hf_ref[...] = _pack_pairs(hf)
    h_hi, h_mid, _ = _split3(hf)
    w_hi = wr_ref[0]
    w_mid = wr_ref[1]
    lgt = (_dot_nt(w_hi, h_hi) + (_dot_nt(w_hi, h_mid) + _dot_nt(w_mid, h_hi))) + br_ref[...]
    rt_ref[...] = jnp.concatenate([_top2_route(lgt), jnp.zeros((SUBLANES - 4, lgt.shape[1]), F32)], axis=0)


def _cross(x1, kv, gx, w_xq, w_xo, gf, w_router, b_router, bsz, seq):
    n = x1.shape[0]
    tm = ROW_TM
    nt = seq // tm
    nm = kv.shape[0] // bsz
    row = lambda i: (i, 0)
    const = lambda i: (0, 0)
    vec = pl.BlockSpec((1, D_MODEL), const)
    wspec = pl.BlockSpec((D_MODEL, D_MODEL), const)
    return pl.pallas_call(
        _cross_kernel,
        grid=(n // tm,),
        in_specs=[pl.BlockSpec((tm, D_MODEL), row),
                  pl.BlockSpec((nm, 2 * D_MODEL), lambda i: (i // nt, 0)),
                  vec, wspec, wspec, vec,
                  pl.BlockSpec((2, _ROUTER_ROWS, D_MODEL), lambda i: (0, 0, 0)),
                  pl.BlockSpec((_ROUTER_ROWS, 1), const)],
        out_specs=[pl.BlockSpec((tm, D_MODEL), row),
                   pl.BlockSpec((tm, D_MODEL // 2), row),
                   pl.BlockSpec((SUBLANES, tm), lambda i: (0, i))],
        out_shape=[jax.ShapeDtypeStruct((n, D_MODEL), F32),
                   jax.ShapeDtypeStruct((n, D_MODEL // 2), jnp.uint32),
                   jax.ShapeDtypeStruct((SUBLANES, n), F32)],
        compiler_params=_params("parallel"),
        name="cross_attention",
    )(x1, kv, gx, w_xq, w_xo, gf, w_router, b_router)


def _expert_kernel(blk_ref, xb_ref, w1_ref, w3_ref, w2_ref, o_ref):
    n_used = blk_ref[pl.num_programs(0)]

    @pl.when(pl.program_id(0) < n_used)
    def _():
        xb = _unpack_pairs(xb_ref[...]).astype(MXU_DTYPE)
        a = _dot(xb, w1_ref[...])
        hmid = a * jax.nn.sigmoid(a) * _dot(xb, w3_ref[...])
        o_ref[...] = _dot(hmid, w2_ref[...])

    @pl.when(pl.program_id(0) >= n_used)
    def _():
        o_ref[...] = jnp.zeros_like(o_ref)


def _experts(blk, xb, w1, w3, w2):
    cap = xb.shape[0]
    nblk = cap // MOE_BLOCK
    row = lambda i, e: (i, 0)
    by_expert = lambda i, e: (e[i], 0, 0)
    grid_spec = pltpu.PrefetchScalarGridSpec(
        num_scalar_prefetch=1,
        grid=(nblk,),
        in_specs=[pl.BlockSpec((MOE_BLOCK, D_MODEL // 2), row),
                  pl.BlockSpec((None, D_MODEL, D_EXPERT), by_expert),
                  pl.BlockSpec((None, D_MODEL, D_EXPERT), by_expert),
                  pl.BlockSpec((None, D_EXPERT, D_MODEL), by_expert)],
        out_specs=pl.BlockSpec((MOE_BLOCK, D_MODEL), row),
    )
    return pl.pallas_call(
        _expert_kernel,
        grid_spec=grid_spec,
        out_shape=jax.ShapeDtypeStruct((cap, D_MODEL), F32),
        compiler_params=_params("arbitrary"),
        name="experts",
    )(blk, xb, w1, w3, w2)


def _final_kernel(x_ref, w_ref, g_ref, *refs):
    o_ref = refs[-1]
    tiles_per_split = pl.num_programs(0) // MOE_SPLITS
    for s in range(MOE_SPLITS):
        @pl.when(pl.program_id(0) // tiles_per_split == s)
        def _(s=s):
            moe = w_ref[:, 0:1] * refs[2 * s][...] + w_ref[:, 1:2] * refs[2 * s + 1][...]
            o_ref[...] = _rms(x_ref[...] + moe, g_ref[...])


def _final(x2, wts, g, ys):
    n = x2.shape[0]
    tm = ROW_TM
    tiles_per_split = n // tm // MOE_SPLITS
    row = lambda i: (i, 0)
    wide = pl.BlockSpec((tm, D_MODEL), row)

    def split_spec(s):
        return pl.BlockSpec((tm, D_MODEL), lambda i: (jnp.clip(i - s * tiles_per_split, 0, tiles_per_split - 1), 0))

    return pl.pallas_call(
        _final_kernel,
        grid=(n // tm,),
        in_specs=[wide, pl.BlockSpec((tm, EXP_TOPK), row), pl.BlockSpec((1, D_MODEL), lambda i: (0, 0))] + [
            split_spec(s) for s in range(MOE_SPLITS) for _ in range(EXP_TOPK)],
        out_specs=wide,
        out_shape=jax.ShapeDtypeStruct((n, D_MODEL), F32),
        compiler_params=_params("parallel"),
        name="final_norm",
    )(x2, wts, g, *[y for pair in ys for y in pair])


def _route(eid, tok0):
    n_tok = eid.shape[1]
    eid = eid.reshape(-1)
    n_asg = eid.shape[0]
    iota = jnp.arange(n_asg, dtype=jnp.int32)
    se, order = lax.sort_key_val(eid, iota)
    counts = jnp.sum((jnp.arange(N_EXPERTS, dtype=jnp.int32)[:, None] == eid[None, :]).astype(jnp.int32), axis=1)
    padded = (counts + MOE_BLOCK - 1) // MOE_BLOCK * MOE_BLOCK
    starts = jnp.cumsum(counts) - counts
    pends = jnp.cumsum(padded)
    pstarts = pends - padded
    cap = ((n_asg + MOE_BLOCK - 1) // MOE_BLOCK + N_EXPERTS) * MOE_BLOCK
    nblk = cap // MOE_BLOCK
    blk_e = jnp.minimum(jnp.searchsorted(pends, jnp.arange(nblk) * MOE_BLOCK, side='right', method='compare_all'),
                        N_EXPERTS - 1).astype(jnp.int32)
    per_row = lambda a: jnp.repeat(a[blk_e], MOE_BLOCK)
    row = jnp.arange(cap, dtype=jnp.int32)
    off = row - per_row(pstarts)
    asg = order[jnp.clip(per_row(starts) + off, 0, n_asg - 1)]
    buf_tok = tok0 + jnp.where(off < per_row(counts), asg % n_tok, row % n_tok)
    dest_sorted = iota + (pstarts - starts)[se]
    _, pos = lax.sort_key_val(order, dest_sorted)
    n_used = (pends[-1] // MOE_BLOCK).astype(jnp.int32)
    return buf_tok, jnp.concatenate([blk_e, n_used[None]]), pos.reshape(EXP_TOPK, n_tok)


def kernel(x, mem, norm_mix_g, w_in, w_ret_o, w_nsa_o, w_out, cmp_pe_k, cmp_w1_k, cmp_w2_k, cmp_pe_v,
           cmp_w1_v, cmp_w2_v, norm_x_g, norm_mem_g, w_xq, w_xkv, w_xo, norm_ffn_g, w_grp, b_grp, w_rt,
           b_rt, w_e1, w_e3, w_e2, norm_f_g):
    bsz, seq, _ = x.shape
    n = bsz * seq
    assert seq % PROJ_TM == 0 and seq % (2 * ATT_TQ) == 0 and w_in.shape[0] == 1
    assert n % (ROW_TM * MOE_SPLITS) == 0
    cast = lambda a: a.astype(MXU_DTYPE)
    xc = x.reshape(n, D_MODEL)
    l = 0

    w_main, w_t, w_gates = _split_w_in(w_in[l])
    qk, rv, rg, ckv, sk, wk, ngl, qt, svt, wvt = _proj(
        xc, norm_mix_g[l][None, :], w_main, w_t, _rope_tables(seq), bsz, seq)
    y_ret = _retention(qk, rv, rg, bsz, seq)
    w2 = jnp.stack([cmp_w2_k[l], cmp_w2_v[l]])
    cmp_k, cmp_vt = _compress(ckv, jnp.stack([cmp_pe_k[l], cmp_pe_v[l]]),
                              cast(jnp.stack([cmp_w1_k[l], cmp_w1_v[l]])),
                              cast(w2), cast(w2.transpose(0, 2, 1)), bsz, seq)
    o_cmp, selt = _cmp_attention(qt, cmp_k, cmp_vt, bsz, seq)
    o_sel = _selected_attention(qt, sk, svt, selt, bsz, seq)
    o_win = _window_attention(qt, wk, wvt, bsz, seq)
    x1 = _merge(xc, y_ret, o_cmp, o_sel, o_win, ngl, norm_mix_g[l][None, :], w_gates,
                cast(w_ret_o[l]), cast(w_nsa_o[l]), cast(w_out[l]))

    kv = _mem_kv(mem.reshape(-1, D_MODEL), norm_mem_g[l][None, :], cast(w_xkv[l]), bsz)
    gap = _ROUTER_E0 - N_EGROUPS
    w_router = jnp.concatenate([w_grp[l].T, jnp.zeros((gap, D_MODEL), F32), w_rt[l].T], axis=0)
    wr_hi = cast(w_router)
    wr_mid = cast(w_router - wr_hi.astype(F32))
    b_router = jnp.concatenate([b_grp[l], jnp.zeros((gap,), F32), b_rt[l]])[:, None]
    x2, hf, routed = _cross(x1, kv, norm_x_g[l][None, :], cast(w_xq[l]), cast(w_xo[l]),
                            norm_ffn_g[l][None, :], jnp.stack([wr_hi, wr_mid]), b_router, bsz, seq)

    eid = routed[0:EXP_TOPK].astype(jnp.int32)
    wts = routed[EXP_TOPK:2 * EXP_TOPK].T
    per_split = n // MOE_SPLITS
    ys = []
    for s in range(MOE_SPLITS):
        buf_tok, blk, pos = _route(eid[:, s * per_split:(s + 1) * per_split], s * per_split)
        y = _experts(blk, hf[buf_tok], w_e1[l], w_e3[l], w_e2[l])
        ys.append([y[pos[j]] for j in range(EXP_TOPK)])
    out = _final(x2, wts, norm_f_g[None, :], ys)
    return out.reshape(bsz, seq, D_MODEL)
```

```python
import functools

import numpy as np
import jax
import jax.numpy as jnp
from jax import lax
from jax.experimental import pallas as pl
from jax.experimental.pallas import tpu as pltpu

MXU_DTYPE = jnp.bfloat16
F32 = jnp.float32

D_MODEL = 1024
N_MEM = 256
EPS = 1e-6
NEG_INF = -1e30
SEL_FORCE = 1e4

R_HEADS = 4
R_DK = 128
R_DV = 256
R_CHUNK = 128
ROPE_BASE = 10000.0

NSA_HEADS = 8
NSA_GROUPS = 2
NSA_HPG = NSA_HEADS // NSA_GROUPS
NSA_DH = 128
CMP_LEN = 32
CMP_STRIDE = 16
SEL_LEN = 64
SEL_TOPK = 16
WINDOW = 512

X_HEADS = 4
X_DH = D_MODEL // X_HEADS

N_EGROUPS = 4
EXP_PER_GROUP = 8
N_EXPERTS = N_EGROUPS * EXP_PER_GROUP
EXP_TOPK = 2
D_EXPERT = 512
MOE_BLOCK = 512

RET_QK = R_HEADS * R_DK
RET_V = R_HEADS * R_DV
NSA_Q = NSA_HEADS * NSA_DH
NSA_KV = NSA_GROUPS * NSA_DH
SPLITS = (RET_QK, RET_QK, RET_V, RET_V, NSA_Q, NSA_KV, NSA_KV, NSA_KV, NSA_KV, NSA_KV, NSA_KV,
          3 * NSA_HEADS, D_MODEL, D_MODEL)

_EXP2_SCALE = (NSA_DH ** -0.5) * float(np.log2(np.e))

LANES = 128
SUBLANES = 8
VMEM_LIMIT = 56 * 1024 * 1024

PROJ_TM = 512
RET_ROWS = 512
ATT_TQ = 256
ATT_TK = 256
ROW_TM = 512
MOE_SPLITS = 2
VT_ROWS = NSA_DH + 2 * SUBLANES
SEL_KW = 2 * NSA_DH
assert ATT_TQ == ATT_TK and SEL_LEN * 2 * SUBLANES >= ATT_TK and WINDOW == 2 * ATT_TK


def _params(*sem):
    return pltpu.CompilerParams(dimension_semantics=sem, vmem_limit_bytes=VMEM_LIMIT)


def _dot(a, b):
    return jnp.dot(a.astype(MXU_DTYPE), b.astype(MXU_DTYPE), preferred_element_type=F32)


def _dot_nt(a, b):
    return lax.dot_general(a.astype(MXU_DTYPE), b.astype(MXU_DTYPE), (((1,), (1,)), ((), ())),
                           preferred_element_type=F32)


def _dot_tn(a, b):
    return lax.dot_general(a.astype(MXU_DTYPE), b.astype(MXU_DTYPE), (((0,), (0,)), ((), ())),
                           preferred_element_type=F32)


def _split3(p):
    hi = p.astype(MXU_DTYPE)
    r1 = p - hi.astype(F32)
    mid = r1.astype(MXU_DTYPE)
    lo = (r1 - mid.astype(F32)).astype(MXU_DTYPE)
    return hi, mid, lo


def _rms(x, g):
    return x * lax.rsqrt(jnp.mean(x * x, axis=-1, keepdims=True) + EPS) * g


_C_RQK = 0
_C_RV = _C_RQK + 2 * RET_QK
_C_RG = _C_RV + RET_V
_C_CKV = _C_RG + RET_V
_C_SK = _C_CKV + 2 * NSA_KV
_C_WK = _C_SK + NSA_KV
_C_NGL = _C_WK + NSA_KV
_C_END = _C_NGL + LANES
_R_NQ = 0
_R_SV = _R_NQ + NSA_Q
_R_WV = _R_SV + NSA_KV
_R_END = _R_WV + NSA_KV


def _proj_kernel(x_ref, g_ref, w_ref, wt_ref, cos_ref, sin_up_ref, sin_dn_ref,
                 qk_ref, rv_ref, rg_ref, ckv_ref, sk_ref, wk_ref, ngl_ref, nqt_ref, svt_ref, wvt_ref):
    hb = _rms(x_ref[...], g_ref[...]).astype(MXU_DTYPE)
    tm = hb.shape[0]

    def mm(off, width):
        return jnp.dot(hb, w_ref[:, off:off + width], preferred_element_type=F32)

    def mm_t(off, height):
        return _dot_nt(wt_ref[off:off + height, :], hb)

    cos = cos_ref[...]
    sin_up = sin_up_ref[...]
    sin_dn = sin_dn_ref[...]
    qk = mm(_C_RQK, 2 * RET_QK)
    for i in range(2 * R_HEADS):
        t = qk[:, i * R_DK:(i + 1) * R_DK]
        r = t * cos + pltpu.roll(t, R_DK - 1, axis=1) * sin_up + pltpu.roll(t, 1, axis=1) * sin_dn
        if i >= R_HEADS:
            r = r * (R_DK ** -0.5)
        qk_ref[:, i * R_DK:(i + 1) * R_DK] = r.astype(qk_ref.dtype)
    rv_ref[...] = mm(_C_RV, RET_V).astype(rv_ref.dtype)
    rg_ref[...] = mm(_C_RG, RET_V)
    ckv_ref[...] = mm(_C_CKV, 2 * NSA_KV)
    sk = mm(_C_SK, NSA_KV).astype(sk_ref.dtype)
    blk = (lax.broadcasted_iota(jnp.int32, (tm, NSA_DH), 0) % ATT_TK) // SEL_LEN
    onehot = jnp.where(lax.broadcasted_iota(jnp.int32, (tm, NSA_DH), 1) == blk, 1.0, 0.0).astype(sk_ref.dtype)
    for g in range(NSA_GROUPS):
        sk_ref[:, g * SEL_KW:g * SEL_KW + NSA_DH] = sk[:, g * NSA_DH:(g + 1) * NSA_DH]
        sk_ref[:, g * SEL_KW + NSA_DH:(g + 1) * SEL_KW] = onehot
    wk_ref[...] = mm(_C_WK, NSA_KV).astype(wk_ref.dtype)
    ngl_ref[...] = mm(_C_NGL, LANES)

    nqt = (mm_t(_R_NQ, NSA_Q) * _EXP2_SCALE).astype(nqt_ref.dtype)
    for g in range(NSA_GROUPS):
        for j in range(tm // ATT_TQ):
            for hh in range(NSA_HPG):
                head = g * NSA_HPG + hh
                nqt_ref[g, j, :, hh * ATT_TQ:(hh + 1) * ATT_TQ] = (
                    nqt[head * NSA_DH:(head + 1) * NSA_DH, j * ATT_TQ:(j + 1) * ATT_TQ])
    for off, out_ref in ((_R_SV, svt_ref), (_R_WV, wvt_ref)):
        vt = mm_t(off, NSA_KV).astype(out_ref.dtype)
        for g in range(NSA_GROUPS):
            for j in range(tm // ATT_TK):
                out_ref[g, j, 0:NSA_DH] = vt[g * NSA_DH:(g + 1) * NSA_DH, j * ATT_TK:(j + 1) * ATT_TK]
                out_ref[g, j, NSA_DH:VT_ROWS] = jnp.ones((VT_ROWS - NSA_DH, ATT_TK), out_ref.dtype)


def _split_w_in(w_in):
    offs = np.cumsum((0,) + SPLITS)
    rq, rk, rv, rg, nq, ck, cv, sk, sv, wk, wv, ngl, ga, gb = [
        w_in[:, offs[i]:offs[i + 1]] for i in range(len(SPLITS))]
    ngl = jnp.pad(ngl, ((0, 0), (0, LANES - ngl.shape[1])))
    w = jnp.concatenate([rq, rk, rv, rg, ck, cv, sk, wk, ngl], axis=1).astype(MXU_DTYPE)
    wt = jnp.concatenate([nq, sv, wv], axis=1).T.astype(MXU_DTYPE)
    w_gates = jnp.concatenate([ga, gb], axis=1).astype(MXU_DTYPE)
    return w, wt, w_gates


def _rope_tables(seq):
    pos = jnp.arange(seq, dtype=F32)
    inv_freq = ROPE_BASE ** (-jnp.arange(0, R_DK, 2, dtype=F32) / R_DK)
    ang = pos[:, None] * inv_freq[None, :]
    cos = jnp.repeat(jnp.cos(ang), 2, axis=1)
    sin = jnp.repeat(jnp.sin(ang), 2, axis=1)
    even = (jnp.arange(R_DK) % 2 == 0)[None, :]
    return cos, jnp.where(even, -sin, 0.0), jnp.where(even, 0.0, sin)


def _proj(x2d, g, w, wt, tables, bsz, seq):
    n = x2d.shape[0]
    tm = PROJ_TM
    nt = seq // tm
    row = lambda i: (i, 0)
    const = lambda i: (0, 0)
    tile = lambda i: (i // nt, 0, i % nt, 0, 0)
    table = pl.BlockSpec((tm, R_DK), lambda i: (i % nt, 0))
    widths = (2 * RET_QK, RET_V, RET_V, 2 * NSA_KV, NSA_GROUPS * SEL_KW, NSA_KV, LANES)
    dtypes = (MXU_DTYPE, MXU_DTYPE, F32, F32, MXU_DTYPE, MXU_DTYPE, F32)
    width = NSA_HPG * ATT_TQ
    vt_shape = jax.ShapeDtypeStruct((bsz, NSA_GROUPS, seq // ATT_TK, VT_ROWS, ATT_TK), MXU_DTYPE)
    vt_spec = pl.BlockSpec((None, NSA_GROUPS, tm // ATT_TK, VT_ROWS, ATT_TK), tile)
    return pl.pallas_call(
        _proj_kernel,
        grid=(n // tm,),
        in_specs=[pl.BlockSpec((tm, D_MODEL), row),
                  pl.BlockSpec((1, D_MODEL), const),
                  pl.BlockSpec((D_MODEL, _C_END), const, pipeline_mode=pl.Buffered(1)),
                  pl.BlockSpec((_R_END, D_MODEL), const, pipeline_mode=pl.Buffered(1)),
                  table, table, table],
        out_specs=[pl.BlockSpec((tm, wd), row) for wd in widths] + [
            pl.BlockSpec((None, NSA_GROUPS, tm // ATT_TQ, NSA_DH, width), tile), vt_spec, vt_spec],
        out_shape=[jax.ShapeDtypeStruct((n, wd), dt) for wd, dt in zip(widths, dtypes)] + [
            jax.ShapeDtypeStruct((bsz, NSA_GROUPS, seq // ATT_TQ, NSA_DH, width), MXU_DTYPE), vt_shape, vt_shape],
        compiler_params=_params("parallel"),
        name="proj",
    )(x2d, g, w, wt, *tables)


def _ret_kernel(qk_ref, v_ref, g_ref, decay_ref, zeta_ref, xi_ref, cd_ref, y_ref, state_ref):
    @pl.when(pl.program_id(1) == 0)
    def _():
        state_ref[...] = jnp.zeros_like(state_ref)

    for c in range(RET_ROWS // R_CHUNK):
        rows = slice(c * R_CHUNK, (c + 1) * R_CHUNK)
        for h in range(R_HEADS):
            q = qk_ref[rows, h * R_DK:(h + 1) * R_DK]
            k = qk_ref[rows, RET_QK + h * R_DK:RET_QK + (h + 1) * R_DK]
            v = v_ref[rows, h * R_DV:(h + 1) * R_DV]
            st = state_ref[h]
            scores = _dot_nt(q, k) * decay_ref[h]
            o = _dot(scores, v) + _dot(q, st) * xi_ref[h]
            o = o * lax.rsqrt(jnp.mean(o * o, axis=-1, keepdims=True) + EPS)
            g = g_ref[rows, h * R_DV:(h + 1) * R_DV]
            y_ref[rows, h * R_DV:(h + 1) * R_DV] = (g * jax.nn.sigmoid(g) * o).astype(y_ref.dtype)
            kz = k.astype(F32) * zeta_ref[h]
            state_ref[h] = st * cd_ref[h] + _dot_tn(kz, v)


def _retention(qk, rv, rg, bsz, seq):
    n = qk.shape[0]
    nt = seq // RET_ROWS
    log_g = jnp.log1p(-jnp.exp2(-5.0 - jnp.arange(R_HEADS, dtype=F32)))
    idx = jnp.arange(R_CHUNK, dtype=F32)
    diff = idx[:, None] - idx[None, :]
    decay = jnp.where(diff >= 0, jnp.exp(log_g[:, None, None] * jnp.maximum(diff, 0.0)), 0.0)
    zeta = jnp.exp(log_g[:, None] * (R_CHUNK - 1.0 - idx)[None, :])[:, :, None]
    xi = jnp.exp(log_g[:, None] * (idx + 1.0)[None, :])[:, :, None]
    cd = jnp.exp(log_g * R_CHUNK)[:, None, None]
    row = lambda b, i: (b * nt + i, 0)
    const3 = lambda b, i: (0, 0, 0)
    return pl.pallas_call(
        _ret_kernel,
        grid=(bsz, nt),
        in_specs=[pl.BlockSpec((RET_ROWS, 2 * RET_QK), row),
                  pl.BlockSpec((RET_ROWS, RET_V), row),
                  pl.BlockSpec((RET_ROWS, RET_V), row),
                  pl.BlockSpec((R_HEADS, R_CHUNK, R_CHUNK), const3),
                  pl.BlockSpec((R_HEADS, R_CHUNK, 1), const3),
                  pl.BlockSpec((R_HEADS, R_CHUNK, 1), const3),
                  pl.BlockSpec((R_HEADS, 1, 1), const3)],
        out_specs=pl.BlockSpec((RET_ROWS, RET_V), row),
        out_shape=jax.ShapeDtypeStruct((n, RET_V), MXU_DTYPE),
        scratch_shapes=[pltpu.VMEM((R_HEADS, R_DK, R_DV), F32)],
        compiler_params=_params("parallel", "arbitrary"),
        name="retention",
    )(qk, rv, rg, decay, zeta, xi, cd)


def _compress_kernel(x_ref, pe_ref, w1_ref, w2_ref, w2t_ref, o_ref, ot_ref, buf_ref, *, seq):
    ncp = seq // CMP_STRIDE
    buf_ref[0:seq, :] = x_ref[...]
    buf_ref[seq:seq + LANES, :] = jnp.zeros((LANES, NSA_DH), F32)
    acc = jnp.zeros((ncp, NSA_DH), F32)
    for l in range(CMP_LEN):
        xl = buf_ref[pl.ds(l, ncp, stride=CMP_STRIDE), :] + pe_ref[l:l + 1, :]
        acc = acc + _dot(xl, w1_ref[l])
    hid = jax.nn.gelu(acc)
    o_ref[...] = _dot(hid, w2_ref[...]).astype(o_ref.dtype)
    ot_ref[...] = _dot_nt(w2t_ref[...], hid).astype(ot_ref.dtype)


def _compress(ckv, pe, w1, w2, w2t, bsz, seq):
    ncp = seq // CMP_STRIDE
    nj = 2 * NSA_GROUPS
    wsel = lambda b, j: (j // NSA_GROUPS, 0, 0)
    return pl.pallas_call(
        functools.partial(_compress_kernel, seq=seq),
        grid=(bsz, nj),
        in_specs=[pl.BlockSpec((seq, NSA_DH), lambda b, j: (b, j)),
                  pl.BlockSpec((None, CMP_LEN, NSA_DH), wsel),
                  pl.BlockSpec((None, CMP_LEN, NSA_DH, NSA_DH), lambda b, j: (j // NSA_GROUPS, 0, 0, 0)),
                  pl.BlockSpec((None, NSA_DH, NSA_DH), wsel),
                  pl.BlockSpec((None, NSA_DH, NSA_DH), wsel)],
        out_specs=[pl.BlockSpec((None, None, ncp, NSA_DH), lambda b, j: (b, j, 0, 0)),
                   pl.BlockSpec((None, None, NSA_DH, ncp), lambda b, j: (b, j, 0, 0))],
        out_shape=[jax.ShapeDtypeStruct((bsz, nj, ncp, NSA_DH), MXU_DTYPE),
                   jax.ShapeDtypeStruct((bsz, nj, NSA_DH, ncp), MXU_DTYPE)],
        scratch_shapes=[pltpu.VMEM((seq + LANES, NSA_DH), F32)],
        compiler_params=_params("parallel", "parallel"),
        name="compress",
    )(ckv, pe, w1, w2, w2t)


def _cmp_attn_kernel(qt_ref, k_ref, vt_ref, ov_ref, o_ref, selt_ref, p_ref, *, ncp, nb):
    tq = ATT_TQ
    width = NSA_HPG * tq
    groups = range(NSA_GROUPS)
    qi = pl.program_id(1)
    t0 = qi * tq
    any_valid = (t0 + lax.broadcasted_iota(jnp.int32, (1, tq), 1)) >= CMP_LEN - 1
    any_valid = jnp.concatenate([any_valid] * NSA_HPG, axis=1)

    def probabilities(rows):
        t = t0 + lax.broadcasted_iota(jnp.int32, (rows, tq), 1)
        n = lax.broadcasted_iota(jnp.int32, (rows, tq), 0)
        valid = (n * CMP_STRIDE + (CMP_LEN - 1) <= t) & (n < ncp - 1)
        valid = jnp.concatenate([valid] * NSA_HPG, axis=1)
        for g in groups:
            s = jnp.where(valid, _dot(k_ref[g, 0:rows, :], qt_ref[g]), NEG_INF)
            e = jnp.exp2(s - jnp.max(s, axis=0, keepdims=True))
            p_ref[g, 0:rows, :] = e * jnp.where(any_valid, 1.0 / jnp.sum(e, axis=0, keepdims=True), 0.0)
            if rows < ncp:
                p_ref[g, rows:, :] = jnp.zeros((ncp - rows, width), F32)

    n_buckets = 4
    per_bucket = ncp // n_buckets
    visible = (t0 + tq - CMP_LEN) // CMP_STRIDE + 1
    bucket = jnp.minimum((visible + per_bucket - 1) // per_bucket, n_buckets)
    for c in range(1, n_buckets + 1):
        pl.when(bucket == c)(functools.partial(probabilities, c * per_bucket))

    j = lax.broadcasted_iota(jnp.int32, (nb, tq), 0)
    tb = (t0 + lax.broadcasted_iota(jnp.int32, (nb, tq), 1)) // SEL_LEN
    forced = (j == 0) | (j == tb) | (j == tb - 1)
    sub = SUBLANES
    k_sel = min(SEL_TOPK, nb)
    for g in groups:
        p = p_ref[g]
        ot = _dot(vt_ref[g], p)
        for h in range(NSA_HPG):
            o_ref[:, (g * NSA_HPG + h) * NSA_DH:(g * NSA_HPG + h + 1) * NSA_DH] = ot[:, h * tq:(h + 1) * tq].T
        psum = sum(p[:, h * tq:(h + 1) * tq] for h in range(NSA_HPG))

        imp = sum(_dot(ov_ref[...], part) for part in _split3(psum))
        imp = jnp.where(j > tb, -SEL_FORCE, jnp.where(forced, SEL_FORCE, imp))

        grp = [imp[r * sub:(r + 1) * sub] for r in range(nb // sub)]
        cnt = [jnp.zeros((sub, tq), F32) for _ in grp]
        for i in range(nb):
            row = jnp.broadcast_to(imp[i:i + 1, :], (sub, tq))
            for r in range(nb // sub):
                if r * sub > i:
                    beats = jnp.where(row >= grp[r], 1.0, 0.0)
                elif r * sub + sub - 1 < i:
                    beats = jnp.where(row > grp[r], 1.0, 0.0)
                else:
                    jr = r * sub + lax.broadcasted_iota(jnp.int32, (sub, tq), 0)
                    beats = jnp.where(jr > i, jnp.where(row >= grp[r], 1.0, 0.0), jnp.where(row > grp[r], 1.0, 0.0))
                cnt[r] = cnt[r] + beats
        for r in range(nb // sub):
            selt_ref[g, r * sub:(r + 1) * sub, :] = jnp.where(cnt[r] < k_sel, 1.0, 0.0)


def _cmp_attention(qt, cmp_k, cmp_vt, bsz, seq):
    n = bsz * seq
    ncp = seq // CMP_STRIDE
    nb = seq // SEL_LEN
    nt = seq // ATT_TQ
    cstart = np.arange(ncp) * CMP_STRIDE
    jstart = np.arange(nb) * SEL_LEN
    ov = ((cstart[None, :] < jstart[:, None] + SEL_LEN) & (cstart[None, :] + CMP_LEN > jstart[:, None])
          & (np.arange(ncp)[None, :] < ncp - 1))
    ov = jnp.asarray(ov, MXU_DTYPE)
    return pl.pallas_call(
        functools.partial(_cmp_attn_kernel, ncp=ncp, nb=nb),
        grid=(bsz, nt),
        in_specs=[pl.BlockSpec((None, NSA_GROUPS, None, NSA_DH, NSA_HPG * ATT_TQ), lambda b, i: (b, 0, i, 0, 0)),
                  pl.BlockSpec((None, NSA_GROUPS, ncp, NSA_DH), lambda b, i: (b, 0, 0, 0)),
                  pl.BlockSpec((None, NSA_GROUPS, NSA_DH, ncp), lambda b, i: (b, 1, 0, 0)),
                  pl.BlockSpec((nb, ncp), lambda b, i: (0, 0))],
        out_specs=[pl.BlockSpec((ATT_TQ, NSA_Q), lambda b, i: (b * nt + i, 0)),
                   pl.BlockSpec((None, NSA_GROUPS, nb, ATT_TQ), lambda b, i: (b, 0, 0, i))],
        out_shape=[jax.ShapeDtypeStruct((n, NSA_Q), F32),
                   jax.ShapeDtypeStruct((bsz, NSA_GROUPS, nb, seq), F32)],
        scratch_shapes=[pltpu.VMEM((NSA_GROUPS, ncp, NSA_HPG * ATT_TQ), F32)],
        compiler_params=_params("parallel", "parallel"),
        name="cmp_attention",
    )(qt, cmp_k, cmp_vt, ov)


def _softmax_step(s, m, vt, acc_ref):
    m_new = jnp.maximum(m, jnp.max(s, axis=0, keepdims=True))
    p = jnp.exp2(s - m_new)
    acc_ref[...] = jnp.exp2(m - m_new) * acc_ref[...] + _dot(vt, p)
    return m_new


def _flash_finish(o_ref, acc_ref, col0=0):
    inv = 1.0 / acc_ref[NSA_DH:NSA_DH + 1, :]
    for h in range(NSA_HPG):
        cols = slice(h * ATT_TQ, (h + 1) * ATT_TQ)
        o_ref[:, col0 + h * NSA_DH:col0 + (h + 1) * NSA_DH] = (acc_ref[0:NSA_DH, cols] * inv[:, cols]).T


def _sel_attn_kernel(qt_ref, k_ref, vt_ref, selt_ref, o_ref, acc_ref, s_ref, qa_ref):
    qi = pl.program_id(1)
    width = NSA_HPG * ATT_TQ
    groups = range(NSA_GROUPS)
    blocks_per_tile = ATT_TK // SEL_LEN
    bias_rows = 2 * SUBLANES
    n_loop = qi * (ATT_TQ // ATT_TK)
    n_steps = (n_loop + 1) // 2 * 2
    tile_of = lambda step: jnp.where(step < n_loop, step, qi)
    rowid = lax.broadcasted_iota(jnp.int32, (bias_rows, ATT_TQ), 0)
    tri = (lax.broadcasted_iota(jnp.int32, (ATT_TK, ATT_TQ), 0) <=
           lax.broadcasted_iota(jnp.int32, (ATT_TK, ATT_TQ), 1))
    tri = jnp.concatenate([tri] * NSA_HPG, axis=1)

    def scores(g, step):
        kt = tile_of(step)
        threshold = jnp.where((step < n_loop) | (step == n_steps), 0.5, 2.0)
        bias = jnp.zeros((bias_rows, ATT_TQ), F32)
        for jb in range(blocks_per_tile):
            picked = selt_ref[g, pl.ds(kt * blocks_per_tile + jb, 1), :] > threshold
            bias = jnp.where(rowid == jb, jnp.where(picked, 0.0, NEG_INF), bias)
        qa_ref[g, NSA_DH:NSA_DH + bias_rows, :] = jnp.concatenate([bias] * NSA_HPG, axis=1).astype(qa_ref.dtype)
        k = k_ref[pl.ds(pl.multiple_of(kt * ATT_TK, ATT_TK), ATT_TK), g * SEL_KW:(g + 1) * SEL_KW]
        return _dot(k, qa_ref[g])

    def half_step(g, step, cur, nxt, m, diagonal=False):
        if not diagonal:
            s_ref[g, nxt] = scores(g, step + 1)
        s = jnp.where(tri, s_ref[g, cur], NEG_INF) if diagonal else s_ref[g, cur]
        return _softmax_step(s, m, vt_ref[g, tile_of(step)], acc_ref.at[g])

    def body(i, ms):
        ms = [half_step(g, 2 * i, 0, 1, ms[g]) for g in groups]
        return tuple(half_step(g, 2 * i + 1, 1, 0, ms[g]) for g in groups)

    for g in groups:
        qa_ref[g, 0:NSA_DH, :] = qt_ref[g]
        qa_ref[g, NSA_DH + bias_rows:, :] = jnp.zeros((SEL_KW - NSA_DH - bias_rows, width), qa_ref.dtype)
        acc_ref[g] = jnp.zeros(acc_ref.shape[1:], F32)
        s_ref[g, 0] = scores(g, 0)
    ms = lax.fori_loop(0, n_steps // 2, body, tuple(jnp.full((1, width), NEG_INF, F32) for _ in groups))
    for g in groups:
        half_step(g, n_steps, 0, 1, ms[g], diagonal=True)
        _flash_finish(o_ref, acc_ref.at[g], g * NSA_HPG * NSA_DH)


def _selected_attention(qt, k, vt, selt, bsz, seq):
    n = bsz * seq
    nt = seq // ATT_TQ
    nkt = seq // ATT_TK
    nb = seq // SEL_LEN
    width = NSA_HPG * ATT_TQ
    return pl.pallas_call(
        _sel_attn_kernel,
        grid=(bsz, nt),
        in_specs=[pl.BlockSpec((None, NSA_GROUPS, None, NSA_DH, width), lambda b, i: (b, 0, i, 0, 0)),
                  pl.BlockSpec((seq, NSA_GROUPS * SEL_KW), lambda b, i: (b, 0)),
                  pl.BlockSpec((None, NSA_GROUPS, nkt, VT_ROWS, ATT_TK), lambda b, i: (b, 0, 0, 0, 0)),
                  pl.BlockSpec((None, NSA_GROUPS, nb, ATT_TQ), lambda b, i: (b, 0, 0, i))],
        out_specs=pl.BlockSpec((ATT_TQ, NSA_Q), lambda b, i: (b * nt + i, 0)),
        out_shape=jax.ShapeDtypeStruct((n, NSA_Q), F32),
        scratch_shapes=[pltpu.VMEM((NSA_GROUPS, VT_ROWS, width), F32),
                        pltpu.VMEM((NSA_GROUPS, 2, ATT_TK, width), F32),
                        pltpu.VMEM((NSA_GROUPS, SEL_KW, width), MXU_DTYPE)],
        compiler_params=_params("parallel", "parallel"),
        name="selected_attention",
    )(qt, k, vt, selt)


def _win_attn_kernel(qt_ref, k_ref, vt_ref, o_ref, acc_ref, s0_ref, s1_ref, m_ref):
    qi = pl.program_id(1)
    groups = range(NSA_GROUPS)
    row = lax.broadcasted_iota(jnp.int32, (ATT_TK, ATT_TQ), 0)
    col = lax.broadcasted_iota(jnp.int32, (ATT_TK, ATT_TQ), 1)
    causal = jnp.concatenate([row <= col] * NSA_HPG, axis=1)
    window_tail = jnp.concatenate([row > col] * NSA_HPG, axis=1)

    def scores(g, kt):
        k = k_ref[pl.ds(pl.multiple_of(kt * ATT_TK, ATT_TK), ATT_TK), g * NSA_DH:(g + 1) * NSA_DH]
        return _dot(k, qt_ref[g])

    for g in groups:
        acc_ref[g] = jnp.zeros(acc_ref.shape[1:], F32)
        s0_ref[g] = scores(g, qi)
    for g in groups:
        s1_ref[g] = scores(g, jnp.maximum(qi - 1, 0))
        m_ref[g] = _softmax_step(jnp.where(causal, s0_ref[g], NEG_INF), jnp.full(m_ref.shape[1:], NEG_INF, F32),
                                 vt_ref[g, qi], acc_ref.at[g])

    @pl.when(qi >= 1)
    def _():
        for g in groups:
            s0_ref[g] = scores(g, jnp.maximum(qi - 2, 0))
            m_ref[g] = _softmax_step(s1_ref[g], m_ref[g], vt_ref[g, qi - 1], acc_ref.at[g])

    @pl.when(qi >= 2)
    def _():
        for g in groups:
            _softmax_step(jnp.where(window_tail, s0_ref[g], NEG_INF), m_ref[g], vt_ref[g, qi - 2], acc_ref.at[g])

    for g in groups:
        _flash_finish(o_ref, acc_ref.at[g], g * NSA_HPG * NSA_DH)


def _window_attention(qt, k, vt, bsz, seq):
    n = bsz * seq
    nt = seq // ATT_TQ
    nkt = seq // ATT_TK
    width = NSA_HPG * ATT_TQ
    return pl.pallas_call(
        _win_attn_kernel,
        grid=(bsz, nt),
        in_specs=[pl.BlockSpec((None, NSA_GROUPS, None, NSA_DH, width), lambda b, i: (b, 0, i, 0, 0)),
                  pl.BlockSpec((seq, NSA_KV), lambda b, i: (b, 0)),
                  pl.BlockSpec((None, NSA_GROUPS, nkt, VT_ROWS, ATT_TK), lambda b, i: (b, 0, 0, 0, 0))],
        out_specs=pl.BlockSpec((ATT_TQ, NSA_Q), lambda b, i: (b * nt + i, 0)),
        out_shape=jax.ShapeDtypeStruct((n, NSA_Q), F32),
        scratch_shapes=[pltpu.VMEM((NSA_GROUPS, VT_ROWS, width), F32),
                        pltpu.VMEM((NSA_GROUPS, ATT_TK, width), F32),
                        pltpu.VMEM((NSA_GROUPS, ATT_TK, width), F32),
                        pltpu.VMEM((NSA_GROUPS, 1, width), F32)],
        compiler_params=_params("parallel", "parallel"),
        name="window_attention",
    )(qt, k, vt)


def _merge_kernel(x_ref, yr_ref, oc_ref, os_ref, ow_ref, ngl_ref, g_ref, wg_ref, wr_ref, wn_ref, wo_ref, o_ref):
    tm = x_ref.shape[0]
    hb = _rms(x_ref[...], g_ref[...]).astype(MXU_DTYPE)
    ga = jnp.dot(hb, wg_ref[:, :D_MODEL], preferred_element_type=F32)
    gb = jnp.dot(hb, wg_ref[:, D_MODEL:], preferred_element_type=F32)
    gates = jax.nn.sigmoid(ngl_ref[...])
    parts = []
    for h in range(NSA_HEADS):
        cols = slice(h * NSA_DH, (h + 1) * NSA_DH)

        def gate(br):
            return jnp.broadcast_to(gates[:, 3 * h + br:3 * h + br + 1], (tm, NSA_DH))

        parts.append(gate(0) * oc_ref[:, cols] + gate(1) * os_ref[:, cols] + gate(2) * ow_ref[:, cols])
    o_nsa = jnp.concatenate(parts, axis=1)
    y_ret = _dot(yr_ref[...], wr_ref[...])
    y_nsa = _dot(o_nsa, wn_ref[...])
    y = jax.nn.sigmoid(ga) * y_ret + jax.nn.sigmoid(gb) * y_nsa
    o_ref[...] = x_ref[...] + _dot(y, wo_ref[...])


def _merge(x2d, y_ret, o_cmp, o_sel, o_win, ngl, g_mix, w_gates, w_ret_o, w_nsa_o, w_out):
    n = x2d.shape[0]
    tm = ROW_TM
    row = lambda i: (i, 0)
    const = lambda i: (0, 0)
    wide = pl.BlockSpec((tm, D_MODEL), row)
    wspec = pl.BlockSpec((D_MODEL, D_MODEL), const)
    return pl.pallas_call(
        _merge_kernel,
        grid=(n // tm,),
        in_specs=[wide, wide, wide, wide, wide,
                  pl.BlockSpec((tm, LANES), row),
                  pl.BlockSpec((1, D_MODEL), const),
                  pl.BlockSpec((D_MODEL, 2 * D_MODEL), const),
                  wspec, wspec, wspec],
        out_specs=wide,
        out_shape=jax.ShapeDtypeStruct((n, D_MODEL), F32),
        compiler_params=_params("parallel"),
        name="merge",
    )(x2d, y_ret, o_cmp, o_sel, o_win, ngl, g_mix, w_gates, w_ret_o, w_nsa_o, w_out)


def _mem_kv_kernel(m_ref, g_ref, w_ref, o_ref):
    o_ref[...] = _dot(_rms(m_ref[...], g_ref[...]), w_ref[...]).astype(o_ref.dtype)


def _mem_kv(mem2d, g, w_xkv, bsz):
    nm = mem2d.shape[0] // bsz
    return pl.pallas_call(
        _mem_kv_kernel,
        grid=(bsz,),
        in_specs=[pl.BlockSpec((nm, D_MODEL), lambda b: (b, 0)),
                  pl.BlockSpec((1, D_MODEL), lambda b: (0, 0)),
                  pl.BlockSpec((D_MODEL, 2 * D_MODEL), lambda b: (0, 0))],
        out_specs=pl.BlockSpec((nm, 2 * D_MODEL), lambda b: (b, 0)),
        out_shape=jax.ShapeDtypeStruct((mem2d.shape[0], 2 * D_MODEL), MXU_DTYPE),
        compiler_params=_params("parallel"),
        name="mem_kv",
    )(mem2d, g, w_xkv)


def _pack_pairs(x):
    half = x.shape[1] // 2
    hi = lax.bitcast_convert_type(x[:, :half].astype(MXU_DTYPE).astype(F32), jnp.uint32)
    lo = lax.bitcast_convert_type(x[:, half:].astype(MXU_DTYPE).astype(F32), jnp.uint32)
    return (hi & jnp.uint32(0xFFFF0000)) | (lo >> 16)


def _unpack_pairs(u):
    hi = lax.bitcast_convert_type(u & jnp.uint32(0xFFFF0000), F32)
    lo = lax.bitcast_convert_type(u << 16, F32)
    return jnp.concatenate([hi, lo], axis=1)


_ROUTER_E0 = 2 * SUBLANES
_ROUTER_ROWS = _ROUTER_E0 + N_EXPERTS


def _top2_route(lgt):
    sub = SUBLANES
    t = lgt.shape[1]
    rowid = lax.broadcasted_iota(jnp.int32, (sub, t), 0)
    first = lambda hit: jnp.min(jnp.where(hit, rowid, sub), axis=0, keepdims=True)
    lg = jnp.where(rowid < N_EGROUPS, lgt[0:sub], NEG_INF)
    gmax = jnp.max(lg, axis=0, keepdims=True)
    grp = first(lg == gmax)
    g_gate = 1.0 / jnp.sum(jnp.exp(lg - gmax), axis=0, keepdims=True)
    experts_of = lambda g: lgt[_ROUTER_E0 + g * EXP_PER_GROUP:_ROUTER_E0 + (g + 1) * EXP_PER_GROUP]
    le = experts_of(N_EGROUPS - 1)
    for g in range(N_EGROUPS - 2, -1, -1):
        le = jnp.where(grp == g, experts_of(g), le)
    ex = jnp.exp(le - jnp.max(le, axis=0, keepdims=True))
    pe = ex / jnp.sum(ex, axis=0, keepdims=True)
    p0 = jnp.max(pe, axis=0, keepdims=True)
    i0 = first(pe == p0)
    rest = jnp.where(rowid == i0, -1.0, pe)
    p1 = jnp.max(rest, axis=0, keepdims=True)
    i1 = first(rest == p1)
    den = p0 + p1
    base = grp * EXP_PER_GROUP
    return jnp.concatenate([(base + i0).astype(F32), (base + i1).astype(F32),
                            g_gate * p0 / den, g_gate * p1 / den], axis=0)


def _cross_kernel(x_ref, kv_ref, gx_ref, wq_ref, wo_ref, gf_ref, wr_ref, br_ref, x2_ref, hf_ref, rt_ref):
    x = x_ref[...]
    q = _dot(_rms(x, gx_ref[...]), wq_ref[...])
    heads = []
    for h in range(X_HEADS):
        k = kv_ref[:, h * X_DH:(h + 1) * X_DH]
        v = kv_ref[:, D_MODEL + h * X_DH:D_MODEL + (h + 1) * X_DH]
        s = _dot_nt(q[:, h * X_DH:(h + 1) * X_DH], k) * (X_DH ** -0.5)
        e = jnp.exp(s - jnp.max(s, axis=-1, keepdims=True))
        p = e / jnp.sum(e, axis=-1, keepdims=True)
        heads.append(_dot(p, v))
    x2 = x + _dot(jnp.concatenate(heads, axis=1), wo_ref[...])
    x2_ref[...] = x2
    hf = _rms(x2, gf_ref[...])
    hf_ref[...] = _pack_pairs(hf)
    h_hi, h_mid, _ = _split3(hf)
    w_hi = wr_ref[0]
    w_mid = wr_ref[1]
    lgt = (_dot_nt(w_hi, h_hi) + (_dot_nt(w_hi, h_mid) + _dot_nt(w_mid, h_hi))) + br_ref[...]
    rt_ref[...] = jnp.concatenate([_top2_route(lgt), jnp.zeros((SUBLANES - 4, lgt.shape[1]), F32)], axis=0)


def _cross(x1, kv, gx, w_xq, w_xo, gf, w_router, b_router, bsz, seq):
    n = x1.shape[0]
    tm = ROW_TM
    nt = seq // tm
    nm = kv.shape[0] // bsz
    row = lambda i: (i, 0)
    const = lambda i: (0, 0)
    vec = pl.BlockSpec((1, D_MODEL), const)
    wspec = pl.BlockSpec((D_MODEL, D_MODEL), const)
    return pl.pallas_call(
        _cross_kernel,
        grid=(n // tm,),
        in_specs=[pl.BlockSpec((tm, D_MODEL), row),
                  pl.BlockSpec((nm, 2 * D_MODEL), lambda i: (i // nt, 0)),
                  vec, wspec, wspec, vec,
                  pl.BlockSpec((2, _ROUTER_ROWS, D_MODEL), lambda i: (0, 0, 0)),
                  pl.BlockSpec((_ROUTER_ROWS, 1), const)],
        out_specs=[pl.BlockSpec((tm, D_MODEL), row),
                   pl.BlockSpec((tm, D_MODEL // 2), row),
                   pl.BlockSpec((SUBLANES, tm), lambda i: (0, i))],
        out_shape=[jax.ShapeDtypeStruct((n, D_MODEL), F32),
                   jax.ShapeDtypeStruct((n, D_MODEL // 2), jnp.uint32),
                   jax.ShapeDtypeStruct((SUBLANES, n), F32)],
        compiler_params=_params("parallel"),
        name="cross_attention",
    )(x1, kv, gx, w_xq, w_xo, gf, w_router, b_router)


def _expert_kernel(blk_ref, xb_ref, w1_ref, w3_ref, w2_ref, o_ref):
    n_used = blk_ref[pl.num_programs(0)]

    @pl.when(pl.program_id(0) < n_used)
    def _():
        xb = _unpack_pairs(xb_ref[...]).astype(MXU_DTYPE)
        a = _dot(xb, w1_ref[...])
        hmid = a * jax.nn.sigmoid(a) * _dot(xb, w3_ref[...])
        o_ref[...] = _dot(hmid, w2_ref[...])

    @pl.when(pl.program_id(0) >= n_used)
    def _():
        o_ref[...] = jnp.zeros_like(o_ref)


def _experts(blk, xb, w1, w3, w2):
    cap = xb.shape[0]
    nblk = cap // MOE_BLOCK
    row = lambda i, e: (i, 0)
    by_expert = lambda i, e: (e[i], 0, 0)
    grid_spec = pltpu.PrefetchScalarGridSpec(
        num_scalar_prefetch=1,
        grid=(nblk,),
        in_specs=[pl.BlockSpec((MOE_BLOCK, D_MODEL // 2), row),
                  pl.BlockSpec((None, D_MODEL, D_EXPERT), by_expert),
                  pl.BlockSpec((None, D_MODEL, D_EXPERT), by_expert),
                  pl.BlockSpec((None, D_EXPERT, D_MODEL), by_expert)],
        out_specs=pl.BlockSpec((MOE_BLOCK, D_MODEL), row),
    )
    return pl.pallas_call(
        _expert_kernel,
        grid_spec=grid_spec,
        out_shape=jax.ShapeDtypeStruct((cap, D_MODEL), F32),
        compiler_params=_params("arbitrary"),
        name="experts",
    )(blk, xb, w1, w3, w2)


def _final_kernel(x_ref, w_ref, g_ref, *refs):
    o_ref = refs[-1]
    tiles_per_split = pl.num_programs(0) // MOE_SPLITS
    for s in range(MOE_SPLITS):
        @pl.when(pl.program_id(0) // tiles_per_split == s)
        def _(s=s):
            moe = w_ref[:, 0:1] * refs[2 * s][...] + w_ref[:, 1:2] * refs[2 * s + 1][...]
            o_ref[...] = _rms(x_ref[...] + moe, g_ref[...])


def _final(x2, wts, g, ys):
    n = x2.shape[0]
    tm = ROW_TM
    tiles_per_split = n // tm // MOE_SPLITS
    row = lambda i: (i, 0)
    wide = pl.BlockSpec((tm, D_MODEL), row)

    def split_spec(s):
        return pl.BlockSpec((tm, D_MODEL), lambda i: (jnp.clip(i - s * tiles_per_split, 0, tiles_per_split - 1), 0))

    return pl.pallas_call(
        _final_kernel,
        grid=(n // tm,),
        in_specs=[wide, pl.BlockSpec((tm, EXP_TOPK), row), pl.BlockSpec((1, D_MODEL), lambda i: (0, 0))] + [
            split_spec(s) for s in range(MOE_SPLITS) for _ in range(EXP_TOPK)],
        out_specs=wide,
        out_shape=jax.ShapeDtypeStruct((n, D_MODEL), F32),
        compiler_params=_params("parallel"),
        name="final_norm",
    )(x2, wts, g, *[y for pair in ys for y in pair])


def _route(eid, tok0):
    n_tok = eid.shape[1]
    eid = eid.reshape(-1)
    n_asg = eid.shape[0]
    iota = jnp.arange(n_asg, dtype=jnp.int32)
    se, order = lax.sort_key_val(eid, iota)
    counts = jnp.sum((jnp.arange(N_EXPERTS, dtype=jnp.int32)[:, None] == eid[None, :]).astype(jnp.int32), axis=1)
    padded = (counts + MOE_BLOCK - 1) // MOE_BLOCK * MOE_BLOCK
    starts = jnp.cumsum(counts) - counts
    pends = jnp.cumsum(padded)
    pstarts = pends - padded
    cap = ((n_asg + MOE_BLOCK - 1) // MOE_BLOCK + N_EXPERTS) * MOE_BLOCK
    nblk = cap // MOE_BLOCK
    blk_e = jnp.minimum(jnp.searchsorted(pends, jnp.arange(nblk) * MOE_BLOCK, side='right', method='compare_all'),
                        N_EXPERTS - 1).astype(jnp.int32)
    per_row = lambda a: jnp.repeat(a[blk_e], MOE_BLOCK)
    row = jnp.arange(cap, dtype=jnp.int32)
    off = row - per_row(pstarts)
    asg = order[jnp.clip(per_row(starts) + off, 0, n_asg - 1)]
    buf_tok = tok0 + jnp.where(off < per_row(counts), asg % n_tok, row % n_tok)
    dest_sorted = iota + (pstarts - starts)[se]
    _, pos = lax.sort_key_val(order, dest_sorted)
    n_used = (pends[-1] // MOE_BLOCK).astype(jnp.int32)
    return buf_tok, jnp.concatenate([blk_e, n_used[None]]), pos.reshape(EXP_TOPK, n_tok)


def kernel(x, mem, norm_mix_g, w_in, w_ret_o, w_nsa_o, w_out, cmp_pe_k, cmp_w1_k, cmp_w2_k, cmp_pe_v,
           cmp_w1_v, cmp_w2_v, norm_x_g, norm_mem_g, w_xq, w_xkv, w_xo, norm_ffn_g, w_grp, b_grp, w_rt,
           b_rt, w_e1, w_e3, w_e2, norm_f_g):
    bsz, seq, _ = x.shape
    n = bsz * seq
    assert seq % PROJ_TM == 0 and seq % (2 * ATT_TQ) == 0 and w_in.shape[0] == 1
    assert n % (ROW_TM * MOE_SPLITS) == 0
    cast = lambda a: a.astype(MXU_DTYPE)
    xc = x.reshape(n, D_MODEL)
    l = 0

    w_main, w_t, w_gates = _split_w_in(w_in[l])
    qk, rv, rg, ckv, sk, wk, ngl, qt, svt, wvt = _proj(
        xc, norm_mix_g[l][None, :], w_main, w_t, _rope_tables(seq), bsz, seq)
    y_ret = _retention(qk, rv, rg, bsz, seq)
    w2 = jnp.stack([cmp_w2_k[l], cmp_w2_v[l]])
    cmp_k, cmp_vt = _compress(ckv, jnp.stack([cmp_pe_k[l], cmp_pe_v[l]]),
                              cast(jnp.stack([cmp_w1_k[l], cmp_w1_v[l]])),
                              cast(w2), cast(w2.transpose(0, 2, 1)), bsz, seq)
    o_cmp, selt = _cmp_attention(qt, cmp_k, cmp_vt, bsz, seq)
    o_sel = _selected_attention(qt, sk, svt, selt, bsz, seq)
    o_win = _window_attention(qt, wk, wvt, bsz, seq)
    x1 = _merge(xc, y_ret, o_cmp, o_sel, o_win, ngl, norm_mix_g[l][None, :], w_gates,
                cast(w_ret_o[l]), cast(w_nsa_o[l]), cast(w_out[l]))

    kv = _mem_kv(mem.reshape(-1, D_MODEL), norm_mem_g[l][None, :], cast(w_xkv[l]), bsz)
    gap = _ROUTER_E0 - N_EGROUPS
    w_router = jnp.concatenate([w_grp[l].T, jnp.zeros((gap, D_MODEL), F32), w_rt[l].T], axis=0)
    wr_hi = cast(w_router)
    wr_mid = cast(w_router - wr_hi.astype(F32))
    b_router = jnp.concatenate([b_grp[l], jnp.zeros((gap,), F32), b_rt[l]])[:, None]
    x2, hf, routed = _cross(x1, kv, norm_x_g[l][None, :], cast(w_xq[l]), cast(w_xo[l]),
                            norm_ffn_g[l][None, :], jnp.stack([wr_hi, wr_mid]), b_router, bsz, seq)

    eid = routed[0:EXP_TOPK].astype(jnp.int32)
    wts = routed[EXP_TOPK:2 * EXP_TOPK].T
    per_split = n // MOE_SPLITS
    ys = []
    for s in range(MOE_SPLITS):
        buf_tok, blk, pos = _route(eid[:, s * per_split:(s + 1) * per_split], s * per_split)
        y = _experts(blk, hf[buf_tok], w_e1[l], w_e3[l], w_e2[l])
        ys.append([y[pos[j]] for j in range(EXP_TOPK)])
    out = _final(x2, wts, norm_f_g[None, :], ys)
    return out.reshape(bsz, seq, D_MODEL)
```

```python
import functools

import numpy as np
import jax
import jax.numpy as jnp
from jax import lax
from jax.experimental import pallas as pl
from jax.experimental.pallas import tpu as pltpu

MXU_DTYPE = jnp.bfloat16
F32 = jnp.float32

D_MODEL = 1024
N_MEM = 256
EPS = 1e-6
NEG_INF = -1e30
SEL_FORCE = 1e4

R_HEADS = 4
R_DK = 128
R_DV = 256
R_CHUNK = 128
ROPE_BASE = 10000.0

NSA_HEADS = 8
NSA_GROUPS = 2
NSA_HPG = NSA_HEADS // NSA_GROUPS
NSA_DH = 128
CMP_LEN = 32
CMP_STRIDE = 16
SEL_LEN = 64
SEL_TOPK = 16
WINDOW = 512

X_HEADS = 4
X_DH = D_MODEL // X_HEADS

N_EGROUPS = 4
EXP_PER_GROUP = 8
N_EXPERTS = N_EGROUPS * EXP_PER_GROUP
EXP_TOPK = 2
D_EXPERT = 512
MOE_BLOCK = 512

RET_QK = R_HEADS * R_DK
RET_V = R_HEADS * R_DV
NSA_Q = NSA_HEADS * NSA_DH
NSA_KV = NSA_GROUPS * NSA_DH
SPLITS = (RET_QK, RET_QK, RET_V, RET_V, NSA_Q, NSA_KV, NSA_KV, NSA_KV, NSA_KV, NSA_KV, NSA_KV,
          3 * NSA_HEADS, D_MODEL, D_MODEL)

_EXP2_SCALE = (NSA_DH ** -0.5) * float(np.log2(np.e))

LANES = 128
SUBLANES = 8
VMEM_LIMIT = 56 * 1024 * 1024

PROJ_TM = 512
RET_ROWS = 512
ATT_TQ = 256
ATT_TK = 256
ROW_TM = 512
MOE_SPLITS = 2
VT_ROWS = NSA_DH + 2 * SUBLANES
SEL_KW = 2 * NSA_DH
assert ATT_TQ == ATT_TK and SEL_LEN * 2 * SUBLANES >= ATT_TK and WINDOW == 2 * ATT_TK


def _params(*sem):
    return pltpu.CompilerParams(dimension_semantics=sem, vmem_limit_bytes=VMEM_LIMIT)


def _dot(a, b):
    return jnp.dot(a.astype(MXU_DTYPE), b.astype(MXU_DTYPE), preferred_element_type=F32)


def _dot_nt(a, b):
    return lax.dot_general(a.astype(MXU_DTYPE), b.astype(MXU_DTYPE), (((1,), (1,)), ((), ())),
                           preferred_element_type=F32)


def _dot_tn(a, b):
    return lax.dot_general(a.astype(MXU_DTYPE), b.astype(MXU_DTYPE), (((0,), (0,)), ((), ())),
                           preferred_element_type=F32)


def _split3(p):
    hi = p.astype(MXU_DTYPE)
    r1 = p - hi.astype(F32)
    mid = r1.astype(MXU_DTYPE)
    lo = (r1 - mid.astype(F32)).astype(MXU_DTYPE)
    return hi, mid, lo


def _rms(x, g):
    return x * lax.rsqrt(jnp.mean(x * x, axis=-1, keepdims=True) + EPS) * g


_C_RQK = 0
_C_RV = _C_RQK + 2 * RET_QK
_C_RG = _C_RV + RET_V
_C_CKV = _C_RG + RET_V
_C_SK = _C_CKV + 2 * NSA_KV
_C_WK = _C_SK + NSA_KV
_C_NGL = _C_WK + NSA_KV
_C_END = _C_NGL + LANES
_R_NQ = 0
_R_SV = _R_NQ + NSA_Q
_R_WV = _R_SV + NSA_KV
_R_END = _R_WV + NSA_KV


def _proj_kernel(x_ref, g_ref, w_ref, wt_ref, cos_ref, sin_up_ref, sin_dn_ref,
                 qk_ref, rv_ref, rg_ref, ckv_ref, sk_ref, wk_ref, ngl_ref, nqt_ref, svt_ref, wvt_ref):
    hb = _rms(x_ref[...], g_ref[...]).astype(MXU_DTYPE)
    tm = hb.shape[0]

    def mm(off, width):
        return jnp.dot(hb, w_ref[:, off:off + width], preferred_element_type=F32)

    def mm_t(off, height):
        return _dot_nt(wt_ref[off:off + height, :], hb)

    cos = cos_ref[...]
    sin_up = sin_up_ref[...]
    sin_dn = sin_dn_ref[...]
    qk = mm(_C_RQK, 2 * RET_QK)
    for i in range(2 * R_HEADS):
        t = qk[:, i * R_DK:(i + 1) * R_DK]
        r = t * cos + pltpu.roll(t, R_DK - 1, axis=1) * sin_up + pltpu.roll(t, 1, axis=1) * sin_dn
        if i >= R_HEADS:
            r = r * (R_DK ** -0.5)
        qk_ref[:, i * R_DK:(i + 1) * R_DK] = r.astype(qk_ref.dtype)
    rv_ref[...] = mm(_C_RV, RET_V).astype(rv_ref.dtype)
    rg_ref[...] = mm(_C_RG, RET_V)
    ckv_ref[...] = mm(_C_CKV, 2 * NSA_KV)
    sk = mm(_C_SK, NSA_KV).astype(sk_ref.dtype)
    blk = (lax.broadcasted_iota(jnp.int32, (tm, NSA_DH), 0) % ATT_TK) // SEL_LEN
    onehot = jnp.where(lax.broadcasted_iota(jnp.int32, (tm, NSA_DH), 1) == blk, 1.0, 0.0).astype(sk_ref.dtype)
    for g in range(NSA_GROUPS):
        sk_ref[:, g * SEL_KW:g * SEL_KW + NSA_DH] = sk[:, g * NSA_DH:(g + 1) * NSA_DH]
        sk_ref[:, g * SEL_KW + NSA_DH:(g + 1) * SEL_KW] = onehot
    wk_ref[...] = mm(_C_WK, NSA_KV).astype(wk_ref.dtype)
    ngl_ref[...] = mm(_C_NGL, LANES)

    nqt = (mm_t(_R_NQ, NSA_Q) * _EXP2_SCALE).astype(nqt_ref.dtype)
    for g in range(NSA_GROUPS):
        for j in range(tm // ATT_TQ):
            for hh in range(NSA_HPG):
                head = g * NSA_HPG + hh
                nqt_ref[g, j, :, hh * ATT_TQ:(hh + 1) * ATT_TQ] = (
                    nqt[head * NSA_DH:(head + 1) * NSA_DH, j * ATT_TQ:(j + 1) * ATT_TQ])
    for off, out_ref in ((_R_SV, svt_ref), (_R_WV, wvt_ref)):
        vt = mm_t(off, NSA_KV).astype(out_ref.dtype)
        for g in range(NSA_GROUPS):
            for j in range(tm // ATT_TK):
                out_ref[g, j, 0:NSA_DH] = vt[g * NSA_DH:(g + 1) * NSA_DH, j * ATT_TK:(j + 1) * ATT_TK]
                out_ref[g, j, NSA_DH:VT_ROWS] = jnp.ones((VT_ROWS - NSA_DH, ATT_TK), out_ref.dtype)


def _split_w_in(w_in):
    offs = np.cumsum((0,) + SPLITS)
    rq, rk, rv, rg, nq, ck, cv, sk, sv, wk, wv, ngl, ga, gb = [
        w_in[:, offs[i]:offs[i + 1]] for i in range(len(SPLITS))]
    ngl = jnp.pad(ngl, ((0, 0), (0, LANES - ngl.shape[1])))
    w = jnp.concatenate([rq, rk, rv, rg, ck, cv, sk, wk, ngl], axis=1).astype(MXU_DTYPE)
    wt = jnp.concatenate([nq, sv, wv], axis=1).T.astype(MXU_DTYPE)
    w_gates = jnp.concatenate([ga, gb], axis=1).astype(MXU_DTYPE)
    return w, wt, w_gates


def _rope_tables(seq):
    pos = jnp.arange(seq, dtype=F32)
    inv_freq = ROPE_BASE ** (-jnp.arange(0, R_DK, 2, dtype=F32) / R_DK)
    ang = pos[:, None] * inv_freq[None, :]
    cos = jnp.repeat(jnp.cos(ang), 2, axis=1)
    sin = jnp.repeat(jnp.sin(ang), 2, axis=1)
    even = (jnp.arange(R_DK) % 2 == 0)[None, :]
    return cos, jnp.where(even, -sin, 0.0), jnp.where(even, 0.0, sin)


def _proj(x2d, g, w, wt, tables, bsz, seq):
    n = x2d.shape[0]
    tm = PROJ_TM
    nt = seq // tm
    row = lambda i: (i, 0)
    const = lambda i: (0, 0)
    tile = lambda i: (i // nt, 0, i % nt, 0, 0)
    table = pl.BlockSpec((tm, R_DK), lambda i: (i % nt, 0))
    widths = (2 * RET_QK, RET_V, RET_V, 2 * NSA_KV, NSA_GROUPS * SEL_KW, NSA_KV, LANES)
    dtypes = (MXU_DTYPE, MXU_DTYPE, F32, F32, MXU_DTYPE, MXU_DTYPE, F32)
    width = NSA_HPG * ATT_TQ
    vt_shape = jax.ShapeDtypeStruct((bsz, NSA_GROUPS, seq // ATT_TK, VT_ROWS, ATT_TK), MXU_DTYPE)
    vt_spec = pl.BlockSpec((None, NSA_GROUPS, tm // ATT_TK, VT_ROWS, ATT_TK), tile)
    return pl.pallas_call(
        _proj_kernel,
        grid=(n // tm,),
        in_specs=[pl.BlockSpec((tm, D_MODEL), row),
                  pl.BlockSpec((1, D_MODEL), const),
                  pl.BlockSpec((D_MODEL, _C_END), const, pipeline_mode=pl.Buffered(1)),
                  pl.BlockSpec((_R_END, D_MODEL), const, pipeline_mode=pl.Buffered(1)),
                  table, table, table],
        out_specs=[pl.BlockSpec((tm, wd), row) for wd in widths] + [
            pl.BlockSpec((None, NSA_GROUPS, tm // ATT_TQ, NSA_DH, width), tile), vt_spec, vt_spec],
        out_shape=[jax.ShapeDtypeStruct((n, wd), dt) for wd, dt in zip(widths, dtypes)] + [
            jax.ShapeDtypeStruct((bsz, NSA_GROUPS, seq // ATT_TQ, NSA_DH, width), MXU_DTYPE), vt_shape, vt_shape],
        compiler_params=_params("parallel"),
        name="proj",
    )(x2d, g, w, wt, *tables)


def _ret_kernel(qk_ref, v_ref, g_ref, decay_ref, zeta_ref, xi_ref, cd_ref, y_ref, state_ref):
    @pl.when(pl.program_id(1) == 0)
    def _():
        state_ref[...] = jnp.zeros_like(state_ref)

    for c in range(RET_ROWS // R_CHUNK):
        rows = slice(c * R_CHUNK, (c + 1) * R_CHUNK)
        for h in range(R_HEADS):
            q = qk_ref[rows, h * R_DK:(h + 1) * R_DK]
            k = qk_ref[rows, RET_QK + h * R_DK:RET_QK + (h + 1) * R_DK]
            v = v_ref[rows, h * R_DV:(h + 1) * R_DV]
            st = state_ref[h]
            scores = _dot_nt(q, k) * decay_ref[h]
            o = _dot(scores, v) + _dot(q, st) * xi_ref[h]
            o = o * lax.rsqrt(jnp.mean(o * o, axis=-1, keepdims=True) + EPS)
            g = g_ref[rows, h * R_DV:(h + 1) * R_DV]
            y_ref[rows, h * R_DV:(h + 1) * R_DV] = (g * jax.nn.sigmoid(g) * o).astype(y_ref.dtype)
            kz = k.astype(F32) * zeta_ref[h]
            state_ref[h] = st * cd_ref[h] + _dot_tn(kz, v)


def _retention(qk, rv, rg, bsz, seq):
    n = qk.shape[0]
    nt = seq // RET_ROWS
    log_g = jnp.log1p(-jnp.exp2(-5.0 - jnp.arange(R_HEADS, dtype=F32)))
    idx = jnp.arange(R_CHUNK, dtype=F32)
    diff = idx[:, None] - idx[None, :]
    decay = jnp.where(diff >= 0, jnp.exp(log_g[:, None, None] * jnp.maximum(diff, 0.0)), 0.0)
    zeta = jnp.exp(log_g[:, None] * (R_CHUNK - 1.0 - idx)[None, :])[:, :, None]
    xi = jnp.exp(log_g[:, None] * (idx + 1.0)[None, :])[:, :, None]
    cd = jnp.exp(log_g * R_CHUNK)[:, None, None]
    row = lambda b, i: (b * nt + i, 0)
    const3 = lambda b, i: (0, 0, 0)
    return pl.pallas_call(
        _ret_kernel,
        grid=(bsz, nt),
        in_specs=[pl.BlockSpec((RET_ROWS, 2 * RET_QK), row),
                  pl.BlockSpec((RET_ROWS, RET_V), row),
                  pl.BlockSpec((RET_ROWS, RET_V), row),
                  pl.BlockSpec((R_HEADS, R_CHUNK, R_CHUNK), const3),
                  pl.BlockSpec((R_HEADS, R_CHUNK, 1), const3),
                  pl.BlockSpec((R_HEADS, R_CHUNK, 1), const3),
                  pl.BlockSpec((R_HEADS, 1, 1), const3)],
        out_specs=pl.BlockSpec((RET_ROWS, RET_V), row),
        out_shape=jax.ShapeDtypeStruct((n, RET_V), MXU_DTYPE),
        scratch_shapes=[pltpu.VMEM((R_HEADS, R_DK, R_DV), F32)],
        compiler_params=_params("parallel", "arbitrary"),
        name="retention",
    )(qk, rv, rg, decay, zeta, xi, cd)


def _compress_kernel(x_ref, pe_ref, w1_ref, w2_ref, w2t_ref, o_ref, ot_ref, buf_ref, *, seq):
    ncp = seq // CMP_STRIDE
    buf_ref[0:seq, :] = x_ref[...]
    buf_ref[seq:seq + LANES, :] = jnp.zeros((LANES, NSA_DH), F32)
    acc = jnp.zeros((ncp, NSA_DH), F32)
    for l in range(CMP_LEN):
        xl = buf_ref[pl.ds(l, ncp, stride=CMP_STRIDE), :] + pe_ref[l:l + 1, :]
        acc = acc + _dot(xl, w1_ref[l])
    hid = jax.nn.gelu(acc)
    o_ref[...] = _dot(hid, w2_ref[...]).astype(o_ref.dtype)
    ot_ref[...] = _dot_nt(w2t_ref[...], hid).astype(ot_ref.dtype)


def _compress(ckv, pe, w1, w2, w2t, bsz, seq):
    ncp = seq // CMP_STRIDE
    nj = 2 * NSA_GROUPS
    wsel = lambda b, j: (j // NSA_GROUPS, 0, 0)
    return pl.pallas_call(
        functools.partial(_compress_kernel, seq=seq),
        grid=(bsz, nj),
        in_specs=[pl.BlockSpec((seq, NSA_DH), lambda b, j: (b, j)),
                  pl.BlockSpec((None, CMP_LEN, NSA_DH), wsel),
                  pl.BlockSpec((None, CMP_LEN, NSA_DH, NSA_DH), lambda b, j: (j // NSA_GROUPS, 0, 0, 0)),
                  pl.BlockSpec((None, NSA_DH, NSA_DH), wsel),
                  pl.BlockSpec((None, NSA_DH, NSA_DH), wsel)],
        out_specs=[pl.BlockSpec((None, None, ncp, NSA_DH), lambda b, j: (b, j, 0, 0)),
                   pl.BlockSpec((None, None, NSA_DH, ncp), lambda b, j: (b, j, 0, 0))],
        out_shape=[jax.ShapeDtypeStruct((bsz, nj, ncp, NSA_DH), MXU_DTYPE),
                   jax.ShapeDtypeStruct((bsz, nj, NSA_DH, ncp), MXU_DTYPE)],
        scratch_shapes=[pltpu.VMEM((seq + LANES, NSA_DH), F32)],
        compiler_params=_params("parallel", "parallel"),
        name="compress",
    )(ckv, pe, w1, w2, w2t)


def _cmp_attn_kernel(qt_ref, k_ref, vt_ref, ov_ref, o_ref, selt_ref, p_ref, *, ncp, nb):
    tq = ATT_TQ
    width = NSA_HPG * tq
    groups = range(NSA_GROUPS)
    qi = pl.program_id(1)
    t0 = qi * tq
    any_valid = (t0 + lax.broadcasted_iota(jnp.int32, (1, tq), 1)) >= CMP_LEN - 1
    any_valid = jnp.concatenate([any_valid] * NSA_HPG, axis=1)

    def probabilities(rows):
        t = t0 + lax.broadcasted_iota(jnp.int32, (rows, tq), 1)
        n = lax.broadcasted_iota(jnp.int32, (rows, tq), 0)
        valid = (n * CMP_STRIDE + (CMP_LEN - 1) <= t) & (n < ncp - 1)
        valid = jnp.concatenate([valid] * NSA_HPG, axis=1)
        for g in groups:
            s = jnp.where(valid, _dot(k_ref[g, 0:rows, :], qt_ref[g]), NEG_INF)
            e = jnp.exp2(s - jnp.max(s, axis=0, keepdims=True))
            p_ref[g, 0:rows, :] = e * jnp.where(any_valid, 1.0 / jnp.sum(e, axis=0, keepdims=True), 0.0)
            if rows < ncp:
                p_ref[g, rows:, :] = jnp.zeros((ncp - rows, width), F32)

    n_buckets = 4
    per_bucket = ncp // n_buckets
    visible = (t0 + tq - CMP_LEN) // CMP_STRIDE + 1
    bucket = jnp.minimum((visible + per_bucket - 1) // per_bucket, n_buckets)
    for c in range(1, n_buckets + 1):
        pl.when(bucket == c)(functools.partial(probabilities, c * per_bucket))

    j = lax.broadcasted_iota(jnp.int32, (nb, tq), 0)
    tb = (t0 + lax.broadcasted_iota(jnp.int32, (nb, tq), 1)) // SEL_LEN
    forced = (j == 0) | (j == tb) | (j == tb - 1)
    sub = SUBLANES
    k_sel = min(SEL_TOPK, nb)
    for g in groups:
        p = p_ref[g]
        ot = _dot(vt_ref[g], p)
        for h in range(NSA_HPG):
            o_ref[:, (g * NSA_HPG + h) * NSA_DH:(g * NSA_HPG + h + 1) * NSA_DH] = ot[:, h * tq:(h + 1) * tq].T
        psum = sum(p[:, h * tq:(h + 1) * tq] for h in range(NSA_HPG))

        imp = sum(_dot(ov_ref[...], part) for part in _split3(psum))
        imp = jnp.where(j > tb, -SEL_FORCE, jnp.where(forced, SEL_FORCE, imp))

        grp = [imp[r * sub:(r + 1) * sub] for r in range(nb // sub)]
        cnt = [jnp.zeros((sub, tq), F32) for _ in grp]
        for i in range(nb):
            row = jnp.broadcast_to(imp[i:i + 1, :], (sub, tq))
            for r in range(nb // sub):
                if r * sub > i:
                    beats = jnp.where(row >= grp[r], 1.0, 0.0)
                elif r * sub + sub - 1 < i:
                    beats = jnp.where(row > grp[r], 1.0, 0.0)
                else:
                    jr = r * sub + lax.broadcasted_iota(jnp.int32, (sub, tq), 0)
                    beats = jnp.where(jr > i, jnp.where(row >= grp[r], 1.0, 0.0), jnp.where(row > grp[r], 1.0, 0.0))
                cnt[r] = cnt[r] + beats
        for r in range(nb // sub):
            selt_ref[g, r * sub:(r + 1) * sub, :] = jnp.where(cnt[r] < k_sel, 1.0, 0.0)


def _cmp_attention(qt, cmp_k, cmp_vt, bsz, seq):
    n = bsz * seq
    ncp = seq // CMP_STRIDE
    nb = seq // SEL_LEN
    nt = seq // ATT_TQ
    cstart = np.arange(ncp) * CMP_STRIDE
    jstart = np.arange(nb) * SEL_LEN
    ov = ((cstart[None, :] < jstart[:, None] + SEL_LEN) & (cstart[None, :] + CMP_LEN > jstart[:, None])
          & (np.arange(ncp)[None, :] < ncp - 1))
    ov = jnp.asarray(ov, MXU_DTYPE)
    return pl.pallas_call(
        functools.partial(_cmp_attn_kernel, ncp=ncp, nb=nb),
        grid=(bsz, nt),
        in_specs=[pl.BlockSpec((None, NSA_GROUPS, None, NSA_DH, NSA_HPG * ATT_TQ), lambda b, i: (b, 0, i, 0, 0)),
                  pl.BlockSpec((None, NSA_GROUPS, ncp, NSA_DH), lambda b, i: (b, 0, 0, 0)),
                  pl.BlockSpec((None, NSA_GROUPS, NSA_DH, ncp), lambda b, i: (b, 1, 0, 0)),
                  pl.BlockSpec((nb, ncp), lambda b, i: (0, 0))],
        out_specs=[pl.BlockSpec((ATT_TQ, NSA_Q), lambda b, i: (b * nt + i, 0)),
                   pl.BlockSpec((None, NSA_GROUPS, nb, ATT_TQ), lambda b, i: (b, 0, 0, i))],
        out_shape=[jax.ShapeDtypeStruct((n, NSA_Q), F32),
                   jax.ShapeDtypeStruct((bsz, NSA_GROUPS, nb, seq), F32)],
        scratch_shapes=[pltpu.VMEM((NSA_GROUPS, ncp, NSA_HPG * ATT_TQ), F32)],
        compiler_params=_params("parallel", "parallel"),
        name="cmp_attention",
    )(qt, cmp_k, cmp_vt, ov)


def _softmax_step(s, m, vt, acc_ref):
    m_new = jnp.maximum(m, jnp.max(s, axis=0, keepdims=True))
    p = jnp.exp2(s - m_new)
    acc_ref[...] = jnp.exp2(m - m_new) * acc_ref[...] + _dot(vt, p)
    return m_new


def _flash_finish(o_ref, acc_ref, col0=0):
    inv = 1.0 / acc_ref[NSA_DH:NSA_DH + 1, :]
    for h in range(NSA_HPG):
        cols = slice(h * ATT_TQ, (h + 1) * ATT_TQ)
        o_ref[:, col0 + h * NSA_DH:col0 + (h + 1) * NSA_DH] = (acc_ref[0:NSA_DH, cols] * inv[:, cols]).T


def _sel_attn_kernel(qt_ref, k_ref, vt_ref, selt_ref, o_ref, acc_ref, s_ref, qa_ref):
    qi = pl.program_id(1)
    width = NSA_HPG * ATT_TQ
    groups = range(NSA_GROUPS)
    blocks_per_tile = ATT_TK // SEL_LEN
    bias_rows = 2 * SUBLANES
    n_loop = qi * (ATT_TQ // ATT_TK)
    n_steps = (n_loop + 1) // 2 * 2
    tile_of = lambda step: jnp.where(step < n_loop, step, qi)
    rowid = lax.broadcasted_iota(jnp.int32, (bias_rows, ATT_TQ), 0)
    tri = (lax.broadcasted_iota(jnp.int32, (ATT_TK, ATT_TQ), 0) <=
           lax.broadcasted_iota(jnp.int32, (ATT_TK, ATT_TQ), 1))
    tri = jnp.concatenate([tri] * NSA_HPG, axis=1)

    def scores(g, step):
        kt = tile_of(step)
        threshold = jnp.where((step < n_loop) | (step == n_steps), 0.5, 2.0)
        bias = jnp.zeros((bias_rows, ATT_TQ), F32)
        for jb in range(blocks_per_tile):
            picked = selt_ref[g, pl.ds(kt * blocks_per_tile + jb, 1), :] > threshold
            bias = jnp.where(rowid == jb, jnp.where(picked, 0.0, NEG_INF), bias)
        qa_ref[g, NSA_DH:NSA_DH + bias_rows, :] = jnp.concatenate([bias] * NSA_HPG, axis=1).astype(qa_ref.dtype)
        k = k_ref[pl.ds(pl.multiple_of(kt * ATT_TK, ATT_TK), ATT_TK), g * SEL_KW:(g + 1) * SEL_KW]
        return _dot(k, qa_ref[g])

    def half_step(g, step, cur, nxt, m, diagonal=False):
        if not diagonal:
            s_ref[g, nxt] = scores(g, step + 1)
        s = jnp.where(tri, s_ref[g, cur], NEG_INF) if diagonal else s_ref[g, cur]
        return _softmax_step(s, m, vt_ref[g, tile_of(step)], acc_ref.at[g])

    def body(i, ms):
        ms = [half_step(g, 2 * i, 0, 1, ms[g]) for g in groups]
        return tuple(half_step(g, 2 * i + 1, 1, 0, ms[g]) for g in groups)

    for g in groups:
        qa_ref[g, 0:NSA_DH, :] = qt_ref[g]
        qa_ref[g, NSA_DH + bias_rows:, :] = jnp.zeros((SEL_KW - NSA_DH - bias_rows, width), qa_ref.dtype)
        acc_ref[g] = jnp.zeros(acc_ref.shape[1:], F32)
        s_ref[g, 0] = scores(g, 0)
    ms = lax.fori_loop(0, n_steps // 2, body, tuple(jnp.full((1, width), NEG_INF, F32) for _ in groups))
    for g in groups:
        half_step(g, n_steps, 0, 1, ms[g], diagonal=True)
        _flash_finish(o_ref, acc_ref.at[g], g * NSA_HPG * NSA_DH)


def _selected_attention(qt, k, vt, selt, bsz, seq):
    n = bsz * seq
    nt = seq // ATT_TQ
    nkt = seq // ATT_TK
    nb = seq // SEL_LEN
    width = NSA_HPG * ATT_TQ
    return pl.pallas_call(
        _sel_attn_kernel,
        grid=(bsz, nt),
        in_specs=[pl.BlockSpec((None, NSA_GROUPS, None, NSA_DH, width), lambda b, i: (b, 0, i, 0, 0)),
                  pl.BlockSpec((seq, NSA_GROUPS * SEL_KW), lambda b, i: (b, 0)),
                  pl.BlockSpec((None, NSA_GROUPS, nkt, VT_ROWS, ATT_TK), lambda b, i: (b, 0, 0, 0, 0)),
                  pl.BlockSpec((None, NSA_GROUPS, nb, ATT_TQ), lambda b, i: (b, 0, 0, i))],
        out_specs=pl.BlockSpec((ATT_TQ, NSA_Q), lambda b, i: (b * nt + i, 0)),
        out_shape=jax.ShapeDtypeStruct((n, NSA_Q), F32),
        scratch_shapes=[pltpu.VMEM((NSA_GROUPS, VT_ROWS, width), F32),
                        pltpu.VMEM((NSA_GROUPS, 2, ATT_TK, width), F32),
                        pltpu.VMEM((NSA_GROUPS, SEL_KW, width), MXU_DTYPE)],
        compiler_params=_params("parallel", "parallel"),
        name="selected_attention",
    )(qt, k, vt, selt)


def _win_attn_kernel(qt_ref, k_ref, vt_ref, o_ref, acc_ref, s0_ref, s1_ref, m_ref):
    qi = pl.program_id(1)
    groups = range(NSA_GROUPS)
    row = lax.broadcasted_iota(jnp.int32, (ATT_TK, ATT_TQ), 0)
    col = lax.broadcasted_iota(jnp.int32, (ATT_TK, ATT_TQ), 1)
    causal = jnp.concatenate([row <= col] * NSA_HPG, axis=1)
    window_tail = jnp.concatenate([row > col] * NSA_HPG, axis=1)

    def scores(g, kt):
        k = k_ref[pl.ds(pl.multiple_of(kt * ATT_TK, ATT_TK), ATT_TK), g * NSA_DH:(g + 1) * NSA_DH]
        return _dot(k, qt_ref[g])

    for g in groups:
        acc_ref[g] = jnp.zeros(acc_ref.shape[1:], F32)
        s0_ref[g] = scores(g, qi)
    for g in groups:
        s1_ref[g] = scores(g, jnp.maximum(qi - 1, 0))
        m_ref[g] = _softmax_step(jnp.where(causal, s0_ref[g], NEG_INF), jnp.full(m_ref.shape[1:], NEG_INF, F32),
                                 vt_ref[g, qi], acc_ref.at[g])

    @pl.when(qi >= 1)
    def _():
        for g in groups:
            s0_ref[g] = scores(g, jnp.maximum(qi - 2, 0))
            m_ref[g] = _softmax_step(s1_ref[g], m_ref[g], vt_ref[g, qi - 1], acc_ref.at[g])

    @pl.when(qi >= 2)
    def _():
        for g in groups:
            _softmax_step(jnp.where(window_tail, s0_ref[g], NEG_INF), m_ref[g], vt_ref[g, qi - 2], acc_ref.at[g])

    for g in groups:
        _flash_finish(o_ref, acc_ref.at[g], g * NSA_HPG * NSA_DH)


def _window_attention(qt, k, vt, bsz, seq):
    n = bsz * seq
    nt = seq // ATT_TQ
    nkt = seq // ATT_TK
    width = NSA_HPG * ATT_TQ
    return pl.pallas_call(
        _win_attn_kernel,
        grid=(bsz, nt),
        in_specs=[pl.BlockSpec((None, NSA_GROUPS, None, NSA_DH, width), lambda b, i: (b, 0, i, 0, 0)),
                  pl.BlockSpec((seq, NSA_KV), lambda b, i: (b, 0)),
                  pl.BlockSpec((None, NSA_GROUPS, nkt, VT_ROWS, ATT_TK), lambda b, i: (b, 0, 0, 0, 0))],
        out_specs=pl.BlockSpec((ATT_TQ, NSA_Q), lambda b, i: (b * nt + i, 0)),
        out_shape=jax.ShapeDtypeStruct((n, NSA_Q), F32),
        scratch_shapes=[pltpu.VMEM((NSA_GROUPS, VT_ROWS, width), F32),
                        pltpu.VMEM((NSA_GROUPS, ATT_TK, width), F32),
                        pltpu.VMEM((NSA_GROUPS, ATT_TK, width), F32),
                        pltpu.VMEM((NSA_GROUPS, 1, width), F32)],
        compiler_params=_params("parallel", "parallel"),
        name="window_attention",
    )(qt, k, vt)


def _merge_kernel(x_ref, yr_ref, oc_ref, os_ref, ow_ref, ngl_ref, g_ref, wg_ref, wr_ref, wn_ref, wo_ref, o_ref):
    tm = x_ref.shape[0]
    hb = _rms(x_ref[...], g_ref[...]).astype(MXU_DTYPE)
    ga = jnp.dot(hb, wg_ref[:, :D_MODEL], preferred_element_type=F32)
    gb = jnp.dot(hb, wg_ref[:, D_MODEL:], preferred_element_type=F32)
    gates = jax.nn.sigmoid(ngl_ref[...])
    parts = []
    for h in range(NSA_HEADS):
        cols = slice(h * NSA_DH, (h + 1) * NSA_DH)

        def gate(br):
            return jnp.broadcast_to(gates[:, 3 * h + br:3 * h + br + 1], (tm, NSA_DH))

        parts.append(gate(0) * oc_ref[:, cols] + gate(1) * os_ref[:, cols] + gate(2) * ow_ref[:, cols])
    o_nsa = jnp.concatenate(parts, axis=1)
    y_ret = _dot(yr_ref[...], wr_ref[...])
    y_nsa = _dot(o_nsa, wn_ref[...])
    y = jax.nn.sigmoid(ga) * y_ret + jax.nn.sigmoid(gb) * y_nsa
    o_ref[...] = x_ref[...] + _dot(y, wo_ref[...])


def _merge(x2d, y_ret, o_cmp, o_sel, o_win, ngl, g_mix, w_gates, w_ret_o, w_nsa_o, w_out):
    n = x2d.shape[0]
    tm = ROW_TM
    row = lambda i: (i, 0)
    const = lambda i: (0, 0)
    wide = pl.BlockSpec((tm, D_MODEL), row)
    wspec = pl.BlockSpec((D_MODEL, D_MODEL), const)
    return pl.pallas_call(
        _merge_kernel,
        grid=(n // tm,),
        in_specs=[wide, wide, wide, wide, wide,
                  pl.BlockSpec((tm, LANES), row),
                  pl.BlockSpec((1, D_MODEL), const),
                  pl.BlockSpec((D_MODEL, 2 * D_MODEL), const),
                  wspec, wspec, wspec],
        out_specs=wide,
        out_shape=jax.ShapeDtypeStruct((n, D_MODEL), F32),
        compiler_params=_params("parallel"),
        name="merge",
    )(x2d, y_ret, o_cmp, o_sel, o_win, ngl, g_mix, w_gates, w_ret_o, w_nsa_o, w_out)


def _mem_kv_kernel(m_ref, g_ref, w_ref, o_ref):
    o_ref[...] = _dot(_rms(m_ref[...], g_ref[...]), w_ref[...]).astype(o_ref.dtype)


def _mem_kv(mem2d, g, w_xkv, bsz):
    nm = mem2d.shape[0] // bsz
    return pl.pallas_call(
        _mem_kv_kernel,
        grid=(bsz,),
        in_specs=[pl.BlockSpec((nm, D_MODEL), lambda b: (b, 0)),
                  pl.BlockSpec((1, D_MODEL), lambda b: (0, 0)),
                  pl.BlockSpec((D_MODEL, 2 * D_MODEL), lambda b: (0, 0))],
        out_specs=pl.BlockSpec((nm, 2 * D_MODEL), lambda b: (b, 0)),
        out_shape=jax.ShapeDtypeStruct((mem2d.shape[0], 2 * D_MODEL), MXU_DTYPE),
        compiler_params=_params("parallel"),
        name="mem_kv",
    )(mem2d, g, w_xkv)


def _pack_pairs(x):
    half = x.shape[1] // 2
    hi = lax.bitcast_convert_type(x[:, :half].astype(MXU_DTYPE).astype(F32), jnp.uint32)
    lo = lax.bitcast_convert_type(x[:, half:].astype(MXU_DTYPE).astype(F32), jnp.uint32)
    return (hi & jnp.uint32(0xFFFF0000)) | (lo >> 16)


def _unpack_pairs(u):
    hi = lax.bitcast_convert_type(u & jnp.uint32(0xFFFF0000), F32)
    lo = lax.bitcast_convert_type(u << 16, F32)
    return jnp.concatenate([hi, lo], axis=1)


_ROUTER_E0 = 2 * SUBLANES
_ROUTER_ROWS = _ROUTER_E0 + N_EXPERTS


def _top2_route(lgt):
    sub = SUBLANES
    t = lgt.shape[1]
    rowid = lax.broadcasted_iota(jnp.int32, (sub, t), 0)
    first = lambda hit: jnp.min(jnp.where(hit, rowid, sub), axis=0, keepdims=True)
    lg = jnp.where(rowid < N_EGROUPS, lgt[0:sub], NEG_INF)
    gmax = jnp.max(lg, axis=0, keepdims=True)
    grp = first(lg == gmax)
    g_gate = 1.0 / jnp.sum(jnp.exp(lg - gmax), axis=0, keepdims=True)
    experts_of = lambda g: lgt[_ROUTER_E0 + g * EXP_PER_GROUP:_ROUTER_E0 + (g + 1) * EXP_PER_GROUP]
    le = experts_of(N_EGROUPS - 1)
    for g in range(N_EGROUPS - 2, -1, -1):
        le = jnp.where(grp == g, experts_of(g), le)
    ex = jnp.exp(le - jnp.max(le, axis=0, keepdims=True))
    pe = ex / jnp.sum(ex, axis=0, keepdims=True)
    p0 = jnp.max(pe, axis=0, keepdims=True)
    i0 = first(pe == p0)
    rest = jnp.where(rowid == i0, -1.0, pe)
    p1 = jnp.max(rest, axis=0, keepdims=True)
    i1 = first(rest == p1)
    den = p0 + p1
    base = grp * EXP_PER_GROUP
    return jnp.concatenate([(base + i0).astype(F32), (base + i1).astype(F32),
                            g_gate * p0 / den, g_gate * p1 / den], axis=0)


def _cross_kernel(x_ref, kv_ref, gx_ref, wq_ref, wo_ref, gf_ref, wr_ref, br_ref, x2_ref, hf_ref, rt_ref):
    x = x_ref[...]
    q = _dot(_rms(x, gx_ref[...]), wq_ref[...])
    heads = []
    for h in range(X_HEADS):
        k = kv_ref[:, h * X_DH:(h + 1) * X_DH]
        v = kv_ref[:, D_MODEL + h * X_DH:D_MODEL + (h + 1) * X_DH]
        s = _dot_nt(q[:, h * X_DH:(h + 1) * X_DH], k) * (X_DH ** -0.5)
        e = jnp.exp(s - jnp.max(s, axis=-1, keepdims=True))
        p = e / jnp.sum(e, axis=-1, keepdims=True)
        heads.append(_dot(p, v))
    x2 = x + _dot(jnp.concatenate(heads, axis=1), wo_ref[...])
    x2_ref[...] = x2
    hf = _rms(x2, gf_ref[...])
    hf_ref[...] = _pack_pairs(hf)
    h_hi, h_mid, _ = _split3(hf)
    w_hi = wr_ref[0]
    w_mid = wr_ref[1]
    lgt = (_dot_nt(w_hi, h_hi) + (_dot_nt(w_hi, h_mid) + _dot_nt(w_mid, h_hi))) + br_ref[...]
    rt_ref[...] = jnp.concatenate([_top2_route(lgt), jnp.zeros((SUBLANES - 4, lgt.shape[1]), F32)], axis=0)


def _cross(x1, kv, gx, w_xq, w_xo, gf, w_router, b_router, bsz, seq):
    n = x1.shape[0]
    tm = ROW_TM
    nt = seq // tm
    nm = kv.shape[0] // bsz
    row = lambda i: (i, 0)
    const = lambda i: (0, 0)
    vec = pl.BlockSpec((1, D_MODEL), const)
    wspec = pl.BlockSpec((D_MODEL, D_MODEL), const)
    return pl.pallas_call(
        _cross_kernel,
        grid=(n // tm,),
        in_specs=[pl.BlockSpec((tm, D_MODEL), row),
                  pl.BlockSpec((nm, 2 * D_MODEL), lambda i: (i // nt, 0)),
                  vec, wspec, wspec, vec,
                  pl.BlockSpec((2, _ROUTER_ROWS, D_MODEL), lambda i: (0, 0, 0)),
                  pl.BlockSpec((_ROUTER_ROWS, 1), const)],
        out_specs=[pl.BlockSpec((tm, D_MODEL), row),
                   pl.BlockSpec((tm, D_MODEL // 2), row),
                   pl.BlockSpec((SUBLANES, tm), lambda i: (0, i))],
        out_shape=[jax.ShapeDtypeStruct((n, D_MODEL), F32),
                   jax.ShapeDtypeStruct((n, D_MODEL // 2), jnp.uint32),
                   jax.ShapeDtypeStruct((SUBLANES, n), F32)],
        compiler_params=_params("parallel"),
        name="cross_attention",
    )(x1, kv, gx, w_xq, w_xo, gf, w_router, b_router)


def _expert_kernel(blk_ref, xb_ref, w1_ref, w3_ref, w2_ref, o_ref):
    n_used = blk_ref[pl.num_programs(0)]

    @pl.when(pl.program_id(0) < n_used)
    def _():
        xb = _unpack_pairs(xb_ref[...]).astype(MXU_DTYPE)
        a = _dot(xb, w1_ref[...])
        hmid = a * jax.nn.sigmoid(a) * _dot(xb, w3_ref[...])
        o_ref[...] = _dot(hmid, w2_ref[...])

    @pl.when(pl.program_id(0) >= n_used)
    def _():
        o_ref[...] = jnp.zeros_like(o_ref)


def _experts(blk, xb, w1, w3, w2):
    cap = xb.shape[0]
    nblk = cap // MOE_BLOCK
    row = lambda i, e: (i, 0)
    by_expert = lambda i, e: (e[i], 0, 0)
    grid_spec = pltpu.PrefetchScalarGridSpec(
        num_scalar_prefetch=1,
        grid=(nblk,),
        in_specs=[pl.BlockSpec((MOE_BLOCK, D_MODEL // 2), row),
                  pl.BlockSpec((None, D_MODEL, D_EXPERT), by_expert),
                  pl.BlockSpec((None, D_MODEL, D_EXPERT), by_expert),
                  pl.BlockSpec((None, D_EXPERT, D_MODEL), by_expert)],
        out_specs=pl.BlockSpec((MOE_BLOCK, D_MODEL), row),
    )
    return pl.pallas_call(
        _expert_kernel,
        grid_spec=grid_spec,
        out_shape=jax.ShapeDtypeStruct((cap, D_MODEL), F32),
        compiler_params=_params("arbitrary"),
        name="experts",
    )(blk, xb, w1, w3, w2)


def _final_kernel(x_ref, w_ref, g_ref, *refs):
    o_ref = refs[-1]
    tiles_per_split = pl.num_programs(0) // MOE_SPLITS
    for s in range(MOE_SPLITS):
        @pl.when(pl.program_id(0) // tiles_per_split == s)
        def _(s=s):
            moe = w_ref[:, 0:1] * refs[2 * s][...] + w_ref[:, 1:2] * refs[2 * s + 1][...]
            o_ref[...] = _rms(x_ref[...] + moe, g_ref[...])


def _final(x2, wts, g, ys):
    n = x2.shape[0]
    tm = ROW_TM
    tiles_per_split = n // tm // MOE_SPLITS
    row = lambda i: (i, 0)
    wide = pl.BlockSpec((tm, D_MODEL), row)

    def split_spec(s):
        return pl.BlockSpec((tm, D_MODEL), lambda i: (jnp.clip(i - s * tiles_per_split, 0, tiles_per_split - 1), 0))

    return pl.pallas_call(
        _final_kernel,
        grid=(n // tm,),
        in_specs=[wide, pl.BlockSpec((tm, EXP_TOPK), row), pl.BlockSpec((1, D_MODEL), lambda i: (0, 0))] + [
            split_spec(s) for s in range(MOE_SPLITS) for _ in range(EXP_TOPK)],
        out_specs=wide,
        out_shape=jax.ShapeDtypeStruct((n, D_MODEL), F32),
        compiler_params=_params("parallel"),
        name="final_norm",
    )(x2, wts, g, *[y for pair in ys for y in pair])


def _route(eid, tok0):
    n_tok = eid.shape[1]
    eid = eid.reshape(-1)
    n_asg = eid.shape[0]
    iota = jnp.arange(n_asg, dtype=jnp.int32)
    _, order = lax.sort_key_val(eid, iota)
    counts = jnp.sum((jnp.arange(N_EXPERTS, dtype=jnp.int32)[:, None] == eid[None, :]).astype(jnp.int32), axis=1)
    padded = (counts + MOE_BLOCK - 1) // MOE_BLOCK * MOE_BLOCK
    starts = jnp.cumsum(counts) - counts
    pends = jnp.cumsum(padded)
    pstarts = pends - padded
    cap = ((n_asg + MOE_BLOCK - 1) // MOE_BLOCK + N_EXPERTS) * MOE_BLOCK
    nblk = cap // MOE_BLOCK
    blk_e = jnp.minimum(jnp.searchsorted(pends, jnp.arange(nblk) * MOE_BLOCK, side='right', method='compare_all'),
                        N_EXPERTS - 1).astype(jnp.int32)
    per_row = lambda a: jnp.repeat(a[blk_e], MOE_BLOCK)
    row = jnp.arange(cap, dtype=jnp.int32)
    off = row - per_row(pstarts)
    run0 = jnp.clip(starts[blk_e] + jnp.arange(nblk, dtype=jnp.int32) * MOE_BLOCK - pstarts[blk_e], 0, n_asg)
    order_padded = jnp.concatenate([order, jnp.zeros((MOE_BLOCK,), jnp.int32)])
    asg = jax.vmap(lambda s: lax.dynamic_slice(order_padded, (s,), (MOE_BLOCK,)))(run0).reshape(-1)
    buf_tok = tok0 + jnp.where(off < per_row(counts), asg % n_tok, row % n_tok)
    shift = pstarts - starts
    jumps = jnp.zeros((n_asg,), jnp.int32).at[starts[1:]].add(shift[1:] - shift[:-1])
    dest_sorted = iota + shift[0] + jnp.cumsum(jumps)
    _, pos = lax.sort_key_val(order, dest_sorted)
    n_used = (pends[-1] // MOE_BLOCK).astype(jnp.int32)
    return buf_tok, jnp.concatenate([blk_e, n_used[None]]), pos.reshape(EXP_TOPK, n_tok)


def kernel(x, mem, norm_mix_g, w_in, w_ret_o, w_nsa_o, w_out, cmp_pe_k, cmp_w1_k, cmp_w2_k, cmp_pe_v,
           cmp_w1_v, cmp_w2_v, norm_x_g, norm_mem_g, w_xq, w_xkv, w_xo, norm_ffn_g, w_grp, b_grp, w_rt,
           b_rt, w_e1, w_e3, w_e2, norm_f_g):
    bsz, seq, _ = x.shape
    n = bsz * seq
    assert seq % PROJ_TM == 0 and seq % (2 * ATT_TQ) == 0 and w_in.shape[0] == 1
    assert n % (ROW_TM * MOE_SPLITS) == 0
    cast = lambda a: a.astype(MXU_DTYPE)
    xc = x.reshape(n, D_MODEL)
    l = 0

    w_main, w_t, w_gates = _split_w_in(w_in[l])
    qk, rv, rg, ckv, sk, wk, ngl, qt, svt, wvt = _proj(
        xc, norm_mix_g[l][None, :], w_main, w_t, _rope_tables(seq), bsz, seq)
    y_ret = _retention(qk, rv, rg, bsz, seq)
    w2 = jnp.stack([cmp_w2_k[l], cmp_w2_v[l]])
    cmp_k, cmp_vt = _compress(ckv, jnp.stack([cmp_pe_k[l], cmp_pe_v[l]]),
                              cast(jnp.stack([cmp_w1_k[l], cmp_w1_v[l]])),
                              cast(w2), cast(w2.transpose(0, 2, 1)), bsz, seq)
    o_cmp, selt = _cmp_attention(qt, cmp_k, cmp_vt, bsz, seq)
    o_sel = _selected_attention(qt, sk, svt, selt, bsz, seq)
    o_win = _window_attention(qt, wk, wvt, bsz, seq)
    x1 = _merge(xc, y_ret, o_cmp, o_sel, o_win, ngl, norm_mix_g[l][None, :], w_gates,
                cast(w_ret_o[l]), cast(w_nsa_o[l]), cast(w_out[l]))

    kv = _mem_kv(mem.reshape(-1, D_MODEL), norm_mem_g[l][None, :], cast(w_xkv[l]), bsz)
    gap = _ROUTER_E0 - N_EGROUPS
    w_router = jnp.concatenate([w_grp[l].T, jnp.zeros((gap, D_MODEL), F32), w_rt[l].T], axis=0)
    wr_hi = cast(w_router)
    wr_mid = cast(w_router - wr_hi.astype(F32))
    b_router = jnp.concatenate([b_grp[l], jnp.zeros((gap,), F32), b_rt[l]])[:, None]
    x2, hf, routed = _cross(x1, kv, norm_x_g[l][None, :], cast(w_xq[l]), cast(w_xo[l]),
                            norm_ffn_g[l][None, :], jnp.stack([wr_hi, wr_mid]), b_router, bsz, seq)

    eid = routed[0:EXP_TOPK].astype(jnp.int32)
    wts = routed[EXP_TOPK:2 * EXP_TOPK].T
    per_split = n // MOE_SPLITS
    ys = []
    for s in range(MOE_SPLITS):
        buf_tok, blk, pos = _route(eid[:, s * per_split:(s + 1) * per_split], s * per_split)
        y = _experts(blk, hf[buf_tok], w_e1[l], w_e3[l], w_e2[l])
        ys.append([y[pos[j]] for j in range(EXP_TOPK)])
    out = _final(x2, wts, norm_f_g[None, :], ys)
    return out.reshape(bsz, seq, D_MODEL)
```

```python
import functools

import numpy as np
import jax
import jax.numpy as jnp
from jax import lax
from jax.experimental import pallas as pl
from jax.experimental.pallas import tpu as pltpu

MXU_DTYPE = jnp.bfloat16
F32 = jnp.float32

D_MODEL = 1024
N_MEM = 256
EPS = 1e-6
NEG_INF = -1e30
SEL_FORCE = 1e4

R_HEADS = 4
R_DK = 128
R_DV = 256
R_CHUNK = 128
ROPE_BASE = 10000.0

NSA_HEADS = 8
NSA_GROUPS = 2
NSA_HPG = NSA_HEADS // NSA_GROUPS
NSA_DH = 128
CMP_LEN = 32
CMP_STRIDE = 16
SEL_LEN = 64
SEL_TOPK = 16
WINDOW = 512

X_HEADS = 4
X_DH = D_MODEL // X_HEADS

N_EGROUPS = 4
EXP_PER_GROUP = 8
N_EXPERTS = N_EGROUPS * EXP_PER_GROUP
EXP_TOPK = 2
D_EXPERT = 512
MOE_BLOCK = 512

RET_QK = R_HEADS * R_DK
RET_V = R_HEADS * R_DV
NSA_Q = NSA_HEADS * NSA_DH
NSA_KV = NSA_GROUPS * NSA_DH
SPLITS = (RET_QK, RET_QK, RET_V, RET_V, NSA_Q, NSA_KV, NSA_KV, NSA_KV, NSA_KV, NSA_KV, NSA_KV,
          3 * NSA_HEADS, D_MODEL, D_MODEL)

_EXP2_SCALE = (NSA_DH ** -0.5) * float(np.log2(np.e))

LANES = 128
SUBLANES = 8
VMEM_LIMIT = 56 * 1024 * 1024

PROJ_TM = 512
RET_ROWS = 512
ATT_TQ = 256
ATT_TK = 256
ROW_TM = 512
MOE_SPLITS = 2
VT_ROWS = NSA_DH + 2 * SUBLANES
SEL_KW = 2 * NSA_DH
assert ATT_TQ == ATT_TK and SEL_LEN * 2 * SUBLANES >= ATT_TK and WINDOW == 2 * ATT_TK


def _params(*sem):
    return pltpu.CompilerParams(dimension_semantics=sem, vmem_limit_bytes=VMEM_LIMIT)


def _dot(a, b):
    return jnp.dot(a.astype(MXU_DTYPE), b.astype(MXU_DTYPE), preferred_element_type=F32)


def _dot_nt(a, b):
    return lax.dot_general(a.astype(MXU_DTYPE), b.astype(MXU_DTYPE), (((1,), (1,)), ((), ())),
                           preferred_element_type=F32)


def _dot_tn(a, b):
    return lax.dot_general(a.astype(MXU_DTYPE), b.astype(MXU_DTYPE), (((0,), (0,)), ((), ())),
                           preferred_element_type=F32)


def _split3(p):
    hi = p.astype(MXU_DTYPE)
    r1 = p - hi.astype(F32)
    mid = r1.astype(MXU_DTYPE)
    lo = (r1 - mid.astype(F32)).astype(MXU_DTYPE)
    return hi, mid, lo


def _rms(x, g):
    return x * lax.rsqrt(jnp.mean(x * x, axis=-1, keepdims=True) + EPS) * g


_C_RQK = 0
_C_RV = _C_RQK + 2 * RET_QK
_C_RG = _C_RV + RET_V
_C_CKV = _C_RG + RET_V
_C_SK = _C_CKV + 2 * NSA_KV
_C_WK = _C_SK + NSA_KV
_C_NGL = _C_WK + NSA_KV
_C_END = _C_NGL + LANES
_R_NQ = 0
_R_SV = _R_NQ + NSA_Q
_R_WV = _R_SV + NSA_KV
_R_END = _R_WV + NSA_KV


def _proj_kernel(x_ref, g_ref, w_ref, wt_ref, cos_ref, sin_up_ref, sin_dn_ref,
                 qk_ref, rv_ref, rg_ref, ckv_ref, sk_ref, wk_ref, ngl_ref, nqt_ref, svt_ref, wvt_ref):
    hb = _rms(x_ref[...], g_ref[...]).astype(MXU_DTYPE)
    tm = hb.shape[0]

    def mm(off, width):
        return jnp.dot(hb, w_ref[:, off:off + width], preferred_element_type=F32)

    def mm_t(off, height):
        return _dot_nt(wt_ref[off:off + height, :], hb)

    cos = cos_ref[...]
    sin_up = sin_up_ref[...]
    sin_dn = sin_dn_ref[...]
    qk = mm(_C_RQK, 2 * RET_QK)
    for i in range(2 * R_HEADS):
        t = qk[:, i * R_DK:(i + 1) * R_DK]
        r = t * cos + pltpu.roll(t, R_DK - 1, axis=1) * sin_up + pltpu.roll(t, 1, axis=1) * sin_dn
        if i >= R_HEADS:
            r = r * (R_DK ** -0.5)
        qk_ref[:, i * R_DK:(i + 1) * R_DK] = r.astype(qk_ref.dtype)
    rv_ref[...] = mm(_C_RV, RET_V).astype(rv_ref.dtype)
    rg_ref[...] = mm(_C_RG, RET_V)
    ckv_ref[...] = mm(_C_CKV, 2 * NSA_KV)
    sk = mm(_C_SK, NSA_KV).astype(sk_ref.dtype)
    blk = (lax.broadcasted_iota(jnp.int32, (tm, NSA_DH), 0) % ATT_TK) // SEL_LEN
    onehot = jnp.where(lax.broadcasted_iota(jnp.int32, (tm, NSA_DH), 1) == blk, 1.0, 0.0).astype(sk_ref.dtype)
    for g in range(NSA_GROUPS):
        sk_ref[:, g * SEL_KW:g * SEL_KW + NSA_DH] = sk[:, g * NSA_DH:(g + 1) * NSA_DH]
        sk_ref[:, g * SEL_KW + NSA_DH:(g + 1) * SEL_KW] = onehot
    wk_ref[...] = mm(_C_WK, NSA_KV).astype(wk_ref.dtype)
    ngl_ref[...] = mm(_C_NGL, LANES)

    nqt = (mm_t(_R_NQ, NSA_Q) * _EXP2_SCALE).astype(nqt_ref.dtype)
    for g in range(NSA_GROUPS):
        for j in range(tm // ATT_TQ):
            for hh in range(NSA_HPG):
                head = g * NSA_HPG + hh
                nqt_ref[g, j, :, hh * ATT_TQ:(hh + 1) * ATT_TQ] = (
                    nqt[head * NSA_DH:(head + 1) * NSA_DH, j * ATT_TQ:(j + 1) * ATT_TQ])
    for off, out_ref in ((_R_SV, svt_ref), (_R_WV, wvt_ref)):
        vt = mm_t(off, NSA_KV).astype(out_ref.dtype)
        for g in range(NSA_GROUPS):
            for j in range(tm // ATT_TK):
                out_ref[g, j, 0:NSA_DH] = vt[g * NSA_DH:(g + 1) * NSA_DH, j * ATT_TK:(j + 1) * ATT_TK]
                out_ref[g, j, NSA_DH:VT_ROWS] = jnp.ones((VT_ROWS - NSA_DH, ATT_TK), out_ref.dtype)


def _split_w_in(w_in):
    offs = np.cumsum((0,) + SPLITS)
    rq, rk, rv, rg, nq, ck, cv, sk, sv, wk, wv, ngl, ga, gb = [
        w_in[:, offs[i]:offs[i + 1]] for i in range(len(SPLITS))]
    ngl = jnp.pad(ngl, ((0, 0), (0, LANES - ngl.shape[1])))
    w = jnp.concatenate([rq, rk, rv, rg, ck, cv, sk, wk, ngl], axis=1).astype(MXU_DTYPE)
    wt = jnp.concatenate([nq, sv, wv], axis=1).T.astype(MXU_DTYPE)
    w_gates = jnp.concatenate([ga, gb], axis=1).astype(MXU_DTYPE)
    return w, wt, w_gates


def _rope_tables(seq):
    pos = jnp.arange(seq, dtype=F32)
    inv_freq = ROPE_BASE ** (-jnp.arange(0, R_DK, 2, dtype=F32) / R_DK)
    ang = pos[:, None] * inv_freq[None, :]
    cos = jnp.repeat(jnp.cos(ang), 2, axis=1)
    sin = jnp.repeat(jnp.sin(ang), 2, axis=1)
    even = (jnp.arange(R_DK) % 2 == 0)[None, :]
    return cos, jnp.where(even, -sin, 0.0), jnp.where(even, 0.0, sin)


def _proj(x2d, g, w, wt, tables, bsz, seq):
    n = x2d.shape[0]
    tm = PROJ_TM
    nt = seq // tm
    row = lambda i: (i, 0)
    const = lambda i: (0, 0)
    tile = lambda i: (i // nt, 0, i % nt, 0, 0)
    table = pl.BlockSpec((tm, R_DK), lambda i: (i % nt, 0))
    widths = (2 * RET_QK, RET_V, RET_V, 2 * NSA_KV, NSA_GROUPS * SEL_KW, NSA_KV, LANES)
    dtypes = (MXU_DTYPE, MXU_DTYPE, F32, F32, MXU_DTYPE, MXU_DTYPE, F32)
    width = NSA_HPG * ATT_TQ
    vt_shape = jax.ShapeDtypeStruct((bsz, NSA_GROUPS, seq // ATT_TK, VT_ROWS, ATT_TK), MXU_DTYPE)
    vt_spec = pl.BlockSpec((None, NSA_GROUPS, tm // ATT_TK, VT_ROWS, ATT_TK), tile)
    return pl.pallas_call(
        _proj_kernel,
        grid=(n // tm,),
        in_specs=[pl.BlockSpec((tm, D_MODEL), row),
                  pl.BlockSpec((1, D_MODEL), const),
                  pl.BlockSpec((D_MODEL, _C_END), const, pipeline_mode=pl.Buffered(1)),
                  pl.BlockSpec((_R_END, D_MODEL), const, pipeline_mode=pl.Buffered(1)),
                  table, table, table],
        out_specs=[pl.BlockSpec((tm, wd), row) for wd in widths] + [
            pl.BlockSpec((None, NSA_GROUPS, tm // ATT_TQ, NSA_DH, width), tile), vt_spec, vt_spec],
        out_shape=[jax.ShapeDtypeStruct((n, wd), dt) for wd, dt in zip(widths, dtypes)] + [
            jax.ShapeDtypeStruct((bsz, NSA_GROUPS, seq // ATT_TQ, NSA_DH, width), MXU_DTYPE), vt_shape, vt_shape],
        compiler_params=_params("parallel"),
        name="proj",
    )(x2d, g, w, wt, *tables)


def _ret_kernel(qk_ref, v_ref, g_ref, decay_ref, zeta_ref, xi_ref, cd_ref, y_ref, state_ref):
    @pl.when(pl.program_id(1) == 0)
    def _():
        state_ref[...] = jnp.zeros_like(state_ref)

    for c in range(RET_ROWS // R_CHUNK):
        rows = slice(c * R_CHUNK, (c + 1) * R_CHUNK)
        for h in range(R_HEADS):
            q = qk_ref[rows, h * R_DK:(h + 1) * R_DK]
            k = qk_ref[rows, RET_QK + h * R_DK:RET_QK + (h + 1) * R_DK]
            v = v_ref[rows, h * R_DV:(h + 1) * R_DV]
            st = state_ref[h]
            scores = _dot_nt(q, k) * decay_ref[h]
            o = _dot(scores, v) + _dot(q, st) * xi_ref[h]
            o = o * lax.rsqrt(jnp.mean(o * o, axis=-1, keepdims=True) + EPS)
            g = g_ref[rows, h * R_DV:(h + 1) * R_DV]
            y_ref[rows, h * R_DV:(h + 1) * R_DV] = (g * jax.nn.sigmoid(g) * o).astype(y_ref.dtype)
            kz = k.astype(F32) * zeta_ref[h]
            state_ref[h] = st * cd_ref[h] + _dot_tn(kz, v)


def _retention(qk, rv, rg, bsz, seq):
    n = qk.shape[0]
    nt = seq // RET_ROWS
    log_g = jnp.log1p(-jnp.exp2(-5.0 - jnp.arange(R_HEADS, dtype=F32)))
    idx = jnp.arange(R_CHUNK, dtype=F32)
    diff = idx[:, None] - idx[None, :]
    decay = jnp.where(diff >= 0, jnp.exp(log_g[:, None, None] * jnp.maximum(diff, 0.0)), 0.0)
    zeta = jnp.exp(log_g[:, None] * (R_CHUNK - 1.0 - idx)[None, :])[:, :, None]
    xi = jnp.exp(log_g[:, None] * (idx + 1.0)[None, :])[:, :, None]
    cd = jnp.exp(log_g * R_CHUNK)[:, None, None]
    row = lambda b, i: (b * nt + i, 0)
    const3 = lambda b, i: (0, 0, 0)
    return pl.pallas_call(
        _ret_kernel,
        grid=(bsz, nt),
        in_specs=[pl.BlockSpec((RET_ROWS, 2 * RET_QK), row),
                  pl.BlockSpec((RET_ROWS, RET_V), row),
                  pl.BlockSpec((RET_ROWS, RET_V), row),
                  pl.BlockSpec((R_HEADS, R_CHUNK, R_CHUNK), const3),
                  pl.BlockSpec((R_HEADS, R_CHUNK, 1), const3),
                  pl.BlockSpec((R_HEADS, R_CHUNK, 1), const3),
                  pl.BlockSpec((R_HEADS, 1, 1), const3)],
        out_specs=pl.BlockSpec((RET_ROWS, RET_V), row),
        out_shape=jax.ShapeDtypeStruct((n, RET_V), MXU_DTYPE),
        scratch_shapes=[pltpu.VMEM((R_HEADS, R_DK, R_DV), F32)],
        compiler_params=_params("parallel", "arbitrary"),
        name="retention",
    )(qk, rv, rg, decay, zeta, xi, cd)


def _compress_kernel(x_ref, pe_ref, w1_ref, w2_ref, w2t_ref, o_ref, ot_ref, buf_ref, *, seq):
    ncp = seq // CMP_STRIDE
    buf_ref[0:seq, :] = x_ref[...]
    buf_ref[seq:seq + LANES, :] = jnp.zeros((LANES, NSA_DH), F32)
    acc = jnp.zeros((ncp, NSA_DH), F32)
    for l in range(CMP_LEN):
        xl = buf_ref[pl.ds(l, ncp, stride=CMP_STRIDE), :] + pe_ref[l:l + 1, :]
        acc = acc + _dot(xl, w1_ref[l])
    hid = jax.nn.gelu(acc)
    o_ref[...] = _dot(hid, w2_ref[...]).astype(o_ref.dtype)
    ot_ref[...] = _dot_nt(w2t_ref[...], hid).astype(ot_ref.dtype)


def _compress(ckv, pe, w1, w2, w2t, bsz, seq):
    ncp = seq // CMP_STRIDE
    nj = 2 * NSA_GROUPS
    wsel = lambda b, j: (j // NSA_GROUPS, 0, 0)
    return pl.pallas_call(
        functools.partial(_compress_kernel, seq=seq),
        grid=(bsz, nj),
        in_specs=[pl.BlockSpec((seq, NSA_DH), lambda b, j: (b, j)),
                  pl.BlockSpec((None, CMP_LEN, NSA_DH), wsel),
                  pl.BlockSpec((None, CMP_LEN, NSA_DH, NSA_DH), lambda b, j: (j // NSA_GROUPS, 0, 0, 0)),
                  pl.BlockSpec((None, NSA_DH, NSA_DH), wsel),
                  pl.BlockSpec((None, NSA_DH, NSA_DH), wsel)],
        out_specs=[pl.BlockSpec((None, None, ncp, NSA_DH), lambda b, j: (b, j, 0, 0)),
                   pl.BlockSpec((None, None, NSA_DH, ncp), lambda b, j: (b, j, 0, 0))],
        out_shape=[jax.ShapeDtypeStruct((bsz, nj, ncp, NSA_DH), MXU_DTYPE),
                   jax.ShapeDtypeStruct((bsz, nj, NSA_DH, ncp), MXU_DTYPE)],
        scratch_shapes=[pltpu.VMEM((seq + LANES, NSA_DH), F32)],
        compiler_params=_params("parallel", "parallel"),
        name="compress",
    )(ckv, pe, w1, w2, w2t)


def _cmp_attn_kernel(qt_ref, k_ref, vt_ref, ov_ref, o_ref, selt_ref, p_ref, imp_ref, *, ncp, nb):
    tq = ATT_TQ
    width = NSA_HPG * tq
    groups = range(NSA_GROUPS)
    qi = pl.program_id(1)
    t0 = qi * tq
    any_valid = (t0 + lax.broadcasted_iota(jnp.int32, (1, tq), 1)) >= CMP_LEN - 1
    any_valid = jnp.concatenate([any_valid] * NSA_HPG, axis=1)

    def probabilities(rows):
        t = t0 + lax.broadcasted_iota(jnp.int32, (rows, tq), 1)
        n = lax.broadcasted_iota(jnp.int32, (rows, tq), 0)
        valid = (n * CMP_STRIDE + (CMP_LEN - 1) <= t) & (n < ncp - 1)
        valid = jnp.concatenate([valid] * NSA_HPG, axis=1)
        for g in groups:
            s = jnp.where(valid, _dot(k_ref[g, 0:rows, :], qt_ref[g]), NEG_INF)
            e = jnp.exp2(s - jnp.max(s, axis=0, keepdims=True))
            p_ref[g, 0:rows, :] = e * jnp.where(any_valid, 1.0 / jnp.sum(e, axis=0, keepdims=True), 0.0)
            if rows < ncp:
                p_ref[g, rows:, :] = jnp.zeros((ncp - rows, width), F32)

    n_buckets = 4
    per_bucket = ncp // n_buckets
    visible = (t0 + tq - CMP_LEN) // CMP_STRIDE + 1
    bucket = jnp.minimum((visible + per_bucket - 1) // per_bucket, n_buckets)
    for c in range(1, n_buckets + 1):
        pl.when(bucket == c)(functools.partial(probabilities, c * per_bucket))

    j = lax.broadcasted_iota(jnp.int32, (nb, tq), 0)
    tb = (t0 + lax.broadcasted_iota(jnp.int32, (nb, tq), 1)) // SEL_LEN
    forced = (j == 0) | (j == tb) | (j == tb - 1)
    sub = SUBLANES
    k_sel = min(SEL_TOPK, nb)
    for g in groups:
        p = p_ref[g]
        ot = _dot(vt_ref[g], p)
        for h in range(NSA_HPG):
            o_ref[:, (g * NSA_HPG + h) * NSA_DH:(g * NSA_HPG + h + 1) * NSA_DH] = ot[:, h * tq:(h + 1) * tq].T
        psum = sum(p[:, h * tq:(h + 1) * tq] for h in range(NSA_HPG))

        imp = sum(_dot(ov_ref[...], part) for part in _split3(psum))
        imp_ref[g] = jnp.where(j > tb, -SEL_FORCE, jnp.where(forced, SEL_FORCE, imp))

    def choose(blocks):
        for g in groups:
            imp = imp_ref[g, 0:blocks, :]
            grp = [imp[r * sub:(r + 1) * sub] for r in range(blocks // sub)]
            cnt = [jnp.zeros((sub, tq), F32) for _ in grp]
            for i in range(blocks):
                row = jnp.broadcast_to(imp[i:i + 1, :], (sub, tq))
                for r in range(blocks // sub):
                    if r * sub > i:
                        beats = jnp.where(row >= grp[r], 1.0, 0.0)
                    elif r * sub + sub - 1 < i:
                        beats = jnp.where(row > grp[r], 1.0, 0.0)
                    else:
                        jr = r * sub + lax.broadcasted_iota(jnp.int32, (sub, tq), 0)
                        beats = jnp.where(jr > i, jnp.where(row >= grp[r], 1.0, 0.0),
                                          jnp.where(row > grp[r], 1.0, 0.0))
                    cnt[r] = cnt[r] + beats
            for r in range(blocks // sub):
                selt_ref[g, r * sub:(r + 1) * sub, :] = jnp.where(cnt[r] < k_sel, 1.0, 0.0)
            if blocks < nb:
                selt_ref[g, blocks:, :] = jnp.zeros((nb - blocks, tq), F32)

    for c in range(1, n_buckets + 1):
        pl.when(bucket == c)(functools.partial(choose, c * (nb // n_buckets)))


def _cmp_attention(qt, cmp_k, cmp_vt, bsz, seq):
    n = bsz * seq
    ncp = seq // CMP_STRIDE
    nb = seq // SEL_LEN
    nt = seq // ATT_TQ
    cstart = np.arange(ncp) * CMP_STRIDE
    jstart = np.arange(nb) * SEL_LEN
    ov = ((cstart[None, :] < jstart[:, None] + SEL_LEN) & (cstart[None, :] + CMP_LEN > jstart[:, None])
          & (np.arange(ncp)[None, :] < ncp - 1))
    ov = jnp.asarray(ov, MXU_DTYPE)
    return pl.pallas_call(
        functools.partial(_cmp_attn_kernel, ncp=ncp, nb=nb),
        grid=(bsz, nt),
        in_specs=[pl.BlockSpec((None, NSA_GROUPS, None, NSA_DH, NSA_HPG * ATT_TQ), lambda b, i: (b, 0, i, 0, 0)),
                  pl.BlockSpec((None, NSA_GROUPS, ncp, NSA_DH), lambda b, i: (b, 0, 0, 0)),
                  pl.BlockSpec((None, NSA_GROUPS, NSA_DH, ncp), lambda b, i: (b, 1, 0, 0)),
                  pl.BlockSpec((nb, ncp), lambda b, i: (0, 0))],
        out_specs=[pl.BlockSpec((ATT_TQ, NSA_Q), lambda b, i: (b * nt + i, 0)),
                   pl.BlockSpec((None, NSA_GROUPS, nb, ATT_TQ), lambda b, i: (b, 0, 0, i))],
        out_shape=[jax.ShapeDtypeStruct((n, NSA_Q), F32),
                   jax.ShapeDtypeStruct((bsz, NSA_GROUPS, nb, seq), F32)],
        scratch_shapes=[pltpu.VMEM((NSA_GROUPS, ncp, NSA_HPG * ATT_TQ), F32),
                        pltpu.VMEM((NSA_GROUPS, nb, ATT_TQ), F32)],
        compiler_params=_params("parallel", "parallel"),
        name="cmp_attention",
    )(qt, cmp_k, cmp_vt, ov)


def _softmax_step(s, m, vt, acc_ref):
    m_new = jnp.maximum(m, jnp.max(s, axis=0, keepdims=True))
    p = jnp.exp2(s - m_new)
    acc_ref[...] = jnp.exp2(m - m_new) * acc_ref[...] + _dot(vt, p)
    return m_new


def _flash_finish(o_ref, acc_ref, col0=0):
    inv = 1.0 / acc_ref[NSA_DH:NSA_DH + 1, :]
    for h in range(NSA_HPG):
        cols = slice(h * ATT_TQ, (h + 1) * ATT_TQ)
        o_ref[:, col0 + h * NSA_DH:col0 + (h + 1) * NSA_DH] = (acc_ref[0:NSA_DH, cols] * inv[:, cols]).T


def _sel_attn_kernel(qt_ref, k_ref, vt_ref, selt_ref, o_ref, acc_ref, s_ref, qa_ref):
    qi = pl.program_id(1)
    width = NSA_HPG * ATT_TQ
    groups = range(NSA_GROUPS)
    blocks_per_tile = ATT_TK // SEL_LEN
    bias_rows = 2 * SUBLANES
    n_loop = qi * (ATT_TQ // ATT_TK)
    n_steps = (n_loop + 1) // 2 * 2
    tile_of = lambda step: jnp.where(step < n_loop, step, qi)
    rowid = lax.broadcasted_iota(jnp.int32, (bias_rows, ATT_TQ), 0)
    tri = (lax.broadcasted_iota(jnp.int32, (ATT_TK, ATT_TQ), 0) <=
           lax.broadcasted_iota(jnp.int32, (ATT_TK, ATT_TQ), 1))
    tri = jnp.concatenate([tri] * NSA_HPG, axis=1)

    def scores(g, step):
        kt = tile_of(step)
        threshold = jnp.where((step < n_loop) | (step == n_steps), 0.5, 2.0)
        bias = jnp.zeros((bias_rows, ATT_TQ), F32)
        for jb in range(blocks_per_tile):
            picked = selt_ref[g, pl.ds(kt * blocks_per_tile + jb, 1), :] > threshold
            bias = jnp.where(rowid == jb, jnp.where(picked, 0.0, NEG_INF), bias)
        qa_ref[g, NSA_DH:NSA_DH + bias_rows, :] = jnp.concatenate([bias] * NSA_HPG, axis=1).astype(qa_ref.dtype)
        k = k_ref[pl.ds(pl.multiple_of(kt * ATT_TK, ATT_TK), ATT_TK), g * SEL_KW:(g + 1) * SEL_KW]
        return _dot(k, qa_ref[g])

    def half_step(g, step, cur, nxt, m, diagonal=False):
        if not diagonal:
            s_ref[g, nxt] = scores(g, step + 1)
        s = jnp.where(tri, s_ref[g, cur], NEG_INF) if diagonal else s_ref[g, cur]
        return _softmax_step(s, m, vt_ref[g, tile_of(step)], acc_ref.at[g])

    def body(i, ms):
        ms = [half_step(g, 2 * i, 0, 1, ms[g]) for g in groups]
        return tuple(half_step(g, 2 * i + 1, 1, 0, ms[g]) for g in groups)

    for g in groups:
        qa_ref[g, 0:NSA_DH, :] = qt_ref[g]
        qa_ref[g, NSA_DH + bias_rows:, :] = jnp.zeros((SEL_KW - NSA_DH - bias_rows, width), qa_ref.dtype)
        acc_ref[g] = jnp.zeros(acc_ref.shape[1:], F32)
        s_ref[g, 0] = scores(g, 0)
    ms = lax.fori_loop(0, n_steps // 2, body, tuple(jnp.full((1, width), NEG_INF, F32) for _ in groups))
    for g in groups:
        half_step(g, n_steps, 0, 1, ms[g], diagonal=True)
        _flash_finish(o_ref, acc_ref.at[g], g * NSA_HPG * NSA_DH)


def _selected_attention(qt, k, vt, selt, bsz, seq):
    n = bsz * seq
    nt = seq // ATT_TQ
    nkt = seq // ATT_TK
    nb = seq // SEL_LEN
    width = NSA_HPG * ATT_TQ
    return pl.pallas_call(
        _sel_attn_kernel,
        grid=(bsz, nt),
        in_specs=[pl.BlockSpec((None, NSA_GROUPS, None, NSA_DH, width), lambda b, i: (b, 0, i, 0, 0)),
                  pl.BlockSpec((seq, NSA_GROUPS * SEL_KW), lambda b, i: (b, 0)),
                  pl.BlockSpec((None, NSA_GROUPS, nkt, VT_ROWS, ATT_TK), lambda b, i: (b, 0, 0, 0, 0)),
                  pl.BlockSpec((None, NSA_GROUPS, nb, ATT_TQ), lambda b, i: (b, 0, 0, i))],
        out_specs=pl.BlockSpec((ATT_TQ, NSA_Q), lambda b, i: (b * nt + i, 0)),
        out_shape=jax.ShapeDtypeStruct((n, NSA_Q), F32),
        scratch_shapes=[pltpu.VMEM((NSA_GROUPS, VT_ROWS, width), F32),
                        pltpu.VMEM((NSA_GROUPS, 2, ATT_TK, width), F32),
                        pltpu.VMEM((NSA_GROUPS, SEL_KW, width), MXU_DTYPE)],
        compiler_params=_params("parallel", "parallel"),
        name="selected_attention",
    )(qt, k, vt, selt)


def _win_attn_kernel(qt_ref, k_ref, vt_ref, o_ref, acc_ref, s0_ref, s1_ref, m_ref):
    qi = pl.program_id(1)
    groups = range(NSA_GROUPS)
    row = lax.broadcasted_iota(jnp.int32, (ATT_TK, ATT_TQ), 0)
    col = lax.broadcasted_iota(jnp.int32, (ATT_TK, ATT_TQ), 1)
    causal = jnp.concatenate([row <= col] * NSA_HPG, axis=1)
    window_tail = jnp.concatenate([row > col] * NSA_HPG, axis=1)

    def scores(g, kt):
        k = k_ref[pl.ds(pl.multiple_of(kt * ATT_TK, ATT_TK), ATT_TK), g * NSA_DH:(g + 1) * NSA_DH]
        return _dot(k, qt_ref[g])

    for g in groups:
        acc_ref[g] = jnp.zeros(acc_ref.shape[1:], F32)
        s0_ref[g] = scores(g, qi)
    for g in groups:
        s1_ref[g] = scores(g, jnp.maximum(qi - 1, 0))
        m_ref[g] = _softmax_step(jnp.where(causal, s0_ref[g], NEG_INF), jnp.full(m_ref.shape[1:], NEG_INF, F32),
                                 vt_ref[g, qi], acc_ref.at[g])

    @pl.when(qi >= 1)
    def _():
        for g in groups:
            s0_ref[g] = scores(g, jnp.maximum(qi - 2, 0))
            m_ref[g] = _softmax_step(s1_ref[g], m_ref[g], vt_ref[g, qi - 1], acc_ref.at[g])

    @pl.when(qi >= 2)
    def _():
        for g in groups:
            _softmax_step(jnp.where(window_tail, s0_ref[g], NEG_INF), m_ref[g], vt_ref[g, qi - 2], acc_ref.at[g])

    for g in groups:
        _flash_finish(o_ref, acc_ref.at[g], g * NSA_HPG * NSA_DH)


def _window_attention(qt, k, vt, bsz, seq):
    n = bsz * seq
    nt = seq // ATT_TQ
    nkt = seq // ATT_TK
    width = NSA_HPG * ATT_TQ
    return pl.pallas_call(
        _win_attn_kernel,
        grid=(bsz, nt),
        in_specs=[pl.BlockSpec((None, NSA_GROUPS, None, NSA_DH, width), lambda b, i: (b, 0, i, 0, 0)),
                  pl.BlockSpec((seq, NSA_KV), lambda b, i: (b, 0)),
                  pl.BlockSpec((None, NSA_GROUPS, nkt, VT_ROWS, ATT_TK), lambda b, i: (b, 0, 0, 0, 0))],
        out_specs=pl.BlockSpec((ATT_TQ, NSA_Q), lambda b, i: (b * nt + i, 0)),
        out_shape=jax.ShapeDtypeStruct((n, NSA_Q), F32),
        scratch_shapes=[pltpu.VMEM((NSA_GROUPS, VT_ROWS, width), F32),
                        pltpu.VMEM((NSA_GROUPS, ATT_TK, width), F32),
                        pltpu.VMEM((NSA_GROUPS, ATT_TK, width), F32),
                        pltpu.VMEM((NSA_GROUPS, 1, width), F32)],
        compiler_params=_params("parallel", "parallel"),
        name="window_attention",
    )(qt, k, vt)


def _merge_kernel(x_ref, yr_ref, oc_ref, os_ref, ow_ref, ngl_ref, g_ref, wg_ref, wr_ref, wn_ref, wo_ref, o_ref):
    tm = x_ref.shape[0]
    hb = _rms(x_ref[...], g_ref[...]).astype(MXU_DTYPE)
    ga = jnp.dot(hb, wg_ref[:, :D_MODEL], preferred_element_type=F32)
    gb = jnp.dot(hb, wg_ref[:, D_MODEL:], preferred_element_type=F32)
    gates = jax.nn.sigmoid(ngl_ref[...])
    parts = []
    for h in range(NSA_HEADS):
        cols = slice(h * NSA_DH, (h + 1) * NSA_DH)

        def gate(br):
            return jnp.broadcast_to(gates[:, 3 * h + br:3 * h + br + 1], (tm, NSA_DH))

        parts.append(gate(0) * oc_ref[:, cols] + gate(1) * os_ref[:, cols] + gate(2) * ow_ref[:, cols])
    o_nsa = jnp.concatenate(parts, axis=1)
    y_ret = _dot(yr_ref[...], wr_ref[...])
    y_nsa = _dot(o_nsa, wn_ref[...])
    y = jax.nn.sigmoid(ga) * y_ret + jax.nn.sigmoid(gb) * y_nsa
    o_ref[...] = x_ref[...] + _dot(y, wo_ref[...])


def _merge(x2d, y_ret, o_cmp, o_sel, o_win, ngl, g_mix, w_gates, w_ret_o, w_nsa_o, w_out):
    n = x2d.shape[0]
    tm = ROW_TM
    row = lambda i: (i, 0)
    const = lambda i: (0, 0)
    wide = pl.BlockSpec((tm, D_MODEL), row)
    wspec = pl.BlockSpec((D_MODEL, D_MODEL), const)
    return pl.pallas_call(
        _merge_kernel,
        grid=(n // tm,),
        in_specs=[wide, wide, wide, wide, wide,
                  pl.BlockSpec((tm, LANES), row),
                  pl.BlockSpec((1, D_MODEL), const),
                  pl.BlockSpec((D_MODEL, 2 * D_MODEL), const),
                  wspec, wspec, wspec],
        out_specs=wide,
        out_shape=jax.ShapeDtypeStruct((n, D_MODEL), F32),
        compiler_params=_params("parallel"),
        name="merge",
    )(x2d, y_ret, o_cmp, o_sel, o_win, ngl, g_mix, w_gates, w_ret_o, w_nsa_o, w_out)


def _mem_kv_kernel(m_ref, g_ref, w_ref, o_ref):
    o_ref[...] = _dot(_rms(m_ref[...], g_ref[...]), w_ref[...]).astype(o_ref.dtype)


def _mem_kv(mem2d, g, w_xkv, bsz):
    nm = mem2d.shape[0] // bsz
    return pl.pallas_call(
        _mem_kv_kernel,
        grid=(bsz,),
        in_specs=[pl.BlockSpec((nm, D_MODEL), lambda b: (b, 0)),
                  pl.BlockSpec((1, D_MODEL), lambda b: (0, 0)),
                  pl.BlockSpec((D_MODEL, 2 * D_MODEL), lambda b: (0, 0))],
        out_specs=pl.BlockSpec((nm, 2 * D_MODEL), lambda b: (b, 0)),
        out_shape=jax.ShapeDtypeStruct((mem2d.shape[0], 2 * D_MODEL), MXU_DTYPE),
        compiler_params=_params("parallel"),
        name="mem_kv",
    )(mem2d, g, w_xkv)


def _pack_pairs(x):
    half = x.shape[1] // 2
    hi = lax.bitcast_convert_type(x[:, :half].astype(MXU_DTYPE).astype(F32), jnp.uint32)
    lo = lax.bitcast_convert_type(x[:, half:].astype(MXU_DTYPE).astype(F32), jnp.uint32)
    return (hi & jnp.uint32(0xFFFF0000)) | (lo >> 16)


def _unpack_pairs(u):
    hi = lax.bitcast_convert_type(u & jnp.uint32(0xFFFF0000), F32)
    lo = lax.bitcast_convert_type(u << 16, F32)
    return jnp.concatenate([hi, lo], axis=1)


_ROUTER_E0 = 2 * SUBLANES
_ROUTER_ROWS = _ROUTER_E0 + N_EXPERTS


def _top2_route(lgt):
    sub = SUBLANES
    t = lgt.shape[1]
    rowid = lax.broadcasted_iota(jnp.int32, (sub, t), 0)
    first = lambda hit: jnp.min(jnp.where(hit, rowid, sub), axis=0, keepdims=True)
    lg = jnp.where(rowid < N_EGROUPS, lgt[0:sub], NEG_INF)
    gmax = jnp.max(lg, axis=0, keepdims=True)
    grp = first(lg == gmax)
    g_gate = 1.0 / jnp.sum(jnp.exp(lg - gmax), axis=0, keepdims=True)
    experts_of = lambda g: lgt[_ROUTER_E0 + g * EXP_PER_GROUP:_ROUTER_E0 + (g + 1) * EXP_PER_GROUP]
    le = experts_of(N_EGROUPS - 1)
    for g in range(N_EGROUPS - 2, -1, -1):
        le = jnp.where(grp == g, experts_of(g), le)
    ex = jnp.exp(le - jnp.max(le, axis=0, keepdims=True))
    pe = ex / jnp.sum(ex, axis=0, keepdims=True)
    p0 = jnp.max(pe, axis=0, keepdims=True)
    i0 = first(pe == p0)
    rest = jnp.where(rowid == i0, -1.0, pe)
    p1 = jnp.max(rest, axis=0, keepdims=True)
    i1 = first(rest == p1)
    den = p0 + p1
    base = grp * EXP_PER_GROUP
    return jnp.concatenate([(base + i0).astype(F32), (base + i1).astype(F32),
                            g_gate * p0 / den, g_gate * p1 / den], axis=0)


def _cross_kernel(x_ref, kv_ref, gx_ref, wq_ref, wo_ref, gf_ref, wr_ref, br_ref, x2_ref, hf_ref, rt_ref):
    x = x_ref[...]
    q = _dot(_rms(x, gx_ref[...]), wq_ref[...])
    heads = []
    for h in range(X_HEADS):
        k = kv_ref[:, h * X_DH:(h + 1) * X_DH]
        v = kv_ref[:, D_MODEL + h * X_DH:D_MODEL + (h + 1) * X_DH]
        s = _dot_nt(q[:, h * X_DH:(h + 1) * X_DH], k) * (X_DH ** -0.5)
        e = jnp.exp(s - jnp.max(s, axis=-1, keepdims=True))
        p = e / jnp.sum(e, axis=-1, keepdims=True)
        heads.append(_dot(p, v))
    x2 = x + _dot(jnp.concatenate(heads, axis=1), wo_ref[...])
    x2_ref[...] = x2
    hf = _rms(x2, gf_ref[...])
    hf_ref[...] = _pack_pairs(hf)
    h_hi, h_mid, _ = _split3(hf)
    w_hi = wr_ref[0]
    w_mid = wr_ref[1]
    lgt = (_dot_nt(w_hi, h_hi) + (_dot_nt(w_hi, h_mid) + _dot_nt(w_mid, h_hi))) + br_ref[...]
    rt_ref[...] = jnp.concatenate([_top2_route(lgt), jnp.zeros((SUBLANES - 4, lgt.shape[1]), F32)], axis=0)


def _cross(x1, kv, gx, w_xq, w_xo, gf, w_router, b_router, bsz, seq):
    n = x1.shape[0]
    tm = ROW_TM
    nt = seq // tm
    nm = kv.shape[0] // bsz
    row = lambda i: (i, 0)
    const = lambda i: (0, 0)
    vec = pl.BlockSpec((1, D_MODEL), const)
    wspec = pl.BlockSpec((D_MODEL, D_MODEL), const)
    return pl.pallas_call(
        _cross_kernel,
        grid=(n // tm,),
        in_specs=[pl.BlockSpec((tm, D_MODEL), row),
                  pl.BlockSpec((nm, 2 * D_MODEL), lambda i: (i // nt, 0)),
                  vec, wspec, wspec, vec,
                  pl.BlockSpec((2, _ROUTER_ROWS, D_MODEL), lambda i: (0, 0, 0)),
                  pl.BlockSpec((_ROUTER_ROWS, 1), const)],
        out_specs=[pl.BlockSpec((tm, D_MODEL), row),
                   pl.BlockSpec((tm, D_MODEL // 2), row),
                   pl.BlockSpec((SUBLANES, tm), lambda i: (0, i))],
        out_shape=[jax.ShapeDtypeStruct((n, D_MODEL), F32),
                   jax.ShapeDtypeStruct((n, D_MODEL // 2), jnp.uint32),
                   jax.ShapeDtypeStruct((SUBLANES, n), F32)],
        compiler_params=_params("parallel"),
        name="cross_attention",
    )(x1, kv, gx, w_xq, w_xo, gf, w_router, b_router)


def _expert_kernel(blk_ref, xb_ref, w1_ref, w3_ref, w2_ref, o_ref):
    n_used = blk_ref[pl.num_programs(0)]

    @pl.when(pl.program_id(0) < n_used)
    def _():
        xb = _unpack_pairs(xb_ref[...]).astype(MXU_DTYPE)
        a = _dot(xb, w1_ref[...])
        hmid = a * jax.nn.sigmoid(a) * _dot(xb, w3_ref[...])
        o_ref[...] = _dot(hmid, w2_ref[...])

    @pl.when(pl.program_id(0) >= n_used)
    def _():
        o_ref[...] = jnp.zeros_like(o_ref)


def _experts(blk, xb, w1, w3, w2):
    cap = xb.shape[0]
    nblk = cap // MOE_BLOCK
    row = lambda i, e: (i, 0)
    by_expert = lambda i, e: (e[i], 0, 0)
    grid_spec = pltpu.PrefetchScalarGridSpec(
        num_scalar_prefetch=1,
        grid=(nblk,),
        in_specs=[pl.BlockSpec((MOE_BLOCK, D_MODEL // 2), row),
                  pl.BlockSpec((None, D_MODEL, D_EXPERT), by_expert),
                  pl.BlockSpec((None, D_MODEL, D_EXPERT), by_expert),
                  pl.BlockSpec((None, D_EXPERT, D_MODEL), by_expert)],
        out_specs=pl.BlockSpec((MOE_BLOCK, D_MODEL), row),
    )
    return pl.pallas_call(
        _expert_kernel,
        grid_spec=grid_spec,
        out_shape=jax.ShapeDtypeStruct((cap, D_MODEL), F32),
        compiler_params=_params("arbitrary"),
        name="experts",
    )(blk, xb, w1, w3, w2)


def _final_kernel(x_ref, w_ref, g_ref, *refs):
    o_ref = refs[-1]
    tiles_per_split = pl.num_programs(0) // MOE_SPLITS
    for s in range(MOE_SPLITS):
        @pl.when(pl.program_id(0) // tiles_per_split == s)
        def _(s=s):
            moe = w_ref[:, 0:1] * refs[2 * s][...] + w_ref[:, 1:2] * refs[2 * s + 1][...]
            o_ref[...] = _rms(x_ref[...] + moe, g_ref[...])


def _final(x2, wts, g, ys):
    n = x2.shape[0]
    tm = ROW_TM
    tiles_per_split = n // tm // MOE_SPLITS
    row = lambda i: (i, 0)
    wide = pl.BlockSpec((tm, D_MODEL), row)

    def split_spec(s):
        return pl.BlockSpec((tm, D_MODEL), lambda i: (jnp.clip(i - s * tiles_per_split, 0, tiles_per_split - 1), 0))

    return pl.pallas_call(
        _final_kernel,
        grid=(n // tm,),
        in_specs=[wide, pl.BlockSpec((tm, EXP_TOPK), row), pl.BlockSpec((1, D_MODEL), lambda i: (0, 0))] + [
            split_spec(s) for s in range(MOE_SPLITS) for _ in range(EXP_TOPK)],
        out_specs=wide,
        out_shape=jax.ShapeDtypeStruct((n, D_MODEL), F32),
        compiler_params=_params("parallel"),
        name="final_norm",
    )(x2, wts, g, *[y for pair in ys for y in pair])


def _route(eid, tok0):
    n_tok = eid.shape[1]
    eid = eid.reshape(-1)
    n_asg = eid.shape[0]
    iota = jnp.arange(n_asg, dtype=jnp.int32)
    _, order = lax.sort_key_val(eid, iota)
    counts = jnp.sum((jnp.arange(N_EXPERTS, dtype=jnp.int32)[:, None] == eid[None, :]).astype(jnp.int32), axis=1)
    padded = (counts + MOE_BLOCK - 1) // MOE_BLOCK * MOE_BLOCK
    starts = jnp.cumsum(counts) - counts
    pends = jnp.cumsum(padded)
    pstarts = pends - padded
    cap = ((n_asg + MOE_BLOCK - 1) // MOE_BLOCK + N_EXPERTS) * MOE_BLOCK
    nblk = cap // MOE_BLOCK
    blk_e = jnp.minimum(jnp.searchsorted(pends, jnp.arange(nblk) * MOE_BLOCK, side='right', method='compare_all'),
                        N_EXPERTS - 1).astype(jnp.int32)
    per_row = lambda a: jnp.repeat(a[blk_e], MOE_BLOCK)
    row = jnp.arange(cap, dtype=jnp.int32)
    off = row - per_row(pstarts)
    asg = order[jnp.clip(per_row(starts) + off, 0, n_asg - 1)]
    buf_tok = tok0 + jnp.where(off < per_row(counts), asg % n_tok, row % n_tok)
    shift = pstarts - starts
    jumps = jnp.zeros((n_asg,), jnp.int32).at[starts[1:]].add(shift[1:] - shift[:-1])
    dest_sorted = iota + shift[0] + jnp.cumsum(jumps)
    _, pos = lax.sort_key_val(order, dest_sorted)
    n_used = (pends[-1] // MOE_BLOCK).astype(jnp.int32)
    return buf_tok, jnp.concatenate([blk_e, n_used[None]]), pos.reshape(EXP_TOPK, n_tok)


def kernel(x, mem, norm_mix_g, w_in, w_ret_o, w_nsa_o, w_out, cmp_pe_k, cmp_w1_k, cmp_w2_k, cmp_pe_v,
           cmp_w1_v, cmp_w2_v, norm_x_g, norm_mem_g, w_xq, w_xkv, w_xo, norm_ffn_g, w_grp, b_grp, w_rt,
           b_rt, w_e1, w_e3, w_e2, norm_f_g):
    bsz, seq, _ = x.shape
    n = bsz * seq
    assert seq % PROJ_TM == 0 and seq % (2 * ATT_TQ) == 0 and w_in.shape[0] == 1
    assert n % (ROW_TM * MOE_SPLITS) == 0
    cast = lambda a: a.astype(MXU_DTYPE)
    xc = x.reshape(n, D_MODEL)
    l = 0

    w_main, w_t, w_gates = _split_w_in(w_in[l])
    qk, rv, rg, ckv, sk, wk, ngl, qt, svt, wvt = _proj(
        xc, norm_mix_g[l][None, :], w_main, w_t, _rope_tables(seq), bsz, seq)
    y_ret = _retention(qk, rv, rg, bsz, seq)
    w2 = jnp.stack([cmp_w2_k[l], cmp_w2_v[l]])
    cmp_k, cmp_vt = _compress(ckv, jnp.stack([cmp_pe_k[l], cmp_pe_v[l]]),
                              cast(jnp.stack([cmp_w1_k[l], cmp_w1_v[l]])),
                              cast(w2), cast(w2.transpose(0, 2, 1)), bsz, seq)
    o_cmp, selt = _cmp_attention(qt, cmp_k, cmp_vt, bsz, seq)
    o_sel = _selected_attention(qt, sk, svt, selt, bsz, seq)
    o_win = _window_attention(qt, wk, wvt, bsz, seq)
    x1 = _merge(xc, y_ret, o_cmp, o_sel, o_win, ngl, norm_mix_g[l][None, :], w_gates,
                cast(w_ret_o[l]), cast(w_nsa_o[l]), cast(w_out[l]))

    kv = _mem_kv(mem.reshape(-1, D_MODEL), norm_mem_g[l][None, :], cast(w_xkv[l]), bsz)
    gap = _ROUTER_E0 - N_EGROUPS
    w_router = jnp.concatenate([w_grp[l].T, jnp.zeros((gap, D_MODEL), F32), w_rt[l].T], axis=0)
    wr_hi = cast(w_router)
    wr_mid = cast(w_router - wr_hi.astype(F32))
    b_router = jnp.concatenate([b_grp[l], jnp.zeros((gap,), F32), b_rt[l]])[:, None]
    x2, hf, routed = _cross(x1, kv, norm_x_g[l][None, :], cast(w_xq[l]), cast(w_xo[l]),
                            norm_ffn_g[l][None, :], jnp.stack([wr_hi, wr_mid]), b_router, bsz, seq)

    eid = routed[0:EXP_TOPK].astype(jnp.int32)
    wts = routed[EXP_TOPK:2 * EXP_TOPK].T
    per_split = n // MOE_SPLITS
    ys = []
    for s in range(MOE_SPLITS):
        buf_tok, blk, pos = _route(eid[:, s * per_split:(s + 1) * per_split], s * per_split)
        y = _experts(blk, hf[buf_tok], w_e1[l], w_e3[l], w_e2[l])
        ys.append([y[pos[j]] for j in range(EXP_TOPK)])
    out = _final(x2, wts, norm_f_g[None, :], ys)
    return out.reshape(bsz, seq, D_MODEL)
```

```python
import functools

import numpy as np
import jax
import jax.numpy as jnp
from jax import lax
from jax.experimental import pallas as pl
from jax.experimental.pallas import tpu as pltpu

MXU_DTYPE = jnp.bfloat16
F32 = jnp.float32

D_MODEL = 1024
N_MEM = 256
EPS = 1e-6
NEG_INF = -1e30
SEL_FORCE = 1e4

R_HEADS = 4
R_DK = 128
R_DV = 256
R_CHUNK = 128
ROPE_BASE = 10000.0

NSA_HEADS = 8
NSA_GROUPS = 2
NSA_HPG = NSA_HEADS // NSA_GROUPS
NSA_DH = 128
CMP_LEN = 32
CMP_STRIDE = 16
SEL_LEN = 64
SEL_TOPK = 16
WINDOW = 512

X_HEADS = 4
X_DH = D_MODEL // X_HEADS

N_EGROUPS = 4
EXP_PER_GROUP = 8
N_EXPERTS = N_EGROUPS * EXP_PER_GROUP
EXP_TOPK = 2
D_EXPERT = 512
MOE_BLOCK = 512

RET_QK = R_HEADS * R_DK
RET_V = R_HEADS * R_DV
NSA_Q = NSA_HEADS * NSA_DH
NSA_KV = NSA_GROUPS * NSA_DH
SPLITS = (RET_QK, RET_QK, RET_V, RET_V, NSA_Q, NSA_KV, NSA_KV, NSA_KV, NSA_KV, NSA_KV, NSA_KV,
          3 * NSA_HEADS, D_MODEL, D_MODEL)

_EXP2_SCALE = (NSA_DH ** -0.5) * float(np.log2(np.e))

LANES = 128
SUBLANES = 8
VMEM_LIMIT = 56 * 1024 * 1024

PROJ_TM = 512
RET_ROWS = 512
ATT_TQ = 256
ATT_TK = 256
ROW_TM = 512
MOE_SPLITS = 2
VT_ROWS = NSA_DH + 2 * SUBLANES
SEL_KW = 2 * NSA_DH
assert ATT_TQ == ATT_TK and SEL_LEN * 2 * SUBLANES >= ATT_TK and WINDOW == 2 * ATT_TK


def _params(*sem):
    return pltpu.CompilerParams(dimension_semantics=sem, vmem_limit_bytes=VMEM_LIMIT)


def _dot(a, b):
    return jnp.dot(a.astype(MXU_DTYPE), b.astype(MXU_DTYPE), preferred_element_type=F32)


def _dot_nt(a, b):
    return lax.dot_general(a.astype(MXU_DTYPE), b.astype(MXU_DTYPE), (((1,), (1,)), ((), ())),
                           preferred_element_type=F32)


def _dot_tn(a, b):
    return lax.dot_general(a.astype(MXU_DTYPE), b.astype(MXU_DTYPE), (((0,), (0,)), ((), ())),
                           preferred_element_type=F32)


def _split3(p):
    hi = p.astype(MXU_DTYPE)
    r1 = p - hi.astype(F32)
    mid = r1.astype(MXU_DTYPE)
    lo = (r1 - mid.astype(F32)).astype(MXU_DTYPE)
    return hi, mid, lo


def _rms(x, g):
    return x * lax.rsqrt(jnp.mean(x * x, axis=-1, keepdims=True) + EPS) * g


_C_RQK = 0
_C_RV = _C_RQK + 2 * RET_QK
_C_RG = _C_RV + RET_V
_C_CKV = _C_RG + RET_V
_C_SK = _C_CKV + 2 * NSA_KV
_C_WK = _C_SK + NSA_KV
_C_NGL = _C_WK + NSA_KV
_C_END = _C_NGL + LANES
_R_NQ = 0
_R_SV = _R_NQ + NSA_Q
_R_WV = _R_SV + NSA_KV
_R_END = _R_WV + NSA_KV


def _proj_kernel(x_ref, g_ref, w_ref, wt_ref, cos_ref, sin_up_ref, sin_dn_ref,
                 qk_ref, rv_ref, rg_ref, ckv_ref, sk_ref, wk_ref, ngl_ref, nqt_ref, svt_ref, wvt_ref):
    hb = _rms(x_ref[...], g_ref[...]).astype(MXU_DTYPE)
    tm = hb.shape[0]

    def mm(off, width):
        return jnp.dot(hb, w_ref[:, off:off + width], preferred_element_type=F32)

    def mm_t(off, height):
        return _dot_nt(wt_ref[off:off + height, :], hb)

    cos = cos_ref[...]
    sin_up = sin_up_ref[...]
    sin_dn = sin_dn_ref[...]
    qk = mm(_C_RQK, 2 * RET_QK)
    for i in range(2 * R_HEADS):
        t = qk[:, i * R_DK:(i + 1) * R_DK]
        r = t * cos + pltpu.roll(t, R_DK - 1, axis=1) * sin_up + pltpu.roll(t, 1, axis=1) * sin_dn
        if i >= R_HEADS:
            r = r * (R_DK ** -0.5)
        qk_ref[:, i * R_DK:(i + 1) * R_DK] = r.astype(qk_ref.dtype)
    rv_ref[...] = mm(_C_RV, RET_V).astype(rv_ref.dtype)
    rg_ref[...] = mm(_C_RG, RET_V)
    ckv_ref[...] = mm(_C_CKV, 2 * NSA_KV)
    sk = mm(_C_SK, NSA_KV).astype(sk_ref.dtype)
    blk = (lax.broadcasted_iota(jnp.int32, (tm, NSA_DH), 0) % ATT_TK) // SEL_LEN
    onehot = jnp.where(lax.broadcasted_iota(jnp.int32, (tm, NSA_DH), 1) == blk, 1.0, 0.0).astype(sk_ref.dtype)
    for g in range(NSA_GROUPS):
        sk_ref[:, g * SEL_KW:g * SEL_KW + NSA_DH] = sk[:, g * NSA_DH:(g + 1) * NSA_DH]
        sk_ref[:, g * SEL_KW + NSA_DH:(g + 1) * SEL_KW] = onehot
    wk_ref[...] = mm(_C_WK, NSA_KV).astype(wk_ref.dtype)
    ngl_ref[...] = mm(_C_NGL, LANES)

    nqt = (mm_t(_R_NQ, NSA_Q) * _EXP2_SCALE).astype(nqt_ref.dtype)
    for g in range(NSA_GROUPS):
        for j in range(tm // ATT_TQ):
            for hh in range(NSA_HPG):
                head = g * NSA_HPG + hh
                nqt_ref[g, j, :, hh * ATT_TQ:(hh + 1) * ATT_TQ] = (
                    nqt[head * NSA_DH:(head + 1) * NSA_DH, j * ATT_TQ:(j + 1) * ATT_TQ])
    for off, out_ref in ((_R_SV, svt_ref), (_R_WV, wvt_ref)):
        vt = mm_t(off, NSA_KV).astype(out_ref.dtype)
        for g in range(NSA_GROUPS):
            for j in range(tm // ATT_TK):
                out_ref[g, j, 0:NSA_DH] = vt[g * NSA_DH:(g + 1) * NSA_DH, j * ATT_TK:(j + 1) * ATT_TK]
                out_ref[g, j, NSA_DH:VT_ROWS] = jnp.ones((VT_ROWS - NSA_DH, ATT_TK), out_ref.dtype)


def _split_w_in(w_in):
    offs = np.cumsum((0,) + SPLITS)
    rq, rk, rv, rg, nq, ck, cv, sk, sv, wk, wv, ngl, ga, gb = [
        w_in[:, offs[i]:offs[i + 1]] for i in range(len(SPLITS))]
    ngl = jnp.pad(ngl, ((0, 0), (0, LANES - ngl.shape[1])))
    w = jnp.concatenate([rq, rk, rv, rg, ck, cv, sk, wk, ngl], axis=1).astype(MXU_DTYPE)
    wt = jnp.concatenate([nq, sv, wv], axis=1).T.astype(MXU_DTYPE)
    w_gates = jnp.concatenate([ga, gb], axis=1).astype(MXU_DTYPE)
    return w, wt, w_gates


def _rope_tables(seq):
    pos = jnp.arange(seq, dtype=F32)
    inv_freq = ROPE_BASE ** (-jnp.arange(0, R_DK, 2, dtype=F32) / R_DK)
    ang = pos[:, None] * inv_freq[None, :]
    cos = jnp.repeat(jnp.cos(ang), 2, axis=1)
    sin = jnp.repeat(jnp.sin(ang), 2, axis=1)
    even = (jnp.arange(R_DK) % 2 == 0)[None, :]
    return cos, jnp.where(even, -sin, 0.0), jnp.where(even, 0.0, sin)


def _proj(x2d, g, w, wt, tables, bsz, seq):
    n = x2d.shape[0]
    tm = PROJ_TM
    nt = seq // tm
    row = lambda i: (i, 0)
    const = lambda i: (0, 0)
    tile = lambda i: (i // nt, 0, i % nt, 0, 0)
    table = pl.BlockSpec((tm, R_DK), lambda i: (i % nt, 0))
    widths = (2 * RET_QK, RET_V, RET_V, 2 * NSA_KV, NSA_GROUPS * SEL_KW, NSA_KV, LANES)
    dtypes = (MXU_DTYPE, MXU_DTYPE, F32, F32, MXU_DTYPE, MXU_DTYPE, F32)
    width = NSA_HPG * ATT_TQ
    vt_shape = jax.ShapeDtypeStruct((bsz, NSA_GROUPS, seq // ATT_TK, VT_ROWS, ATT_TK), MXU_DTYPE)
    vt_spec = pl.BlockSpec((None, NSA_GROUPS, tm // ATT_TK, VT_ROWS, ATT_TK), tile)
    return pl.pallas_call(
        _proj_kernel,
        grid=(n // tm,),
        in_specs=[pl.BlockSpec((tm, D_MODEL), row),
                  pl.BlockSpec((1, D_MODEL), const),
                  pl.BlockSpec((D_MODEL, _C_END), const, pipeline_mode=pl.Buffered(1)),
                  pl.BlockSpec((_R_END, D_MODEL), const, pipeline_mode=pl.Buffered(1)),
                  table, table, table],
        out_specs=[pl.BlockSpec((tm, wd), row) for wd in widths] + [
            pl.BlockSpec((None, NSA_GROUPS, tm // ATT_TQ, NSA_DH, width), tile), vt_spec, vt_spec],
        out_shape=[jax.ShapeDtypeStruct((n, wd), dt) for wd, dt in zip(widths, dtypes)] + [
            jax.ShapeDtypeStruct((bsz, NSA_GROUPS, seq // ATT_TQ, NSA_DH, width), MXU_DTYPE), vt_shape, vt_shape],
        compiler_params=_params("parallel"),
        name="proj",
    )(x2d, g, w, wt, *tables)


def _ret_kernel(qk_ref, v_ref, g_ref, decay_ref, zeta_ref, xi_ref, cd_ref, y_ref, state_ref):
    @pl.when(pl.program_id(1) == 0)
    def _():
        state_ref[...] = jnp.zeros_like(state_ref)

    chunks = range(RET_ROWS // R_CHUNK)
    heads = range(R_HEADS)
    rows = lambda c: slice(c * R_CHUNK, (c + 1) * R_CHUNK)
    q_of = lambda c, h: qk_ref[rows(c), h * R_DK:(h + 1) * R_DK]
    k_of = lambda c, h: qk_ref[rows(c), RET_QK + h * R_DK:RET_QK + (h + 1) * R_DK]
    v_of = lambda c, h: v_ref[rows(c), h * R_DV:(h + 1) * R_DV]

    scores = {(c, h): _dot_nt(q_of(c, h), k_of(c, h)) * decay_ref[h] for c in chunks for h in heads}
    kv = {(c, h): _dot_tn(k_of(c, h).astype(F32) * zeta_ref[h], v_of(c, h)) for c in chunks for h in heads}
    seen = {}
    for h in heads:
        st = state_ref[h]
        for c in chunks:
            seen[c, h] = st
            st = st * cd_ref[h] + kv[c, h]
        state_ref[h] = st
    for c in chunks:
        for h in heads:
            o = _dot(scores[c, h], v_of(c, h)) + _dot(q_of(c, h), seen[c, h]) * xi_ref[h]
            o = o * lax.rsqrt(jnp.mean(o * o, axis=-1, keepdims=True) + EPS)
            g = g_ref[rows(c), h * R_DV:(h + 1) * R_DV]
            y_ref[rows(c), h * R_DV:(h + 1) * R_DV] = (g * jax.nn.sigmoid(g) * o).astype(y_ref.dtype)


def _retention(qk, rv, rg, bsz, seq):
    n = qk.shape[0]
    nt = seq // RET_ROWS
    log_g = jnp.log1p(-jnp.exp2(-5.0 - jnp.arange(R_HEADS, dtype=F32)))
    idx = jnp.arange(R_CHUNK, dtype=F32)
    diff = idx[:, None] - idx[None, :]
    decay = jnp.where(diff >= 0, jnp.exp(log_g[:, None, None] * jnp.maximum(diff, 0.0)), 0.0)
    zeta = jnp.exp(log_g[:, None] * (R_CHUNK - 1.0 - idx)[None, :])[:, :, None]
    xi = jnp.exp(log_g[:, None] * (idx + 1.0)[None, :])[:, :, None]
    cd = jnp.exp(log_g * R_CHUNK)[:, None, None]
    row = lambda b, i: (b * nt + i, 0)
    const3 = lambda b, i: (0, 0, 0)
    return pl.pallas_call(
        _ret_kernel,
        grid=(bsz, nt),
        in_specs=[pl.BlockSpec((RET_ROWS, 2 * RET_QK), row),
                  pl.BlockSpec((RET_ROWS, RET_V), row),
                  pl.BlockSpec((RET_ROWS, RET_V), row),
                  pl.BlockSpec((R_HEADS, R_CHUNK, R_CHUNK), const3),
                  pl.BlockSpec((R_HEADS, R_CHUNK, 1), const3),
                  pl.BlockSpec((R_HEADS, R_CHUNK, 1), const3),
                  pl.BlockSpec((R_HEADS, 1, 1), const3)],
        out_specs=pl.BlockSpec((RET_ROWS, RET_V), row),
        out_shape=jax.ShapeDtypeStruct((n, RET_V), MXU_DTYPE),
        scratch_shapes=[pltpu.VMEM((R_HEADS, R_DK, R_DV), F32)],
        compiler_params=_params("parallel", "arbitrary"),
        name="retention",
    )(qk, rv, rg, decay, zeta, xi, cd)


def _compress_kernel(x_ref, pe_ref, w1_ref, w2_ref, w2t_ref, o_ref, ot_ref, buf_ref, *, seq):
    ncp = seq // CMP_STRIDE
    buf_ref[0:seq, :] = x_ref[...]
    buf_ref[seq:seq + LANES, :] = jnp.zeros((LANES, NSA_DH), F32)
    acc = jnp.zeros((ncp, NSA_DH), F32)
    for l in range(CMP_LEN):
        xl = buf_ref[pl.ds(l, ncp, stride=CMP_STRIDE), :] + pe_ref[l:l + 1, :]
        acc = acc + _dot(xl, w1_ref[l])
    hid = jax.nn.gelu(acc)
    o_ref[...] = _dot(hid, w2_ref[...]).astype(o_ref.dtype)
    ot_ref[...] = _dot_nt(w2t_ref[...], hid).astype(ot_ref.dtype)


def _compress(ckv, pe, w1, w2, w2t, bsz, seq):
    ncp = seq // CMP_STRIDE
    nj = 2 * NSA_GROUPS
    wsel = lambda b, j: (j // NSA_GROUPS, 0, 0)
    return pl.pallas_call(
        functools.partial(_compress_kernel, seq=seq),
        grid=(bsz, nj),
        in_specs=[pl.BlockSpec((seq, NSA_DH), lambda b, j: (b, j)),
                  pl.BlockSpec((None, CMP_LEN, NSA_DH), wsel),
                  pl.BlockSpec((None, CMP_LEN, NSA_DH, NSA_DH), lambda b, j: (j // NSA_GROUPS, 0, 0, 0)),
                  pl.BlockSpec((None, NSA_DH, NSA_DH), wsel),
                  pl.BlockSpec((None, NSA_DH, NSA_DH), wsel)],
        out_specs=[pl.BlockSpec((None, None, ncp, NSA_DH), lambda b, j: (b, j, 0, 0)),
                   pl.BlockSpec((None, None, NSA_DH, ncp), lambda b, j: (b, j, 0, 0))],
        out_shape=[jax.ShapeDtypeStruct((bsz, nj, ncp, NSA_DH), MXU_DTYPE),
                   jax.ShapeDtypeStruct((bsz, nj, NSA_DH, ncp), MXU_DTYPE)],
        scratch_shapes=[pltpu.VMEM((seq + LANES, NSA_DH), F32)],
        compiler_params=_params("parallel", "parallel"),
        name="compress",
    )(ckv, pe, w1, w2, w2t)


def _cmp_attn_kernel(qt_ref, k_ref, vt_ref, ov_ref, o_ref, selt_ref, p_ref, imp_ref, *, ncp, nb):
    tq = ATT_TQ
    width = NSA_HPG * tq
    groups = range(NSA_GROUPS)
    qi = pl.program_id(1)
    t0 = qi * tq
    any_valid = (t0 + lax.broadcasted_iota(jnp.int32, (1, tq), 1)) >= CMP_LEN - 1
    any_valid = jnp.concatenate([any_valid] * NSA_HPG, axis=1)

    def probabilities(rows):
        t = t0 + lax.broadcasted_iota(jnp.int32, (rows, tq), 1)
        n = lax.broadcasted_iota(jnp.int32, (rows, tq), 0)
        valid = (n * CMP_STRIDE + (CMP_LEN - 1) <= t) & (n < ncp - 1)
        valid = jnp.concatenate([valid] * NSA_HPG, axis=1)
        for g in groups:
            s = jnp.where(valid, _dot(k_ref[g, 0:rows, :], qt_ref[g]), NEG_INF)
            e = jnp.exp2(s - jnp.max(s, axis=0, keepdims=True))
            p_ref[g, 0:rows, :] = e * jnp.where(any_valid, 1.0 / jnp.sum(e, axis=0, keepdims=True), 0.0)
            if rows < ncp:
                p_ref[g, rows:, :] = jnp.zeros((ncp - rows, width), F32)

    n_buckets = 4
    per_bucket = ncp // n_buckets
    visible = (t0 + tq - CMP_LEN) // CMP_STRIDE + 1
    bucket = jnp.minimum((visible + per_bucket - 1) // per_bucket, n_buckets)
    for c in range(1, n_buckets + 1):
        pl.when(bucket == c)(functools.partial(probabilities, c * per_bucket))

    j = lax.broadcasted_iota(jnp.int32, (nb, tq), 0)
    tb = (t0 + lax.broadcasted_iota(jnp.int32, (nb, tq), 1)) // SEL_LEN
    forced = (j == 0) | (j == tb) | (j == tb - 1)
    sub = SUBLANES
    k_sel = min(SEL_TOPK, nb)
    for g in groups:
        p = p_ref[g]
        ot = _dot(vt_ref[g], p)
        for h in range(NSA_HPG):
            o_ref[:, (g * NSA_HPG + h) * NSA_DH:(g * NSA_HPG + h + 1) * NSA_DH] = ot[:, h * tq:(h + 1) * tq].T
        psum = sum(p[:, h * tq:(h + 1) * tq] for h in range(NSA_HPG))

        imp = sum(_dot(ov_ref[...], part) for part in _split3(psum))
        imp_ref[g] = jnp.where(j > tb, -SEL_FORCE, jnp.where(forced, SEL_FORCE, imp))

    def choose(blocks):
        for g in groups:
            imp = imp_ref[g, 0:blocks, :]
            grp = [imp[r * sub:(r + 1) * sub] for r in range(blocks // sub)]
            cnt = [jnp.zeros((sub, tq), F32) for _ in grp]
            for i in range(blocks):
                row = jnp.broadcast_to(imp[i:i + 1, :], (sub, tq))
                for r in range(blocks // sub):
                    if r * sub > i:
                        beats = jnp.where(row >= grp[r], 1.0, 0.0)
                    elif r * sub + sub - 1 < i:
                        beats = jnp.where(row > grp[r], 1.0, 0.0)
                    else:
                        jr = r * sub + lax.broadcasted_iota(jnp.int32, (sub, tq), 0)
                        beats = jnp.where(jr > i, jnp.where(row >= grp[r], 1.0, 0.0),
                                          jnp.where(row > grp[r], 1.0, 0.0))
                    cnt[r] = cnt[r] + beats
            for r in range(blocks // sub):
                selt_ref[g, r * sub:(r + 1) * sub, :] = jnp.where(cnt[r] < k_sel, 1.0, 0.0)
            if blocks < nb:
                selt_ref[g, blocks:, :] = jnp.zeros((nb - blocks, tq), F32)

    for c in range(1, n_buckets + 1):
        pl.when(bucket == c)(functools.partial(choose, c * (nb // n_buckets)))


def _cmp_attention(qt, cmp_k, cmp_vt, bsz, seq):
    n = bsz * seq
    ncp = seq // CMP_STRIDE
    nb = seq // SEL_LEN
    nt = seq // ATT_TQ
    cstart = np.arange(ncp) * CMP_STRIDE
    jstart = np.arange(nb) * SEL_LEN
    ov = ((cstart[None, :] < jstart[:, None] + SEL_LEN) & (cstart[None, :] + CMP_LEN > jstart[:, None])
          & (np.arange(ncp)[None, :] < ncp - 1))
    ov = jnp.asarray(ov, MXU_DTYPE)
    return pl.pallas_call(
        functools.partial(_cmp_attn_kernel, ncp=ncp, nb=nb),
        grid=(bsz, nt),
        in_specs=[pl.BlockSpec((None, NSA_GROUPS, None, NSA_DH, NSA_HPG * ATT_TQ), lambda b, i: (b, 0, i, 0, 0)),
                  pl.BlockSpec((None, NSA_GROUPS, ncp, NSA_DH), lambda b, i: (b, 0, 0, 0)),
                  pl.BlockSpec((None, NSA_GROUPS, NSA_DH, ncp), lambda b, i: (b, 1, 0, 0)),
                  pl.BlockSpec((nb, ncp), lambda b, i: (0, 0))],
        out_specs=[pl.BlockSpec((ATT_TQ, NSA_Q), lambda b, i: (b * nt + i, 0)),
                   pl.BlockSpec((None, NSA_GROUPS, nb, ATT_TQ), lambda b, i: (b, 0, 0, i))],
        out_shape=[jax.ShapeDtypeStruct((n, NSA_Q), F32),
                   jax.ShapeDtypeStruct((bsz, NSA_GROUPS, nb, seq), F32)],
        scratch_shapes=[pltpu.VMEM((NSA_GROUPS, ncp, NSA_HPG * ATT_TQ), F32),
                        pltpu.VMEM((NSA_GROUPS, nb, ATT_TQ), F32)],
        compiler_params=_params("parallel", "parallel"),
        name="cmp_attention",
    )(qt, cmp_k, cmp_vt, ov)


def _softmax_step(s, m, vt, acc_ref):
    m_new = jnp.maximum(m, jnp.max(s, axis=0, keepdims=True))
    p = jnp.exp2(s - m_new)
    acc_ref[...] = jnp.exp2(m - m_new) * acc_ref[...] + _dot(vt, p)
    return m_new


def _flash_finish(o_ref, acc_ref, col0=0):
    inv = 1.0 / acc_ref[NSA_DH:NSA_DH + 1, :]
    for h in range(NSA_HPG):
        cols = slice(h * ATT_TQ, (h + 1) * ATT_TQ)
        o_ref[:, col0 + h * NSA_DH:col0 + (h + 1) * NSA_DH] = (acc_ref[0:NSA_DH, cols] * inv[:, cols]).T


def _sel_attn_kernel(qt_ref, k_ref, vt_ref, selt_ref, o_ref, acc_ref, s_ref, qa_ref):
    qi = pl.program_id(1)
    width = NSA_HPG * ATT_TQ
    groups = range(NSA_GROUPS)
    blocks_per_tile = ATT_TK // SEL_LEN
    bias_rows = 2 * SUBLANES
    n_loop = qi * (ATT_TQ // ATT_TK)
    n_steps = (n_loop + 1) // 2 * 2
    tile_of = lambda step: jnp.where(step < n_loop, step, qi)
    rowid = lax.broadcasted_iota(jnp.int32, (bias_rows, ATT_TQ), 0)
    tri = (lax.broadcasted_iota(jnp.int32, (ATT_TK, ATT_TQ), 0) <=
           lax.broadcasted_iota(jnp.int32, (ATT_TK, ATT_TQ), 1))
    tri = jnp.concatenate([tri] * NSA_HPG, axis=1)

    def scores(g, step):
        kt = tile_of(step)
        threshold = jnp.where((step < n_loop) | (step == n_steps), 0.5, 2.0)
        bias = jnp.zeros((bias_rows, ATT_TQ), F32)
        for jb in range(blocks_per_tile):
            picked = selt_ref[g, pl.ds(kt * blocks_per_tile + jb, 1), :] > threshold
            bias = jnp.where(rowid == jb, jnp.where(picked, 0.0, NEG_INF), bias)
        qa_ref[g, NSA_DH:NSA_DH + bias_rows, :] = jnp.concatenate([bias] * NSA_HPG, axis=1).astype(qa_ref.dtype)
        k = k_ref[pl.ds(pl.multiple_of(kt * ATT_TK, ATT_TK), ATT_TK), g * SEL_KW:(g + 1) * SEL_KW]
        return _dot(k, qa_ref[g])

    def half_step(g, step, cur, nxt, m, diagonal=False):
        if not diagonal:
            s_ref[g, nxt] = scores(g, step + 1)
        s = jnp.where(tri, s_ref[g, cur], NEG_INF) if diagonal else s_ref[g, cur]
        return _softmax_step(s, m, vt_ref[g, tile_of(step)], acc_ref.at[g])

    def body(i, ms):
        ms = [half_step(g, 2 * i, 0, 1, ms[g]) for g in groups]
        return tuple(half_step(g, 2 * i + 1, 1, 0, ms[g]) for g in groups)

    for g in groups:
        qa_ref[g, 0:NSA_DH, :] = qt_ref[g]
        qa_ref[g, NSA_DH + bias_rows:, :] = jnp.zeros((SEL_KW - NSA_DH - bias_rows, width), qa_ref.dtype)
        acc_ref[g] = jnp.zeros(acc_ref.shape[1:], F32)
        s_ref[g, 0] = scores(g, 0)
    ms = lax.fori_loop(0, n_steps // 2, body, tuple(jnp.full((1, width), NEG_INF, F32) for _ in groups))
    for g in groups:
        half_step(g, n_steps, 0, 1, ms[g], diagonal=True)
        _flash_finish(o_ref, acc_ref.at[g], g * NSA_HPG * NSA_DH)


def _selected_attention(qt, k, vt, selt, bsz, seq):
    n = bsz * seq
    nt = seq // ATT_TQ
    nkt = seq // ATT_TK
    nb = seq // SEL_LEN
    width = NSA_HPG * ATT_TQ
    return pl.pallas_call(
        _sel_attn_kernel,
        grid=(bsz, nt),
        in_specs=[pl.BlockSpec((None, NSA_GROUPS, None, NSA_DH, width), lambda b, i: (b, 0, i, 0, 0)),
                  pl.BlockSpec((seq, NSA_GROUPS * SEL_KW), lambda b, i: (b, 0)),
                  pl.BlockSpec((None, NSA_GROUPS, nkt, VT_ROWS, ATT_TK), lambda b, i: (b, 0, 0, 0, 0)),
                  pl.BlockSpec((None, NSA_GROUPS, nb, ATT_TQ), lambda b, i: (b, 0, 0, i))],
        out_specs=pl.BlockSpec((ATT_TQ, NSA_Q), lambda b, i: (b * nt + i, 0)),
        out_shape=jax.ShapeDtypeStruct((n, NSA_Q), F32),
        scratch_shapes=[pltpu.VMEM((NSA_GROUPS, VT_ROWS, width), F32),
                        pltpu.VMEM((NSA_GROUPS, 2, ATT_TK, width), F32),
                        pltpu.VMEM((NSA_GROUPS, SEL_KW, width), MXU_DTYPE)],
        compiler_params=_params("parallel", "parallel"),
        name="selected_attention",
    )(qt, k, vt, selt)


def _win_attn_kernel(qt_ref, k_ref, vt_ref, o_ref, acc_ref, s0_ref, s1_ref, m_ref):
    qi = pl.program_id(1)
    groups = range(NSA_GROUPS)
    row = lax.broadcasted_iota(jnp.int32, (ATT_TK, ATT_TQ), 0)
    col = lax.broadcasted_iota(jnp.int32, (ATT_TK, ATT_TQ), 1)
    causal = jnp.concatenate([row <= col] * NSA_HPG, axis=1)
    window_tail = jnp.concatenate([row > col] * NSA_HPG, axis=1)

    def scores(g, kt):
        k = k_ref[pl.ds(pl.multiple_of(kt * ATT_TK, ATT_TK), ATT_TK), g * NSA_DH:(g + 1) * NSA_DH]
        return _dot(k, qt_ref[g])

    for g in groups:
        acc_ref[g] = jnp.zeros(acc_ref.shape[1:], F32)
        s0_ref[g] = scores(g, qi)
    for g in groups:
        s1_ref[g] = scores(g, jnp.maximum(qi - 1, 0))
        m_ref[g] = _softmax_step(jnp.where(causal, s0_ref[g], NEG_INF), jnp.full(m_ref.shape[1:], NEG_INF, F32),
                                 vt_ref[g, qi], acc_ref.at[g])

    @pl.when(qi >= 1)
    def _():
        for g in groups:
            s0_ref[g] = scores(g, jnp.maximum(qi - 2, 0))
            m_ref[g] = _softmax_step(s1_ref[g], m_ref[g], vt_ref[g, qi - 1], acc_ref.at[g])

    @pl.when(qi >= 2)
    def _():
        for g in groups:
            _softmax_step(jnp.where(window_tail, s0_ref[g], NEG_INF), m_ref[g], vt_ref[g, qi - 2], acc_ref.at[g])

    for g in groups:
        _flash_finish(o_ref, acc_ref.at[g], g * NSA_HPG * NSA_DH)


def _window_attention(qt, k, vt, bsz, seq):
    n = bsz * seq
    nt = seq // ATT_TQ
    nkt = seq // ATT_TK
    width = NSA_HPG * ATT_TQ
    return pl.pallas_call(
        _win_attn_kernel,
        grid=(bsz, nt),
        in_specs=[pl.BlockSpec((None, NSA_GROUPS, None, NSA_DH, width), lambda b, i: (b, 0, i, 0, 0)),
                  pl.BlockSpec((seq, NSA_KV), lambda b, i: (b, 0)),
                  pl.BlockSpec((None, NSA_GROUPS, nkt, VT_ROWS, ATT_TK), lambda b, i: (b, 0, 0, 0, 0))],
        out_specs=pl.BlockSpec((ATT_TQ, NSA_Q), lambda b, i: (b * nt + i, 0)),
        out_shape=jax.ShapeDtypeStruct((n, NSA_Q), F32),
        scratch_shapes=[pltpu.VMEM((NSA_GROUPS, VT_ROWS, width), F32),
                        pltpu.VMEM((NSA_GROUPS, ATT_TK, width), F32),
                        pltpu.VMEM((NSA_GROUPS, ATT_TK, width), F32),
                        pltpu.VMEM((NSA_GROUPS, 1, width), F32)],
        compiler_params=_params("parallel", "parallel"),
        name="window_attention",
    )(qt, k, vt)


def _merge_kernel(x_ref, yr_ref, oc_ref, os_ref, ow_ref, ngl_ref, g_ref, wg_ref, wr_ref, wn_ref, wo_ref, o_ref):
    tm = x_ref.shape[0]
    hb = _rms(x_ref[...], g_ref[...]).astype(MXU_DTYPE)
    ga = jnp.dot(hb, wg_ref[:, :D_MODEL], preferred_element_type=F32)
    gb = jnp.dot(hb, wg_ref[:, D_MODEL:], preferred_element_type=F32)
    gates = jax.nn.sigmoid(ngl_ref[...])
    parts = []
    for h in range(NSA_HEADS):
        cols = slice(h * NSA_DH, (h + 1) * NSA_DH)

        def gate(br):
            return jnp.broadcast_to(gates[:, 3 * h + br:3 * h + br + 1], (tm, NSA_DH))

        parts.append(gate(0) * oc_ref[:, cols] + gate(1) * os_ref[:, cols] + gate(2) * ow_ref[:, cols])
    o_nsa = jnp.concatenate(parts, axis=1)
    y_ret = _dot(yr_ref[...], wr_ref[...])
    y_nsa = _dot(o_nsa, wn_ref[...])
    y = jax.nn.sigmoid(ga) * y_ret + jax.nn.sigmoid(gb) * y_nsa
    o_ref[...] = x_ref[...] + _dot(y, wo_ref[...])


def _merge(x2d, y_ret, o_cmp, o_sel, o_win, ngl, g_mix, w_gates, w_ret_o, w_nsa_o, w_out):
    n = x2d.shape[0]
    tm = ROW_TM
    row = lambda i: (i, 0)
    const = lambda i: (0, 0)
    wide = pl.BlockSpec((tm, D_MODEL), row)
    wspec = pl.BlockSpec((D_MODEL, D_MODEL), const)
    return pl.pallas_call(
        _merge_kernel,
        grid=(n // tm,),
        in_specs=[wide, wide, wide, wide, wide,
                  pl.BlockSpec((tm, LANES), row),
                  pl.BlockSpec((1, D_MODEL), const),
                  pl.BlockSpec((D_MODEL, 2 * D_MODEL), const),
                  wspec, wspec, wspec],
        out_specs=wide,
        out_shape=jax.ShapeDtypeStruct((n, D_MODEL), F32),
        compiler_params=_params("parallel"),
        name="merge",
    )(x2d, y_ret, o_cmp, o_sel, o_win, ngl, g_mix, w_gates, w_ret_o, w_nsa_o, w_out)


def _mem_kv_kernel(m_ref, g_ref, w_ref, o_ref):
    o_ref[...] = _dot(_rms(m_ref[...], g_ref[...]), w_ref[...]).astype(o_ref.dtype)


def _mem_kv(mem2d, g, w_xkv, bsz):
    nm = mem2d.shape[0] // bsz
    return pl.pallas_call(
        _mem_kv_kernel,
        grid=(bsz,),
        in_specs=[pl.BlockSpec((nm, D_MODEL), lambda b: (b, 0)),
                  pl.BlockSpec((1, D_MODEL), lambda b: (0, 0)),
                  pl.BlockSpec((D_MODEL, 2 * D_MODEL), lambda b: (0, 0))],
        out_specs=pl.BlockSpec((nm, 2 * D_MODEL), lambda b: (b, 0)),
        out_shape=jax.ShapeDtypeStruct((mem2d.shape[0], 2 * D_MODEL), MXU_DTYPE),
        compiler_params=_params("parallel"),
        name="mem_kv",
    )(mem2d, g, w_xkv)


def _pack_pairs(x):
    half = x.shape[1] // 2
    hi = lax.bitcast_convert_type(x[:, :half].astype(MXU_DTYPE).astype(F32), jnp.uint32)
    lo = lax.bitcast_convert_type(x[:, half:].astype(MXU_DTYPE).astype(F32), jnp.uint32)
    return (hi & jnp.uint32(0xFFFF0000)) | (lo >> 16)


def _unpack_pairs(u):
    hi = lax.bitcast_convert_type(u & jnp.uint32(0xFFFF0000), F32)
    lo = lax.bitcast_convert_type(u << 16, F32)
    return jnp.concatenate([hi, lo], axis=1)


_ROUTER_E0 = 2 * SUBLANES
_ROUTER_ROWS = _ROUTER_E0 + N_EXPERTS


def _top2_route(lgt):
    sub = SUBLANES
    t = lgt.shape[1]
    rowid = lax.broadcasted_iota(jnp.int32, (sub, t), 0)
    first = lambda hit: jnp.min(jnp.where(hit, rowid, sub), axis=0, keepdims=True)
    lg = jnp.where(rowid < N_EGROUPS, lgt[0:sub], NEG_INF)
    gmax = jnp.max(lg, axis=0, keepdims=True)
    grp = first(lg == gmax)
    g_gate = 1.0 / jnp.sum(jnp.exp(lg - gmax), axis=0, keepdims=True)
    experts_of = lambda g: lgt[_ROUTER_E0 + g * EXP_PER_GROUP:_ROUTER_E0 + (g + 1) * EXP_PER_GROUP]
    le = experts_of(N_EGROUPS - 1)
    for g in range(N_EGROUPS - 2, -1, -1):
        le = jnp.where(grp == g, experts_of(g), le)
    ex = jnp.exp(le - jnp.max(le, axis=0, keepdims=True))
    pe = ex / jnp.sum(ex, axis=0, keepdims=True)
    p0 = jnp.max(pe, axis=0, keepdims=True)
    i0 = first(pe == p0)
    rest = jnp.where(rowid == i0, -1.0, pe)
    p1 = jnp.max(rest, axis=0, keepdims=True)
    i1 = first(rest == p1)
    den = p0 + p1
    base = grp * EXP_PER_GROUP
    return jnp.concatenate([(base + i0).astype(F32), (base + i1).astype(F32),
                            g_gate * p0 / den, g_gate * p1 / den], axis=0)


def _cross_kernel(x_ref, kv_ref, gx_ref, wq_ref, wo_ref, gf_ref, wr_ref, br_ref, x2_ref, hf_ref, rt_ref):
    x = x_ref[...]
    q = _dot(_rms(x, gx_ref[...]), wq_ref[...])
    scores = [_dot_nt(q[:, h * X_DH:(h + 1) * X_DH], kv_ref[:, h * X_DH:(h + 1) * X_DH]) * (X_DH ** -0.5)
              for h in range(X_HEADS)]
    probs = []
    for s in scores:
        e = jnp.exp(s - jnp.max(s, axis=-1, keepdims=True))
        probs.append(e / jnp.sum(e, axis=-1, keepdims=True))
    heads = [_dot(p, kv_ref[:, D_MODEL + h * X_DH:D_MODEL + (h + 1) * X_DH]) for h, p in enumerate(probs)]
    x2 = x + _dot(jnp.concatenate(heads, axis=1), wo_ref[...])
    x2_ref[...] = x2
    hf = _rms(x2, gf_ref[...])
    hf_ref[...] = _pack_pairs(hf)
    h_hi, h_mid, _ = _split3(hf)
    w_hi = wr_ref[0]
    w_mid = wr_ref[1]
    lgt = (_dot_nt(w_hi, h_hi) + (_dot_nt(w_hi, h_mid) + _dot_nt(w_mid, h_hi))) + br_ref[...]
    rt_ref[...] = jnp.concatenate([_top2_route(lgt), jnp.zeros((SUBLANES - 4, lgt.shape[1]), F32)], axis=0)


def _cross(x1, kv, gx, w_xq, w_xo, gf, w_router, b_router, bsz, seq):
    n = x1.shape[0]
    tm = ROW_TM
    nt = seq // tm
    nm = kv.shape[0] // bsz
    row = lambda i: (i, 0)
    const = lambda i: (0, 0)
    vec = pl.BlockSpec((1, D_MODEL), const)
    wspec = pl.BlockSpec((D_MODEL, D_MODEL), const)
    return pl.pallas_call(
        _cross_kernel,
        grid=(n // tm,),
        in_specs=[pl.BlockSpec((tm, D_MODEL), row),
                  pl.BlockSpec((nm, 2 * D_MODEL), lambda i: (i // nt, 0)),
                  vec, wspec, wspec, vec,
                  pl.BlockSpec((2, _ROUTER_ROWS, D_MODEL), lambda i: (0, 0, 0)),
                  pl.BlockSpec((_ROUTER_ROWS, 1), const)],
        out_specs=[pl.BlockSpec((tm, D_MODEL), row),
                   pl.BlockSpec((tm, D_MODEL // 2), row),
                   pl.BlockSpec((SUBLANES, tm), lambda i: (0, i))],
        out_shape=[jax.ShapeDtypeStruct((n, D_MODEL), F32),
                   jax.ShapeDtypeStruct((n, D_MODEL // 2), jnp.uint32),
                   jax.ShapeDtypeStruct((SUBLANES, n), F32)],
        compiler_params=_params("parallel"),
        name="cross_attention",
    )(x1, kv, gx, w_xq, w_xo, gf, w_router, b_router)


def _expert_kernel(blk_ref, xb_ref, w1_ref, w3_ref, w2_ref, o_ref):
    n_used = blk_ref[pl.num_programs(0)]

    @pl.when(pl.program_id(0) < n_used)
    def _():
        xb = _unpack_pairs(xb_ref[...]).astype(MXU_DTYPE)
        a = _dot(xb, w1_ref[...])
        b = _dot(xb, w3_ref[...])
        o_ref[...] = _dot(a * jax.nn.sigmoid(a) * b, w2_ref[...])

    @pl.when(pl.program_id(0) >= n_used)
    def _():
        o_ref[...] = jnp.zeros_like(o_ref)


def _experts(blk, xb, w1, w3, w2):
    cap = xb.shape[0]
    nblk = cap // MOE_BLOCK
    row = lambda i, e: (i, 0)
    by_expert = lambda i, e: (e[i], 0, 0)
    grid_spec = pltpu.PrefetchScalarGridSpec(
        num_scalar_prefetch=1,
        grid=(nblk,),
        in_specs=[pl.BlockSpec((MOE_BLOCK, D_MODEL // 2), row),
                  pl.BlockSpec((None, D_MODEL, D_EXPERT), by_expert),
                  pl.BlockSpec((None, D_MODEL, D_EXPERT), by_expert),
                  pl.BlockSpec((None, D_EXPERT, D_MODEL), by_expert)],
        out_specs=pl.BlockSpec((MOE_BLOCK, D_MODEL), row),
    )
    return pl.pallas_call(
        _expert_kernel,
        grid_spec=grid_spec,
        out_shape=jax.ShapeDtypeStruct((cap, D_MODEL), F32),
        compiler_params=_params("arbitrary"),
        name="experts",
    )(blk, xb, w1, w3, w2)


def _final_kernel(x_ref, w_ref, g_ref, *refs):
    o_ref = refs[-1]
    tiles_per_split = pl.num_programs(0) // MOE_SPLITS
    for s in range(MOE_SPLITS):
        @pl.when(pl.program_id(0) // tiles_per_split == s)
        def _(s=s):
            moe = w_ref[:, 0:1] * refs[2 * s][...] + w_ref[:, 1:2] * refs[2 * s + 1][...]
            o_ref[...] = _rms(x_ref[...] + moe, g_ref[...])


def _final(x2, wts, g, ys):
    n = x2.shape[0]
    tm = ROW_TM
    tiles_per_split = n // tm // MOE_SPLITS
    row = lambda i: (i, 0)
    wide = pl.BlockSpec((tm, D_MODEL), row)

    def split_spec(s):
        return pl.BlockSpec((tm, D_MODEL), lambda i: (jnp.clip(i - s * tiles_per_split, 0, tiles_per_split - 1), 0))

    return pl.pallas_call(
        _final_kernel,
        grid=(n // tm,),
        in_specs=[wide, pl.BlockSpec((tm, EXP_TOPK), row), pl.BlockSpec((1, D_MODEL), lambda i: (0, 0))] + [
            split_spec(s) for s in range(MOE_SPLITS) for _ in range(EXP_TOPK)],
        out_specs=wide,
        out_shape=jax.ShapeDtypeStruct((n, D_MODEL), F32),
        compiler_params=_params("parallel"),
        name="final_norm",
    )(x2, wts, g, *[y for pair in ys for y in pair])


def _route(eid, tok0):
    n_tok = eid.shape[1]
    eid = eid.reshape(-1)
    n_asg = eid.shape[0]
    iota = jnp.arange(n_asg, dtype=jnp.int32)
    _, order = lax.sort_key_val(eid, iota)
    counts = jnp.sum((jnp.arange(N_EXPERTS, dtype=jnp.int32)[:, None] == eid[None, :]).astype(jnp.int32), axis=1)
    padded = (counts + MOE_BLOCK - 1) // MOE_BLOCK * MOE_BLOCK
    starts = jnp.cumsum(counts) - counts
    pends = jnp.cumsum(padded)
    pstarts = pends - padded
    cap = ((n_asg + MOE_BLOCK - 1) // MOE_BLOCK + N_EXPERTS) * MOE_BLOCK
    nblk = cap // MOE_BLOCK
    blk_e = jnp.minimum(jnp.searchsorted(pends, jnp.arange(nblk) * MOE_BLOCK, side='right', method='compare_all'),
                        N_EXPERTS - 1).astype(jnp.int32)
    per_row = lambda a: jnp.repeat(a[blk_e], MOE_BLOCK)
    row = jnp.arange(cap, dtype=jnp.int32)
    off = row - per_row(pstarts)
    asg = order[jnp.clip(per_row(starts) + off, 0, n_asg - 1)]
    buf_tok = tok0 + jnp.where(off < per_row(counts), asg % n_tok, row % n_tok)
    shift = pstarts - starts
    jumps = jnp.zeros((n_asg,), jnp.int32).at[starts[1:]].add(shift[1:] - shift[:-1])
    dest_sorted = iota + shift[0] + jnp.cumsum(jumps)
    _, pos = lax.sort_key_val(order, dest_sorted)
    n_used = (pends[-1] // MOE_BLOCK).astype(jnp.int32)
    return buf_tok, jnp.concatenate([blk_e, n_used[None]]), pos.reshape(EXP_TOPK, n_tok)


def kernel(x, mem, norm_mix_g, w_in, w_ret_o, w_nsa_o, w_out, cmp_pe_k, cmp_w1_k, cmp_w2_k, cmp_pe_v,
           cmp_w1_v, cmp_w2_v, norm_x_g, norm_mem_g, w_xq, w_xkv, w_xo, norm_ffn_g, w_grp, b_grp, w_rt,
           b_rt, w_e1, w_e3, w_e2, norm_f_g):
    bsz, seq, _ = x.shape
    n = bsz * seq
    assert seq % PROJ_TM == 0 and seq % (2 * ATT_TQ) == 0 and w_in.shape[0] == 1
    assert n % (ROW_TM * MOE_SPLITS) == 0
    cast = lambda a: a.astype(MXU_DTYPE)
    xc = x.reshape(n, D_MODEL)
    l = 0

    w_main, w_t, w_gates = _split_w_in(w_in[l])
    qk, rv, rg, ckv, sk, wk, ngl, qt, svt, wvt = _proj(
        xc, norm_mix_g[l][None, :], w_main, w_t, _rope_tables(seq), bsz, seq)
    y_ret = _retention(qk, rv, rg, bsz, seq)
    w2 = jnp.stack([cmp_w2_k[l], cmp_w2_v[l]])
    cmp_k, cmp_vt = _compress(ckv, jnp.stack([cmp_pe_k[l], cmp_pe_v[l]]),
                              cast(jnp.stack([cmp_w1_k[l], cmp_w1_v[l]])),
                              cast(w2), cast(w2.transpose(0, 2, 1)), bsz, seq)
    o_cmp, selt = _cmp_attention(qt, cmp_k, cmp_vt, bsz, seq)
    o_sel = _selected_attention(qt, sk, svt, selt, bsz, seq)
    o_win = _window_attention(qt, wk, wvt, bsz, seq)
    x1 = _merge(xc, y_ret, o_cmp, o_sel, o_win, ngl, norm_mix_g[l][None, :], w_gates,
                cast(w_ret_o[l]), cast(w_nsa_o[l]), cast(w_out[l]))

    kv = _mem_kv(mem.reshape(-1, D_MODEL), norm_mem_g[l][None, :], cast(w_xkv[l]), bsz)
    gap = _ROUTER_E0 - N_EGROUPS
    w_router = jnp.concatenate([w_grp[l].T, jnp.zeros((gap, D_MODEL), F32), w_rt[l].T], axis=0)
    wr_hi = cast(w_router)
    wr_mid = cast(w_router - wr_hi.astype(F32))
    b_router = jnp.concatenate([b_grp[l], jnp.zeros((gap,), F32), b_rt[l]])[:, None]
    x2, hf, routed = _cross(x1, kv, norm_x_g[l][None, :], cast(w_xq[l]), cast(w_xo[l]),
                            norm_ffn_g[l][None, :], jnp.stack([wr_hi, wr_mid]), b_router, bsz, seq)

    eid = routed[0:EXP_TOPK].astype(jnp.int32)
    wts = routed[EXP_TOPK:2 * EXP_TOPK].T
    per_split = n // MOE_SPLITS
    ys = []
    for s in range(MOE_SPLITS):
        buf_tok, blk, pos = _route(eid[:, s * per_split:(s + 1) * per_split], s * per_split)
        y = _experts(blk, hf[buf_tok], w_e1[l], w_e3[l], w_e2[l])
        ys.append([y[pos[j]] for j in range(EXP_TOPK)])
    out = _final(x2, wts, norm_f_g[None, :], ys)
    return out.reshape(bsz, seq, D_MODEL)
```

```python
import functools

import numpy as np
import jax
import jax.numpy as jnp
from jax import lax
from jax.experimental import pallas as pl
from jax.experimental.pallas import tpu as pltpu

MXU_DTYPE = jnp.bfloat16
F32 = jnp.float32

D_MODEL = 1024
N_MEM = 256
EPS = 1e-6
NEG_INF = -1e30
SEL_FORCE = 1e4

R_HEADS = 4
R_DK = 128
R_DV = 256
R_CHUNK = 128
ROPE_BASE = 10000.0

NSA_HEADS = 8
NSA_GROUPS = 2
NSA_HPG = NSA_HEADS // NSA_GROUPS
NSA_DH = 128
CMP_LEN = 32
CMP_STRIDE = 16
SEL_LEN = 64
SEL_TOPK = 16
WINDOW = 512

X_HEADS = 4
X_DH = D_MODEL // X_HEADS

N_EGROUPS = 4
EXP_PER_GROUP = 8
N_EXPERTS = N_EGROUPS * EXP_PER_GROUP
EXP_TOPK = 2
D_EXPERT = 512
MOE_BLOCK = 512

RET_QK = R_HEADS * R_DK
RET_V = R_HEADS * R_DV
NSA_Q = NSA_HEADS * NSA_DH
NSA_KV = NSA_GROUPS * NSA_DH
SPLITS = (RET_QK, RET_QK, RET_V, RET_V, NSA_Q, NSA_KV, NSA_KV, NSA_KV, NSA_KV, NSA_KV, NSA_KV,
          3 * NSA_HEADS, D_MODEL, D_MODEL)

_EXP2_SCALE = (NSA_DH ** -0.5) * float(np.log2(np.e))

LANES = 128
SUBLANES = 8
VMEM_LIMIT = 56 * 1024 * 1024

PROJ_TM = 512
RET_ROWS = 512
ATT_TQ = 256
ATT_TK = 256
ROW_TM = 512
MOE_SPLITS = 2
VT_ROWS = NSA_DH + 2 * SUBLANES
SEL_KW = 2 * NSA_DH
assert ATT_TQ == ATT_TK and SEL_LEN * 2 * SUBLANES >= ATT_TK and WINDOW == 2 * ATT_TK


def _params(*sem):
    return pltpu.CompilerParams(dimension_semantics=sem, vmem_limit_bytes=VMEM_LIMIT)


def _dot(a, b):
    return jnp.dot(a.astype(MXU_DTYPE), b.astype(MXU_DTYPE), preferred_element_type=F32)


def _dot_nt(a, b):
    return lax.dot_general(a.astype(MXU_DTYPE), b.astype(MXU_DTYPE), (((1,), (1,)), ((), ())),
                           preferred_element_type=F32)


def _dot_tn(a, b):
    return lax.dot_general(a.astype(MXU_DTYPE), b.astype(MXU_DTYPE), (((0,), (0,)), ((), ())),
                           preferred_element_type=F32)


def _split3(p):
    hi = p.astype(MXU_DTYPE)
    r1 = p - hi.astype(F32)
    mid = r1.astype(MXU_DTYPE)
    lo = (r1 - mid.astype(F32)).astype(MXU_DTYPE)
    return hi, mid, lo


def _rms(x, g):
    return x * lax.rsqrt(jnp.mean(x * x, axis=-1, keepdims=True) + EPS) * g


_C_RQK = 0
_C_RV = _C_RQK + 2 * RET_QK
_C_RG = _C_RV + RET_V
_C_CKV = _C_RG + RET_V
_C_SK = _C_CKV + 2 * NSA_KV
_C_WK = _C_SK + NSA_KV
_C_NGL = _C_WK + NSA_KV
_C_END = _C_NGL + LANES
_R_NQ = 0
_R_SV = _R_NQ + NSA_Q
_R_WV = _R_SV + NSA_KV
_R_END = _R_WV + NSA_KV


def _proj_kernel(x_ref, g_ref, w_ref, wt_ref, cos_ref, sin_up_ref, sin_dn_ref,
                 qk_ref, rv_ref, rg_ref, ckv_ref, sk_ref, wk_ref, ngl_ref, nqt_ref, svt_ref, wvt_ref):
    hb = _rms(x_ref[...], g_ref[...]).astype(MXU_DTYPE)
    tm = hb.shape[0]

    def mm(off, width):
        return jnp.dot(hb, w_ref[:, off:off + width], preferred_element_type=F32)

    def mm_t(off, height):
        return _dot_nt(wt_ref[off:off + height, :], hb)

    cos = cos_ref[...]
    sin_up = sin_up_ref[...]
    sin_dn = sin_dn_ref[...]
    qk = mm(_C_RQK, 2 * RET_QK)
    for i in range(2 * R_HEADS):
        t = qk[:, i * R_DK:(i + 1) * R_DK]
        r = t * cos + pltpu.roll(t, R_DK - 1, axis=1) * sin_up + pltpu.roll(t, 1, axis=1) * sin_dn
        if i >= R_HEADS:
            r = r * (R_DK ** -0.5)
        qk_ref[:, i * R_DK:(i + 1) * R_DK] = r.astype(qk_ref.dtype)
    rv_ref[...] = mm(_C_RV, RET_V).astype(rv_ref.dtype)
    rg_ref[...] = mm(_C_RG, RET_V)
    ckv_ref[...] = mm(_C_CKV, 2 * NSA_KV)
    sk = mm(_C_SK, NSA_KV).astype(sk_ref.dtype)
    blk = (lax.broadcasted_iota(jnp.int32, (tm, NSA_DH), 0) % ATT_TK) // SEL_LEN
    onehot = jnp.where(lax.broadcasted_iota(jnp.int32, (tm, NSA_DH), 1) == blk, 1.0, 0.0).astype(sk_ref.dtype)
    for g in range(NSA_GROUPS):
        sk_ref[:, g * SEL_KW:g * SEL_KW + NSA_DH] = sk[:, g * NSA_DH:(g + 1) * NSA_DH]
        sk_ref[:, g * SEL_KW + NSA_DH:(g + 1) * SEL_KW] = onehot
    wk_ref[...] = mm(_C_WK, NSA_KV).astype(wk_ref.dtype)
    ngl_ref[...] = mm(_C_NGL, LANES)

    nqt = (mm_t(_R_NQ, NSA_Q) * _EXP2_SCALE).astype(nqt_ref.dtype)
    for g in range(NSA_GROUPS):
        for j in range(tm // ATT_TQ):
            for hh in range(NSA_HPG):
                head = g * NSA_HPG + hh
                nqt_ref[g, j, :, hh * ATT_TQ:(hh + 1) * ATT_TQ] = (
                    nqt[head * NSA_DH:(head + 1) * NSA_DH, j * ATT_TQ:(j + 1) * ATT_TQ])
    for off, out_ref in ((_R_SV, svt_ref), (_R_WV, wvt_ref)):
        vt = mm_t(off, NSA_KV).astype(out_ref.dtype)
        for g in range(NSA_GROUPS):
            for j in range(tm // ATT_TK):
                out_ref[g, j, 0:NSA_DH] = vt[g * NSA_DH:(g + 1) * NSA_DH, j * ATT_TK:(j + 1) * ATT_TK]
                out_ref[g, j, NSA_DH:VT_ROWS] = jnp.ones((VT_ROWS - NSA_DH, ATT_TK), out_ref.dtype)


def _split_w_in(w_in):
    offs = np.cumsum((0,) + SPLITS)
    rq, rk, rv, rg, nq, ck, cv, sk, sv, wk, wv, ngl, ga, gb = [
        w_in[:, offs[i]:offs[i + 1]] for i in range(len(SPLITS))]
    ngl = jnp.pad(ngl, ((0, 0), (0, LANES - ngl.shape[1])))
    w = jnp.concatenate([rq, rk, rv, rg, ck, cv, sk, wk, ngl], axis=1).astype(MXU_DTYPE)
    wt = jnp.concatenate([nq, sv, wv], axis=1).T.astype(MXU_DTYPE)
    w_gates = jnp.concatenate([ga, gb], axis=1).astype(MXU_DTYPE)
    return w, wt, w_gates


def _rope_tables(seq):
    pos = jnp.arange(seq, dtype=F32)
    inv_freq = ROPE_BASE ** (-jnp.arange(0, R_DK, 2, dtype=F32) / R_DK)
    ang = pos[:, None] * inv_freq[None, :]
    cos = jnp.repeat(jnp.cos(ang), 2, axis=1)
    sin = jnp.repeat(jnp.sin(ang), 2, axis=1)
    even = (jnp.arange(R_DK) % 2 == 0)[None, :]
    return cos, jnp.where(even, -sin, 0.0), jnp.where(even, 0.0, sin)


def _proj(x2d, g, w, wt, tables, bsz, seq):
    n = x2d.shape[0]
    tm = PROJ_TM
    nt = seq // tm
    row = lambda i: (i, 0)
    const = lambda i: (0, 0)
    tile = lambda i: (i // nt, 0, i % nt, 0, 0)
    table = pl.BlockSpec((tm, R_DK), lambda i: (i % nt, 0))
    widths = (2 * RET_QK, RET_V, RET_V, 2 * NSA_KV, NSA_GROUPS * SEL_KW, NSA_KV, LANES)
    dtypes = (MXU_DTYPE, MXU_DTYPE, F32, F32, MXU_DTYPE, MXU_DTYPE, F32)
    width = NSA_HPG * ATT_TQ
    vt_shape = jax.ShapeDtypeStruct((bsz, NSA_GROUPS, seq // ATT_TK, VT_ROWS, ATT_TK), MXU_DTYPE)
    vt_spec = pl.BlockSpec((None, NSA_GROUPS, tm // ATT_TK, VT_ROWS, ATT_TK), tile)
    return pl.pallas_call(
        _proj_kernel,
        grid=(n // tm,),
        in_specs=[pl.BlockSpec((tm, D_MODEL), row),
                  pl.BlockSpec((1, D_MODEL), const),
                  pl.BlockSpec((D_MODEL, _C_END), const, pipeline_mode=pl.Buffered(1)),
                  pl.BlockSpec((_R_END, D_MODEL), const, pipeline_mode=pl.Buffered(1)),
                  table, table, table],
        out_specs=[pl.BlockSpec((tm, wd), row) for wd in widths] + [
            pl.BlockSpec((None, NSA_GROUPS, tm // ATT_TQ, NSA_DH, width), tile), vt_spec, vt_spec],
        out_shape=[jax.ShapeDtypeStruct((n, wd), dt) for wd, dt in zip(widths, dtypes)] + [
            jax.ShapeDtypeStruct((bsz, NSA_GROUPS, seq // ATT_TQ, NSA_DH, width), MXU_DTYPE), vt_shape, vt_shape],
        compiler_params=_params("parallel"),
        name="proj",
    )(x2d, g, w, wt, *tables)


def _ret_kernel(qk_ref, v_ref, g_ref, decay_ref, zeta_ref, xi_ref, cd_ref, y_ref, state_ref):
    @pl.when(pl.program_id(1) == 0)
    def _():
        state_ref[...] = jnp.zeros_like(state_ref)

    chunks = range(RET_ROWS // R_CHUNK)
    heads = range(R_HEADS)
    rows = lambda c: slice(c * R_CHUNK, (c + 1) * R_CHUNK)
    q_of = lambda c, h: qk_ref[rows(c), h * R_DK:(h + 1) * R_DK]
    k_of = lambda c, h: qk_ref[rows(c), RET_QK + h * R_DK:RET_QK + (h + 1) * R_DK]
    v_of = lambda c, h: v_ref[rows(c), h * R_DV:(h + 1) * R_DV]

    scores = {(c, h): _dot_nt(q_of(c, h), k_of(c, h)) * decay_ref[h] for c in chunks for h in heads}
    kv = {(c, h): _dot_tn(k_of(c, h).astype(F32) * zeta_ref[h], v_of(c, h)) for c in chunks for h in heads}
    seen = {}
    for h in heads:
        st = state_ref[h]
        for c in chunks:
            seen[c, h] = st
            st = st * cd_ref[h] + kv[c, h]
        state_ref[h] = st
    for c in chunks:
        for h in heads:
            o = _dot(scores[c, h], v_of(c, h)) + _dot(q_of(c, h), seen[c, h]) * xi_ref[h]
            o = o * lax.rsqrt(jnp.mean(o * o, axis=-1, keepdims=True) + EPS)
            g = g_ref[rows(c), h * R_DV:(h + 1) * R_DV]
            y_ref[rows(c), h * R_DV:(h + 1) * R_DV] = (g * jax.nn.sigmoid(g) * o).astype(y_ref.dtype)


def _retention(qk, rv, rg, bsz, seq):
    n = qk.shape[0]
    nt = seq // RET_ROWS
    log_g = jnp.log1p(-jnp.exp2(-5.0 - jnp.arange(R_HEADS, dtype=F32)))
    idx = jnp.arange(R_CHUNK, dtype=F32)
    diff = idx[:, None] - idx[None, :]
    decay = jnp.where(diff >= 0, jnp.exp(log_g[:, None, None] * jnp.maximum(diff, 0.0)), 0.0)
    zeta = jnp.exp(log_g[:, None] * (R_CHUNK - 1.0 - idx)[None, :])[:, :, None]
    xi = jnp.exp(log_g[:, None] * (idx + 1.0)[None, :])[:, :, None]
    cd = jnp.exp(log_g * R_CHUNK)[:, None, None]
    row = lambda b, i: (b * nt + i, 0)
    const3 = lambda b, i: (0, 0, 0)
    return pl.pallas_call(
        _ret_kernel,
        grid=(bsz, nt),
        in_specs=[pl.BlockSpec((RET_ROWS, 2 * RET_QK), row),
                  pl.BlockSpec((RET_ROWS, RET_V), row),
                  pl.BlockSpec((RET_ROWS, RET_V), row),
                  pl.BlockSpec((R_HEADS, R_CHUNK, R_CHUNK), const3),
                  pl.BlockSpec((R_HEADS, R_CHUNK, 1), const3),
                  pl.BlockSpec((R_HEADS, R_CHUNK, 1), const3),
                  pl.BlockSpec((R_HEADS, 1, 1), const3)],
        out_specs=pl.BlockSpec((RET_ROWS, RET_V), row),
        out_shape=jax.ShapeDtypeStruct((n, RET_V), MXU_DTYPE),
        scratch_shapes=[pltpu.VMEM((R_HEADS, R_DK, R_DV), F32)],
        compiler_params=_params("parallel", "arbitrary"),
        name="retention",
    )(qk, rv, rg, decay, zeta, xi, cd)


def _compress_kernel(x_ref, pe_ref, w1_ref, w2_ref, w2t_ref, o_ref, ot_ref, buf_ref, *, seq):
    ncp = seq // CMP_STRIDE
    buf_ref[0:seq, :] = x_ref[...]
    buf_ref[seq:seq + LANES, :] = jnp.zeros((LANES, NSA_DH), F32)
    acc = jnp.zeros((ncp, NSA_DH), F32)
    for l in range(CMP_LEN):
        xl = buf_ref[pl.ds(l, ncp, stride=CMP_STRIDE), :] + pe_ref[l:l + 1, :]
        acc = acc + _dot(xl, w1_ref[l])
    hid = jax.nn.gelu(acc)
    o_ref[...] = _dot(hid, w2_ref[...]).astype(o_ref.dtype)
    ot_ref[...] = _dot_nt(w2t_ref[...], hid).astype(ot_ref.dtype)


def _compress(ckv, pe, w1, w2, w2t, bsz, seq):
    ncp = seq // CMP_STRIDE
    nj = 2 * NSA_GROUPS
    wsel = lambda b, j: (j // NSA_GROUPS, 0, 0)
    return pl.pallas_call(
        functools.partial(_compress_kernel, seq=seq),
        grid=(bsz, nj),
        in_specs=[pl.BlockSpec((seq, NSA_DH), lambda b, j: (b, j)),
                  pl.BlockSpec((None, CMP_LEN, NSA_DH), wsel),
                  pl.BlockSpec((None, CMP_LEN, NSA_DH, NSA_DH), lambda b, j: (j // NSA_GROUPS, 0, 0, 0)),
                  pl.BlockSpec((None, NSA_DH, NSA_DH), wsel),
                  pl.BlockSpec((None, NSA_DH, NSA_DH), wsel)],
        out_specs=[pl.BlockSpec((None, None, ncp, NSA_DH), lambda b, j: (b, j, 0, 0)),
                   pl.BlockSpec((None, None, NSA_DH, ncp), lambda b, j: (b, j, 0, 0))],
        out_shape=[jax.ShapeDtypeStruct((bsz, nj, ncp, NSA_DH), MXU_DTYPE),
                   jax.ShapeDtypeStruct((bsz, nj, NSA_DH, ncp), MXU_DTYPE)],
        scratch_shapes=[pltpu.VMEM((seq + LANES, NSA_DH), F32)],
        compiler_params=_params("parallel", "parallel"),
        name="compress",
    )(ckv, pe, w1, w2, w2t)


def _cmp_attn_kernel(qt_ref, k_ref, vt_ref, ov_ref, o_ref, selt_ref, p_ref, imp_ref, *, ncp, nb):
    tq = ATT_TQ
    width = NSA_HPG * tq
    groups = range(NSA_GROUPS)
    qi = pl.program_id(1)
    t0 = qi * tq
    any_valid = (t0 + lax.broadcasted_iota(jnp.int32, (1, tq), 1)) >= CMP_LEN - 1
    any_valid = jnp.concatenate([any_valid] * NSA_HPG, axis=1)

    def probabilities(rows):
        t = t0 + lax.broadcasted_iota(jnp.int32, (rows, tq), 1)
        n = lax.broadcasted_iota(jnp.int32, (rows, tq), 0)
        valid = (n * CMP_STRIDE + (CMP_LEN - 1) <= t) & (n < ncp - 1)
        valid = jnp.concatenate([valid] * NSA_HPG, axis=1)
        for g in groups:
            s = jnp.where(valid, _dot(k_ref[g, 0:rows, :], qt_ref[g]), NEG_INF)
            e = jnp.exp2(s - jnp.max(s, axis=0, keepdims=True))
            p_ref[g, 0:rows, :] = e * jnp.where(any_valid, 1.0 / jnp.sum(e, axis=0, keepdims=True), 0.0)
            if rows < ncp:
                p_ref[g, rows:, :] = jnp.zeros((ncp - rows, width), F32)

    n_buckets = 4
    per_bucket = ncp // n_buckets
    visible = (t0 + tq - CMP_LEN) // CMP_STRIDE + 1
    bucket = jnp.minimum((visible + per_bucket - 1) // per_bucket, n_buckets)
    for c in range(1, n_buckets + 1):
        pl.when(bucket == c)(functools.partial(probabilities, c * per_bucket))

    j = lax.broadcasted_iota(jnp.int32, (nb, tq), 0)
    tb = (t0 + lax.broadcasted_iota(jnp.int32, (nb, tq), 1)) // SEL_LEN
    forced = (j == 0) | (j == tb) | (j == tb - 1)
    sub = SUBLANES
    k_sel = min(SEL_TOPK, nb)
    for g in groups:
        p = p_ref[g]
        ot = _dot(vt_ref[g], p)
        for h in range(NSA_HPG):
            o_ref[:, (g * NSA_HPG + h) * NSA_DH:(g * NSA_HPG + h + 1) * NSA_DH] = ot[:, h * tq:(h + 1) * tq].T
        psum = sum(p[:, h * tq:(h + 1) * tq] for h in range(NSA_HPG))

        imp = sum(_dot(ov_ref[...], part) for part in _split3(psum))
        imp_ref[g] = jnp.where(j > tb, -SEL_FORCE, jnp.where(forced, SEL_FORCE, imp))

    def choose(blocks):
        for g in groups:
            imp = imp_ref[g, 0:blocks, :]
            grp = [imp[r * sub:(r + 1) * sub] for r in range(blocks // sub)]
            cnt = [jnp.zeros((sub, tq), F32) for _ in grp]
            for i in range(blocks):
                row = jnp.broadcast_to(imp[i:i + 1, :], (sub, tq))
                for r in range(blocks // sub):
                    if r * sub > i:
                        beats = jnp.where(row >= grp[r], 1.0, 0.0)
                    elif r * sub + sub - 1 < i:
                        beats = jnp.where(row > grp[r], 1.0, 0.0)
                    else:
                        jr = r * sub + lax.broadcasted_iota(jnp.int32, (sub, tq), 0)
                        beats = jnp.where(jr > i, jnp.where(row >= grp[r], 1.0, 0.0),
                                          jnp.where(row > grp[r], 1.0, 0.0))
                    cnt[r] = cnt[r] + beats
            for r in range(blocks // sub):
                selt_ref[g, r * sub:(r + 1) * sub, :] = jnp.where(cnt[r] < k_sel, 1.0, 0.0)
            if blocks < nb:
                selt_ref[g, blocks:, :] = jnp.zeros((nb - blocks, tq), F32)

    for c in range(1, n_buckets + 1):
        pl.when(bucket == c)(functools.partial(choose, c * (nb // n_buckets)))


def _cmp_attention(qt, cmp_k, cmp_vt, bsz, seq):
    n = bsz * seq
    ncp = seq // CMP_STRIDE
    nb = seq // SEL_LEN
    nt = seq // ATT_TQ
    cstart = np.arange(ncp) * CMP_STRIDE
    jstart = np.arange(nb) * SEL_LEN
    ov = ((cstart[None, :] < jstart[:, None] + SEL_LEN) & (cstart[None, :] + CMP_LEN > jstart[:, None])
          & (np.arange(ncp)[None, :] < ncp - 1))
    ov = jnp.asarray(ov, MXU_DTYPE)
    return pl.pallas_call(
        functools.partial(_cmp_attn_kernel, ncp=ncp, nb=nb),
        grid=(bsz, nt),
        in_specs=[pl.BlockSpec((None, NSA_GROUPS, None, NSA_DH, NSA_HPG * ATT_TQ), lambda b, i: (b, 0, i, 0, 0)),
                  pl.BlockSpec((None, NSA_GROUPS, ncp, NSA_DH), lambda b, i: (b, 0, 0, 0)),
                  pl.BlockSpec((None, NSA_GROUPS, NSA_DH, ncp), lambda b, i: (b, 1, 0, 0)),
                  pl.BlockSpec((nb, ncp), lambda b, i: (0, 0))],
        out_specs=[pl.BlockSpec((ATT_TQ, NSA_Q), lambda b, i: (b * nt + i, 0)),
                   pl.BlockSpec((None, NSA_GROUPS, nb, ATT_TQ), lambda b, i: (b, 0, 0, i))],
        out_shape=[jax.ShapeDtypeStruct((n, NSA_Q), F32),
                   jax.ShapeDtypeStruct((bsz, NSA_GROUPS, nb, seq), F32)],
        scratch_shapes=[pltpu.VMEM((NSA_GROUPS, ncp, NSA_HPG * ATT_TQ), F32),
                        pltpu.VMEM((NSA_GROUPS, nb, ATT_TQ), F32)],
        compiler_params=_params("parallel", "parallel"),
        name="cmp_attention",
    )(qt, cmp_k, cmp_vt, ov)


def _softmax_step(s, m, vt, acc_ref):
    m_new = jnp.maximum(m, jnp.max(s, axis=0, keepdims=True))
    p = jnp.exp2(s - m_new)
    acc_ref[...] = jnp.exp2(m - m_new) * acc_ref[...] + _dot(vt, p)
    return m_new


def _flash_finish(o_ref, acc_ref, col0=0):
    inv = 1.0 / acc_ref[NSA_DH:NSA_DH + 1, :]
    for h in range(NSA_HPG):
        cols = slice(h * ATT_TQ, (h + 1) * ATT_TQ)
        o_ref[:, col0 + h * NSA_DH:col0 + (h + 1) * NSA_DH] = (acc_ref[0:NSA_DH, cols] * inv[:, cols]).T


def _sel_attn_kernel(qt_ref, k_ref, vt_ref, selt_ref, o_ref, acc_ref, s_ref, qa_ref):
    qi = pl.program_id(1)
    width = NSA_HPG * ATT_TQ
    groups = range(NSA_GROUPS)
    blocks_per_tile = ATT_TK // SEL_LEN
    bias_rows = 2 * SUBLANES
    n_loop = qi * (ATT_TQ // ATT_TK)
    n_steps = (n_loop + 1) // 2 * 2
    tile_of = lambda step: jnp.where(step < n_loop, step, qi)
    rowid = lax.broadcasted_iota(jnp.int32, (bias_rows, ATT_TQ), 0)
    tri = (lax.broadcasted_iota(jnp.int32, (ATT_TK, ATT_TQ), 0) <=
           lax.broadcasted_iota(jnp.int32, (ATT_TK, ATT_TQ), 1))
    tri = jnp.concatenate([tri] * NSA_HPG, axis=1)

    def scores(g, step):
        kt = tile_of(step)
        threshold = jnp.where((step < n_loop) | (step == n_steps), 0.5, 2.0)
        bias = jnp.zeros((bias_rows, ATT_TQ), F32)
        for jb in range(blocks_per_tile):
            picked = selt_ref[g, pl.ds(kt * blocks_per_tile + jb, 1), :] > threshold
            bias = jnp.where(rowid == jb, jnp.where(picked, 0.0, NEG_INF), bias)
        qa_ref[g, NSA_DH:NSA_DH + bias_rows, :] = jnp.concatenate([bias] * NSA_HPG, axis=1).astype(qa_ref.dtype)
        k = k_ref[pl.ds(pl.multiple_of(kt * ATT_TK, ATT_TK), ATT_TK), g * SEL_KW:(g + 1) * SEL_KW]
        return _dot(k, qa_ref[g])

    def half_step(step, cur, nxt, ms, diagonal=False):
        if not diagonal:
            for g in groups:
                s_ref[g, nxt] = scores(g, step + 1)
        out = []
        for g in groups:
            s = jnp.where(tri, s_ref[g, cur], NEG_INF) if diagonal else s_ref[g, cur]
            out.append(_softmax_step(s, ms[g], vt_ref[g, tile_of(step)], acc_ref.at[g]))
        return tuple(out)

    def body(i, ms):
        return half_step(2 * i + 1, 1, 0, half_step(2 * i, 0, 1, ms))

    for g in groups:
        qa_ref[g, 0:NSA_DH, :] = qt_ref[g]
        qa_ref[g, NSA_DH + bias_rows:, :] = jnp.zeros((SEL_KW - NSA_DH - bias_rows, width), qa_ref.dtype)
        acc_ref[g] = jnp.zeros(acc_ref.shape[1:], F32)
        s_ref[g, 0] = scores(g, 0)
    ms = lax.fori_loop(0, n_steps // 2, body, tuple(jnp.full((1, width), NEG_INF, F32) for _ in groups))
    half_step(n_steps, 0, 1, ms, diagonal=True)
    for g in groups:
        _flash_finish(o_ref, acc_ref.at[g], g * NSA_HPG * NSA_DH)


def _selected_attention(qt, k, vt, selt, bsz, seq):
    n = bsz * seq
    nt = seq // ATT_TQ
    nkt = seq // ATT_TK
    nb = seq // SEL_LEN
    width = NSA_HPG * ATT_TQ
    return pl.pallas_call(
        _sel_attn_kernel,
        grid=(bsz, nt),
        in_specs=[pl.BlockSpec((None, NSA_GROUPS, None, NSA_DH, width), lambda b, i: (b, 0, i, 0, 0)),
                  pl.BlockSpec((seq, NSA_GROUPS * SEL_KW), lambda b, i: (b, 0)),
                  pl.BlockSpec((None, NSA_GROUPS, nkt, VT_ROWS, ATT_TK), lambda b, i: (b, 0, 0, 0, 0)),
                  pl.BlockSpec((None, NSA_GROUPS, nb, ATT_TQ), lambda b, i: (b, 0, 0, i))],
        out_specs=pl.BlockSpec((ATT_TQ, NSA_Q), lambda b, i: (b * nt + i, 0)),
        out_shape=jax.ShapeDtypeStruct((n, NSA_Q), F32),
        scratch_shapes=[pltpu.VMEM((NSA_GROUPS, VT_ROWS, width), F32),
                        pltpu.VMEM((NSA_GROUPS, 2, ATT_TK, width), F32),
                        pltpu.VMEM((NSA_GROUPS, SEL_KW, width), MXU_DTYPE)],
        compiler_params=_params("parallel", "parallel"),
        name="selected_attention",
    )(qt, k, vt, selt)


def _win_attn_kernel(qt_ref, k_ref, vt_ref, o_ref, acc_ref, s0_ref, s1_ref, m_ref):
    qi = pl.program_id(1)
    groups = range(NSA_GROUPS)
    row = lax.broadcasted_iota(jnp.int32, (ATT_TK, ATT_TQ), 0)
    col = lax.broadcasted_iota(jnp.int32, (ATT_TK, ATT_TQ), 1)
    causal = jnp.concatenate([row <= col] * NSA_HPG, axis=1)
    window_tail = jnp.concatenate([row > col] * NSA_HPG, axis=1)

    def scores(g, kt):
        k = k_ref[pl.ds(pl.multiple_of(kt * ATT_TK, ATT_TK), ATT_TK), g * NSA_DH:(g + 1) * NSA_DH]
        return _dot(k, qt_ref[g])

    for g in groups:
        acc_ref[g] = jnp.zeros(acc_ref.shape[1:], F32)
        s0_ref[g] = scores(g, qi)
    for g in groups:
        s1_ref[g] = scores(g, jnp.maximum(qi - 1, 0))
    for g in groups:
        m_ref[g] = _softmax_step(jnp.where(causal, s0_ref[g], NEG_INF), jnp.full(m_ref.shape[1:], NEG_INF, F32),
                                 vt_ref[g, qi], acc_ref.at[g])

    @pl.when(qi >= 1)
    def _():
        for g in groups:
            s0_ref[g] = scores(g, jnp.maximum(qi - 2, 0))
        for g in groups:
            m_ref[g] = _softmax_step(s1_ref[g], m_ref[g], vt_ref[g, qi - 1], acc_ref.at[g])

    @pl.when(qi >= 2)
    def _():
        for g in groups:
            _softmax_step(jnp.where(window_tail, s0_ref[g], NEG_INF), m_ref[g], vt_ref[g, qi - 2], acc_ref.at[g])

    for g in groups:
        _flash_finish(o_ref, acc_ref.at[g], g * NSA_HPG * NSA_DH)


def _window_attention(qt, k, vt, bsz, seq):
    n = bsz * seq
    nt = seq // ATT_TQ
    nkt = seq // ATT_TK
    width = NSA_HPG * ATT_TQ
    return pl.pallas_call(
        _win_attn_kernel,
        grid=(bsz, nt),
        in_specs=[pl.BlockSpec((None, NSA_GROUPS, None, NSA_DH, width), lambda b, i: (b, 0, i, 0, 0)),
                  pl.BlockSpec((seq, NSA_KV), lambda b, i: (b, 0)),
                  pl.BlockSpec((None, NSA_GROUPS, nkt, VT_ROWS, ATT_TK), lambda b, i: (b, 0, 0, 0, 0))],
        out_specs=pl.BlockSpec((ATT_TQ, NSA_Q), lambda b, i: (b * nt + i, 0)),
        out_shape=jax.ShapeDtypeStruct((n, NSA_Q), F32),
        scratch_shapes=[pltpu.VMEM((NSA_GROUPS, VT_ROWS, width), F32),
                        pltpu.VMEM((NSA_GROUPS, ATT_TK, width), F32),
                        pltpu.VMEM((NSA_GROUPS, ATT_TK, width), F32),
                        pltpu.VMEM((NSA_GROUPS, 1, width), F32)],
        compiler_params=_params("parallel", "parallel"),
        name="window_attention",
    )(qt, k, vt)


def _merge_kernel(x_ref, yr_ref, oc_ref, os_ref, ow_ref, ngl_ref, g_ref, wg_ref, wr_ref, wn_ref, wo_ref, o_ref):
    tm = x_ref.shape[0]
    hb = _rms(x_ref[...], g_ref[...]).astype(MXU_DTYPE)
    ga = jnp.dot(hb, wg_ref[:, :D_MODEL], preferred_element_type=F32)
    gb = jnp.dot(hb, wg_ref[:, D_MODEL:], preferred_element_type=F32)
    gates = jax.nn.sigmoid(ngl_ref[...])
    parts = []
    for h in range(NSA_HEADS):
        cols = slice(h * NSA_DH, (h + 1) * NSA_DH)

        def gate(br):
            return jnp.broadcast_to(gates[:, 3 * h + br:3 * h + br + 1], (tm, NSA_DH))

        parts.append(gate(0) * oc_ref[:, cols] + gate(1) * os_ref[:, cols] + gate(2) * ow_ref[:, cols])
    o_nsa = jnp.concatenate(parts, axis=1)
    y_ret = _dot(yr_ref[...], wr_ref[...])
    y_nsa = _dot(o_nsa, wn_ref[...])
    y = jax.nn.sigmoid(ga) * y_ret + jax.nn.sigmoid(gb) * y_nsa
    o_ref[...] = x_ref[...] + _dot(y, wo_ref[...])


def _merge(x2d, y_ret, o_cmp, o_sel, o_win, ngl, g_mix, w_gates, w_ret_o, w_nsa_o, w_out):
    n = x2d.shape[0]
    tm = ROW_TM
    row = lambda i: (i, 0)
    const = lambda i: (0, 0)
    wide = pl.BlockSpec((tm, D_MODEL), row)
    wspec = pl.BlockSpec((D_MODEL, D_MODEL), const)
    return pl.pallas_call(
        _merge_kernel,
        grid=(n // tm,),
        in_specs=[wide, wide, wide, wide, wide,
                  pl.BlockSpec((tm, LANES), row),
                  pl.BlockSpec((1, D_MODEL), const),
                  pl.BlockSpec((D_MODEL, 2 * D_MODEL), const),
                  wspec, wspec, wspec],
        out_specs=wide,
        out_shape=jax.ShapeDtypeStruct((n, D_MODEL), F32),
        compiler_params=_params("parallel"),
        name="merge",
    )(x2d, y_ret, o_cmp, o_sel, o_win, ngl, g_mix, w_gates, w_ret_o, w_nsa_o, w_out)


def _mem_kv_kernel(m_ref, g_ref, w_ref, o_ref):
    o_ref[...] = _dot(_rms(m_ref[...], g_ref[...]), w_ref[...]).astype(o_ref.dtype)


def _mem_kv(mem2d, g, w_xkv, bsz):
    nm = mem2d.shape[0] // bsz
    return pl.pallas_call(
        _mem_kv_kernel,
        grid=(bsz,),
        in_specs=[pl.BlockSpec((nm, D_MODEL), lambda b: (b, 0)),
                  pl.BlockSpec((1, D_MODEL), lambda b: (0, 0)),
                  pl.BlockSpec((D_MODEL, 2 * D_MODEL), lambda b: (0, 0))],
        out_specs=pl.BlockSpec((nm, 2 * D_MODEL), lambda b: (b, 0)),
        out_shape=jax.ShapeDtypeStruct((mem2d.shape[0], 2 * D_MODEL), MXU_DTYPE),
        compiler_params=_params("parallel"),
        name="mem_kv",
    )(mem2d, g, w_xkv)


def _pack_pairs(x):
    half = x.shape[1] // 2
    hi = lax.bitcast_convert_type(x[:, :half].astype(MXU_DTYPE).astype(F32), jnp.uint32)
    lo = lax.bitcast_convert_type(x[:, half:].astype(MXU_DTYPE).astype(F32), jnp.uint32)
    return (hi & jnp.uint32(0xFFFF0000)) | (lo >> 16)


def _unpack_pairs(u):
    hi = lax.bitcast_convert_type(u & jnp.uint32(0xFFFF0000), F32)
    lo = lax.bitcast_convert_type(u << 16, F32)
    return jnp.concatenate([hi, lo], axis=1)


_ROUTER_E0 = 2 * SUBLANES
_ROUTER_ROWS = _ROUTER_E0 + N_EXPERTS


def _top2_route(lgt):
    sub = SUBLANES
    t = lgt.shape[1]
    rowid = lax.broadcasted_iota(jnp.int32, (sub, t), 0)
    first = lambda hit: jnp.min(jnp.where(hit, rowid, sub), axis=0, keepdims=True)
    lg = jnp.where(rowid < N_EGROUPS, lgt[0:sub], NEG_INF)
    gmax = jnp.max(lg, axis=0, keepdims=True)
    grp = first(lg == gmax)
    g_gate = 1.0 / jnp.sum(jnp.exp(lg - gmax), axis=0, keepdims=True)
    experts_of = lambda g: lgt[_ROUTER_E0 + g * EXP_PER_GROUP:_ROUTER_E0 + (g + 1) * EXP_PER_GROUP]
    le = experts_of(N_EGROUPS - 1)
    for g in range(N_EGROUPS - 2, -1, -1):
        le = jnp.where(grp == g, experts_of(g), le)
    ex = jnp.exp(le - jnp.max(le, axis=0, keepdims=True))
    pe = ex / jnp.sum(ex, axis=0, keepdims=True)
    p0 = jnp.max(pe, axis=0, keepdims=True)
    i0 = first(pe == p0)
    rest = jnp.where(rowid == i0, -1.0, pe)
    p1 = jnp.max(rest, axis=0, keepdims=True)
    i1 = first(rest == p1)
    den = p0 + p1
    base = grp * EXP_PER_GROUP
    return jnp.concatenate([(base + i0).astype(F32), (base + i1).astype(F32),
                            g_gate * p0 / den, g_gate * p1 / den], axis=0)


def _cross_kernel(x_ref, kv_ref, gx_ref, wq_ref, wo_ref, gf_ref, wr_ref, br_ref, x2_ref, hf_ref, rt_ref):
    x = x_ref[...]
    q = _dot(_rms(x, gx_ref[...]), wq_ref[...])
    scores = [_dot_nt(q[:, h * X_DH:(h + 1) * X_DH], kv_ref[:, h * X_DH:(h + 1) * X_DH]) * (X_DH ** -0.5)
              for h in range(X_HEADS)]
    probs = []
    for s in scores:
        e = jnp.exp(s - jnp.max(s, axis=-1, keepdims=True))
        probs.append(e / jnp.sum(e, axis=-1, keepdims=True))
    heads = [_dot(p, kv_ref[:, D_MODEL + h * X_DH:D_MODEL + (h + 1) * X_DH]) for h, p in enumerate(probs)]
    x2 = x + _dot(jnp.concatenate(heads, axis=1), wo_ref[...])
    x2_ref[...] = x2
    hf = _rms(x2, gf_ref[...])
    hf_ref[...] = _pack_pairs(hf)
    h_hi, h_mid, _ = _split3(hf)
    w_hi = wr_ref[0]
    w_mid = wr_ref[1]
    lgt = (_dot_nt(w_hi, h_hi) + (_dot_nt(w_hi, h_mid) + _dot_nt(w_mid, h_hi))) + br_ref[...]
    rt_ref[...] = jnp.concatenate([_top2_route(lgt), jnp.zeros((SUBLANES - 4, lgt.shape[1]), F32)], axis=0)


def _cross(x1, kv, gx, w_xq, w_xo, gf, w_router, b_router, bsz, seq):
    n = x1.shape[0]
    tm = ROW_TM
    nt = seq // tm
    nm = kv.shape[0] // bsz
    row = lambda i: (i, 0)
    const = lambda i: (0, 0)
    vec = pl.BlockSpec((1, D_MODEL), const)
    wspec = pl.BlockSpec((D_MODEL, D_MODEL), const)
    return pl.pallas_call(
        _cross_kernel,
        grid=(n // tm,),
        in_specs=[pl.BlockSpec((tm, D_MODEL), row),
                  pl.BlockSpec((nm, 2 * D_MODEL), lambda i: (i // nt, 0)),
                  vec, wspec, wspec, vec,
                  pl.BlockSpec((2, _ROUTER_ROWS, D_MODEL), lambda i: (0, 0, 0)),
                  pl.BlockSpec((_ROUTER_ROWS, 1), const)],
        out_specs=[pl.BlockSpec((tm, D_MODEL), row),
                   pl.BlockSpec((tm, D_MODEL // 2), row),
                   pl.BlockSpec((SUBLANES, tm), lambda i: (0, i))],
        out_shape=[jax.ShapeDtypeStruct((n, D_MODEL), F32),
                   jax.ShapeDtypeStruct((n, D_MODEL // 2), jnp.uint32),
                   jax.ShapeDtypeStruct((SUBLANES, n), F32)],
        compiler_params=_params("parallel"),
        name="cross_attention",
    )(x1, kv, gx, w_xq, w_xo, gf, w_router, b_router)


def _expert_kernel(blk_ref, xb_ref, w1_ref, w3_ref, w2_ref, o_ref):
    n_used = blk_ref[pl.num_programs(0)]

    @pl.when(pl.program_id(0) < n_used)
    def _():
        xb = _unpack_pairs(xb_ref[...]).astype(MXU_DTYPE)
        a = _dot(xb, w1_ref[...])
        b = _dot(xb, w3_ref[...])
        o_ref[...] = _dot(a * jax.nn.sigmoid(a) * b, w2_ref[...])

    @pl.when(pl.program_id(0) >= n_used)
    def _():
        o_ref[...] = jnp.zeros_like(o_ref)


def _experts(blk, xb, w1, w3, w2):
    cap = xb.shape[0]
    nblk = cap // MOE_BLOCK
    row = lambda i, e: (i, 0)
    by_expert = lambda i, e: (e[i], 0, 0)
    grid_spec = pltpu.PrefetchScalarGridSpec(
        num_scalar_prefetch=1,
        grid=(nblk,),
        in_specs=[pl.BlockSpec((MOE_BLOCK, D_MODEL // 2), row),
                  pl.BlockSpec((None, D_MODEL, D_EXPERT), by_expert),
                  pl.BlockSpec((None, D_MODEL, D_EXPERT), by_expert),
                  pl.BlockSpec((None, D_EXPERT, D_MODEL), by_expert)],
        out_specs=pl.BlockSpec((MOE_BLOCK, D_MODEL), row),
    )
    return pl.pallas_call(
        _expert_kernel,
        grid_spec=grid_spec,
        out_shape=jax.ShapeDtypeStruct((cap, D_MODEL), F32),
        compiler_params=_params("arbitrary"),
        name="experts",
    )(blk, xb, w1, w3, w2)


def _final_kernel(x_ref, w_ref, g_ref, *refs):
    o_ref = refs[-1]
    tiles_per_split = pl.num_programs(0) // MOE_SPLITS
    for s in range(MOE_SPLITS):
        @pl.when(pl.program_id(0) // tiles_per_split == s)
        def _(s=s):
            moe = w_ref[:, 0:1] * refs[2 * s][...] + w_ref[:, 1:2] * refs[2 * s + 1][...]
            o_ref[...] = _rms(x_ref[...] + moe, g_ref[...])


def _final(x2, wts, g, ys):
    n = x2.shape[0]
    tm = ROW_TM
    tiles_per_split = n // tm // MOE_SPLITS
    row = lambda i: (i, 0)
    wide = pl.BlockSpec((tm, D_MODEL), row)

    def split_spec(s):
        return pl.BlockSpec((tm, D_MODEL), lambda i: (jnp.clip(i - s * tiles_per_split, 0, tiles_per_split - 1), 0))

    return pl.pallas_call(
        _final_kernel,
        grid=(n // tm,),
        in_specs=[wide, pl.BlockSpec((tm, EXP_TOPK), row), pl.BlockSpec((1, D_MODEL), lambda i: (0, 0))] + [
            split_spec(s) for s in range(MOE_SPLITS) for _ in range(EXP_TOPK)],
        out_specs=wide,
        out_shape=jax.ShapeDtypeStruct((n, D_MODEL), F32),
        compiler_params=_params("parallel"),
        name="final_norm",
    )(x2, wts, g, *[y for pair in ys for y in pair])


def _route(eid, tok0):
    n_tok = eid.shape[1]
    eid = eid.reshape(-1)
    n_asg = eid.shape[0]
    counts = jnp.sum((jnp.arange(N_EXPERTS, dtype=jnp.int32)[:, None] == eid[None, :]).astype(jnp.int32), axis=1)
    padded = (counts + MOE_BLOCK - 1) // MOE_BLOCK * MOE_BLOCK
    pends = jnp.cumsum(padded)
    cap = ((n_asg + MOE_BLOCK - 1) // MOE_BLOCK + N_EXPERTS) * MOE_BLOCK
    nblk = cap // MOE_BLOCK
    blk_e = jnp.minimum(jnp.searchsorted(pends, jnp.arange(nblk) * MOE_BLOCK, side='right', method='compare_all'),
                        N_EXPERTS - 1).astype(jnp.int32)
    filler = jnp.arange(cap - n_asg, dtype=jnp.int32)
    filler_e = jnp.searchsorted(jnp.cumsum(padded - counts), filler, side='right', method='compare_all')
    keys = jnp.concatenate([2 * eid, 2 * filler_e.astype(jnp.int32) + 1])
    toks = jnp.concatenate([jnp.arange(n_asg, dtype=jnp.int32) % n_tok, filler % n_tok])
    rows = jnp.arange(cap, dtype=jnp.int32)
    _, buf_tok, perm = lax.sort((keys, toks, rows), num_keys=1)
    _, pos = lax.sort_key_val(perm, rows)
    n_used = (pends[-1] // MOE_BLOCK).astype(jnp.int32)
    return tok0 + buf_tok, jnp.concatenate([blk_e, n_used[None]]), pos[:n_asg].reshape(EXP_TOPK, n_tok)


def kernel(x, mem, norm_mix_g, w_in, w_ret_o, w_nsa_o, w_out, cmp_pe_k, cmp_w1_k, cmp_w2_k, cmp_pe_v,
           cmp_w1_v, cmp_w2_v, norm_x_g, norm_mem_g, w_xq, w_xkv, w_xo, norm_ffn_g, w_grp, b_grp, w_rt,
           b_rt, w_e1, w_e3, w_e2, norm_f_g):
    bsz, seq, _ = x.shape
    n = bsz * seq
    assert seq % PROJ_TM == 0 and seq % (2 * ATT_TQ) == 0 and w_in.shape[0] == 1
    assert n % (ROW_TM * MOE_SPLITS) == 0
    cast = lambda a: a.astype(MXU_DTYPE)
    xc = x.reshape(n, D_MODEL)
    l = 0

    w_main, w_t, w_gates = _split_w_in(w_in[l])
    qk, rv, rg, ckv, sk, wk, ngl, qt, svt, wvt = _proj(
        xc, norm_mix_g[l][None, :], w_main, w_t, _rope_tables(seq), bsz, seq)
    y_ret = _retention(qk, rv, rg, bsz, seq)
    w2 = jnp.stack([cmp_w2_k[l], cmp_w2_v[l]])
    cmp_k, cmp_vt = _compress(ckv, jnp.stack([cmp_pe_k[l], cmp_pe_v[l]]),
                              cast(jnp.stack([cmp_w1_k[l], cmp_w1_v[l]])),
                              cast(w2), cast(w2.transpose(0, 2, 1)), bsz, seq)
    o_cmp, selt = _cmp_attention(qt, cmp_k, cmp_vt, bsz, seq)
    o_sel = _selected_attention(qt, sk, svt, selt, bsz, seq)
    o_win = _window_attention(qt, wk, wvt, bsz, seq)
    x1 = _merge(xc, y_ret, o_cmp, o_sel, o_win, ngl, norm_mix_g[l][None, :], w_gates,
                cast(w_ret_o[l]), cast(w_nsa_o[l]), cast(w_out[l]))

    kv = _mem_kv(mem.reshape(-1, D_MODEL), norm_mem_g[l][None, :], cast(w_xkv[l]), bsz)
    gap = _ROUTER_E0 - N_EGROUPS
    w_router = jnp.concatenate([w_grp[l].T, jnp.zeros((gap, D_MODEL), F32), w_rt[l].T], axis=0)
    wr_hi = cast(w_router)
    wr_mid = cast(w_router - wr_hi.astype(F32))
    b_router = jnp.concatenate([b_grp[l], jnp.zeros((gap,), F32), b_rt[l]])[:, None]
    x2, hf, routed = _cross(x1, kv, norm_x_g[l][None, :], cast(w_xq[l]), cast(w_xo[l]),
                            norm_ffn_g[l][None, :], jnp.stack([wr_hi, wr_mid]), b_router, bsz, seq)

    eid = routed[0:EXP_TOPK].astype(jnp.int32)
    wts = routed[EXP_TOPK:2 * EXP_TOPK].T
    per_split = n // MOE_SPLITS
    ys = []
    for s in range(MOE_SPLITS):
        buf_tok, blk, pos = _route(eid[:, s * per_split:(s + 1) * per_split], s * per_split)
        y = _experts(blk, hf[buf_tok], w_e1[l], w_e3[l], w_e2[l])
        ys.append([y[pos[j]] for j in range(EXP_TOPK)])
    out = _final(x2, wts, norm_f_g[None, :], ys)
    return out.reshape(bsz, seq, D_MODEL)
```

```python
import functools

import numpy as np
import jax
import jax.numpy as jnp
from jax import lax
from jax.experimental import pallas as pl
from jax.experimental.pallas import tpu as pltpu

MXU_DTYPE = jnp.bfloat16
F32 = jnp.float32

D_MODEL = 1024
N_MEM = 256
EPS = 1e-6
NEG_INF = -1e30
SEL_FORCE = 1e4

R_HEADS = 4
R_DK = 128
R_DV = 256
R_CHUNK = 128
ROPE_BASE = 10000.0

NSA_HEADS = 8
NSA_GROUPS = 2
NSA_HPG = NSA_HEADS // NSA_GROUPS
NSA_DH = 128
CMP_LEN = 32
CMP_STRIDE = 16
SEL_LEN = 64
SEL_TOPK = 16
WINDOW = 512

X_HEADS = 4
X_DH = D_MODEL // X_HEADS

N_EGROUPS = 4
EXP_PER_GROUP = 8
N_EXPERTS = N_EGROUPS * EXP_PER_GROUP
EXP_TOPK = 2
D_EXPERT = 512
MOE_BLOCK = 512

RET_QK = R_HEADS * R_DK
RET_V = R_HEADS * R_DV
NSA_Q = NSA_HEADS * NSA_DH
NSA_KV = NSA_GROUPS * NSA_DH
SPLITS = (RET_QK, RET_QK, RET_V, RET_V, NSA_Q, NSA_KV, NSA_KV, NSA_KV, NSA_KV, NSA_KV, NSA_KV,
          3 * NSA_HEADS, D_MODEL, D_MODEL)

_EXP2_SCALE = (NSA_DH ** -0.5) * float(np.log2(np.e))

LANES = 128
SUBLANES = 8
VMEM_LIMIT = 56 * 1024 * 1024

PROJ_TM = 512
RET_ROWS = 512
ATT_TQ = 256
ATT_TK = 256
ROW_TM = 512
MOE_SPLITS = 2
VT_ROWS = NSA_DH + 2 * SUBLANES
SEL_KW = 2 * NSA_DH
assert ATT_TQ == ATT_TK and SEL_LEN * 2 * SUBLANES >= ATT_TK and WINDOW == 2 * ATT_TK


def _params(*sem):
    return pltpu.CompilerParams(dimension_semantics=sem, vmem_limit_bytes=VMEM_LIMIT)


def _dot(a, b):
    return jnp.dot(a.astype(MXU_DTYPE), b.astype(MXU_DTYPE), preferred_element_type=F32)


def _dot_nt(a, b):
    return lax.dot_general(a.astype(MXU_DTYPE), b.astype(MXU_DTYPE), (((1,), (1,)), ((), ())),
                           preferred_element_type=F32)


def _dot_tn(a, b):
    return lax.dot_general(a.astype(MXU_DTYPE), b.astype(MXU_DTYPE), (((0,), (0,)), ((), ())),
                           preferred_element_type=F32)


def _split3(p):
    hi = p.astype(MXU_DTYPE)
    r1 = p - hi.astype(F32)
    mid = r1.astype(MXU_DTYPE)
    lo = (r1 - mid.astype(F32)).astype(MXU_DTYPE)
    return hi, mid, lo


def _rms(x, g):
    return x * lax.rsqrt(jnp.mean(x * x, axis=-1, keepdims=True) + EPS) * g


_C_RQK = 0
_C_RV = _C_RQK + 2 * RET_QK
_C_RG = _C_RV + RET_V
_C_CKV = _C_RG + RET_V
_C_SK = _C_CKV + 2 * NSA_KV
_C_WK = _C_SK + NSA_KV
_C_NGL = _C_WK + NSA_KV
_C_END = _C_NGL + LANES
_R_NQ = 0
_R_SV = _R_NQ + NSA_Q
_R_WV = _R_SV + NSA_KV
_R_END = _R_WV + NSA_KV


def _proj_kernel(x_ref, g_ref, w_ref, wt_ref, cos_ref, sin_up_ref, sin_dn_ref,
                 qk_ref, rv_ref, rg_ref, ckv_ref, sk_ref, wk_ref, ngl_ref, nqt_ref, svt_ref, wvt_ref):
    hb = _rms(x_ref[...], g_ref[...]).astype(MXU_DTYPE)
    tm = hb.shape[0]

    def mm(off, width):
        return jnp.dot(hb, w_ref[:, off:off + width], preferred_element_type=F32)

    def mm_t(off, height):
        return _dot_nt(wt_ref[off:off + height, :], hb)

    cos = cos_ref[...]
    sin_up = sin_up_ref[...]
    sin_dn = sin_dn_ref[...]
    qk = mm(_C_RQK, 2 * RET_QK)
    for i in range(2 * R_HEADS):
        t = qk[:, i * R_DK:(i + 1) * R_DK]
        r = t * cos + pltpu.roll(t, R_DK - 1, axis=1) * sin_up + pltpu.roll(t, 1, axis=1) * sin_dn
        if i >= R_HEADS:
            r = r * (R_DK ** -0.5)
        qk_ref[:, i * R_DK:(i + 1) * R_DK] = r.astype(qk_ref.dtype)
    rv_ref[...] = mm(_C_RV, RET_V).astype(rv_ref.dtype)
    rg_ref[...] = mm(_C_RG, RET_V)
    ckv_ref[...] = mm(_C_CKV, 2 * NSA_KV)
    sk = mm(_C_SK, NSA_KV).astype(sk_ref.dtype)
    blk = (lax.broadcasted_iota(jnp.int32, (tm, NSA_DH), 0) % ATT_TK) // SEL_LEN
    onehot = jnp.where(lax.broadcasted_iota(jnp.int32, (tm, NSA_DH), 1) == blk, 1.0, 0.0).astype(sk_ref.dtype)
    for g in range(NSA_GROUPS):
        sk_ref[:, g * SEL_KW:g * SEL_KW + NSA_DH] = sk[:, g * NSA_DH:(g + 1) * NSA_DH]
        sk_ref[:, g * SEL_KW + NSA_DH:(g + 1) * SEL_KW] = onehot
    wk_ref[...] = mm(_C_WK, NSA_KV).astype(wk_ref.dtype)
    ngl_ref[...] = mm(_C_NGL, LANES)

    nqt = (mm_t(_R_NQ, NSA_Q) * _EXP2_SCALE).astype(nqt_ref.dtype)
    for g in range(NSA_GROUPS):
        for j in range(tm // ATT_TQ):
            for hh in range(NSA_HPG):
                head = g * NSA_HPG + hh
                nqt_ref[g, j, :, hh * ATT_TQ:(hh + 1) * ATT_TQ] = (
                    nqt[head * NSA_DH:(head + 1) * NSA_DH, j * ATT_TQ:(j + 1) * ATT_TQ])
    for off, out_ref in ((_R_SV, svt_ref), (_R_WV, wvt_ref)):
        vt = mm_t(off, NSA_KV).astype(out_ref.dtype)
        for g in range(NSA_GROUPS):
            for j in range(tm // ATT_TK):
                out_ref[g, j, 0:NSA_DH] = vt[g * NSA_DH:(g + 1) * NSA_DH, j * ATT_TK:(j + 1) * ATT_TK]
                out_ref[g, j, NSA_DH:VT_ROWS] = jnp.ones((VT_ROWS - NSA_DH, ATT_TK), out_ref.dtype)


def _split_w_in(w_in):
    offs = np.cumsum((0,) + SPLITS)
    rq, rk, rv, rg, nq, ck, cv, sk, sv, wk, wv, ngl, ga, gb = [
        w_in[:, offs[i]:offs[i + 1]] for i in range(len(SPLITS))]
    ngl = jnp.pad(ngl, ((0, 0), (0, LANES - ngl.shape[1])))
    w = jnp.concatenate([rq, rk, rv, rg, ck, cv, sk, wk, ngl], axis=1).astype(MXU_DTYPE)
    wt = jnp.concatenate([nq, sv, wv], axis=1).T.astype(MXU_DTYPE)
    w_gates = jnp.concatenate([ga, gb], axis=1).astype(MXU_DTYPE)
    return w, wt, w_gates


def _rope_tables(seq):
    pos = jnp.arange(seq, dtype=F32)
    inv_freq = ROPE_BASE ** (-jnp.arange(0, R_DK, 2, dtype=F32) / R_DK)
    ang = pos[:, None] * inv_freq[None, :]
    cos = jnp.repeat(jnp.cos(ang), 2, axis=1)
    sin = jnp.repeat(jnp.sin(ang), 2, axis=1)
    even = (jnp.arange(R_DK) % 2 == 0)[None, :]
    return cos, jnp.where(even, -sin, 0.0), jnp.where(even, 0.0, sin)


def _proj(x2d, g, w, wt, tables, bsz, seq):
    n = x2d.shape[0]
    tm = PROJ_TM
    nt = seq // tm
    row = lambda i: (i, 0)
    const = lambda i: (0, 0)
    tile = lambda i: (i // nt, 0, i % nt, 0, 0)
    table = pl.BlockSpec((tm, R_DK), lambda i: (i % nt, 0))
    widths = (2 * RET_QK, RET_V, RET_V, 2 * NSA_KV, NSA_GROUPS * SEL_KW, NSA_KV, LANES)
    dtypes = (MXU_DTYPE, MXU_DTYPE, F32, F32, MXU_DTYPE, MXU_DTYPE, F32)
    width = NSA_HPG * ATT_TQ
    vt_shape = jax.ShapeDtypeStruct((bsz, NSA_GROUPS, seq // ATT_TK, VT_ROWS, ATT_TK), MXU_DTYPE)
    vt_spec = pl.BlockSpec((None, NSA_GROUPS, tm // ATT_TK, VT_ROWS, ATT_TK), tile)
    return pl.pallas_call(
        _proj_kernel,
        grid=(n // tm,),
        in_specs=[pl.BlockSpec((tm, D_MODEL), row),
                  pl.BlockSpec((1, D_MODEL), const),
                  pl.BlockSpec((D_MODEL, _C_END), const, pipeline_mode=pl.Buffered(1)),
                  pl.BlockSpec((_R_END, D_MODEL), const, pipeline_mode=pl.Buffered(1)),
                  table, table, table],
        out_specs=[pl.BlockSpec((tm, wd), row) for wd in widths] + [
            pl.BlockSpec((None, NSA_GROUPS, tm // ATT_TQ, NSA_DH, width), tile), vt_spec, vt_spec],
        out_shape=[jax.ShapeDtypeStruct((n, wd), dt) for wd, dt in zip(widths, dtypes)] + [
            jax.ShapeDtypeStruct((bsz, NSA_GROUPS, seq // ATT_TQ, NSA_DH, width), MXU_DTYPE), vt_shape, vt_shape],
        compiler_params=_params("parallel"),
        name="proj",
    )(x2d, g, w, wt, *tables)


def _ret_kernel(qk_ref, v_ref, g_ref, decay_ref, zeta_ref, xi_ref, cd_ref, y_ref, state_ref):
    @pl.when(pl.program_id(1) == 0)
    def _():
        state_ref[...] = jnp.zeros_like(state_ref)

    chunks = range(RET_ROWS // R_CHUNK)
    heads = range(R_HEADS)
    rows = lambda c: slice(c * R_CHUNK, (c + 1) * R_CHUNK)
    q_of = lambda c, h: qk_ref[rows(c), h * R_DK:(h + 1) * R_DK]
    k_of = lambda c, h: qk_ref[rows(c), RET_QK + h * R_DK:RET_QK + (h + 1) * R_DK]
    v_of = lambda c, h: v_ref[rows(c), h * R_DV:(h + 1) * R_DV]

    scores = {(c, h): _dot_nt(q_of(c, h), k_of(c, h)) * decay_ref[h] for c in chunks for h in heads}
    kv = {(c, h): _dot_tn(k_of(c, h).astype(F32) * zeta_ref[h], v_of(c, h)) for c in chunks for h in heads}
    seen = {}
    for h in heads:
        st = state_ref[h]
        for c in chunks:
            seen[c, h] = st
            st = st * cd_ref[h] + kv[c, h]
        state_ref[h] = st
    for c in chunks:
        for h in heads:
            o = _dot(scores[c, h], v_of(c, h)) + _dot(q_of(c, h), seen[c, h]) * xi_ref[h]
            o = o * lax.rsqrt(jnp.mean(o * o, axis=-1, keepdims=True) + EPS)
            g = g_ref[rows(c), h * R_DV:(h + 1) * R_DV]
            y_ref[rows(c), h * R_DV:(h + 1) * R_DV] = (g * jax.nn.sigmoid(g) * o).astype(y_ref.dtype)


def _retention(qk, rv, rg, bsz, seq):
    n = qk.shape[0]
    nt = seq // RET_ROWS
    log_g = jnp.log1p(-jnp.exp2(-5.0 - jnp.arange(R_HEADS, dtype=F32)))
    idx = jnp.arange(R_CHUNK, dtype=F32)
    diff = idx[:, None] - idx[None, :]
    decay = jnp.where(diff >= 0, jnp.exp(log_g[:, None, None] * jnp.maximum(diff, 0.0)), 0.0)
    zeta = jnp.exp(log_g[:, None] * (R_CHUNK - 1.0 - idx)[None, :])[:, :, None]
    xi = jnp.exp(log_g[:, None] * (idx + 1.0)[None, :])[:, :, None]
    cd = jnp.exp(log_g * R_CHUNK)[:, None, None]
    row = lambda b, i: (b * nt + i, 0)
    const3 = lambda b, i: (0, 0, 0)
    return pl.pallas_call(
        _ret_kernel,
        grid=(bsz, nt),
        in_specs=[pl.BlockSpec((RET_ROWS, 2 * RET_QK), row),
                  pl.BlockSpec((RET_ROWS, RET_V), row),
                  pl.BlockSpec((RET_ROWS, RET_V), row),
                  pl.BlockSpec((R_HEADS, R_CHUNK, R_CHUNK), const3),
                  pl.BlockSpec((R_HEADS, R_CHUNK, 1), const3),
                  pl.BlockSpec((R_HEADS, R_CHUNK, 1), const3),
                  pl.BlockSpec((R_HEADS, 1, 1), const3)],
        out_specs=pl.BlockSpec((RET_ROWS, RET_V), row),
        out_shape=jax.ShapeDtypeStruct((n, RET_V), MXU_DTYPE),
        scratch_shapes=[pltpu.VMEM((R_HEADS, R_DK, R_DV), F32)],
        compiler_params=_params("parallel", "arbitrary"),
        name="retention",
    )(qk, rv, rg, decay, zeta, xi, cd)


def _compress_kernel(x_ref, pe_ref, w1_ref, w2_ref, w2t_ref, o_ref, ot_ref, buf_ref, *, seq):
    ncp = seq // CMP_STRIDE
    buf_ref[0:seq, :] = x_ref[...]
    buf_ref[seq:seq + LANES, :] = jnp.zeros((LANES, NSA_DH), F32)
    acc = jnp.zeros((ncp, NSA_DH), F32)
    for l in range(CMP_LEN):
        xl = buf_ref[pl.ds(l, ncp, stride=CMP_STRIDE), :] + pe_ref[l:l + 1, :]
        acc = acc + _dot(xl, w1_ref[l])
    hid = jax.nn.gelu(acc)
    o_ref[...] = _dot(hid, w2_ref[...]).astype(o_ref.dtype)
    ot_ref[...] = _dot_nt(w2t_ref[...], hid).astype(ot_ref.dtype)


def _compress(ckv, pe, w1, w2, w2t, bsz, seq):
    ncp = seq // CMP_STRIDE
    nj = 2 * NSA_GROUPS
    wsel = lambda b, j: (j // NSA_GROUPS, 0, 0)
    return pl.pallas_call(
        functools.partial(_compress_kernel, seq=seq),
        grid=(bsz, nj),
        in_specs=[pl.BlockSpec((seq, NSA_DH), lambda b, j: (b, j)),
                  pl.BlockSpec((None, CMP_LEN, NSA_DH), wsel),
                  pl.BlockSpec((None, CMP_LEN, NSA_DH, NSA_DH), lambda b, j: (j // NSA_GROUPS, 0, 0, 0)),
                  pl.BlockSpec((None, NSA_DH, NSA_DH), wsel),
                  pl.BlockSpec((None, NSA_DH, NSA_DH), wsel)],
        out_specs=[pl.BlockSpec((None, None, ncp, NSA_DH), lambda b, j: (b, j, 0, 0)),
                   pl.BlockSpec((None, None, NSA_DH, ncp), lambda b, j: (b, j, 0, 0))],
        out_shape=[jax.ShapeDtypeStruct((bsz, nj, ncp, NSA_DH), MXU_DTYPE),
                   jax.ShapeDtypeStruct((bsz, nj, NSA_DH, ncp), MXU_DTYPE)],
        scratch_shapes=[pltpu.VMEM((seq + LANES, NSA_DH), F32)],
        compiler_params=_params("parallel", "parallel"),
        name="compress",
    )(ckv, pe, w1, w2, w2t)


def _cmp_attn_kernel(qt_ref, k_ref, vt_ref, ov_ref, o_ref, selt_ref, p_ref, imp_ref, *, ncp, nb):
    tq = ATT_TQ
    width = NSA_HPG * tq
    groups = range(NSA_GROUPS)
    qi = pl.program_id(1)
    t0 = qi * tq
    any_valid = (t0 + lax.broadcasted_iota(jnp.int32, (1, tq), 1)) >= CMP_LEN - 1
    any_valid = jnp.concatenate([any_valid] * NSA_HPG, axis=1)

    def probabilities(rows):
        t = t0 + lax.broadcasted_iota(jnp.int32, (rows, tq), 1)
        n = lax.broadcasted_iota(jnp.int32, (rows, tq), 0)
        valid = (n * CMP_STRIDE + (CMP_LEN - 1) <= t) & (n < ncp - 1)
        valid = jnp.concatenate([valid] * NSA_HPG, axis=1)
        scores = [_dot(k_ref[g, 0:rows, :], qt_ref[g]) for g in groups]
        for g in groups:
            s = jnp.where(valid, scores[g], NEG_INF)
            e = jnp.exp2(s - jnp.max(s, axis=0, keepdims=True))
            p_ref[g, 0:rows, :] = e * jnp.where(any_valid, 1.0 / jnp.sum(e, axis=0, keepdims=True), 0.0)
            if rows < ncp:
                p_ref[g, rows:, :] = jnp.zeros((ncp - rows, width), F32)

    n_buckets = 4
    per_bucket = ncp // n_buckets
    visible = (t0 + tq - CMP_LEN) // CMP_STRIDE + 1
    bucket = jnp.minimum((visible + per_bucket - 1) // per_bucket, n_buckets)
    for c in range(1, n_buckets + 1):
        pl.when(bucket == c)(functools.partial(probabilities, c * per_bucket))

    j = lax.broadcasted_iota(jnp.int32, (nb, tq), 0)
    tb = (t0 + lax.broadcasted_iota(jnp.int32, (nb, tq), 1)) // SEL_LEN
    forced = (j == 0) | (j == tb) | (j == tb - 1)
    sub = SUBLANES
    k_sel = min(SEL_TOPK, nb)
    for g in groups:
        p = p_ref[g]
        ot = _dot(vt_ref[g], p)
        for h in range(NSA_HPG):
            o_ref[:, (g * NSA_HPG + h) * NSA_DH:(g * NSA_HPG + h + 1) * NSA_DH] = ot[:, h * tq:(h + 1) * tq].T
        psum = sum(p[:, h * tq:(h + 1) * tq] for h in range(NSA_HPG))

        imp = sum(_dot(ov_ref[...], part) for part in _split3(psum))
        imp_ref[g] = jnp.where(j > tb, -SEL_FORCE, jnp.where(forced, SEL_FORCE, imp))

    def choose(blocks):
        for g in groups:
            imp = imp_ref[g, 0:blocks, :]
            grp = [imp[r * sub:(r + 1) * sub] for r in range(blocks // sub)]
            cnt = [jnp.zeros((sub, tq), F32) for _ in grp]
            for i in range(blocks):
                row = jnp.broadcast_to(imp[i:i + 1, :], (sub, tq))
                for r in range(blocks // sub):
                    if r * sub > i:
                        beats = jnp.where(row >= grp[r], 1.0, 0.0)
                    elif r * sub + sub - 1 < i:
                        beats = jnp.where(row > grp[r], 1.0, 0.0)
                    else:
                        jr = r * sub + lax.broadcasted_iota(jnp.int32, (sub, tq), 0)
                        beats = jnp.where(jr > i, jnp.where(row >= grp[r], 1.0, 0.0),
                                          jnp.where(row > grp[r], 1.0, 0.0))
                    cnt[r] = cnt[r] + beats
            for r in range(blocks // sub):
                selt_ref[g, r * sub:(r + 1) * sub, :] = jnp.where(cnt[r] < k_sel, 1.0, 0.0)
            if blocks < nb:
                selt_ref[g, blocks:, :] = jnp.zeros((nb - blocks, tq), F32)

    for c in range(1, n_buckets + 1):
        pl.when(bucket == c)(functools.partial(choose, c * (nb // n_buckets)))


def _cmp_attention(qt, cmp_k, cmp_vt, bsz, seq):
    n = bsz * seq
    ncp = seq // CMP_STRIDE
    nb = seq // SEL_LEN
    nt = seq // ATT_TQ
    cstart = np.arange(ncp) * CMP_STRIDE
    jstart = np.arange(nb) * SEL_LEN
    ov = ((cstart[None, :] < jstart[:, None] + SEL_LEN) & (cstart[None, :] + CMP_LEN > jstart[:, None])
          & (np.arange(ncp)[None, :] < ncp - 1))
    ov = jnp.asarray(ov, MXU_DTYPE)
    return pl.pallas_call(
        functools.partial(_cmp_attn_kernel, ncp=ncp, nb=nb),
        grid=(bsz, nt),
        in_specs=[pl.BlockSpec((None, NSA_GROUPS, None, NSA_DH, NSA_HPG * ATT_TQ), lambda b, i: (b, 0, i, 0, 0)),
                  pl.BlockSpec((None, NSA_GROUPS, ncp, NSA_DH), lambda b, i: (b, 0, 0, 0)),
                  pl.BlockSpec((None, NSA_GROUPS, NSA_DH, ncp), lambda b, i: (b, 1, 0, 0)),
                  pl.BlockSpec((nb, ncp), lambda b, i: (0, 0))],
        out_specs=[pl.BlockSpec((ATT_TQ, NSA_Q), lambda b, i: (b * nt + i, 0)),
                   pl.BlockSpec((None, NSA_GROUPS, nb, ATT_TQ), lambda b, i: (b, 0, 0, i))],
        out_shape=[jax.ShapeDtypeStruct((n, NSA_Q), F32),
                   jax.ShapeDtypeStruct((bsz, NSA_GROUPS, nb, seq), F32)],
        scratch_shapes=[pltpu.VMEM((NSA_GROUPS, ncp, NSA_HPG * ATT_TQ), F32),
                        pltpu.VMEM((NSA_GROUPS, nb, ATT_TQ), F32)],
        compiler_params=_params("parallel", "parallel"),
        name="cmp_attention",
    )(qt, cmp_k, cmp_vt, ov)


def _softmax_step(s, m, vt, acc_ref):
    m_new = jnp.maximum(m, jnp.max(s, axis=0, keepdims=True))
    p = jnp.exp2(s - m_new)
    acc_ref[...] = jnp.exp2(m - m_new) * acc_ref[...] + _dot(vt, p)
    return m_new


def _flash_finish(o_ref, acc_ref, col0=0):
    inv = 1.0 / acc_ref[NSA_DH:NSA_DH + 1, :]
    for h in range(NSA_HPG):
        cols = slice(h * ATT_TQ, (h + 1) * ATT_TQ)
        o_ref[:, col0 + h * NSA_DH:col0 + (h + 1) * NSA_DH] = (acc_ref[0:NSA_DH, cols] * inv[:, cols]).T


def _sel_attn_kernel(qt_ref, k_ref, vt_ref, selt_ref, o_ref, acc_ref, s_ref, qa_ref):
    qi = pl.program_id(1)
    width = NSA_HPG * ATT_TQ
    groups = range(NSA_GROUPS)
    blocks_per_tile = ATT_TK // SEL_LEN
    bias_rows = 2 * SUBLANES
    n_loop = qi * (ATT_TQ // ATT_TK)
    n_steps = (n_loop + 1) // 2 * 2
    tile_of = lambda step: jnp.where(step < n_loop, step, qi)
    rowid = lax.broadcasted_iota(jnp.int32, (bias_rows, ATT_TQ), 0)
    tri = (lax.broadcasted_iota(jnp.int32, (ATT_TK, ATT_TQ), 0) <=
           lax.broadcasted_iota(jnp.int32, (ATT_TK, ATT_TQ), 1))
    tri = jnp.concatenate([tri] * NSA_HPG, axis=1)

    def scores(g, step):
        kt = tile_of(step)
        threshold = jnp.where((step < n_loop) | (step == n_steps), 0.5, 2.0)
        bias = jnp.zeros((bias_rows, ATT_TQ), F32)
        for jb in range(blocks_per_tile):
            picked = selt_ref[g, pl.ds(kt * blocks_per_tile + jb, 1), :] > threshold
            bias = jnp.where(rowid == jb, jnp.where(picked, 0.0, NEG_INF), bias)
        qa_ref[g, NSA_DH:NSA_DH + bias_rows, :] = jnp.concatenate([bias] * NSA_HPG, axis=1).astype(qa_ref.dtype)
        k = k_ref[pl.ds(pl.multiple_of(kt * ATT_TK, ATT_TK), ATT_TK), g * SEL_KW:(g + 1) * SEL_KW]
        return _dot(k, qa_ref[g])

    def half_step(step, cur, nxt, ms, diagonal=False):
        if not diagonal:
            for g in groups:
                s_ref[g, nxt] = scores(g, step + 1)
        out = []
        for g in groups:
            s = jnp.where(tri, s_ref[g, cur], NEG_INF) if diagonal else s_ref[g, cur]
            out.append(_softmax_step(s, ms[g], vt_ref[g, tile_of(step)], acc_ref.at[g]))
        return tuple(out)

    def body(i, ms):
        return half_step(2 * i + 1, 1, 0, half_step(2 * i, 0, 1, ms))

    for g in groups:
        qa_ref[g, 0:NSA_DH, :] = qt_ref[g]
        qa_ref[g, NSA_DH + bias_rows:, :] = jnp.zeros((SEL_KW - NSA_DH - bias_rows, width), qa_ref.dtype)
        acc_ref[g] = jnp.zeros(acc_ref.shape[1:], F32)
        s_ref[g, 0] = scores(g, 0)
    ms = lax.fori_loop(0, n_steps // 2, body, tuple(jnp.full((1, width), NEG_INF, F32) for _ in groups))
    half_step(n_steps, 0, 1, ms, diagonal=True)
    for g in groups:
        _flash_finish(o_ref, acc_ref.at[g], g * NSA_HPG * NSA_DH)


def _selected_attention(qt, k, vt, selt, bsz, seq):
    n = bsz * seq
    nt = seq // ATT_TQ
    nkt = seq // ATT_TK
    nb = seq // SEL_LEN
    width = NSA_HPG * ATT_TQ
    return pl.pallas_call(
        _sel_attn_kernel,
        grid=(bsz, nt),
        in_specs=[pl.BlockSpec((None, NSA_GROUPS, None, NSA_DH, width), lambda b, i: (b, 0, i, 0, 0)),
                  pl.BlockSpec((seq, NSA_GROUPS * SEL_KW), lambda b, i: (b, 0)),
                  pl.BlockSpec((None, NSA_GROUPS, nkt, VT_ROWS, ATT_TK), lambda b, i: (b, 0, 0, 0, 0)),
                  pl.BlockSpec((None, NSA_GROUPS, nb, ATT_TQ), lambda b, i: (b, 0, 0, i))],
        out_specs=pl.BlockSpec((ATT_TQ, NSA_Q), lambda b, i: (b * nt + i, 0)),
        out_shape=jax.ShapeDtypeStruct((n, NSA_Q), F32),
        scratch_shapes=[pltpu.VMEM((NSA_GROUPS, VT_ROWS, width), F32),
                        pltpu.VMEM((NSA_GROUPS, 2, ATT_TK, width), F32),
                        pltpu.VMEM((NSA_GROUPS, SEL_KW, width), MXU_DTYPE)],
        compiler_params=_params("parallel", "parallel"),
        name="selected_attention",
    )(qt, k, vt, selt)


def _win_attn_kernel(qt_ref, k_ref, vt_ref, o_ref, acc_ref, s0_ref, s1_ref, m_ref):
    qi = pl.program_id(1)
    groups = range(NSA_GROUPS)
    row = lax.broadcasted_iota(jnp.int32, (ATT_TK, ATT_TQ), 0)
    col = lax.broadcasted_iota(jnp.int32, (ATT_TK, ATT_TQ), 1)
    causal = jnp.concatenate([row <= col] * NSA_HPG, axis=1)
    window_tail = jnp.concatenate([row > col] * NSA_HPG, axis=1)

    def scores(g, kt):
        k = k_ref[pl.ds(pl.multiple_of(kt * ATT_TK, ATT_TK), ATT_TK), g * NSA_DH:(g + 1) * NSA_DH]
        return _dot(k, qt_ref[g])

    for g in groups:
        acc_ref[g] = jnp.zeros(acc_ref.shape[1:], F32)
        s0_ref[g] = scores(g, qi)
    for g in groups:
        s1_ref[g] = scores(g, jnp.maximum(qi - 1, 0))
    for g in groups:
        m_ref[g] = _softmax_step(jnp.where(causal, s0_ref[g], NEG_INF), jnp.full(m_ref.shape[1:], NEG_INF, F32),
                                 vt_ref[g, qi], acc_ref.at[g])

    @pl.when(qi >= 1)
    def _():
        for g in groups:
            s0_ref[g] = scores(g, jnp.maximum(qi - 2, 0))
        for g in groups:
            m_ref[g] = _softmax_step(s1_ref[g], m_ref[g], vt_ref[g, qi - 1], acc_ref.at[g])

    @pl.when(qi >= 2)
    def _():
        for g in groups:
            _softmax_step(jnp.where(window_tail, s0_ref[g], NEG_INF), m_ref[g], vt_ref[g, qi - 2], acc_ref.at[g])

    for g in groups:
        _flash_finish(o_ref, acc_ref.at[g], g * NSA_HPG * NSA_DH)


def _window_attention(qt, k, vt, bsz, seq):
    n = bsz * seq
    nt = seq // ATT_TQ
    nkt = seq // ATT_TK
    width = NSA_HPG * ATT_TQ
    return pl.pallas_call(
        _win_attn_kernel,
        grid=(bsz, nt),
        in_specs=[pl.BlockSpec((None, NSA_GROUPS, None, NSA_DH, width), lambda b, i: (b, 0, i, 0, 0)),
                  pl.BlockSpec((seq, NSA_KV), lambda b, i: (b, 0)),
                  pl.BlockSpec((None, NSA_GROUPS, nkt, VT_ROWS, ATT_TK), lambda b, i: (b, 0, 0, 0, 0))],
        out_specs=pl.BlockSpec((ATT_TQ, NSA_Q), lambda b, i: (b * nt + i, 0)),
        out_shape=jax.ShapeDtypeStruct((n, NSA_Q), F32),
        scratch_shapes=[pltpu.VMEM((NSA_GROUPS, VT_ROWS, width), F32),
                        pltpu.VMEM((NSA_GROUPS, ATT_TK, width), F32),
                        pltpu.VMEM((NSA_GROUPS, ATT_TK, width), F32),
                        pltpu.VMEM((NSA_GROUPS, 1, width), F32)],
        compiler_params=_params("parallel", "parallel"),
        name="window_attention",
    )(qt, k, vt)


def _merge_kernel(x_ref, yr_ref, oc_ref, os_ref, ow_ref, ngl_ref, g_ref, wg_ref, wr_ref, wn_ref, wo_ref, o_ref):
    tm = x_ref.shape[0]
    hb = _rms(x_ref[...], g_ref[...]).astype(MXU_DTYPE)
    ga = jnp.dot(hb, wg_ref[:, :D_MODEL], preferred_element_type=F32)
    gb = jnp.dot(hb, wg_ref[:, D_MODEL:], preferred_element_type=F32)
    gates = jax.nn.sigmoid(ngl_ref[...])
    parts = []
    for h in range(NSA_HEADS):
        cols = slice(h * NSA_DH, (h + 1) * NSA_DH)

        def gate(br):
            return jnp.broadcast_to(gates[:, 3 * h + br:3 * h + br + 1], (tm, NSA_DH))

        parts.append(gate(0) * oc_ref[:, cols] + gate(1) * os_ref[:, cols] + gate(2) * ow_ref[:, cols])
    o_nsa = jnp.concatenate(parts, axis=1)
    y_ret = _dot(yr_ref[...], wr_ref[...])
    y_nsa = _dot(o_nsa, wn_ref[...])
    y = jax.nn.sigmoid(ga) * y_ret + jax.nn.sigmoid(gb) * y_nsa
    o_ref[...] = x_ref[...] + _dot(y, wo_ref[...])


def _merge(x2d, y_ret, o_cmp, o_sel, o_win, ngl, g_mix, w_gates, w_ret_o, w_nsa_o, w_out):
    n = x2d.shape[0]
    tm = ROW_TM
    row = lambda i: (i, 0)
    const = lambda i: (0, 0)
    wide = pl.BlockSpec((tm, D_MODEL), row)
    wspec = pl.BlockSpec((D_MODEL, D_MODEL), const)
    return pl.pallas_call(
        _merge_kernel,
        grid=(n // tm,),
        in_specs=[wide, wide, wide, wide, wide,
                  pl.BlockSpec((tm, LANES), row),
                  pl.BlockSpec((1, D_MODEL), const),
                  pl.BlockSpec((D_MODEL, 2 * D_MODEL), const),
                  wspec, wspec, wspec],
        out_specs=wide,
        out_shape=jax.ShapeDtypeStruct((n, D_MODEL), F32),
        compiler_params=_params("parallel"),
        name="merge",
    )(x2d, y_ret, o_cmp, o_sel, o_win, ngl, g_mix, w_gates, w_ret_o, w_nsa_o, w_out)


def _mem_kv_kernel(m_ref, g_ref, w_ref, o_ref):
    o_ref[...] = _dot(_rms(m_ref[...], g_ref[...]), w_ref[...]).astype(o_ref.dtype)


def _mem_kv(mem2d, g, w_xkv, bsz):
    nm = mem2d.shape[0] // bsz
    return pl.pallas_call(
        _mem_kv_kernel,
        grid=(bsz,),
        in_specs=[pl.BlockSpec((nm, D_MODEL), lambda b: (b, 0)),
                  pl.BlockSpec((1, D_MODEL), lambda b: (0, 0)),
                  pl.BlockSpec((D_MODEL, 2 * D_MODEL), lambda b: (0, 0))],
        out_specs=pl.BlockSpec((nm, 2 * D_MODEL), lambda b: (b, 0)),
        out_shape=jax.ShapeDtypeStruct((mem2d.shape[0], 2 * D_MODEL), MXU_DTYPE),
        compiler_params=_params("parallel"),
        name="mem_kv",
    )(mem2d, g, w_xkv)


def _pack_pairs(x):
    half = x.shape[1] // 2
    hi = lax.bitcast_convert_type(x[:, :half].astype(MXU_DTYPE).astype(F32), jnp.uint32)
    lo = lax.bitcast_convert_type(x[:, half:].astype(MXU_DTYPE).astype(F32), jnp.uint32)
    return (hi & jnp.uint32(0xFFFF0000)) | (lo >> 16)


def _unpack_pairs(u):
    hi = lax.bitcast_convert_type(u & jnp.uint32(0xFFFF0000), F32)
    lo = lax.bitcast_convert_type(u << 16, F32)
    return jnp.concatenate([hi, lo], axis=1)


_ROUTER_E0 = 2 * SUBLANES
_ROUTER_ROWS = _ROUTER_E0 + N_EXPERTS


def _top2_route(lgt):
    sub = SUBLANES
    t = lgt.shape[1]
    rowid = lax.broadcasted_iota(jnp.int32, (sub, t), 0)
    first = lambda hit: jnp.min(jnp.where(hit, rowid, sub), axis=0, keepdims=True)
    lg = jnp.where(rowid < N_EGROUPS, lgt[0:sub], NEG_INF)
    gmax = jnp.max(lg, axis=0, keepdims=True)
    grp = first(lg == gmax)
    g_gate = 1.0 / jnp.sum(jnp.exp(lg - gmax), axis=0, keepdims=True)
    experts_of = lambda g: lgt[_ROUTER_E0 + g * EXP_PER_GROUP:_ROUTER_E0 + (g + 1) * EXP_PER_GROUP]
    le = experts_of(N_EGROUPS - 1)
    for g in range(N_EGROUPS - 2, -1, -1):
        le = jnp.where(grp == g, experts_of(g), le)
    ex = jnp.exp(le - jnp.max(le, axis=0, keepdims=True))
    pe = ex / jnp.sum(ex, axis=0, keepdims=True)
    p0 = jnp.max(pe, axis=0, keepdims=True)
    i0 = first(pe == p0)
    rest = jnp.where(rowid == i0, -1.0, pe)
    p1 = jnp.max(rest, axis=0, keepdims=True)
    i1 = first(rest == p1)
    den = p0 + p1
    base = grp * EXP_PER_GROUP
    return jnp.concatenate([(base + i0).astype(F32), (base + i1).astype(F32),
                            g_gate * p0 / den, g_gate * p1 / den], axis=0)


def _cross_kernel(x_ref, kv_ref, gx_ref, wq_ref, wo_ref, gf_ref, wr_ref, br_ref, x2_ref, hf_ref, rt_ref):
    x = x_ref[...]
    q = _dot(_rms(x, gx_ref[...]), wq_ref[...])
    scores = [_dot_nt(q[:, h * X_DH:(h + 1) * X_DH], kv_ref[:, h * X_DH:(h + 1) * X_DH]) * (X_DH ** -0.5)
              for h in range(X_HEADS)]
    probs = []
    for s in scores:
        e = jnp.exp(s - jnp.max(s, axis=-1, keepdims=True))
        probs.append(e / jnp.sum(e, axis=-1, keepdims=True))
    heads = [_dot(p, kv_ref[:, D_MODEL + h * X_DH:D_MODEL + (h + 1) * X_DH]) for h, p in enumerate(probs)]
    x2 = x + _dot(jnp.concatenate(heads, axis=1), wo_ref[...])
    x2_ref[...] = x2
    hf = _rms(x2, gf_ref[...])
    hf_ref[...] = _pack_pairs(hf)
    h_hi, h_mid, _ = _split3(hf)
    by_hi = _dot_nt(wr_ref[...].reshape(2 * _ROUTER_ROWS, D_MODEL), h_hi)
    lgt = (by_hi[:_ROUTER_ROWS] + (_dot_nt(wr_ref[0], h_mid) + by_hi[_ROUTER_ROWS:])) + br_ref[...]
    rt_ref[...] = jnp.concatenate([_top2_route(lgt), jnp.zeros((SUBLANES - 4, lgt.shape[1]), F32)], axis=0)


def _cross(x1, kv, gx, w_xq, w_xo, gf, w_router, b_router, bsz, seq):
    n = x1.shape[0]
    tm = ROW_TM
    nt = seq // tm
    nm = kv.shape[0] // bsz
    row = lambda i: (i, 0)
    const = lambda i: (0, 0)
    vec = pl.BlockSpec((1, D_MODEL), const)
    wspec = pl.BlockSpec((D_MODEL, D_MODEL), const)
    return pl.pallas_call(
        _cross_kernel,
        grid=(n // tm,),
        in_specs=[pl.BlockSpec((tm, D_MODEL), row),
                  pl.BlockSpec((nm, 2 * D_MODEL), lambda i: (i // nt, 0)),
                  vec, wspec, wspec, vec,
                  pl.BlockSpec((2, _ROUTER_ROWS, D_MODEL), lambda i: (0, 0, 0)),
                  pl.BlockSpec((_ROUTER_ROWS, 1), const)],
        out_specs=[pl.BlockSpec((tm, D_MODEL), row),
                   pl.BlockSpec((tm, D_MODEL // 2), row),
                   pl.BlockSpec((SUBLANES, tm), lambda i: (0, i))],
        out_shape=[jax.ShapeDtypeStruct((n, D_MODEL), F32),
                   jax.ShapeDtypeStruct((n, D_MODEL // 2), jnp.uint32),
                   jax.ShapeDtypeStruct((SUBLANES, n), F32)],
        compiler_params=_params("parallel"),
        name="cross_attention",
    )(x1, kv, gx, w_xq, w_xo, gf, w_router, b_router)


def _expert_kernel(blk_ref, xb_ref, w1_ref, w3_ref, w2_ref, o_ref):
    n_used = blk_ref[pl.num_programs(0)]

    @pl.when(pl.program_id(0) < n_used)
    def _():
        xb = _unpack_pairs(xb_ref[...]).astype(MXU_DTYPE)
        a = _dot(xb, w1_ref[...])
        b = _dot(xb, w3_ref[...])
        o_ref[...] = _dot(a * jax.nn.sigmoid(a) * b, w2_ref[...])

    @pl.when(pl.program_id(0) >= n_used)
    def _():
        o_ref[...] = jnp.zeros_like(o_ref)


def _experts(blk, xb, w1, w3, w2):
    cap = xb.shape[0]
    nblk = cap // MOE_BLOCK
    row = lambda i, e: (i, 0)
    by_expert = lambda i, e: (e[i], 0, 0)
    grid_spec = pltpu.PrefetchScalarGridSpec(
        num_scalar_prefetch=1,
        grid=(nblk,),
        in_specs=[pl.BlockSpec((MOE_BLOCK, D_MODEL // 2), row),
                  pl.BlockSpec((None, D_MODEL, D_EXPERT), by_expert),
                  pl.BlockSpec((None, D_MODEL, D_EXPERT), by_expert),
                  pl.BlockSpec((None, D_EXPERT, D_MODEL), by_expert)],
        out_specs=pl.BlockSpec((MOE_BLOCK, D_MODEL), row),
    )
    return pl.pallas_call(
        _expert_kernel,
        grid_spec=grid_spec,
        out_shape=jax.ShapeDtypeStruct((cap, D_MODEL), F32),
        compiler_params=_params("arbitrary"),
        name="experts",
    )(blk, xb, w1, w3, w2)


def _final_kernel(x_ref, w_ref, g_ref, *refs):
    o_ref = refs[-1]
    tiles_per_split = pl.num_programs(0) // MOE_SPLITS
    for s in range(MOE_SPLITS):
        @pl.when(pl.program_id(0) // tiles_per_split == s)
        def _(s=s):
            moe = w_ref[:, 0:1] * refs[2 * s][...] + w_ref[:, 1:2] * refs[2 * s + 1][...]
            o_ref[...] = _rms(x_ref[...] + moe, g_ref[...])


def _final(x2, wts, g, ys):
    n = x2.shape[0]
    tm = ROW_TM
    tiles_per_split = n // tm // MOE_SPLITS
    row = lambda i: (i, 0)
    wide = pl.BlockSpec((tm, D_MODEL), row)

    def split_spec(s):
        return pl.BlockSpec((tm, D_MODEL), lambda i: (jnp.clip(i - s * tiles_per_split, 0, tiles_per_split - 1), 0))

    return pl.pallas_call(
        _final_kernel,
        grid=(n // tm,),
        in_specs=[wide, pl.BlockSpec((tm, EXP_TOPK), row), pl.BlockSpec((1, D_MODEL), lambda i: (0, 0))] + [
            split_spec(s) for s in range(MOE_SPLITS) for _ in range(EXP_TOPK)],
        out_specs=wide,
        out_shape=jax.ShapeDtypeStruct((n, D_MODEL), F32),
        compiler_params=_params("parallel"),
        name="final_norm",
    )(x2, wts, g, *[y for pair in ys for y in pair])


def _route(eid, tok0):
    n_tok = eid.shape[1]
    eid = eid.reshape(-1)
    n_asg = eid.shape[0]
    counts = jnp.sum((jnp.arange(N_EXPERTS, dtype=jnp.int32)[:, None] == eid[None, :]).astype(jnp.int32), axis=1)
    padded = (counts + MOE_BLOCK - 1) // MOE_BLOCK * MOE_BLOCK
    pends = jnp.cumsum(padded)
    cap = ((n_asg + MOE_BLOCK - 1) // MOE_BLOCK + N_EXPERTS) * MOE_BLOCK
    nblk = cap // MOE_BLOCK
    blk_e = jnp.minimum(jnp.searchsorted(pends, jnp.arange(nblk) * MOE_BLOCK, side='right', method='compare_all'),
                        N_EXPERTS - 1).astype(jnp.int32)
    filler = jnp.arange(cap - n_asg, dtype=jnp.int32)
    filler_e = jnp.searchsorted(jnp.cumsum(padded - counts), filler, side='right', method='compare_all')
    keys = jnp.concatenate([2 * eid, 2 * filler_e.astype(jnp.int32) + 1])
    toks = jnp.concatenate([jnp.arange(n_asg, dtype=jnp.int32) % n_tok, filler % n_tok])
    rows = jnp.arange(cap, dtype=jnp.int32)
    _, buf_tok, perm = lax.sort((keys, toks, rows), num_keys=1)
    _, pos = lax.sort_key_val(perm, rows)
    n_used = (pends[-1] // MOE_BLOCK).astype(jnp.int32)
    return tok0 + buf_tok, jnp.concatenate([blk_e, n_used[None]]), pos[:n_asg].reshape(EXP_TOPK, n_tok)


def kernel(x, mem, norm_mix_g, w_in, w_ret_o, w_nsa_o, w_out, cmp_pe_k, cmp_w1_k, cmp_w2_k, cmp_pe_v,
           cmp_w1_v, cmp_w2_v, norm_x_g, norm_mem_g, w_xq, w_xkv, w_xo, norm_ffn_g, w_grp, b_grp, w_rt,
           b_rt, w_e1, w_e3, w_e2, norm_f_g):
    bsz, seq, _ = x.shape
    n = bsz * seq
    assert seq % PROJ_TM == 0 and seq % (2 * ATT_TQ) == 0 and w_in.shape[0] == 1
    assert n % (ROW_TM * MOE_SPLITS) == 0
    cast = lambda a: a.astype(MXU_DTYPE)
    xc = x.reshape(n, D_MODEL)
    l = 0

    w_main, w_t, w_gates = _split_w_in(w_in[l])
    qk, rv, rg, ckv, sk, wk, ngl, qt, svt, wvt = _proj(
        xc, norm_mix_g[l][None, :], w_main, w_t, _rope_tables(seq), bsz, seq)
    y_ret = _retention(qk, rv, rg, bsz, seq)
    w2 = jnp.stack([cmp_w2_k[l], cmp_w2_v[l]])
    cmp_k, cmp_vt = _compress(ckv, jnp.stack([cmp_pe_k[l], cmp_pe_v[l]]),
                              cast(jnp.stack([cmp_w1_k[l], cmp_w1_v[l]])),
                              cast(w2), cast(w2.transpose(0, 2, 1)), bsz, seq)
    o_cmp, selt = _cmp_attention(qt, cmp_k, cmp_vt, bsz, seq)
    o_sel = _selected_attention(qt, sk, svt, selt, bsz, seq)
    o_win = _window_attention(qt, wk, wvt, bsz, seq)
    x1 = _merge(xc, y_ret, o_cmp, o_sel, o_win, ngl, norm_mix_g[l][None, :], w_gates,
                cast(w_ret_o[l]), cast(w_nsa_o[l]), cast(w_out[l]))

    kv = _mem_kv(mem.reshape(-1, D_MODEL), norm_mem_g[l][None, :], cast(w_xkv[l]), bsz)
    gap = _ROUTER_E0 - N_EGROUPS
    w_router = jnp.concatenate([w_grp[l].T, jnp.zeros((gap, D_MODEL), F32), w_rt[l].T], axis=0)
    wr_hi = cast(w_router)
    wr_mid = cast(w_router - wr_hi.astype(F32))
    b_router = jnp.concatenate([b_grp[l], jnp.zeros((gap,), F32), b_rt[l]])[:, None]
    x2, hf, routed = _cross(x1, kv, norm_x_g[l][None, :], cast(w_xq[l]), cast(w_xo[l]),
                            norm_ffn_g[l][None, :], jnp.stack([wr_hi, wr_mid]), b_router, bsz, seq)

    eid = routed[0:EXP_TOPK].astype(jnp.int32)
    wts = routed[EXP_TOPK:2 * EXP_TOPK].T
    per_split = n // MOE_SPLITS
    ys = []
    for s in range(MOE_SPLITS):
        buf_tok, blk, pos = _route(eid[:, s * per_split:(s + 1) * per_split], s * per_split)
        y = _experts(blk, hf[buf_tok], w_e1[l], w_e3[l], w_e2[l])
        ys.append([y[pos[j]] for j in range(EXP_TOPK)])
    out = _final(x2, wts, norm_f_g[None, :], ys)
    return out.reshape(bsz, seq, D_MODEL)
```

```python
import functools

import numpy as np
import jax
import jax.numpy as jnp
from jax import lax
from jax.experimental import pallas as pl
from jax.experimental.pallas import tpu as pltpu

MXU_DTYPE = jnp.bfloat16
F32 = jnp.float32

D_MODEL = 1024
N_MEM = 256
EPS = 1e-6
NEG_INF = -1e30
SEL_FORCE = 1e4

R_HEADS = 4
R_DK = 128
R_DV = 256
R_CHUNK = 128
ROPE_BASE = 10000.0

NSA_HEADS = 8
NSA_GROUPS = 2
NSA_HPG = NSA_HEADS // NSA_GROUPS
NSA_DH = 128
CMP_LEN = 32
CMP_STRIDE = 16
SEL_LEN = 64
SEL_TOPK = 16
WINDOW = 512

X_HEADS = 4
X_DH = D_MODEL // X_HEADS

N_EGROUPS = 4
EXP_PER_GROUP = 8
N_EXPERTS = N_EGROUPS * EXP_PER_GROUP
EXP_TOPK = 2
D_EXPERT = 512
MOE_BLOCK = 512

RET_QK = R_HEADS * R_DK
RET_V = R_HEADS * R_DV
NSA_Q = NSA_HEADS * NSA_DH
NSA_KV = NSA_GROUPS * NSA_DH
SPLITS = (RET_QK, RET_QK, RET_V, RET_V, NSA_Q, NSA_KV, NSA_KV, NSA_KV, NSA_KV, NSA_KV, NSA_KV,
          3 * NSA_HEADS, D_MODEL, D_MODEL)

_EXP2_SCALE = (NSA_DH ** -0.5) * float(np.log2(np.e))

LANES = 128
SUBLANES = 8
VMEM_LIMIT = 56 * 1024 * 1024

PROJ_TM = 512
RET_ROWS = 512
ATT_TQ = 256
ATT_TK = 256
ROW_TM = 512
MOE_SPLITS = 1
VT_ROWS = NSA_DH + 2 * SUBLANES
SEL_KW = 2 * NSA_DH
assert ATT_TQ == ATT_TK and SEL_LEN * 2 * SUBLANES >= ATT_TK and WINDOW == 2 * ATT_TK


def _params(*sem):
    return pltpu.CompilerParams(dimension_semantics=sem, vmem_limit_bytes=VMEM_LIMIT)


def _dot(a, b):
    return jnp.dot(a.astype(MXU_DTYPE), b.astype(MXU_DTYPE), preferred_element_type=F32)


def _dot_nt(a, b):
    return lax.dot_general(a.astype(MXU_DTYPE), b.astype(MXU_DTYPE), (((1,), (1,)), ((), ())),
                           preferred_element_type=F32)


def _dot_tn(a, b):
    return lax.dot_general(a.astype(MXU_DTYPE), b.astype(MXU_DTYPE), (((0,), (0,)), ((), ())),
                           preferred_element_type=F32)


def _split3(p):
    hi = p.astype(MXU_DTYPE)
    r1 = p - hi.astype(F32)
    mid = r1.astype(MXU_DTYPE)
    lo = (r1 - mid.astype(F32)).astype(MXU_DTYPE)
    return hi, mid, lo


def _rms(x, g):
    return x * lax.rsqrt(jnp.mean(x * x, axis=-1, keepdims=True) + EPS) * g


_C_RQK = 0
_C_RV = _C_RQK + 2 * RET_QK
_C_RG = _C_RV + RET_V
_C_CKV = _C_RG + RET_V
_C_SK = _C_CKV + 2 * NSA_KV
_C_WK = _C_SK + NSA_KV
_C_NGL = _C_WK + NSA_KV
_C_END = _C_NGL + LANES
_R_NQ = 0
_R_SV = _R_NQ + NSA_Q
_R_WV = _R_SV + NSA_KV
_R_END = _R_WV + NSA_KV


def _proj_kernel(x_ref, g_ref, w_ref, wt_ref, cos_ref, sin_up_ref, sin_dn_ref,
                 qk_ref, rv_ref, rg_ref, ckv_ref, sk_ref, wk_ref, ngl_ref, nqt_ref, svt_ref, wvt_ref):
    hb = _rms(x_ref[...], g_ref[...]).astype(MXU_DTYPE)
    tm = hb.shape[0]

    def mm(off, width):
        return jnp.dot(hb, w_ref[:, off:off + width], preferred_element_type=F32)

    def mm_t(off, height):
        return _dot_nt(wt_ref[off:off + height, :], hb)

    cos = cos_ref[...]
    sin_up = sin_up_ref[...]
    sin_dn = sin_dn_ref[...]
    qk = mm(_C_RQK, 2 * RET_QK)
    for i in range(2 * R_HEADS):
        t = qk[:, i * R_DK:(i + 1) * R_DK]
        r = t * cos + pltpu.roll(t, R_DK - 1, axis=1) * sin_up + pltpu.roll(t, 1, axis=1) * sin_dn
        if i >= R_HEADS:
            r = r * (R_DK ** -0.5)
        qk_ref[:, i * R_DK:(i + 1) * R_DK] = r.astype(qk_ref.dtype)
    rv_ref[...] = mm(_C_RV, RET_V).astype(rv_ref.dtype)
    rg_ref[...] = mm(_C_RG, RET_V)
    ckv_ref[...] = mm(_C_CKV, 2 * NSA_KV)
    sk = mm(_C_SK, NSA_KV).astype(sk_ref.dtype)
    blk = (lax.broadcasted_iota(jnp.int32, (tm, NSA_DH), 0) % ATT_TK) // SEL_LEN
    onehot = jnp.where(lax.broadcasted_iota(jnp.int32, (tm, NSA_DH), 1) == blk, 1.0, 0.0).astype(sk_ref.dtype)
    for g in range(NSA_GROUPS):
        sk_ref[:, g * SEL_KW:g * SEL_KW + NSA_DH] = sk[:, g * NSA_DH:(g + 1) * NSA_DH]
        sk_ref[:, g * SEL_KW + NSA_DH:(g + 1) * SEL_KW] = onehot
    wk_ref[...] = mm(_C_WK, NSA_KV).astype(wk_ref.dtype)
    ngl_ref[...] = mm(_C_NGL, LANES)

    nqt = (mm_t(_R_NQ, NSA_Q) * _EXP2_SCALE).astype(nqt_ref.dtype)
    for g in range(NSA_GROUPS):
        for j in range(tm // ATT_TQ):
            for hh in range(NSA_HPG):
                head = g * NSA_HPG + hh
                nqt_ref[g, j, :, hh * ATT_TQ:(hh + 1) * ATT_TQ] = (
                    nqt[head * NSA_DH:(head + 1) * NSA_DH, j * ATT_TQ:(j + 1) * ATT_TQ])
    for off, out_ref in ((_R_SV, svt_ref), (_R_WV, wvt_ref)):
        vt = mm_t(off, NSA_KV).astype(out_ref.dtype)
        for g in range(NSA_GROUPS):
            for j in range(tm // ATT_TK):
                out_ref[g, j, 0:NSA_DH] = vt[g * NSA_DH:(g + 1) * NSA_DH, j * ATT_TK:(j + 1) * ATT_TK]
                out_ref[g, j, NSA_DH:VT_ROWS] = jnp.ones((VT_ROWS - NSA_DH, ATT_TK), out_ref.dtype)


def _split_w_in(w_in):
    offs = np.cumsum((0,) + SPLITS)
    rq, rk, rv, rg, nq, ck, cv, sk, sv, wk, wv, ngl, ga, gb = [
        w_in[:, offs[i]:offs[i + 1]] for i in range(len(SPLITS))]
    ngl = jnp.pad(ngl, ((0, 0), (0, LANES - ngl.shape[1])))
    w = jnp.concatenate([rq, rk, rv, rg, ck, cv, sk, wk, ngl], axis=1).astype(MXU_DTYPE)
    wt = jnp.concatenate([nq, sv, wv], axis=1).T.astype(MXU_DTYPE)
    w_gates = jnp.concatenate([ga, gb], axis=1).astype(MXU_DTYPE)
    return w, wt, w_gates


def _rope_tables(seq):
    pos = jnp.arange(seq, dtype=F32)
    inv_freq = ROPE_BASE ** (-jnp.arange(0, R_DK, 2, dtype=F32) / R_DK)
    ang = pos[:, None] * inv_freq[None, :]
    cos = jnp.repeat(jnp.cos(ang), 2, axis=1)
    sin = jnp.repeat(jnp.sin(ang), 2, axis=1)
    even = (jnp.arange(R_DK) % 2 == 0)[None, :]
    return cos, jnp.where(even, -sin, 0.0), jnp.where(even, 0.0, sin)


def _proj(x2d, g, w, wt, tables, bsz, seq):
    n = x2d.shape[0]
    tm = PROJ_TM
    nt = seq // tm
    row = lambda i: (i, 0)
    const = lambda i: (0, 0)
    tile = lambda i: (i // nt, 0, i % nt, 0, 0)
    table = pl.BlockSpec((tm, R_DK), lambda i: (i % nt, 0))
    widths = (2 * RET_QK, RET_V, RET_V, 2 * NSA_KV, NSA_GROUPS * SEL_KW, NSA_KV, LANES)
    dtypes = (MXU_DTYPE, MXU_DTYPE, F32, F32, MXU_DTYPE, MXU_DTYPE, F32)
    width = NSA_HPG * ATT_TQ
    vt_shape = jax.ShapeDtypeStruct((bsz, NSA_GROUPS, seq // ATT_TK, VT_ROWS, ATT_TK), MXU_DTYPE)
    vt_spec = pl.BlockSpec((None, NSA_GROUPS, tm // ATT_TK, VT_ROWS, ATT_TK), tile)
    return pl.pallas_call(
        _proj_kernel,
        grid=(n // tm,),
        in_specs=[pl.BlockSpec((tm, D_MODEL), row),
                  pl.BlockSpec((1, D_MODEL), const),
                  pl.BlockSpec((D_MODEL, _C_END), const, pipeline_mode=pl.Buffered(1)),
                  pl.BlockSpec((_R_END, D_MODEL), const, pipeline_mode=pl.Buffered(1)),
                  table, table, table],
        out_specs=[pl.BlockSpec((tm, wd), row) for wd in widths] + [
            pl.BlockSpec((None, NSA_GROUPS, tm // ATT_TQ, NSA_DH, width), tile), vt_spec, vt_spec],
        out_shape=[jax.ShapeDtypeStruct((n, wd), dt) for wd, dt in zip(widths, dtypes)] + [
            jax.ShapeDtypeStruct((bsz, NSA_GROUPS, seq // ATT_TQ, NSA_DH, width), MXU_DTYPE), vt_shape, vt_shape],
        compiler_params=_params("parallel"),
        name="proj",
    )(x2d, g, w, wt, *tables)


def _ret_kernel(qk_ref, v_ref, g_ref, decay_ref, zeta_ref, xi_ref, cd_ref, y_ref, state_ref):
    @pl.when(pl.program_id(1) == 0)
    def _():
        state_ref[...] = jnp.zeros_like(state_ref)

    chunks = range(RET_ROWS // R_CHUNK)
    heads = range(R_HEADS)
    rows = lambda c: slice(c * R_CHUNK, (c + 1) * R_CHUNK)
    q_of = lambda c, h: qk_ref[rows(c), h * R_DK:(h + 1) * R_DK]
    k_of = lambda c, h: qk_ref[rows(c), RET_QK + h * R_DK:RET_QK + (h + 1) * R_DK]
    v_of = lambda c, h: v_ref[rows(c), h * R_DV:(h + 1) * R_DV]

    scores = {(c, h): _dot_nt(q_of(c, h), k_of(c, h)) * decay_ref[h] for c in chunks for h in heads}
    kv = {(c, h): _dot_tn(k_of(c, h).astype(F32) * zeta_ref[h], v_of(c, h)) for c in chunks for h in heads}
    seen = {}
    for h in heads:
        st = state_ref[h]
        for c in chunks:
            seen[c, h] = st
            st = st * cd_ref[h] + kv[c, h]
        state_ref[h] = st
    for c in chunks:
        for h in heads:
            o = _dot(scores[c, h], v_of(c, h)) + _dot(q_of(c, h), seen[c, h]) * xi_ref[h]
            o = o * lax.rsqrt(jnp.mean(o * o, axis=-1, keepdims=True) + EPS)
            g = g_ref[rows(c), h * R_DV:(h + 1) * R_DV]
            y_ref[rows(c), h * R_DV:(h + 1) * R_DV] = (g * jax.nn.sigmoid(g) * o).astype(y_ref.dtype)


def _retention(qk, rv, rg, bsz, seq):
    n = qk.shape[0]
    nt = seq // RET_ROWS
    log_g = jnp.log1p(-jnp.exp2(-5.0 - jnp.arange(R_HEADS, dtype=F32)))
    idx = jnp.arange(R_CHUNK, dtype=F32)
    diff = idx[:, None] - idx[None, :]
    decay = jnp.where(diff >= 0, jnp.exp(log_g[:, None, None] * jnp.maximum(diff, 0.0)), 0.0)
    zeta = jnp.exp(log_g[:, None] * (R_CHUNK - 1.0 - idx)[None, :])[:, :, None]
    xi = jnp.exp(log_g[:, None] * (idx + 1.0)[None, :])[:, :, None]
    cd = jnp.exp(log_g * R_CHUNK)[:, None, None]
    row = lambda b, i: (b * nt + i, 0)
    const3 = lambda b, i: (0, 0, 0)
    return pl.pallas_call(
        _ret_kernel,
        grid=(bsz, nt),
        in_specs=[pl.BlockSpec((RET_ROWS, 2 * RET_QK), row),
                  pl.BlockSpec((RET_ROWS, RET_V), row),
                  pl.BlockSpec((RET_ROWS, RET_V), row),
                  pl.BlockSpec((R_HEADS, R_CHUNK, R_CHUNK), const3),
                  pl.BlockSpec((R_HEADS, R_CHUNK, 1), const3),
                  pl.BlockSpec((R_HEADS, R_CHUNK, 1), const3),
                  pl.BlockSpec((R_HEADS, 1, 1), const3)],
        out_specs=pl.BlockSpec((RET_ROWS, RET_V), row),
        out_shape=jax.ShapeDtypeStruct((n, RET_V), MXU_DTYPE),
        scratch_shapes=[pltpu.VMEM((R_HEADS, R_DK, R_DV), F32)],
        compiler_params=_params("parallel", "arbitrary"),
        name="retention",
    )(qk, rv, rg, decay, zeta, xi, cd)


def _compress_kernel(x_ref, pe_ref, w1_ref, w2_ref, w2t_ref, o_ref, ot_ref, buf_ref, *, seq):
    ncp = seq // CMP_STRIDE
    buf_ref[0:seq, :] = x_ref[...]
    buf_ref[seq:seq + LANES, :] = jnp.zeros((LANES, NSA_DH), F32)
    acc = jnp.zeros((ncp, NSA_DH), F32)
    for l in range(CMP_LEN):
        xl = buf_ref[pl.ds(l, ncp, stride=CMP_STRIDE), :] + pe_ref[l:l + 1, :]
        acc = acc + _dot(xl, w1_ref[l])
    hid = jax.nn.gelu(acc)
    o_ref[...] = _dot(hid, w2_ref[...]).astype(o_ref.dtype)
    ot_ref[...] = _dot_nt(w2t_ref[...], hid).astype(ot_ref.dtype)


def _compress(ckv, pe, w1, w2, w2t, bsz, seq):
    ncp = seq // CMP_STRIDE
    nj = 2 * NSA_GROUPS
    wsel = lambda b, j: (j // NSA_GROUPS, 0, 0)
    return pl.pallas_call(
        functools.partial(_compress_kernel, seq=seq),
        grid=(bsz, nj),
        in_specs=[pl.BlockSpec((seq, NSA_DH), lambda b, j: (b, j)),
                  pl.BlockSpec((None, CMP_LEN, NSA_DH), wsel),
                  pl.BlockSpec((None, CMP_LEN, NSA_DH, NSA_DH), lambda b, j: (j // NSA_GROUPS, 0, 0, 0)),
                  pl.BlockSpec((None, NSA_DH, NSA_DH), wsel),
                  pl.BlockSpec((None, NSA_DH, NSA_DH), wsel)],
        out_specs=[pl.BlockSpec((None, None, ncp, NSA_DH), lambda b, j: (b, j, 0, 0)),
                   pl.BlockSpec((None, None, NSA_DH, ncp), lambda b, j: (b, j, 0, 0))],
        out_shape=[jax.ShapeDtypeStruct((bsz, nj, ncp, NSA_DH), MXU_DTYPE),
                   jax.ShapeDtypeStruct((bsz, nj, NSA_DH, ncp), MXU_DTYPE)],
        scratch_shapes=[pltpu.VMEM((seq + LANES, NSA_DH), F32)],
        compiler_params=_params("parallel", "parallel"),
        name="compress",
    )(ckv, pe, w1, w2, w2t)


def _cmp_attn_kernel(qt_ref, k_ref, vt_ref, ov_ref, o_ref, selt_ref, p_ref, imp_ref, *, ncp, nb):
    tq = ATT_TQ
    width = NSA_HPG * tq
    groups = range(NSA_GROUPS)
    qi = pl.program_id(1)
    t0 = qi * tq
    any_valid = (t0 + lax.broadcasted_iota(jnp.int32, (1, tq), 1)) >= CMP_LEN - 1
    any_valid = jnp.concatenate([any_valid] * NSA_HPG, axis=1)

    def probabilities(rows):
        t = t0 + lax.broadcasted_iota(jnp.int32, (rows, tq), 1)
        n = lax.broadcasted_iota(jnp.int32, (rows, tq), 0)
        valid = (n * CMP_STRIDE + (CMP_LEN - 1) <= t) & (n < ncp - 1)
        valid = jnp.concatenate([valid] * NSA_HPG, axis=1)
        for g in groups:
            s = jnp.where(valid, _dot(k_ref[g, 0:rows, :], qt_ref[g]), NEG_INF)
            e = jnp.exp2(s - jnp.max(s, axis=0, keepdims=True))
            p_ref[g, 0:rows, :] = e * jnp.where(any_valid, 1.0 / jnp.sum(e, axis=0, keepdims=True), 0.0)
            if rows < ncp:
                p_ref[g, rows:, :] = jnp.zeros((ncp - rows, width), F32)

    n_buckets = 4
    per_bucket = ncp // n_buckets
    visible = (t0 + tq - CMP_LEN) // CMP_STRIDE + 1
    bucket = jnp.minimum((visible + per_bucket - 1) // per_bucket, n_buckets)
    for c in range(1, n_buckets + 1):
        pl.when(bucket == c)(functools.partial(probabilities, c * per_bucket))

    j = lax.broadcasted_iota(jnp.int32, (nb, tq), 0)
    tb = (t0 + lax.broadcasted_iota(jnp.int32, (nb, tq), 1)) // SEL_LEN
    forced = (j == 0) | (j == tb) | (j == tb - 1)
    sub = SUBLANES
    k_sel = min(SEL_TOPK, nb)
    for g in groups:
        p = p_ref[g]
        ot = _dot(vt_ref[g], p)
        for h in range(NSA_HPG):
            o_ref[:, (g * NSA_HPG + h) * NSA_DH:(g * NSA_HPG + h + 1) * NSA_DH] = ot[:, h * tq:(h + 1) * tq].T
        psum = sum(p[:, h * tq:(h + 1) * tq] for h in range(NSA_HPG))

        imp = sum(_dot(ov_ref[...], part) for part in _split3(psum))
        imp_ref[g] = jnp.where(j > tb, -SEL_FORCE, jnp.where(forced, SEL_FORCE, imp))

    def choose(blocks):
        for g in groups:
            imp = imp_ref[g, 0:blocks, :]
            grp = [imp[r * sub:(r + 1) * sub] for r in range(blocks // sub)]
            cnt = [jnp.zeros((sub, tq), F32) for _ in grp]
            for i in range(blocks):
                row = jnp.broadcast_to(imp[i:i + 1, :], (sub, tq))
                for r in range(blocks // sub):
                    if r * sub > i:
                        beats = jnp.where(row >= grp[r], 1.0, 0.0)
                    elif r * sub + sub - 1 < i:
                        beats = jnp.where(row > grp[r], 1.0, 0.0)
                    else:
                        jr = r * sub + lax.broadcasted_iota(jnp.int32, (sub, tq), 0)
                        beats = jnp.where(jr > i, jnp.where(row >= grp[r], 1.0, 0.0),
                                          jnp.where(row > grp[r], 1.0, 0.0))
                    cnt[r] = cnt[r] + beats
            for r in range(blocks // sub):
                selt_ref[g, r * sub:(r + 1) * sub, :] = jnp.where(cnt[r] < k_sel, 1.0, 0.0)
            if blocks < nb:
                selt_ref[g, blocks:, :] = jnp.zeros((nb - blocks, tq), F32)

    for c in range(1, n_buckets + 1):
        pl.when(bucket == c)(functools.partial(choose, c * (nb // n_buckets)))


def _cmp_attention(qt, cmp_k, cmp_vt, bsz, seq):
    n = bsz * seq
    ncp = seq // CMP_STRIDE
    nb = seq // SEL_LEN
    nt = seq // ATT_TQ
    cstart = np.arange(ncp) * CMP_STRIDE
    jstart = np.arange(nb) * SEL_LEN
    ov = ((cstart[None, :] < jstart[:, None] + SEL_LEN) & (cstart[None, :] + CMP_LEN > jstart[:, None])
          & (np.arange(ncp)[None, :] < ncp - 1))
    ov = jnp.asarray(ov, MXU_DTYPE)
    return pl.pallas_call(
        functools.partial(_cmp_attn_kernel, ncp=ncp, nb=nb),
        grid=(bsz, nt),
        in_specs=[pl.BlockSpec((None, NSA_GROUPS, None, NSA_DH, NSA_HPG * ATT_TQ), lambda b, i: (b, 0, i, 0, 0)),
                  pl.BlockSpec((None, NSA_GROUPS, ncp, NSA_DH), lambda b, i: (b, 0, 0, 0)),
                  pl.BlockSpec((None, NSA_GROUPS, NSA_DH, ncp), lambda b, i: (b, 1, 0, 0)),
                  pl.BlockSpec((nb, ncp), lambda b, i: (0, 0))],
        out_specs=[pl.BlockSpec((ATT_TQ, NSA_Q), lambda b, i: (b * nt + i, 0)),
                   pl.BlockSpec((None, NSA_GROUPS, nb, ATT_TQ), lambda b, i: (b, 0, 0, i))],
        out_shape=[jax.ShapeDtypeStruct((n, NSA_Q), F32),
                   jax.ShapeDtypeStruct((bsz, NSA_GROUPS, nb, seq), F32)],
        scratch_shapes=[pltpu.VMEM((NSA_GROUPS, ncp, NSA_HPG * ATT_TQ), F32),
                        pltpu.VMEM((NSA_GROUPS, nb, ATT_TQ), F32)],
        compiler_params=_params("parallel", "parallel"),
        name="cmp_attention",
    )(qt, cmp_k, cmp_vt, ov)


def _softmax_step(s, m, vt, acc_ref):
    m_new = jnp.maximum(m, jnp.max(s, axis=0, keepdims=True))
    p = jnp.exp2(s - m_new)
    acc_ref[...] = jnp.exp2(m - m_new) * acc_ref[...] + _dot(vt, p)
    return m_new


def _flash_finish(o_ref, acc_ref, col0=0):
    inv = 1.0 / acc_ref[NSA_DH:NSA_DH + 1, :]
    for h in range(NSA_HPG):
        cols = slice(h * ATT_TQ, (h + 1) * ATT_TQ)
        o_ref[:, col0 + h * NSA_DH:col0 + (h + 1) * NSA_DH] = (acc_ref[0:NSA_DH, cols] * inv[:, cols]).T


def _sel_attn_kernel(qt_ref, k_ref, vt_ref, selt_ref, o_ref, acc_ref, s_ref, qa_ref):
    qi = pl.program_id(1)
    width = NSA_HPG * ATT_TQ
    groups = range(NSA_GROUPS)
    blocks_per_tile = ATT_TK // SEL_LEN
    bias_rows = 2 * SUBLANES
    n_loop = qi * (ATT_TQ // ATT_TK)
    n_steps = (n_loop + 1) // 2 * 2
    tile_of = lambda step: jnp.where(step < n_loop, step, qi)
    rowid = lax.broadcasted_iota(jnp.int32, (bias_rows, ATT_TQ), 0)
    tri = (lax.broadcasted_iota(jnp.int32, (ATT_TK, ATT_TQ), 0) <=
           lax.broadcasted_iota(jnp.int32, (ATT_TK, ATT_TQ), 1))
    tri = jnp.concatenate([tri] * NSA_HPG, axis=1)

    def scores(g, step):
        kt = tile_of(step)
        threshold = jnp.where((step < n_loop) | (step == n_steps), 0.5, 2.0)
        bias = jnp.zeros((bias_rows, ATT_TQ), F32)
        for jb in range(blocks_per_tile):
            picked = selt_ref[g, pl.ds(kt * blocks_per_tile + jb, 1), :] > threshold
            bias = jnp.where(rowid == jb, jnp.where(picked, 0.0, NEG_INF), bias)
        qa_ref[g, NSA_DH:NSA_DH + bias_rows, :] = jnp.concatenate([bias] * NSA_HPG, axis=1).astype(qa_ref.dtype)
        k = k_ref[pl.ds(pl.multiple_of(kt * ATT_TK, ATT_TK), ATT_TK), g * SEL_KW:(g + 1) * SEL_KW]
        return _dot(k, qa_ref[g])

    def half_step(step, cur, nxt, ms, diagonal=False):
        if not diagonal:
            for g in groups:
                s_ref[g, nxt] = scores(g, step + 1)
        out = []
        for g in groups:
            s = jnp.where(tri, s_ref[g, cur], NEG_INF) if diagonal else s_ref[g, cur]
            out.append(_softmax_step(s, ms[g], vt_ref[g, tile_of(step)], acc_ref.at[g]))
        return tuple(out)

    def body(i, ms):
        return half_step(2 * i + 1, 1, 0, half_step(2 * i, 0, 1, ms))

    for g in groups:
        qa_ref[g, 0:NSA_DH, :] = qt_ref[g]
        qa_ref[g, NSA_DH + bias_rows:, :] = jnp.zeros((SEL_KW - NSA_DH - bias_rows, width), qa_ref.dtype)
        acc_ref[g] = jnp.zeros(acc_ref.shape[1:], F32)
        s_ref[g, 0] = scores(g, 0)
    ms = lax.fori_loop(0, n_steps // 2, body, tuple(jnp.full((1, width), NEG_INF, F32) for _ in groups))
    half_step(n_steps, 0, 1, ms, diagonal=True)
    for g in groups:
        _flash_finish(o_ref, acc_ref.at[g], g * NSA_HPG * NSA_DH)


def _selected_attention(qt, k, vt, selt, bsz, seq):
    n = bsz * seq
    nt = seq // ATT_TQ
    nkt = seq // ATT_TK
    nb = seq // SEL_LEN
    width = NSA_HPG * ATT_TQ
    return pl.pallas_call(
        _sel_attn_kernel,
        grid=(bsz, nt),
        in_specs=[pl.BlockSpec((None, NSA_GROUPS, None, NSA_DH, width), lambda b, i: (b, 0, i, 0, 0)),
                  pl.BlockSpec((seq, NSA_GROUPS * SEL_KW), lambda b, i: (b, 0)),
                  pl.BlockSpec((None, NSA_GROUPS, nkt, VT_ROWS, ATT_TK), lambda b, i: (b, 0, 0, 0, 0)),
                  pl.BlockSpec((None, NSA_GROUPS, nb, ATT_TQ), lambda b, i: (b, 0, 0, i))],
        out_specs=pl.BlockSpec((ATT_TQ, NSA_Q), lambda b, i: (b * nt + i, 0)),
        out_shape=jax.ShapeDtypeStruct((n, NSA_Q), F32),
        scratch_shapes=[pltpu.VMEM((NSA_GROUPS, VT_ROWS, width), F32),
                        pltpu.VMEM((NSA_GROUPS, 2, ATT_TK, width), F32),
                        pltpu.VMEM((NSA_GROUPS, SEL_KW, width), MXU_DTYPE)],
        compiler_params=_params("parallel", "parallel"),
        name="selected_attention",
    )(qt, k, vt, selt)


def _win_attn_kernel(qt_ref, k_ref, vt_ref, o_ref, acc_ref, s0_ref, s1_ref, m_ref):
    qi = pl.program_id(1)
    groups = range(NSA_GROUPS)
    row = lax.broadcasted_iota(jnp.int32, (ATT_TK, ATT_TQ), 0)
    col = lax.broadcasted_iota(jnp.int32, (ATT_TK, ATT_TQ), 1)
    causal = jnp.concatenate([row <= col] * NSA_HPG, axis=1)
    window_tail = jnp.concatenate([row > col] * NSA_HPG, axis=1)

    def scores(g, kt):
        k = k_ref[pl.ds(pl.multiple_of(kt * ATT_TK, ATT_TK), ATT_TK), g * NSA_DH:(g + 1) * NSA_DH]
        return _dot(k, qt_ref[g])

    for g in groups:
        acc_ref[g] = jnp.zeros(acc_ref.shape[1:], F32)
        s0_ref[g] = scores(g, qi)
    for g in groups:
        s1_ref[g] = scores(g, jnp.maximum(qi - 1, 0))
    for g in groups:
        m_ref[g] = _softmax_step(jnp.where(causal, s0_ref[g], NEG_INF), jnp.full(m_ref.shape[1:], NEG_INF, F32),
                                 vt_ref[g, qi], acc_ref.at[g])

    @pl.when(qi >= 1)
    def _():
        for g in groups:
            s0_ref[g] = scores(g, jnp.maximum(qi - 2, 0))
        for g in groups:
            m_ref[g] = _softmax_step(s1_ref[g], m_ref[g], vt_ref[g, qi - 1], acc_ref.at[g])

    @pl.when(qi >= 2)
    def _():
        for g in groups:
            _softmax_step(jnp.where(window_tail, s0_ref[g], NEG_INF), m_ref[g], vt_ref[g, qi - 2], acc_ref.at[g])

    for g in groups:
        _flash_finish(o_ref, acc_ref.at[g], g * NSA_HPG * NSA_DH)


def _window_attention(qt, k, vt, bsz, seq):
    n = bsz * seq
    nt = seq // ATT_TQ
    nkt = seq // ATT_TK
    width = NSA_HPG * ATT_TQ
    return pl.pallas_call(
        _win_attn_kernel,
        grid=(bsz, nt),
        in_specs=[pl.BlockSpec((None, NSA_GROUPS, None, NSA_DH, width), lambda b, i: (b, 0, i, 0, 0)),
                  pl.BlockSpec((seq, NSA_KV), lambda b, i: (b, 0)),
                  pl.BlockSpec((None, NSA_GROUPS, nkt, VT_ROWS, ATT_TK), lambda b, i: (b, 0, 0, 0, 0))],
        out_specs=pl.BlockSpec((ATT_TQ, NSA_Q), lambda b, i: (b * nt + i, 0)),
        out_shape=jax.ShapeDtypeStruct((n, NSA_Q), F32),
        scratch_shapes=[pltpu.VMEM((NSA_GROUPS, VT_ROWS, width), F32),
                        pltpu.VMEM((NSA_GROUPS, ATT_TK, width), F32),
                        pltpu.VMEM((NSA_GROUPS, ATT_TK, width), F32),
                        pltpu.VMEM((NSA_GROUPS, 1, width), F32)],
        compiler_params=_params("parallel", "parallel"),
        name="window_attention",
    )(qt, k, vt)


def _merge_kernel(x_ref, yr_ref, oc_ref, os_ref, ow_ref, ngl_ref, g_ref, wg_ref, wr_ref, wn_ref, wo_ref, o_ref):
    tm = x_ref.shape[0]
    hb = _rms(x_ref[...], g_ref[...]).astype(MXU_DTYPE)
    ga = jnp.dot(hb, wg_ref[:, :D_MODEL], preferred_element_type=F32)
    gb = jnp.dot(hb, wg_ref[:, D_MODEL:], preferred_element_type=F32)
    gates = jax.nn.sigmoid(ngl_ref[...])
    parts = []
    for h in range(NSA_HEADS):
        cols = slice(h * NSA_DH, (h + 1) * NSA_DH)

        def gate(br):
            return jnp.broadcast_to(gates[:, 3 * h + br:3 * h + br + 1], (tm, NSA_DH))

        parts.append(gate(0) * oc_ref[:, cols] + gate(1) * os_ref[:, cols] + gate(2) * ow_ref[:, cols])
    o_nsa = jnp.concatenate(parts, axis=1)
    y_ret = _dot(yr_ref[...], wr_ref[...])
    y_nsa = _dot(o_nsa, wn_ref[...])
    y = jax.nn.sigmoid(ga) * y_ret + jax.nn.sigmoid(gb) * y_nsa
    o_ref[...] = x_ref[...] + _dot(y, wo_ref[...])


def _merge(x2d, y_ret, o_cmp, o_sel, o_win, ngl, g_mix, w_gates, w_ret_o, w_nsa_o, w_out):
    n = x2d.shape[0]
    tm = ROW_TM
    row = lambda i: (i, 0)
    const = lambda i: (0, 0)
    wide = pl.BlockSpec((tm, D_MODEL), row)
    wspec = pl.BlockSpec((D_MODEL, D_MODEL), const)
    return pl.pallas_call(
        _merge_kernel,
        grid=(n // tm,),
        in_specs=[wide, wide, wide, wide, wide,
                  pl.BlockSpec((tm, LANES), row),
                  pl.BlockSpec((1, D_MODEL), const),
                  pl.BlockSpec((D_MODEL, 2 * D_MODEL), const),
                  wspec, wspec, wspec],
        out_specs=wide,
        out_shape=jax.ShapeDtypeStruct((n, D_MODEL), F32),
        compiler_params=_params("parallel"),
        name="merge",
    )(x2d, y_ret, o_cmp, o_sel, o_win, ngl, g_mix, w_gates, w_ret_o, w_nsa_o, w_out)


def _mem_kv_kernel(m_ref, g_ref, w_ref, o_ref):
    o_ref[...] = _dot(_rms(m_ref[...], g_ref[...]), w_ref[...]).astype(o_ref.dtype)


def _mem_kv(mem2d, g, w_xkv, bsz):
    nm = mem2d.shape[0] // bsz
    return pl.pallas_call(
        _mem_kv_kernel,
        grid=(bsz,),
        in_specs=[pl.BlockSpec((nm, D_MODEL), lambda b: (b, 0)),
                  pl.BlockSpec((1, D_MODEL), lambda b: (0, 0)),
                  pl.BlockSpec((D_MODEL, 2 * D_MODEL), lambda b: (0, 0))],
        out_specs=pl.BlockSpec((nm, 2 * D_MODEL), lambda b: (b, 0)),
        out_shape=jax.ShapeDtypeStruct((mem2d.shape[0], 2 * D_MODEL), MXU_DTYPE),
        compiler_params=_params("parallel"),
        name="mem_kv",
    )(mem2d, g, w_xkv)


def _pack_pairs(x):
    half = x.shape[1] // 2
    hi = lax.bitcast_convert_type(x[:, :half].astype(MXU_DTYPE).astype(F32), jnp.uint32)
    lo = lax.bitcast_convert_type(x[:, half:].astype(MXU_DTYPE).astype(F32), jnp.uint32)
    return (hi & jnp.uint32(0xFFFF0000)) | (lo >> 16)


def _unpack_pairs(u):
    hi = lax.bitcast_convert_type(u & jnp.uint32(0xFFFF0000), F32)
    lo = lax.bitcast_convert_type(u << 16, F32)
    return jnp.concatenate([hi, lo], axis=1)


_ROUTER_E0 = 2 * SUBLANES
_ROUTER_ROWS = _ROUTER_E0 + N_EXPERTS


def _top2_route(lgt):
    sub = SUBLANES
    t = lgt.shape[1]
    rowid = lax.broadcasted_iota(jnp.int32, (sub, t), 0)
    first = lambda hit: jnp.min(jnp.where(hit, rowid, sub), axis=0, keepdims=True)
    lg = jnp.where(rowid < N_EGROUPS, lgt[0:sub], NEG_INF)
    gmax = jnp.max(lg, axis=0, keepdims=True)
    grp = first(lg == gmax)
    g_gate = 1.0 / jnp.sum(jnp.exp(lg - gmax), axis=0, keepdims=True)
    experts_of = lambda g: lgt[_ROUTER_E0 + g * EXP_PER_GROUP:_ROUTER_E0 + (g + 1) * EXP_PER_GROUP]
    le = experts_of(N_EGROUPS - 1)
    for g in range(N_EGROUPS - 2, -1, -1):
        le = jnp.where(grp == g, experts_of(g), le)
    ex = jnp.exp(le - jnp.max(le, axis=0, keepdims=True))
    pe = ex / jnp.sum(ex, axis=0, keepdims=True)
    p0 = jnp.max(pe, axis=0, keepdims=True)
    i0 = first(pe == p0)
    rest = jnp.where(rowid == i0, -1.0, pe)
    p1 = jnp.max(rest, axis=0, keepdims=True)
    i1 = first(rest == p1)
    den = p0 + p1
    base = grp * EXP_PER_GROUP
    return jnp.concatenate([(base + i0).astype(F32), (base + i1).astype(F32),
                            g_gate * p0 / den, g_gate * p1 / den], axis=0)


def _cross_kernel(x_ref, kv_ref, gx_ref, wq_ref, wo_ref, gf_ref, wr_ref, br_ref, x2_ref, hf_ref, rt_ref):
    x = x_ref[...]
    q = _dot(_rms(x, gx_ref[...]), wq_ref[...])
    scores = [_dot_nt(q[:, h * X_DH:(h + 1) * X_DH], kv_ref[:, h * X_DH:(h + 1) * X_DH]) * (X_DH ** -0.5)
              for h in range(X_HEADS)]
    probs = []
    for s in scores:
        e = jnp.exp(s - jnp.max(s, axis=-1, keepdims=True))
        probs.append(e / jnp.sum(e, axis=-1, keepdims=True))
    heads = [_dot(p, kv_ref[:, D_MODEL + h * X_DH:D_MODEL + (h + 1) * X_DH]) for h, p in enumerate(probs)]
    x2 = x + _dot(jnp.concatenate(heads, axis=1), wo_ref[...])
    x2_ref[...] = x2
    hf = _rms(x2, gf_ref[...])
    hf_ref[...] = _pack_pairs(hf)
    h_hi, h_mid, _ = _split3(hf)
    w_hi = wr_ref[0]
    w_mid = wr_ref[1]
    lgt = (_dot_nt(w_hi, h_hi) + (_dot_nt(w_hi, h_mid) + _dot_nt(w_mid, h_hi))) + br_ref[...]
    rt_ref[...] = jnp.concatenate([_top2_route(lgt), jnp.zeros((SUBLANES - 4, lgt.shape[1]), F32)], axis=0)


def _cross(x1, kv, gx, w_xq, w_xo, gf, w_router, b_router, bsz, seq):
    n = x1.shape[0]
    tm = ROW_TM
    nt = seq // tm
    nm = kv.shape[0] // bsz
    row = lambda i: (i, 0)
    const = lambda i: (0, 0)
    vec = pl.BlockSpec((1, D_MODEL), const)
    wspec = pl.BlockSpec((D_MODEL, D_MODEL), const)
    return pl.pallas_call(
        _cross_kernel,
        grid=(n // tm,),
        in_specs=[pl.BlockSpec((tm, D_MODEL), row),
                  pl.BlockSpec((nm, 2 * D_MODEL), lambda i: (i // nt, 0)),
                  vec, wspec, wspec, vec,
                  pl.BlockSpec((2, _ROUTER_ROWS, D_MODEL), lambda i: (0, 0, 0)),
                  pl.BlockSpec((_ROUTER_ROWS, 1), const)],
        out_specs=[pl.BlockSpec((tm, D_MODEL), row),
                   pl.BlockSpec((tm, D_MODEL // 2), row),
                   pl.BlockSpec((SUBLANES, tm), lambda i: (0, i))],
        out_shape=[jax.ShapeDtypeStruct((n, D_MODEL), F32),
                   jax.ShapeDtypeStruct((n, D_MODEL // 2), jnp.uint32),
                   jax.ShapeDtypeStruct((SUBLANES, n), F32)],
        compiler_params=_params("parallel"),
        name="cross_attention",
    )(x1, kv, gx, w_xq, w_xo, gf, w_router, b_router)


def _expert_kernel(blk_ref, xb_ref, w1_ref, w3_ref, w2_ref, o_ref):
    n_used = blk_ref[pl.num_programs(0)]

    @pl.when(pl.program_id(0) < n_used)
    def _():
        xb = _unpack_pairs(xb_ref[...]).astype(MXU_DTYPE)
        a = _dot(xb, w1_ref[...])
        b = _dot(xb, w3_ref[...])
        o_ref[...] = _dot(a * jax.nn.sigmoid(a) * b, w2_ref[...])

    @pl.when(pl.program_id(0) >= n_used)
    def _():
        o_ref[...] = jnp.zeros_like(o_ref)


def _experts(blk, xb, w1, w3, w2):
    cap = xb.shape[0]
    nblk = cap // MOE_BLOCK
    row = lambda i, e: (i, 0)
    by_expert = lambda i, e: (e[i], 0, 0)
    grid_spec = pltpu.PrefetchScalarGridSpec(
        num_scalar_prefetch=1,
        grid=(nblk,),
        in_specs=[pl.BlockSpec((MOE_BLOCK, D_MODEL // 2), row),
                  pl.BlockSpec((None, D_MODEL, D_EXPERT), by_expert),
                  pl.BlockSpec((None, D_MODEL, D_EXPERT), by_expert),
                  pl.BlockSpec((None, D_EXPERT, D_MODEL), by_expert)],
        out_specs=pl.BlockSpec((MOE_BLOCK, D_MODEL), row),
    )
    return pl.pallas_call(
        _expert_kernel,
        grid_spec=grid_spec,
        out_shape=jax.ShapeDtypeStruct((cap, D_MODEL), F32),
        compiler_params=_params("arbitrary"),
        name="experts",
    )(blk, xb, w1, w3, w2)


def _final_kernel(x_ref, w_ref, g_ref, *refs):
    o_ref = refs[-1]
    tiles_per_split = pl.num_programs(0) // MOE_SPLITS
    for s in range(MOE_SPLITS):
        @pl.when(pl.program_id(0) // tiles_per_split == s)
        def _(s=s):
            moe = w_ref[:, 0:1] * refs[2 * s][...] + w_ref[:, 1:2] * refs[2 * s + 1][...]
            o_ref[...] = _rms(x_ref[...] + moe, g_ref[...])


def _final(x2, wts, g, ys):
    n = x2.shape[0]
    tm = ROW_TM
    tiles_per_split = n // tm // MOE_SPLITS
    row = lambda i: (i, 0)
    wide = pl.BlockSpec((tm, D_MODEL), row)

    def split_spec(s):
        return pl.BlockSpec((tm, D_MODEL), lambda i: (jnp.clip(i - s * tiles_per_split, 0, tiles_per_split - 1), 0))

    return pl.pallas_call(
        _final_kernel,
        grid=(n // tm,),
        in_specs=[wide, pl.BlockSpec((tm, EXP_TOPK), row), pl.BlockSpec((1, D_MODEL), lambda i: (0, 0))] + [
            split_spec(s) for s in range(MOE_SPLITS) for _ in range(EXP_TOPK)],
        out_specs=wide,
        out_shape=jax.ShapeDtypeStruct((n, D_MODEL), F32),
        compiler_params=_params("parallel"),
        name="final_norm",
    )(x2, wts, g, *[y for pair in ys for y in pair])


def _route(eid, tok0):
    n_tok = eid.shape[1]
    eid = eid.reshape(-1)
    n_asg = eid.shape[0]
    counts = jnp.sum((jnp.arange(N_EXPERTS, dtype=jnp.int32)[:, None] == eid[None, :]).astype(jnp.int32), axis=1)
    padded = (counts + MOE_BLOCK - 1) // MOE_BLOCK * MOE_BLOCK
    pends = jnp.cumsum(padded)
    cap = ((n_asg + MOE_BLOCK - 1) // MOE_BLOCK + N_EXPERTS) * MOE_BLOCK
    nblk = cap // MOE_BLOCK
    blk_e = jnp.minimum(jnp.searchsorted(pends, jnp.arange(nblk) * MOE_BLOCK, side='right', method='compare_all'),
                        N_EXPERTS - 1).astype(jnp.int32)
    filler = jnp.arange(cap - n_asg, dtype=jnp.int32)
    filler_e = jnp.searchsorted(jnp.cumsum(padded - counts), filler, side='right', method='compare_all')
    keys = jnp.concatenate([2 * eid, 2 * filler_e.astype(jnp.int32) + 1])
    toks = jnp.concatenate([jnp.arange(n_asg, dtype=jnp.int32) % n_tok, filler % n_tok])
    rows = jnp.arange(cap, dtype=jnp.int32)
    _, buf_tok, perm = lax.sort((keys, toks, rows), num_keys=1)
    _, pos = lax.sort_key_val(perm, rows)
    n_used = (pends[-1] // MOE_BLOCK).astype(jnp.int32)
    return tok0 + buf_tok, jnp.concatenate([blk_e, n_used[None]]), pos[:n_asg].reshape(EXP_TOPK, n_tok)


def kernel(x, mem, norm_mix_g, w_in, w_ret_o, w_nsa_o, w_out, cmp_pe_k, cmp_w1_k, cmp_w2_k, cmp_pe_v,
           cmp_w1_v, cmp_w2_v, norm_x_g, norm_mem_g, w_xq, w_xkv, w_xo, norm_ffn_g, w_grp, b_grp, w_rt,
           b_rt, w_e1, w_e3, w_e2, norm_f_g):
    bsz, seq, _ = x.shape
    n = bsz * seq
    assert seq % PROJ_TM == 0 and seq % (2 * ATT_TQ) == 0 and w_in.shape[0] == 1
    assert n % (ROW_TM * MOE_SPLITS) == 0
    cast = lambda a: a.astype(MXU_DTYPE)
    xc = x.reshape(n, D_MODEL)
    l = 0

    w_main, w_t, w_gates = _split_w_in(w_in[l])
    qk, rv, rg, ckv, sk, wk, ngl, qt, svt, wvt = _proj(
        xc, norm_mix_g[l][None, :], w_main, w_t, _rope_tables(seq), bsz, seq)
    y_ret = _retention(qk, rv, rg, bsz, seq)
    w2 = jnp.stack([cmp_w2_k[l], cmp_w2_v[l]])
    cmp_k, cmp_vt = _compress(ckv, jnp.stack([cmp_pe_k[l], cmp_pe_v[l]]),
                              cast(jnp.stack([cmp_w1_k[l], cmp_w1_v[l]])),
                              cast(w2), cast(w2.transpose(0, 2, 1)), bsz, seq)
    o_cmp, selt = _cmp_attention(qt, cmp_k, cmp_vt, bsz, seq)
    o_sel = _selected_attention(qt, sk, svt, selt, bsz, seq)
    o_win = _window_attention(qt, wk, wvt, bsz, seq)
    x1 = _merge(xc, y_ret, o_cmp, o_sel, o_win, ngl, norm_mix_g[l][None, :], w_gates,
                cast(w_ret_o[l]), cast(w_nsa_o[l]), cast(w_out[l]))

    kv = _mem_kv(mem.reshape(-1, D_MODEL), norm_mem_g[l][None, :], cast(w_xkv[l]), bsz)
    gap = _ROUTER_E0 - N_EGROUPS
    w_router = jnp.concatenate([w_grp[l].T, jnp.zeros((gap, D_MODEL), F32), w_rt[l].T], axis=0)
    wr_hi = cast(w_router)
    wr_mid = cast(w_router - wr_hi.astype(F32))
    b_router = jnp.concatenate([b_grp[l], jnp.zeros((gap,), F32), b_rt[l]])[:, None]
    x2, hf, routed = _cross(x1, kv, norm_x_g[l][None, :], cast(w_xq[l]), cast(w_xo[l]),
                            norm_ffn_g[l][None, :], jnp.stack([wr_hi, wr_mid]), b_router, bsz, seq)

    eid = routed[0:EXP_TOPK].astype(jnp.int32)
    wts = routed[EXP_TOPK:2 * EXP_TOPK].T
    per_split = n // MOE_SPLITS
    ys = []
    for s in range(MOE_SPLITS):
        buf_tok, blk, pos = _route(eid[:, s * per_split:(s + 1) * per_split], s * per_split)
        y = _experts(blk, hf[buf_tok], w_e1[l], w_e3[l], w_e2[l])
        ys.append([y[pos[j]] for j in range(EXP_TOPK)])
    out = _final(x2, wts, norm_f_g[None, :], ys)
    return out.reshape(bsz, seq, D_MODEL)
```
